```python
import math
import jax, jax.numpy as jnp
from jax import lax
import numpy as np

D_MODEL = 1024
BATCH = 4
SEQ = 4096
DEPTH = 4
DEC_BATCH = 128
DEC_SEQ = 8
PAST_LEN = 2048
PAGE_SIZE = 128

N_A = DEPTH // 2
N_B = DEPTH - N_A
GLA_HEADS = 4
GLA_DV = (3 * D_MODEL) // (4 * GLA_HEADS)
GLA_DK = GLA_DV // 2
GLA_RANK = 16
GLA_TAU = 16.0
GLA_CHUNK = 64
HD = 64
NSA_HEADS = (3 * D_MODEL) // (4 * HD)
NSA_KV = 2
NSA_HPG = NSA_HEADS // NSA_KV
CMP_S = 16
CMP_L = 2 * CMP_S
CMP_HID = 4 * HD
SLC_L = 64
SLC_K = 16
WINDOW = 512
Q_BLK = 128
MEM_TOK = 256
MEM_HEADS = 4
MEM_HD = D_MODEL // (4 * MEM_HEADS)
N_EXP = 32
TOP_K = 4
D_FF = D_MODEL
SWIGLU_LIMIT = 7.0
SWIGLU_ALPHA = 1.702
MOE_BLK = 128
GLA_W = GLA_HEADS * GLA_DV
NSA_W = NSA_HEADS * HD
MEM_W = MEM_HEADS * MEM_HD
MIX_W = D_MODEL
IN_A = 2 * GLA_HEADS * GLA_DK + 2 * GLA_W + GLA_RANK + MEM_W
IN_B = NSA_W + 3 * NSA_HEADS + MEM_W
DN_ALPHA = (2 * DEPTH) ** 0.25
DN_BETA = (8 * DEPTH) ** -0.25
LN_EPS = 1e-5
NEG = -1e30

kernel_name = 'yoco_gla_nsa_moe_decoder'


def layer_norm(x, g, b):
    xf = x.astype(jnp.float32)
    mu = jnp.mean(xf, axis=-1, keepdims=True)
    var = jnp.mean(jnp.square(xf - mu), axis=-1, keepdims=True)
    return ((xf - mu) * lax.rsqrt(var + LN_EPS) * g + b).astype(x.dtype)


def masked_softmax(s, mask):
    p = jax.nn.softmax(jnp.where(mask, s.astype(jnp.float32), NEG), axis=-1)
    return p * mask


def alibi_slopes(n):
    def pow2(m):
        start = 2.0 ** (-8.0 / m)
        return [start ** (i + 1) for i in range(m)]
    if math.log2(n).is_integer():
        s = pow2(n)
    else:
        c = 2 ** math.floor(math.log2(n))
        s = pow2(c) + pow2(2 * c)[0::2][: n - c]
    return jnp.asarray(s, jnp.float32)


def last_rows(a, n):
    t = a.shape[1]
    if t >= n:
        return a[:, t - n:]
    return jnp.pad(a, ((0, 0), (n - t, 0)) + ((0, 0),) * (a.ndim - 2))


def memory_attention(mq, mem_kv):
    B, T, _ = mq.shape
    q = mq.reshape(B, T, MEM_HEADS, MEM_HD)
    s = jnp.einsum('bthd,bmhd->bhtm', q, mem_kv[:, :, 0]).astype(jnp.float32) * (MEM_HD ** -0.5)
    p = jax.nn.softmax(s, axis=-1).astype(mq.dtype)
    return jnp.einsum('bhtm,bmhd->bthd', p, mem_kv[:, :, 1]).reshape(B, T, MEM_W)


def gla_scan(q, k, v, log_a, s0):
    B, T, H, _ = q.shape
    C = GLA_CHUNK if T % GLA_CHUNK == 0 else T
    n = T // C

    def to_chunks(a):
        return jnp.moveaxis(a.astype(jnp.float32).reshape(B, n, C, *a.shape[2:]), 1, 0)

    tri = jnp.tril(jnp.ones((C, C), bool))

    def step(S, inp):
        qc, kc, vc, gc = inp
        b = jnp.cumsum(gc, axis=1)
        bl = b[:, -1]
        qt = qc * jnp.exp(b)
        kt = kc * jnp.exp(-b)
        att = jnp.where(tri, jnp.einsum('bchk,bshk->bhcs', qt, kt), 0.0)
        o = jnp.einsum('bchk,bhkv->bchv', qt, S) + jnp.einsum('bhcs,bshv->bchv', att, vc)
        kd = kc * jnp.exp(bl[:, None] - b)
        S = jnp.exp(bl)[..., None] * S + jnp.einsum('bshk,bshv->bhkv', kd, vc)
        return S, o

    sT, o = lax.scan(step, s0.astype(jnp.float32), (to_chunks(q), to_chunks(k), to_chunks(v), to_chunks(log_a)))
    return jnp.moveaxis(o, 0, 1).reshape(B, T, H, -1), sT


def mixer_a(x, mem_kv, s0, w_in, w_gate, b_gate, gla_norm, w_out):
    B, T, _ = x.shape
    hk = GLA_HEADS * GLA_DK
    cuts = [hk, 2 * hk, 2 * hk + GLA_W, 2 * hk + 2 * GLA_W, 2 * hk + 2 * GLA_W + GLA_RANK]
    q, k, v, r, glr, mq = jnp.split(x @ w_in, cuts, axis=-1)
    q = q.reshape(B, T, GLA_HEADS, GLA_DK) * (GLA_DK ** -0.5)
    k = k.reshape(B, T, GLA_HEADS, GLA_DK)
    v = v.reshape(B, T, GLA_HEADS, GLA_DV)
    log_a = jax.nn.log_sigmoid((glr @ w_gate + b_gate).astype(jnp.float32)) / GLA_TAU
    o, s_new = gla_scan(q, k, v, log_a.reshape(B, T, GLA_HEADS, GLA_DK), s0)
    o = o * lax.rsqrt(jnp.mean(o * o, axis=-1, keepdims=True) + LN_EPS) * gla_norm.reshape(GLA_HEADS, GLA_DV)
    o = o.astype(x.dtype).reshape(B, T, GLA_W) * jax.nn.silu(r)
    om = memory_attention(mq, mem_kv)
    return jnp.concatenate([o, om], axis=-1) @ w_out, s_new.astype(s0.dtype)


def compress(kv_tok, cmp_pos, w_cmp1, b_cmp1, w_cmp2):
    B, T = kv_tok.shape[:2]
    nch = T // CMP_S
    c = kv_tok[:, : nch * CMP_S].reshape(B, nch, CMP_S, 2, NSA_KV, HD)
    blk = jnp.concatenate([c[:, :-1], c[:, 1:]], axis=2)
    blk = blk + jnp.transpose(cmp_pos, (1, 0, 2))[:, :, None, :]
    blk = jnp.transpose(blk, (0, 1, 3, 4, 2, 5)).reshape(B, nch - 1, 2, NSA_KV, CMP_L * HD)
    hid = jax.nn.gelu(jnp.einsum('bnsgf,sfh->bnsgh', blk, w_cmp1) + b_cmp1[:, None, :])
    return jnp.einsum('bnsgh,shd->bnsgd', hid, w_cmp2)


def selection_blocks(kv_tok):
    B, T = kv_tok.shape[:2]
    nsb = -(-T // SLC_L)
    kv = jnp.pad(kv_tok, ((0, 0), (0, nsb * SLC_L - T), (0, 0), (0, 0), (0, 0)))
    return jnp.transpose(kv.reshape(B, nsb, SLC_L, 2, NSA_KV, HD), (0, 4, 1, 2, 3, 5))


def nsa_branches(q, gates, qpos, ckv, skv, wkv, wpos, slopes):
    B, Q = q.shape[:2]
    f32 = jnp.float32
    qg = q.reshape(B, Q, NSA_KV, NSA_HPG, HD)
    sl = slopes.reshape(NSA_KV, NSA_HPG)
    nc = ckv.shape[1]
    cpos = jnp.arange(nc) * CMP_S + (CMP_L - 1)
    cd = (qpos[:, None] - cpos[None, :]).astype(f32)
    s = jnp.einsum('bqghd,bngd->bghqn', qg, ckv[:, :, 0]).astype(f32) - sl[:, :, None, None] * cd
    p_cmp = masked_softmax(s, cd >= 0)
    o_cmp = jnp.einsum('bghqn,bngd->bqghd', p_cmp.astype(q.dtype), ckv[:, :, 1])
    nsb = skv.shape[2]
    ratio = SLC_L // CMP_S
    imp = p_cmp.sum(axis=2)
    chunk = jnp.pad(imp, ((0, 0), (0, 0), (0, 0), (0, 1))) + jnp.pad(imp, ((0, 0), (0, 0), (0, 0), (1, 0)))
    chunk = jnp.pad(chunk, ((0, 0), (0, 0), (0, 0), (0, nsb * ratio - nc - 1)))
    blk = chunk.reshape(B, NSA_KV, Q, nsb, ratio).sum(-1)
    bidx = jnp.arange(nsb)
    cur = qpos // SLC_L
    valid = bidx[None, :] * SLC_L <= qpos[:, None]
    forced = (bidx[None, :] == 0) | (bidx[None, :] == cur[:, None]) | (bidx[None, :] == cur[:, None] - 1)
    score = jnp.where(valid, jnp.where(forced, 1e9, blk), -1e9)
    k_sel = min(SLC_K, nsb)
    _, sidx = lax.top_k(score, k_sel)
    svalid = jnp.take_along_axis(jnp.broadcast_to(valid, score.shape), sidx, axis=-1)
    bi = jnp.arange(B)[:, None, None, None]
    gi = jnp.arange(NSA_KV)[None, :, None, None]
    sel = skv[bi, gi, sidx]
    spos = sidx[..., None] * SLC_L + jnp.arange(SLC_L)
    sd = (qpos[None, None, :, None, None] - spos).astype(f32)
    smask = (sd >= 0) & svalid[..., None]
    s = jnp.einsum('bqghd,bgqkld->bghqkl', qg, sel[..., 0, :]).astype(f32) - sl[None, :, :, None, None, None] * sd[:, :, None]
    p = masked_softmax(s.reshape(B, NSA_KV, NSA_HPG, Q, -1), smask[:, :, None].reshape(B, NSA_KV, 1, Q, -1))
    o_slc = jnp.einsum('bghqkl,bgqkld->bqghd', p.reshape(s.shape).astype(q.dtype), sel[..., 1, :])
    wd = (qpos[:, None] - wpos[None, :]).astype(f32)
    wmask = (wd >= 0) & (wd <= WINDOW) & (wpos[None, :] >= 0)
    s = jnp.einsum('bqghd,bwgd->bghqw', qg, wkv[:, :, 0]).astype(f32) - sl[:, :, None, None] * wd
    p = masked_softmax(s, wmask)
    o_win = jnp.einsum('bghqw,bwgd->bqghd', p.astype(q.dtype), wkv[:, :, 1])
    g = gates.reshape(B, Q, NSA_KV, NSA_HPG, 3, 1)
    o = g[..., 0, :] * o_cmp + g[..., 1, :] * o_slc + g[..., 2, :] * o_win
    return o.reshape(B, Q, NSA_W)


def nsa_prompt(q, gates, ckv, skv, win_kv, slopes):
    B, T = q.shape[:2]
    wpad = jnp.pad(win_kv, ((0, 0), (WINDOW, 0), (0, 0), (0, 0), (0, 0)))

    def blk(i):
        q0 = i * Q_BLK
        qb = lax.dynamic_slice_in_dim(q, q0, Q_BLK, axis=1)
        gb = lax.dynamic_slice_in_dim(gates, q0, Q_BLK, axis=1)
        wb = lax.dynamic_slice_in_dim(wpad, q0, WINDOW + Q_BLK, axis=1)
        qpos = q0 + jnp.arange(Q_BLK)
        wpos = q0 - WINDOW + jnp.arange(WINDOW + Q_BLK)
        return nsa_branches(qb, gb, qpos, ckv, skv, wb, wpos, slopes)

    out = lax.map(blk, jnp.arange(T // Q_BLK))
    return jnp.moveaxis(out, 0, 1).reshape(B, T, NSA_W)


def mixer_b(x, mem_kv, nsa_fn, w_in, b_gate, w_out):
    B, T, _ = x.shape
    q, gl, mq = jnp.split(x @ w_in, [NSA_W, NSA_W + 3 * NSA_HEADS], axis=-1)
    q = q.reshape(B, T, NSA_HEADS, HD) * (HD ** -0.5)
    gates = jax.nn.sigmoid((gl + b_gate).astype(jnp.float32)).reshape(B, T, NSA_HEADS, 3).astype(x.dtype)
    o = nsa_fn(q, gates)
    om = memory_attention(mq, mem_kv)
    return jnp.concatenate([o, om], axis=-1) @ w_out


def moe_ffn(x, w_r, b_r, w1, b1, w2, b2):
    B, T, D = x.shape
    N = B * T
    xf = x.reshape(N, D)
    logits = (xf @ w_r + b_r).astype(jnp.float32)
    top_v, top_i = lax.top_k(logits, TOP_K)
    gate = jax.nn.softmax(top_v, axis=-1)
    A = N * TOP_K
    flat_e = top_i.reshape(A)
    flat_tok = jnp.arange(A, dtype=jnp.int32) // TOP_K
    order = jnp.argsort(flat_e)
    se = flat_e[order]
    counts = jnp.bincount(flat_e, length=N_EXP)
    start = jnp.cumsum(counts) - counts
    padded = (counts + MOE_BLK - 1) // MOE_BLK * MOE_BLK
    pend = jnp.cumsum(padded)
    pstart = pend - padded
    dest = pstart[se] + jnp.arange(A) - start[se]
    nblk = -(-A // MOE_BLK) + N_EXP
    P = nblk * MOE_BLK
    slot_tok = jnp.full((P,), N, jnp.int32).at[dest].set(flat_tok[order])
    blk_exp = jnp.minimum(jnp.searchsorted(pend, jnp.arange(nblk) * MOE_BLK, side='right'), N_EXP - 1)
    xpad = jnp.concatenate([xf, jnp.zeros((1, D), xf.dtype)], axis=0)
    xb = xpad[slot_tok].reshape(nblk, MOE_BLK, D)

    def expert_block(args):
        xe, e = args
        h = xe @ w1[e] + b1[e]
        g = jnp.minimum(h[:, :D_FF], SWIGLU_LIMIT)
        u = jnp.clip(h[:, D_FF:], -SWIGLU_LIMIT, SWIGLU_LIMIT)
        a = (u + 1.0) * g * jax.nn.sigmoid(SWIGLU_ALPHA * g)
        return a @ w2[e] + b2[e]

    yb = lax.map(expert_block, (xb, blk_exp)).reshape(P, D)
    contrib = yb[dest] * gate.reshape(A)[order][:, None].astype(x.dtype)
    y = jnp.zeros((N, D), x.dtype).at[flat_tok[order]].add(contrib)
    return y.reshape(B, T, D)


def setup_inputs(seed: int = 0) -> dict:
    key = jax.random.key(seed)
    ks = iter(jax.random.split(key, 40))

    def nrm(shape, scale):
        return jax.random.normal(next(ks), shape, jnp.float32) * scale

    n_pages = PAST_LEN // PAGE_SIZE
    n_pool = (DEC_BATCH * n_pages * 5) // 4
    wb = min(WINDOW, PAST_LEN)
    page_table = jax.random.permutation(next(ks), n_pool)[: DEC_BATCH * n_pages].reshape(DEC_BATCH, n_pages).astype(jnp.int32)
    return {
        'x_prompt': nrm((BATCH, SEQ, D_MODEL), 1.0),
        'x_sample': nrm((DEC_BATCH, DEC_SEQ, D_MODEL), 1.0),
        'mem_prompt': nrm((BATCH, MEM_TOK, D_MODEL), 1.0),
        'state_gla': nrm((N_A, DEC_BATCH, GLA_HEADS, GLA_DK, GLA_DV), 0.3),
        'cache_nsa_kv': nrm((n_pool, PAGE_SIZE, 4, NSA_KV, HD), 1.0),
        'state_win_kv': nrm((DEC_BATCH, wb, 2, NSA_KV, HD), 1.0),
        'cache_mem_kv': nrm((DEPTH, DEC_BATCH, MEM_TOK, 2, MEM_HEADS, MEM_HD), 1.0),
        'page_table': page_table,
        'w_in_a': nrm((N_A, D_MODEL, IN_A), D_MODEL ** -0.5),
        'w_gate_a': nrm((N_A, GLA_RANK, GLA_HEADS * GLA_DK), GLA_RANK ** -0.5),
        'b_gate_a': nrm((N_A, GLA_HEADS * GLA_DK), 0.1),
        'gla_norm': 1.0 + nrm((N_A, GLA_W), 0.01),
        'w_in_b': nrm((N_B, D_MODEL, IN_B), D_MODEL ** -0.5),
        'b_gate_b': nrm((N_B, 3 * NSA_HEADS), 0.1),
        'w_kv_b': nrm((D_MODEL, 6 * NSA_KV * HD), D_MODEL ** -0.5),
        'cmp_pos': nrm((2, CMP_L, HD), 0.1),
        'w_cmp1': nrm((2, CMP_L * HD, CMP_HID), (CMP_L * HD) ** -0.5),
        'b_cmp1': nrm((2, CMP_HID), 0.02),
        'w_cmp2': nrm((2, CMP_HID, HD), CMP_HID ** -0.5),
        'w_mem_kv': nrm((DEPTH, D_MODEL, 2 * MEM_W), D_MODEL ** -0.5),
        'w_out': nrm((DEPTH, MIX_W, D_MODEL), MIX_W ** -0.5 * DN_BETA),
        'ln1_g': 1.0 + nrm((DEPTH, D_MODEL), 0.01),
        'ln1_b': nrm((DEPTH, D_MODEL), 0.01),
        'ln2_g': 1.0 + nrm((DEPTH, D_MODEL), 0.01),
        'ln2_b': nrm((DEPTH, D_MODEL), 0.01),
        'w_router': nrm((DEPTH, D_MODEL, N_EXP), D_MODEL ** -0.5),
        'b_router': nrm((DEPTH, N_EXP), 0.01),
        'w_e1': nrm((DEPTH, N_EXP, D_MODEL, 2 * D_FF), D_MODEL ** -0.5),
        'b_e1': nrm((DEPTH, N_EXP, 2 * D_FF), 0.01),
        'w_e2': nrm((DEPTH, N_EXP, D_FF, D_MODEL), D_FF ** -0.5 * DN_BETA),
        'b_e2': nrm((DEPTH, N_EXP, D_MODEL), 0.01),
    }


def reference(x_prompt, x_sample, mem_prompt, state_gla, cache_nsa_kv, state_win_kv, cache_mem_kv, page_table,
              w_in_a, w_gate_a, b_gate_a, gla_norm, w_in_b, b_gate_b, w_kv_b, cmp_pos, w_cmp1, b_cmp1, w_cmp2,
              w_mem_kv, w_out, ln1_g, ln1_b, ln2_g, ln2_b, w_router, b_router, w_e1, b_e1, w_e2, b_e2):
    slopes = alibi_slopes(NSA_HEADS)
    wb = min(WINDOW, PAST_LEN)

    def shared_kv(h):
        B, T, _ = h.shape
        kv = (h @ w_kv_b).reshape(B, T, 6, NSA_KV, HD)
        return kv[:, :, :4], kv[:, :, 4:]

    def history_ctx(rows):
        ckv = compress(rows[:, :, :2], cmp_pos, w_cmp1, b_cmp1, w_cmp2)
        skv = selection_blocks(rows[:, :, 2:])
        return ckv, skv

    def make_nsa_prompt(h):
        rows, win = shared_kv(h)
        ckv, skv = history_ctx(rows)
        fn = lambda q, g: nsa_prompt(q, g, ckv, skv, win, slopes)
        return fn, (rows, last_rows(win, wb))

    def make_nsa_sample(h):
        rows, win = shared_kv(h)
        nb, n_pages = page_table.shape
        past = cache_nsa_kv[page_table].reshape(nb, n_pages * PAGE_SIZE, 4, NSA_KV, HD)
        ckv, skv = history_ctx(jnp.concatenate([past, rows.astype(past.dtype)], axis=1))
        wkv = jnp.concatenate([state_win_kv, win.astype(state_win_kv.dtype)], axis=1)
        wpos = PAST_LEN - wb + jnp.arange(wb + DEC_SEQ)
        qpos = PAST_LEN + jnp.arange(DEC_SEQ)
        fn = lambda q, g: nsa_branches(q, g, qpos, ckv, skv, wkv, wpos, slopes)
        return fn, (rows, wkv[:, -wb:])

    def trunk(x, mem_kv, gla_s0, make_nsa):
        new_gla = []
        nsa_fn, nsa_state = None, None
        for l in range(DEPTH):
            if l < N_A:
                mix, s_new = mixer_a(x, mem_kv[l], gla_s0[l], w_in_a[l], w_gate_a[l], b_gate_a[l], gla_norm[l], w_out[l])
                new_gla.append(s_new)
            else:
                if l == N_A:
                    nsa_fn, nsa_state = make_nsa(x)
                j = l - N_A
                mix = mixer_b(x, mem_kv[l], nsa_fn, w_in_b[j], b_gate_b[j], w_out[l])
            x = layer_norm(DN_ALPHA * x + mix, ln1_g[l], ln1_b[l])
            x = layer_norm(DN_ALPHA * x + moe_ffn(x, w_router[l], b_router[l], w_e1[l], b_e1[l], w_e2[l], b_e2[l]), ln2_g[l], ln2_b[l])
        return x, jnp.stack(new_gla), nsa_state

    bp = x_prompt.shape[0]
    mem_kv_prompt = jnp.einsum('bmd,ldf->lbmf', mem_prompt, w_mem_kv).reshape(DEPTH, bp, MEM_TOK, 2, MEM_HEADS, MEM_HD)
    gla0 = jnp.zeros((N_A, bp, GLA_HEADS, GLA_DK, GLA_DV), x_prompt.dtype)
    y_prompt, gla_p, (nsa_p, win_p) = trunk(x_prompt, mem_kv_prompt, gla0, make_nsa_prompt)
    y_sample, gla_s, (nsa_s, win_s) = trunk(x_sample, cache_mem_kv, state_gla, make_nsa_sample)
    return (y_prompt, y_sample, gla_p, gla_s, nsa_p, nsa_s, win_p, win_s, mem_kv_prompt)
```

```python
import functools
import math

import jax
import jax.numpy as jnp
from jax import lax
from jax.experimental import pallas as pl
from jax.experimental.pallas import tpu as pltpu

D_MODEL = 1024
DEPTH = 4
PAGE_SIZE = 128
N_A = DEPTH // 2
GLA_HEADS = 4
GLA_DV = (3 * D_MODEL) // (4 * GLA_HEADS)
GLA_DK = GLA_DV // 2
GLA_RANK = 16
GLA_TAU = 16.0
GLA_CHUNK = 64
HD = 64
NSA_HEADS = (3 * D_MODEL) // (4 * HD)
NSA_KV = 2
NSA_HPG = NSA_HEADS // NSA_KV
CMP_S = 16
CMP_L = 2 * CMP_S
SLC_L = 64
SLC_K = 16
WINDOW = 512
Q_BLK = 128
MEM_TOK = 256
MEM_HEADS = 4
MEM_HD = D_MODEL // (4 * MEM_HEADS)
N_EXP = 32
TOP_K = 4
D_FF = D_MODEL
SWIGLU_LIMIT = 7.0
SWIGLU_ALPHA = 1.702
GLA_W = GLA_HEADS * GLA_DV
NSA_W = NSA_HEADS * HD
MEM_W = MEM_HEADS * MEM_HD
DN_ALPHA = (2 * DEPTH) ** 0.25
LN_EPS = 1e-5
NEG = -1e30

VMEM_LIMIT_BYTES = 56 * 1024 * 1024
ROW_TILE = 512
MOE_ROWS = 256
FF_CHUNK = 512


def _cparams(sem):
    return pltpu.CompilerParams(dimension_semantics=sem, vmem_limit_bytes=VMEM_LIMIT_BYTES)


def _proj_kernel(x_ref, *refs, n_out):
    xb = x_ref[...].astype(jnp.bfloat16)
    for w_ref, o_ref in zip(refs[:n_out], refs[n_out:]):
        o_ref[...] = jnp.dot(xb, w_ref[...], preferred_element_type=jnp.float32).astype(o_ref.dtype)


def _proj(x, ws, out_dtypes, tm=ROW_TILE):
    m, k = x.shape
    tm = min(tm, m)
    in_specs = [pl.BlockSpec((tm, k), lambda i: (i, 0))]
    in_specs += [pl.BlockSpec(w.shape, lambda i: (0, 0)) for w in ws]
    out_specs = [pl.BlockSpec((tm, w.shape[1]), lambda i: (i, 0)) for w in ws]
    out_shape = [jax.ShapeDtypeStruct((m, w.shape[1]), dt) for w, dt in zip(ws, out_dtypes)]
    return pl.pallas_call(
        functools.partial(_proj_kernel, n_out=len(ws)),
        grid=(m // tm,),
        in_specs=in_specs,
        out_specs=out_specs,
        out_shape=out_shape,
        compiler_params=_cparams(("parallel",)),
        name="proj",
    )(x, *ws)


def _layer_norm_rows(y, g, b):
    mu = jnp.mean(y, axis=-1, keepdims=True)
    d = y - mu
    var = jnp.mean(d * d, axis=-1, keepdims=True)
    return d * lax.rsqrt(var + LN_EPS) * g + b


def _mix_ln_kernel(a_ref, w_ref, x_ref, g_ref, b_ref, wr_ref, br_ref, x1_ref, lg_ref):
    mix = jnp.dot(a_ref[...].astype(jnp.bfloat16), w_ref[...], preferred_element_type=jnp.float32)
    x1 = _layer_norm_rows(DN_ALPHA * x_ref[...] + mix, g_ref[...], b_ref[...])
    x1_ref[...] = x1
    lg_ref[...] = jnp.dot(x1, wr_ref[...], preferred_element_type=jnp.float32,
                          precision=lax.Precision.HIGHEST) + br_ref[...]


def _mix_ln(a, w_out, x, g, b, w_r, b_r, tm=ROW_TILE):
    m, k = a.shape
    d = x.shape[1]
    row = lambda i: (i, 0)
    fix = lambda i: (0, 0)
    return pl.pallas_call(
        _mix_ln_kernel,
        grid=(m // tm,),
        in_specs=[pl.BlockSpec((tm, k), row), pl.BlockSpec((k, d), fix), pl.BlockSpec((tm, d), row),
                  pl.BlockSpec((1, d), fix), pl.BlockSpec((1, d), fix),
                  pl.BlockSpec((d, N_EXP), fix), pl.BlockSpec((1, N_EXP), fix)],
        out_specs=[pl.BlockSpec((tm, d), row), pl.BlockSpec((tm, N_EXP), row)],
        out_shape=[jax.ShapeDtypeStruct((m, d), jnp.float32), jax.ShapeDtypeStruct((m, N_EXP), jnp.float32)],
        compiler_params=_cparams(("parallel",)),
        name="mix_ln",
    )(a, w_out, x, g.reshape(1, d), b.reshape(1, d), w_r, b_r.reshape(1, N_EXP))


def _moe_kernel(be_ref, nreal_ref, x_ref, w1_ref, b1_ref, w2_ref, b2_ref, o_ref, w1s, w2s):
    i = pl.program_id(0)
    real = i < nreal_ref[0]
    prev = be_ref[jnp.maximum(i - 1, 0)]
    fresh = jnp.logical_or(i == 0, be_ref[i] != prev)

    @pl.when(jnp.logical_and(real, fresh))
    def _():
        w1s[...] = w1_ref[0].astype(jnp.bfloat16)
        w2s[...] = w2_ref[0].astype(jnp.bfloat16)

    @pl.when(real)
    def _():
        x = x_ref[...]
        acc = jnp.zeros((MOE_ROWS, D_MODEL), jnp.float32)
        for c in range(D_FF // FF_CHUNK):
            lo = c * FF_CHUNK
            hg = jnp.dot(x, w1s[:, lo:lo + FF_CHUNK], preferred_element_type=jnp.float32)
            hg = hg + b1_ref[0, :, lo:lo + FF_CHUNK]
            hu = jnp.dot(x, w1s[:, D_FF + lo:D_FF + lo + FF_CHUNK], preferred_element_type=jnp.float32)
            hu = hu + b1_ref[0, :, D_FF + lo:D_FF + lo + FF_CHUNK]
            g = jnp.minimum(hg, SWIGLU_LIMIT)
            u = jnp.clip(hu, -SWIGLU_LIMIT, SWIGLU_LIMIT)
            a = (u + 1.0) * g * jax.nn.sigmoid(SWIGLU_ALPHA * g)
            acc = acc + jnp.dot(a.astype(jnp.bfloat16), w2s[lo:lo + FF_CHUNK, :],
                                preferred_element_type=jnp.float32)
        o_ref[...] = acc + b2_ref[0]

    @pl.when(jnp.logical_not(real))
    def _():
        o_ref[...] = jnp.zeros_like(o_ref)


def _moe_ffn_blocks(xb, blk_exp, nreal, w1, b1, w2, b2):
    p, d = xb.shape
    nblk = p // MOE_ROWS
    grid_spec = pltpu.PrefetchScalarGridSpec(
        num_scalar_prefetch=2,
        grid=(nblk,),
        in_specs=[
            pl.BlockSpec((MOE_ROWS, d), lambda i, be, nr: (i, 0)),
            pl.BlockSpec((1, d, 2 * D_FF), lambda i, be, nr: (be[i], 0, 0)),
            pl.BlockSpec((1, 1, 2 * D_FF), lambda i, be, nr: (be[i], 0, 0)),
            pl.BlockSpec((1, D_FF, d), lambda i, be, nr: (be[i], 0, 0)),
            pl.BlockSpec((1, 1, d), lambda i, be, nr: (be[i], 0, 0)),
        ],
        out_specs=pl.BlockSpec((MOE_ROWS, d), lambda i, be, nr: (i, 0)),
        scratch_shapes=[pltpu.VMEM((d, 2 * D_FF), jnp.bfloat16), pltpu.VMEM((D_FF, d), jnp.bfloat16)],
    )
    return pl.pallas_call(
        _moe_kernel,
        grid_spec=grid_spec,
        out_shape=jax.ShapeDtypeStruct((p, d), jnp.float32),
        compiler_params=_cparams(("arbitrary",)),
        name="moe_ffn",
    )(blk_exp, nreal, xb, w1, b1.reshape(N_EXP, 1, 2 * D_FF), w2, b2.reshape(N_EXP, 1, d))


def _moe(x1, logits, w1, b1, w2, b2):
    n, d = x1.shape
    top_v, top_i = lax.top_k(logits, TOP_K)
    gate = jax.nn.softmax(top_v, axis=-1)
    a = n * TOP_K
    flat_e = top_i.reshape(a)
    onehot = (flat_e[:, None] == jnp.arange(N_EXP, dtype=jnp.int32)[None, :]).astype(jnp.int32)
    csum = jnp.cumsum(onehot, axis=0)
    counts = csum[-1]
    rank = jnp.take_along_axis(csum, flat_e[:, None], axis=1)[:, 0] - 1
    padded = (counts + MOE_ROWS - 1) // MOE_ROWS * MOE_ROWS
    pend = jnp.cumsum(padded)
    pstart = pend - padded
    dest = pstart[flat_e] + rank
    nblk = -(-a // MOE_ROWS) + N_EXP
    p = nblk * MOE_ROWS
    slot_tok = jnp.full((p,), n, jnp.int32).at[dest].set(jnp.arange(a, dtype=jnp.int32) // TOP_K)
    blk_start = jnp.arange(nblk, dtype=jnp.int32) * MOE_ROWS
    nreal = (pend[-1] // MOE_ROWS).astype(jnp.int32)
    blk_exp = jnp.searchsorted(pend, blk_start, side='right').astype(jnp.int32)
    last_exp = blk_exp[jnp.maximum(nreal - 1, 0)]
    blk_exp = jnp.where(blk_start < pend[-1], jnp.minimum(blk_exp, N_EXP - 1), last_exp)
    xpad = jnp.concatenate([x1.astype(jnp.bfloat16), jnp.zeros((1, d), jnp.bfloat16)], axis=0)
    xb = xpad[slot_tok]
    yb = _moe_ffn_blocks(xb, blk_exp, nreal.reshape(1), w1, b1, w2, b2)
    contrib = yb[dest].reshape(n, TOP_K, d) * gate[:, :, None]
    return contrib.sum(axis=1)


def _layer_norm(x, g, b):
    mu = jnp.mean(x, axis=-1, keepdims=True)
    var = jnp.mean(jnp.square(x - mu), axis=-1, keepdims=True)
    return (x - mu) * lax.rsqrt(var + LN_EPS) * g + b


def _masked_softmax(s, mask):
    p = jax.nn.softmax(jnp.where(mask, s.astype(jnp.float32), NEG), axis=-1)
    return p * mask


def _alibi_slopes(n):
    def pow2(m):
        start = 2.0 ** (-8.0 / m)
        return [start ** (i + 1) for i in range(m)]
    if math.log2(n).is_integer():
        s = pow2(n)
    else:
        c = 2 ** math.floor(math.log2(n))
        s = pow2(c) + pow2(2 * c)[0::2][: n - c]
    return jnp.asarray(s, jnp.float32)


def _last_rows(a, n):
    t = a.shape[1]
    if t >= n:
        return a[:, t - n:]
    return jnp.pad(a, ((0, 0), (n - t, 0)) + ((0, 0),) * (a.ndim - 2))


def _memory_attention(mq, mem_kv):
    B, T, _ = mq.shape
    q = mq.reshape(B, T, MEM_HEADS, MEM_HD)
    s = jnp.einsum('bthd,bmhd->bhtm', q, mem_kv[:, :, 0]).astype(jnp.float32) * (MEM_HD ** -0.5)
    p = jax.nn.softmax(s, axis=-1).astype(mq.dtype)
    return jnp.einsum('bhtm,bmhd->bthd', p, mem_kv[:, :, 1]).reshape(B, T, MEM_W)


def _gla_scan(q, k, v, log_a, s0):
    B, T, H, _ = q.shape
    C = GLA_CHUNK if T % GLA_CHUNK == 0 else T
    n = T // C

    def to_chunks(a):
        return jnp.moveaxis(a.astype(jnp.float32).reshape(B, n, C, *a.shape[2:]), 1, 0)

    tri = jnp.tril(jnp.ones((C, C), bool))

    def step(S, inp):
        qc, kc, vc, gc = inp
        b = jnp.cumsum(gc, axis=1)
        bl = b[:, -1]
        qt = qc * jnp.exp(b)
        kt = kc * jnp.exp(-b)
        att = jnp.where(tri, jnp.einsum('bchk,bshk->bhcs', qt, kt), 0.0)
        o = jnp.einsum('bchk,bhkv->bchv', qt, S) + jnp.einsum('bhcs,bshv->bchv', att, vc)
        kd = kc * jnp.exp(bl[:, None] - b)
        S = jnp.exp(bl)[..., None] * S + jnp.einsum('bshk,bshv->bhkv', kd, vc)
        return S, o

    sT, o = lax.scan(step, s0.astype(jnp.float32), (to_chunks(q), to_chunks(k), to_chunks(v), to_chunks(log_a)))
    return jnp.moveaxis(o, 0, 1).reshape(B, T, H, -1), sT


def _gla_mix(q, k, v, r, glr, s0, w_gate, b_gate, gla_norm):
    B, T, _ = q.shape
    q = q.reshape(B, T, GLA_HEADS, GLA_DK) * (GLA_DK ** -0.5)
    k = k.reshape(B, T, GLA_HEADS, GLA_DK)
    v = v.reshape(B, T, GLA_HEADS, GLA_DV)
    log_a = jax.nn.log_sigmoid((glr @ w_gate + b_gate).astype(jnp.float32)) / GLA_TAU
    o, s_new = _gla_scan(q, k, v, log_a.reshape(B, T, GLA_HEADS, GLA_DK), s0)
    o = o * lax.rsqrt(jnp.mean(o * o, axis=-1, keepdims=True) + LN_EPS) * gla_norm.reshape(GLA_HEADS, GLA_DV)
    return o.reshape(B, T, GLA_W) * jax.nn.silu(r), s_new


def _compress(kv_tok, cmp_pos, w_cmp1, b_cmp1, w_cmp2):
    B, T = kv_tok.shape[:2]
    nch = T // CMP_S
    c = kv_tok[:, : nch * CMP_S].reshape(B, nch, CMP_S, 2, NSA_KV, HD)
    blk = jnp.concatenate([c[:, :-1], c[:, 1:]], axis=2)
    blk = blk + jnp.transpose(cmp_pos, (1, 0, 2))[:, :, None, :]
    blk = jnp.transpose(blk, (0, 1, 3, 4, 2, 5)).reshape(B, nch - 1, 2, NSA_KV, CMP_L * HD)
    hid = jax.nn.gelu(jnp.einsum('bnsgf,sfh->bnsgh', blk, w_cmp1) + b_cmp1[:, None, :])
    return jnp.einsum('bnsgh,shd->bnsgd', hid, w_cmp2)


def _selection_blocks(kv_tok):
    B, T = kv_tok.shape[:2]
    nsb = -(-T // SLC_L)
    kv = jnp.pad(kv_tok, ((0, 0), (0, nsb * SLC_L - T), (0, 0), (0, 0), (0, 0)))
    return jnp.transpose(kv.reshape(B, nsb, SLC_L, 2, NSA_KV, HD), (0, 4, 1, 2, 3, 5))


def _nsa_branches(q, gates, qpos, ckv, skv, wkv, wpos, slopes):
    B, Q = q.shape[:2]
    f32 = jnp.float32
    qg = q.reshape(B, Q, NSA_KV, NSA_HPG, HD)
    sl = slopes.reshape(NSA_KV, NSA_HPG)
    nc = ckv.shape[1]
    cpos = jnp.arange(nc) * CMP_S + (CMP_L - 1)
    cd = (qpos[:, None] - cpos[None, :]).astype(f32)
    s = jnp.einsum('bqghd,bngd->bghqn', qg, ckv[:, :, 0]).astype(f32) - sl[:, :, None, None] * cd
    p_cmp = _masked_softmax(s, cd >= 0)
    o_cmp = jnp.einsum('bghqn,bngd->bqghd', p_cmp.astype(q.dtype), ckv[:, :, 1])
    nsb = skv.shape[2]
    ratio = SLC_L // CMP_S
    imp = p_cmp.sum(axis=2)
    chunk = jnp.pad(imp, ((0, 0), (0, 0), (0, 0), (0, 1))) + jnp.pad(imp, ((0, 0), (0, 0), (0, 0), (1, 0)))
    chunk = jnp.pad(chunk, ((0, 0), (0, 0), (0, 0), (0, nsb * ratio - nc - 1)))
    blk = chunk.reshape(B, NSA_KV, Q, nsb, ratio).sum(-1)
    bidx = jnp.arange(nsb)
    cur = qpos // SLC_L
    valid = bidx[None, :] * SLC_L <= qpos[:, None]
    forced = (bidx[None, :] == 0) | (bidx[None, :] == cur[:, None]) | (bidx[None, :] == cur[:, None] - 1)
    score = jnp.where(valid, jnp.where(forced, 1e9, blk), -1e9)
    k_sel = min(SLC_K, nsb)
    _, sidx = lax.top_k(score, k_sel)
    selmask = (sidx[..., None] == bidx).any(axis=-2) & valid
    spos = bidx[:, None] * SLC_L + jnp.arange(SLC_L)
    sd = (qpos[:, None, None] - spos[None]).astype(f32)
    smask = (sd >= 0)[None, None] & selmask[..., None]
    s = jnp.einsum('bqghd,bgnld->bghqnl', qg, skv[..., 0, :]).astype(f32) - sl[None, :, :, None, None, None] * sd
    p = _masked_softmax(s.reshape(B, NSA_KV, NSA_HPG, Q, -1), smask[:, :, None].reshape(B, NSA_KV, 1, Q, -1))
    o_slc = jnp.einsum('bghqnl,bgnld->bqghd', p.reshape(s.shape).astype(q.dtype), skv[..., 1, :])
    wd = (qpos[:, None] - wpos[None, :]).astype(f32)
    wmask = (wd >= 0) & (wd <= WINDOW) & (wpos[None, :] >= 0)
    s = jnp.einsum('bqghd,bwgd->bghqw', qg, wkv[:, :, 0]).astype(f32) - sl[:, :, None, None] * wd
    p = _masked_softmax(s, wmask)
    o_win = jnp.einsum('bghqw,bwgd->bqghd', p.astype(q.dtype), wkv[:, :, 1])
    g = gates.reshape(B, Q, NSA_KV, NSA_HPG, 3, 1)
    o = g[..., 0, :] * o_cmp + g[..., 1, :] * o_slc + g[..., 2, :] * o_win
    return o.reshape(B, Q, NSA_W)


def _nsa_prompt(q, gates, ckv, skv, win_kv, slopes):
    B, T = q.shape[:2]
    wpad = jnp.pad(win_kv, ((0, 0), (WINDOW, 0), (0, 0), (0, 0), (0, 0)))

    def blk(i):
        q0 = i * Q_BLK
        qb = lax.dynamic_slice_in_dim(q, q0, Q_BLK, axis=1)
        gb = lax.dynamic_slice_in_dim(gates, q0, Q_BLK, axis=1)
        wb = lax.dynamic_slice_in_dim(wpad, q0, WINDOW + Q_BLK, axis=1)
        qpos = q0 + jnp.arange(Q_BLK)
        wpos = q0 - WINDOW + jnp.arange(WINDOW + Q_BLK)
        return _nsa_branches(qb, gb, qpos, ckv, skv, wb, wpos, slopes)

    out = lax.map(blk, jnp.arange(T // Q_BLK))
    return jnp.moveaxis(out, 0, 1).reshape(B, T, NSA_W)


def kernel(x_prompt, x_sample, mem_prompt, state_gla, cache_nsa_kv, state_win_kv, cache_mem_kv, page_table,
           w_in_a, w_gate_a, b_gate_a, gla_norm, w_in_b, b_gate_b, w_kv_b, cmp_pos, w_cmp1, b_cmp1, w_cmp2,
           w_mem_kv, w_out, ln1_g, ln1_b, ln2_g, ln2_b, w_router, b_router, w_e1, b_e1, w_e2, b_e2):
    f32, bf16 = jnp.float32, jnp.bfloat16
    bp, tp, d = x_prompt.shape
    bs, ts, _ = x_sample.shape
    n_p, n_s = bp * tp, bs * ts
    slopes = _alibi_slopes(NSA_HEADS)
    wb = min(WINDOW, state_win_kv.shape[1])
    past_len = page_table.shape[1] * PAGE_SIZE

    mem_rows = mem_prompt.reshape(bp * MEM_TOK, d)
    mem_kv_l = _proj(mem_rows, [w_mem_kv[l].astype(bf16) for l in range(DEPTH)], [f32] * DEPTH)
    mem_kv_prompt = jnp.stack(mem_kv_l).reshape(DEPTH, bp, MEM_TOK, 2, MEM_HEADS, MEM_HD)

    x = jnp.concatenate([x_prompt.reshape(n_p, d), x_sample.reshape(n_s, d)], axis=0)

    def split(a, width):
        return a[:n_p].reshape(bp, tp, width), a[n_p:].reshape(bs, ts, width)

    hk = GLA_HEADS * GLA_DK
    cuts_a = [0, hk, 2 * hk, 2 * hk + GLA_W, 2 * hk + 2 * GLA_W, 2 * hk + 2 * GLA_W + GLA_RANK,
              2 * hk + 2 * GLA_W + GLA_RANK + MEM_W]
    cuts_b = [0, NSA_W, NSA_W + 3 * NSA_HEADS, NSA_W + 3 * NSA_HEADS + MEM_W]

    gla_p, gla_s = [], []
    nsa_ctx = None
    for l in range(DEPTH):
        if l < N_A:
            ws = [w_in_a[l][:, cuts_a[i]:cuts_a[i + 1]].astype(bf16) for i in range(6)]
            q, k, v, r, glr, mq = _proj(x, ws, [f32] * 6)
            parts = []
            for grp, (s0, memkv) in enumerate(((jnp.zeros((bp,) + state_gla.shape[2:], f32), mem_kv_prompt[l]),
                                               (state_gla[l], cache_mem_kv[l]))):
                sel = lambda a, w: split(a, w)[grp]
                o, s_new = _gla_mix(sel(q, hk), sel(k, hk), sel(v, GLA_W), sel(r, GLA_W), sel(glr, GLA_RANK),
                                    s0, w_gate_a[l], b_gate_a[l], gla_norm[l])
                om = _memory_attention(sel(mq, MEM_W), memkv)
                parts.append(jnp.concatenate([o, om], axis=-1).reshape(-1, d))
                (gla_p if grp == 0 else gla_s).append(s_new)
            mix_in = jnp.concatenate(parts, axis=0)
        else:
            j = l - N_A
            if l == N_A:
                (kv,) = _proj(x, [w_kv_b.astype(bf16)], [f32])
                kv_p, kv_s = split(kv, 6 * NSA_KV * HD)
                kv_p = kv_p.reshape(bp, tp, 6, NSA_KV, HD)
                kv_s = kv_s.reshape(bs, ts, 6, NSA_KV, HD)
                rows_p, win_p = kv_p[:, :, :4], kv_p[:, :, 4:]
                rows_s, win_s = kv_s[:, :, :4], kv_s[:, :, 4:]
                ckv_p = _compress(rows_p[:, :, :2], cmp_pos, w_cmp1, b_cmp1, w_cmp2)
                skv_p = _selection_blocks(rows_p[:, :, 2:])
                past = cache_nsa_kv[page_table].reshape(bs, past_len, 4, NSA_KV, HD)
                hist = jnp.concatenate([past, rows_s], axis=1)
                ckv_s = _compress(hist[:, :, :2], cmp_pos, w_cmp1, b_cmp1, w_cmp2)
                skv_s = _selection_blocks(hist[:, :, 2:])
                wkv_s = jnp.concatenate([state_win_kv, win_s], axis=1)
                nsa_ctx = (ckv_p, skv_p, win_p, ckv_s, skv_s, wkv_s)
                nsa_out = (rows_p, _last_rows(win_p, wb), rows_s, wkv_s[:, -wb:])
            ckv_p, skv_p, win_p, ckv_s, skv_s, wkv_s = nsa_ctx
            ws = [w_in_b[j][:, cuts_b[i]:cuts_b[i + 1]].astype(bf16) for i in range(3)]
            q, gl, mq = _proj(x, ws, [f32] * 3)
            gates = jax.nn.sigmoid(gl + b_gate_b[j])
            q = q * (HD ** -0.5)
            q_p, q_s = split(q, NSA_W)
            g_p, g_s = split(gates, 3 * NSA_HEADS)
            mq_p, mq_s = split(mq, MEM_W)
            o_p = _nsa_prompt(q_p.reshape(bp, tp, NSA_HEADS, HD), g_p.reshape(bp, tp, NSA_HEADS, 3),
                              ckv_p, skv_p, win_p, slopes)
            wpos = past_len - wb + jnp.arange(wb + ts)
            qpos = past_len + jnp.arange(ts)
            o_s = _nsa_branches(q_s.reshape(bs, ts, NSA_HEADS, HD), g_s.reshape(bs, ts, NSA_HEADS, 3),
                                qpos, ckv_s, skv_s, wkv_s, wpos, slopes)
            om_p = _memory_attention(mq_p, mem_kv_prompt[l])
            om_s = _memory_attention(mq_s, cache_mem_kv[l])
            mix_in = jnp.concatenate([jnp.concatenate([o_p, om_p], axis=-1).reshape(n_p, d),
                                      jnp.concatenate([o_s, om_s], axis=-1).reshape(n_s, d)], axis=0)
        x1, logits = _mix_ln(mix_in, w_out[l].astype(bf16), x, ln1_g[l], ln1_b[l], w_router[l], b_router[l])
        y = _moe(x1, logits, w_e1[l], b_e1[l], w_e2[l], b_e2[l])
        x = _layer_norm(DN_ALPHA * x1 + y, ln2_g[l], ln2_b[l])

    rows_p, win_p_out, rows_s, win_s_out = nsa_out
    return (x[:n_p].reshape(bp, tp, d), x[n_p:].reshape(bs, ts, d),
            jnp.stack(gla_p), jnp.stack(gla_s), rows_p, rows_s, win_p_out, win_s_out, mem_kv_prompt)
```

```python
import functools
import math

import jax
import jax.numpy as jnp
from jax import lax
from jax.experimental import pallas as pl
from jax.experimental.pallas import tpu as pltpu

D_MODEL = 1024
DEPTH = 4
PAGE_SIZE = 128
N_A = DEPTH // 2
GLA_HEADS = 4
GLA_DV = (3 * D_MODEL) // (4 * GLA_HEADS)
GLA_DK = GLA_DV // 2
GLA_RANK = 16
GLA_TAU = 16.0
GLA_CHUNK = 64
HD = 64
NSA_HEADS = (3 * D_MODEL) // (4 * HD)
NSA_KV = 2
NSA_HPG = NSA_HEADS // NSA_KV
CMP_S = 16
CMP_L = 2 * CMP_S
SLC_L = 64
SLC_K = 16
WINDOW = 512
Q_BLK = 128
MEM_TOK = 256
MEM_HEADS = 4
MEM_HD = D_MODEL // (4 * MEM_HEADS)
N_EXP = 32
TOP_K = 4
D_FF = D_MODEL
SWIGLU_LIMIT = 7.0
SWIGLU_ALPHA = 1.702
GLA_W = GLA_HEADS * GLA_DV
NSA_W = NSA_HEADS * HD
MEM_W = MEM_HEADS * MEM_HD
DN_ALPHA = (2 * DEPTH) ** 0.25
LN_EPS = 1e-5
NEG = -1e30

f32, bf16, i32 = jnp.float32, jnp.bfloat16, jnp.int32

VMEM_LIMIT_BYTES = 56 * 1024 * 1024
LANES = 128
ROW_TILE = 512
MOE_ROWS = 256
FF_CHUNK = 512
SLC_TILE = 512
WIN_KEYS = WINDOW + Q_BLK
NSB_PAD = 64
_NT = (((1,), (1,)), ((), ()))


def _cparams(sem):
    return pltpu.CompilerParams(dimension_semantics=sem, vmem_limit_bytes=VMEM_LIMIT_BYTES)


def _alibi_slopes(n):
    def pow2(m):
        start = 2.0 ** (-8.0 / m)
        return [start ** (i + 1) for i in range(m)]
    if math.log2(n).is_integer():
        return pow2(n)
    c = 2 ** math.floor(math.log2(n))
    return pow2(c) + pow2(2 * c)[0::2][: n - c]


SLOPES = _alibi_slopes(NSA_HEADS)
SLC_SHIFT = int(math.log2(SLC_L))
RATIO_SHIFT = int(math.log2(SLC_L // CMP_S))


def _proj_kernel(x_ref, *refs, n_w, outs):
    xb = x_ref[...].astype(bf16)
    res = {}
    for o_ref, (wi, lo, hi) in zip(refs[n_w:], outs):
        if wi not in res:
            res[wi] = jnp.dot(xb, refs[wi][...], preferred_element_type=f32)
        o_ref[...] = res[wi][:, lo:hi].astype(o_ref.dtype)


def _proj(x, ws, outs, tm=ROW_TILE):
    m, k = x.shape
    tm = min(tm, m)
    in_specs = [pl.BlockSpec((tm, k), lambda i: (i, 0))]
    in_specs += [pl.BlockSpec(w.shape, lambda i: (0, 0)) for w in ws]
    out_specs = [pl.BlockSpec((tm, hi - lo), lambda i: (i, 0)) for _, lo, hi, _ in outs]
    out_shape = [jax.ShapeDtypeStruct((m, hi - lo), dt) for _, lo, hi, dt in outs]
    return pl.pallas_call(
        functools.partial(_proj_kernel, n_w=len(ws), outs=tuple(o[:3] for o in outs)),
        grid=(m // tm,),
        in_specs=in_specs,
        out_specs=out_specs,
        out_shape=out_shape,
        compiler_params=_cparams(("parallel",)),
        name="proj",
    )(x, *ws)


def _whole(ws, dtype=f32):
    return [(i, 0, w.shape[1], dtype) for i, w in enumerate(ws)]


def _layer_norm_rows(y, g, b):
    mu = jnp.mean(y, axis=-1, keepdims=True)
    d = y - mu
    var = jnp.mean(d * d, axis=-1, keepdims=True)
    return d * lax.rsqrt(var + LN_EPS) * g + b


def _mix_ln_kernel(*refs, n_parts):
    a_refs, w_refs = refs[:n_parts], refs[n_parts:2 * n_parts]
    x_ref, g_ref, b_ref, wr_ref, br_ref, x1_ref, lg_ref = refs[2 * n_parts:]
    mix = jnp.dot(a_refs[0][...].astype(bf16), w_refs[0][...], preferred_element_type=f32)
    for a_ref, w_ref in zip(a_refs[1:], w_refs[1:]):
        mix = mix + jnp.dot(a_ref[...].astype(bf16), w_ref[...], preferred_element_type=f32)
    x1 = _layer_norm_rows(DN_ALPHA * x_ref[...] + mix, g_ref[...], b_ref[...])
    x1_ref[...] = x1
    lg_ref[...] = jnp.dot(x1, wr_ref[...], preferred_element_type=f32,
                          precision=lax.Precision.HIGHEST) + br_ref[...]


def _mix_ln(parts, x, g, b, w_r, b_r, tm=ROW_TILE):
    m, d = x.shape
    row = lambda i: (i, 0)
    fix = lambda i: (0, 0)
    in_specs = [pl.BlockSpec((tm, a.shape[1]), row) for a, _ in parts]
    in_specs += [pl.BlockSpec(w.shape, fix) for _, w in parts]
    in_specs += [pl.BlockSpec((tm, d), row), pl.BlockSpec((1, d), fix), pl.BlockSpec((1, d), fix),
                 pl.BlockSpec((d, N_EXP), fix), pl.BlockSpec((1, N_EXP), fix)]
    return pl.pallas_call(
        functools.partial(_mix_ln_kernel, n_parts=len(parts)),
        grid=(m // tm,),
        in_specs=in_specs,
        out_specs=[pl.BlockSpec((tm, d), row), pl.BlockSpec((tm, N_EXP), row)],
        out_shape=[jax.ShapeDtypeStruct((m, d), f32), jax.ShapeDtypeStruct((m, N_EXP), f32)],
        compiler_params=_cparams(("parallel",)),
        name="mix_ln",
    )(*[a for a, _ in parts], *[w for _, w in parts], x, g.reshape(1, d), b.reshape(1, d), w_r,
      b_r.reshape(1, N_EXP))


def _moe_kernel(be_ref, nreal_ref, x_ref, w1_ref, b1_ref, w2_ref, b2_ref, o_ref, w1s, w2s):
    i = pl.program_id(0)
    real = i < nreal_ref[0]
    prev = be_ref[jnp.maximum(i - 1, 0)]
    fresh = jnp.logical_or(i == 0, be_ref[i] != prev)

    @pl.when(jnp.logical_and(real, fresh))
    def _():
        w1s[...] = w1_ref[0].astype(bf16)
        w2s[...] = w2_ref[0].astype(bf16)

    @pl.when(real)
    def _():
        x = x_ref[...]
        acc = jnp.zeros((MOE_ROWS, D_MODEL), f32)
        for c in range(D_FF // FF_CHUNK):
            lo = c * FF_CHUNK
            hg = jnp.dot(x, w1s[:, lo:lo + FF_CHUNK], preferred_element_type=f32)
            hg = hg + b1_ref[0, :, lo:lo + FF_CHUNK]
            hu = jnp.dot(x, w1s[:, D_FF + lo:D_FF + lo + FF_CHUNK], preferred_element_type=f32)
            hu = hu + b1_ref[0, :, D_FF + lo:D_FF + lo + FF_CHUNK]
            g = jnp.minimum(hg, SWIGLU_LIMIT)
            u = jnp.clip(hu, -SWIGLU_LIMIT, SWIGLU_LIMIT)
            a = (u + 1.0) * g * jax.nn.sigmoid(SWIGLU_ALPHA * g)
            acc = acc + jnp.dot(a.astype(bf16), w2s[lo:lo + FF_CHUNK, :], preferred_element_type=f32)
        o_ref[...] = acc + b2_ref[0]

    @pl.when(jnp.logical_not(real))
    def _():
        o_ref[...] = jnp.zeros_like(o_ref)


def _moe_ffn_blocks(xb, blk_exp, nreal, w1, b1, w2, b2):
    p, d = xb.shape
    nblk = p // MOE_ROWS
    grid_spec = pltpu.PrefetchScalarGridSpec(
        num_scalar_prefetch=2,
        grid=(nblk,),
        in_specs=[
            pl.BlockSpec((MOE_ROWS, d), lambda i, be, nr: (i, 0)),
            pl.BlockSpec((1, d, 2 * D_FF), lambda i, be, nr: (be[i], 0, 0)),
            pl.BlockSpec((1, 1, 2 * D_FF), lambda i, be, nr: (be[i], 0, 0)),
            pl.BlockSpec((1, D_FF, d), lambda i, be, nr: (be[i], 0, 0)),
            pl.BlockSpec((1, 1, d), lambda i, be, nr: (be[i], 0, 0)),
        ],
        out_specs=pl.BlockSpec((MOE_ROWS, d), lambda i, be, nr: (i, 0)),
        scratch_shapes=[pltpu.VMEM((d, 2 * D_FF), bf16), pltpu.VMEM((D_FF, d), bf16)],
    )
    return pl.pallas_call(
        _moe_kernel,
        grid_spec=grid_spec,
        out_shape=jax.ShapeDtypeStruct((p, d), f32),
        compiler_params=_cparams(("arbitrary",)),
        name="moe_ffn",
    )(blk_exp, nreal, xb, w1, b1.reshape(N_EXP, 1, 2 * D_FF), w2, b2.reshape(N_EXP, 1, d))


def _moe(x1, logits, w1, b1, w2, b2):
    n, d = x1.shape
    top_v, top_i = lax.top_k(logits, TOP_K)
    gate = jax.nn.softmax(top_v, axis=-1)
    a = n * TOP_K
    flat_e = top_i.reshape(a)
    onehot = (flat_e[:, None] == jnp.arange(N_EXP, dtype=i32)[None, :]).astype(i32)
    csum = jnp.cumsum(onehot, axis=0)
    counts = csum[-1]
    rank = jnp.take_along_axis(csum, flat_e[:, None], axis=1)[:, 0] - 1
    padded = (counts + MOE_ROWS - 1) // MOE_ROWS * MOE_ROWS
    pend = jnp.cumsum(padded)
    pstart = pend - padded
    dest = pstart[flat_e] + rank
    nblk = -(-a // MOE_ROWS) + N_EXP
    p = nblk * MOE_ROWS
    slot_tok = jnp.full((p,), n, i32).at[dest].set(jnp.arange(a, dtype=i32) // TOP_K)
    blk_start = jnp.arange(nblk, dtype=i32) * MOE_ROWS
    nreal = (pend[-1] // MOE_ROWS).astype(i32)
    blk_exp = jnp.searchsorted(pend, blk_start, side='right').astype(i32)
    last_exp = blk_exp[jnp.maximum(nreal - 1, 0)]
    blk_exp = jnp.where(blk_start < pend[-1], jnp.minimum(blk_exp, N_EXP - 1), last_exp)
    xpad = jnp.concatenate([x1.astype(bf16), jnp.zeros((1, d), bf16)], axis=0)
    xb = xpad[slot_tok]
    yb = _moe_ffn_blocks(xb, blk_exp, nreal.reshape(1), w1, b1, w2, b2)
    contrib = yb[dest].reshape(n, TOP_K, d) * gate[:, :, None]
    return contrib.sum(axis=1)


def _mem_attn_kernel(q_ref, kv_ref, o_ref):
    q = q_ref[0]
    kv = kv_ref[0].astype(bf16)
    tq = q.shape[0]
    lane = lax.broadcasted_iota(i32, (tq, LANES), 1)
    zero = jnp.zeros((tq, LANES), bf16)
    for pair in range(MEM_HEADS // 2):
        qt = q[:, LANES * pair:LANES * (pair + 1)]
        kt = kv[:, LANES * pair:LANES * (pair + 1)]
        vt = kv[:, MEM_W + LANES * pair:MEM_W + LANES * (pair + 1)]
        halves = []
        for h in range(2):
            keep = (lane < MEM_HD) if h == 0 else (lane >= MEM_HD)
            s = lax.dot_general(jnp.where(keep, qt, zero), kt, _NT, preferred_element_type=f32)
            m = jnp.max(s, axis=-1, keepdims=True)
            e = jnp.exp(s - m)
            p = e / jnp.sum(e, axis=-1, keepdims=True)
            halves.append(jnp.dot(p.astype(bf16), vt, preferred_element_type=f32))
        o_ref[0, :, LANES * pair:LANES * (pair + 1)] = jnp.where(lane < MEM_HD, halves[0], halves[1])


def _mem_attn(mq, mem_kv):
    b, t, w = mq.shape
    tq = min(t, ROW_TILE)
    return pl.pallas_call(
        _mem_attn_kernel,
        grid=(b, t // tq),
        in_specs=[pl.BlockSpec((1, tq, w), lambda bb, i: (bb, i, 0)),
                  pl.BlockSpec((1, MEM_TOK, 2 * w), lambda bb, i: (bb, 0, 0))],
        out_specs=pl.BlockSpec((1, tq, w), lambda bb, i: (bb, i, 0)),
        out_shape=jax.ShapeDtypeStruct((b, t, w), f32),
        compiler_params=_cparams(("parallel", "parallel")),
        name="mem_attn",
    )(mq, mem_kv)


def _gla_scan(q, k, v, log_a, s0):
    B, T, H, _ = q.shape
    C = GLA_CHUNK if T % GLA_CHUNK == 0 else T
    n = T // C

    def to_chunks(a):
        return jnp.moveaxis(a.astype(f32).reshape(B, n, C, *a.shape[2:]), 1, 0)

    tri = jnp.tril(jnp.ones((C, C), bool))

    def step(S, inp):
        qc, kc, vc, gc = inp
        b = jnp.cumsum(gc, axis=1)
        bl = b[:, -1]
        qt = qc * jnp.exp(b)
        kt = kc * jnp.exp(-b)
        att = jnp.where(tri, jnp.einsum('bchk,bshk->bhcs', qt, kt), 0.0)
        o = jnp.einsum('bchk,bhkv->bchv', qt, S) + jnp.einsum('bhcs,bshv->bchv', att, vc)
        kd = kc * jnp.exp(bl[:, None] - b)
        S = jnp.exp(bl)[..., None] * S + jnp.einsum('bshk,bshv->bhkv', kd, vc)
        return S, o

    sT, o = lax.scan(step, s0.astype(f32), (to_chunks(q), to_chunks(k), to_chunks(v), to_chunks(log_a)))
    return jnp.moveaxis(o, 0, 1).reshape(B, T, H, -1), sT


def _gla_mix(q, k, v, r, glr, s0, w_gate, b_gate, gla_norm):
    B, T, _ = q.shape
    q = q.reshape(B, T, GLA_HEADS, GLA_DK) * (GLA_DK ** -0.5)
    k = k.reshape(B, T, GLA_HEADS, GLA_DK)
    v = v.reshape(B, T, GLA_HEADS, GLA_DV)
    log_a = jax.nn.log_sigmoid((glr @ w_gate + b_gate).astype(f32)) / GLA_TAU
    o, s_new = _gla_scan(q, k, v, log_a.reshape(B, T, GLA_HEADS, GLA_DK), s0)
    o = o * lax.rsqrt(jnp.mean(o * o, axis=-1, keepdims=True) + LN_EPS) * gla_norm.reshape(GLA_HEADS, GLA_DV)
    return o.reshape(B, T, GLA_W) * jax.nn.silu(r), s_new


def _softmax_rows(sm, maskf):
    m = jnp.max(sm, axis=-1, keepdims=True)
    e = jnp.exp(sm - m)
    return e / jnp.sum(e, axis=-1, keepdims=True) * maskf


def _block_scores_t(imp):
    nc = imp.shape[1]
    jj = lax.broadcasted_iota(i32, (NSB_PAD, nc), 0)
    nn = lax.broadcasted_iota(i32, (NSB_PAD, nc), 1)
    mt = ((nn >> RATIO_SHIFT) == jj).astype(f32) + (((nn + 1) >> RATIO_SHIFT) == jj).astype(f32)
    return lax.dot_general(mt, imp, _NT, preferred_element_type=f32, precision=lax.Precision.HIGHEST)


def _select_t(blk_t, qpos_t):
    nq = blk_t.shape[1]
    j_t = lax.broadcasted_iota(i32, (NSB_PAD, nq), 0)
    valid = j_t * SLC_L <= qpos_t
    cur = qpos_t >> SLC_SHIFT
    forced = (j_t == 0) | (j_t == cur) | (j_t == cur - 1)
    score = jnp.where(valid, jnp.where(forced, 1e9, blk_t), -1e9)
    cnt = jnp.zeros((NSB_PAD, nq), i32)
    for k in range(NSB_PAD):
        row = score[k:k + 1, :]
        beats = (row > score) | ((row == score) & (j_t > k))
        cnt = cnt + beats.astype(i32)
    return ((cnt < SLC_K) & valid).astype(f32)


def _untranspose(sel_t):
    nq = sel_t.shape[1]
    eye = lax.broadcasted_iota(i32, (nq, nq), 0) == lax.broadcasted_iota(i32, (nq, nq), 1)
    return lax.dot_general(eye.astype(bf16), sel_t.astype(bf16), _NT, preferred_element_type=f32).astype(bf16)


def _nsa_prompt_kernel(qp_ref, gl_ref, bg_ref, kc_ref, vc_ref, ks_ref, vs_ref, kw_ref, vw_ref, o_ref,
                       p_sc, m_sc, l_sc, acc_sc, og_sc):
    nc = kc_ref.shape[1]
    q0 = pl.program_id(1) * Q_BLK
    qpos_i = q0 + lax.broadcasted_iota(i32, (Q_BLK, 1), 0)
    lane = lax.broadcasted_iota(i32, (Q_BLK, LANES), 1)
    gates = jax.nn.sigmoid(gl_ref[0] + bg_ref[...])
    qp = qp_ref[0]
    zero_b = jnp.zeros((Q_BLK, LANES), bf16)

    for g in range(NSA_KV):
        half = (lane < HD) if g == 0 else (lane >= HD)
        qg = jnp.concatenate([jnp.where(half, qp[:, LANES * hh:LANES * (hh + 1)], zero_b)
                              for hh in range(NSA_HPG)], axis=0)
        slopes = [SLOPES[NSA_HPG * g + hh] for hh in range(NSA_HPG)]

        n_idx = lax.broadcasted_iota(i32, (Q_BLK, nc), 1)
        cd = (qpos_i - (n_idx * CMP_S + (CMP_L - 1))).astype(f32)
        cmask = cd >= 0
        cmaskf = cmask.astype(f32)
        s = lax.dot_general(qg, kc_ref[0], _NT, preferred_element_type=f32)
        imp = jnp.zeros((Q_BLK, nc), f32)
        for hh in range(NSA_HPG):
            rows = slice(hh * Q_BLK, (hh + 1) * Q_BLK)
            p = _softmax_rows(jnp.where(cmask, s[rows] - slopes[hh] * cd, NEG), cmaskf)
            imp = imp + p
            p_sc[rows, :nc] = p.astype(bf16)
        o_cmp = jnp.dot(p_sc[:, :nc], vc_ref[0], preferred_element_type=f32)

        qpos_t = q0 + lax.broadcasted_iota(i32, (NSB_PAD, Q_BLK), 1)
        sel_b = _untranspose(_select_t(_block_scores_t(imp), qpos_t))

        m_sc[...] = jnp.full(m_sc.shape, NEG, f32)
        l_sc[...] = jnp.zeros(l_sc.shape, f32)
        acc_sc[...] = jnp.zeros(acc_sc.shape, f32)

        def tile(t, carry):
            k0 = pl.multiple_of(t * SLC_TILE, SLC_TILE)
            kt = ks_ref[0, pl.ds(k0, SLC_TILE), :]
            vt = vs_ref[0, pl.ds(k0, SLC_TILE), :]
            st = lax.dot_general(qg, kt, _NT, preferred_element_type=f32)
            kpos = k0 + lax.broadcasted_iota(i32, (Q_BLK, SLC_TILE), 1)
            sd = (qpos_i - kpos).astype(f32)
            blk_of_key = (k0 + lax.broadcasted_iota(i32, (NSB_PAD, SLC_TILE), 1)) >> SLC_SHIFT
            expand = (lax.broadcasted_iota(i32, (NSB_PAD, SLC_TILE), 0) == blk_of_key).astype(bf16)
            selexp = jnp.dot(sel_b, expand, preferred_element_type=f32)
            mask = (sd >= 0) & (selexp > 0.5)
            for hh in range(NSA_HPG):
                rows = slice(hh * Q_BLK, (hh + 1) * Q_BLK)
                sm = jnp.where(mask, st[rows] - slopes[hh] * sd, NEG)
                m_old = m_sc[rows]
                m_new = jnp.maximum(m_old, jnp.max(sm, axis=-1, keepdims=True))
                a = jnp.exp(m_old - m_new)
                e = jnp.where(mask, jnp.exp(sm - m_new), 0.0)
                l_sc[rows] = a * l_sc[rows] + jnp.sum(e, axis=-1, keepdims=True)
                m_sc[rows] = m_new
                acc_sc[rows] = a * acc_sc[rows]
                p_sc[rows, :SLC_TILE] = e.astype(bf16)
            acc_sc[...] += jnp.dot(p_sc[:, :SLC_TILE], vt, preferred_element_type=f32)
            return carry

        lax.fori_loop(0, (q0 + Q_BLK + SLC_TILE - 1) // SLC_TILE, tile, 0)
        l = l_sc[...]
        o_slc = jnp.where(l > 0, acc_sc[...] / jnp.where(l > 0, l, 1.0), 0.0)

        start = pl.multiple_of(jnp.maximum(q0 - WINDOW, 0), Q_BLK)
        kt = kw_ref[0, pl.ds(start, WIN_KEYS), :]
        vt = vw_ref[0, pl.ds(start, WIN_KEYS), :]
        st = lax.dot_general(qg, kt, _NT, preferred_element_type=f32)
        wd = (qpos_i - (start + lax.broadcasted_iota(i32, (Q_BLK, WIN_KEYS), 1))).astype(f32)
        wmask = (wd >= 0) & (wd <= WINDOW)
        wmaskf = wmask.astype(f32)
        for hh in range(NSA_HPG):
            rows = slice(hh * Q_BLK, (hh + 1) * Q_BLK)
            p = _softmax_rows(jnp.where(wmask, st[rows] - slopes[hh] * wd, NEG), wmaskf)
            p_sc[rows, :WIN_KEYS] = p.astype(bf16)
        o_win = jnp.dot(p_sc[:, :WIN_KEYS], vt, preferred_element_type=f32)

        for hh in range(NSA_HPG):
            rows = slice(hh * Q_BLK, (hh + 1) * Q_BLK)
            c = 3 * (NSA_HPG * g + hh)
            og_sc[g, rows] = (gates[:, c:c + 1] * o_cmp[rows] + gates[:, c + 1:c + 2] * o_slc[rows]
                              + gates[:, c + 2:c + 3] * o_win[rows])

    for hh in range(NSA_HPG):
        rows = slice(hh * Q_BLK, (hh + 1) * Q_BLK)
        o_ref[0, :, LANES * hh:LANES * (hh + 1)] = jnp.where(lane < HD, og_sc[0, rows], og_sc[1, rows])


def _nsa_prompt_call(qp, gl, bg, kc, vc, ks, vs, kw, vw):
    b, t, w = qp.shape
    nc = kc.shape[1]
    assert t % SLC_TILE == 0 and t >= WIN_KEYS and nc % LANES == 0 and t // SLC_L <= NSB_PAD
    rows = NSA_HPG * Q_BLK
    qblk = lambda bb, i: (bb, i, 0)
    whole = lambda bb, i: (bb, 0, 0)
    return pl.pallas_call(
        _nsa_prompt_kernel,
        grid=(b, t // Q_BLK),
        in_specs=[pl.BlockSpec((1, Q_BLK, w), qblk), pl.BlockSpec((1, Q_BLK, LANES), qblk),
                  pl.BlockSpec((1, LANES), lambda bb, i: (0, 0)),
                  pl.BlockSpec((1, nc, LANES), whole), pl.BlockSpec((1, nc, LANES), whole),
                  pl.BlockSpec((1, t, LANES), whole), pl.BlockSpec((1, t, LANES), whole),
                  pl.BlockSpec((1, t, LANES), whole), pl.BlockSpec((1, t, LANES), whole)],
        out_specs=pl.BlockSpec((1, Q_BLK, w), qblk),
        out_shape=jax.ShapeDtypeStruct((b, t, w), f32),
        scratch_shapes=[pltpu.VMEM((rows, WIN_KEYS), bf16), pltpu.VMEM((rows, 1), f32), pltpu.VMEM((rows, 1), f32),
                        pltpu.VMEM((rows, LANES), f32), pltpu.VMEM((NSA_KV, rows, LANES), f32)],
        compiler_params=_cparams(("parallel", "parallel")),
        name="nsa_prompt",
    )(qp, gl, bg, kc, vc, ks, vs, kw, vw)


def _nsa_sample_kernel(pt_ref, qp_ref, gl_ref, bg_ref, kc_ref, vc_ref, kn_ref, vn_ref, wst_ref, wkn_ref, wvn_ref,
                       *rest, n_pages, past_len):
    page_refs = rest[:n_pages]
    o_ref, k_sc, v_sc = rest[n_pages:]
    nq = qp_ref.shape[1]
    nc = kc_ref.shape[1]
    nkeys = (n_pages + 1) * PAGE_SIZE
    rows_g = NSA_HPG * nq
    qpos_i = past_len + lax.broadcasted_iota(i32, (nq, 1), 0)
    lane = lax.broadcasted_iota(i32, (nq, LANES), 1)
    gates = jax.nn.sigmoid(gl_ref[0] + bg_ref[...])
    qp = qp_ref[0]
    zero_b = jnp.zeros((nq, LANES), bf16)
    qs = jnp.concatenate([jnp.where((lane < HD) if g == 0 else (lane >= HD),
                                    qp[:, LANES * hh:LANES * (hh + 1)], zero_b)
                          for g in range(NSA_KV) for hh in range(NSA_HPG)], axis=0)

    def rows_of(g, hh):
        r0 = g * rows_g + hh * nq
        return slice(r0, r0 + nq)

    n_idx = lax.broadcasted_iota(i32, (nq, nc), 1)
    cd = (qpos_i - (n_idx * CMP_S + (CMP_L - 1))).astype(f32)
    cmask = cd >= 0
    cmaskf = cmask.astype(f32)
    s = lax.dot_general(qs, kc_ref[0], _NT, preferred_element_type=f32)
    imps, ps = [], []
    for g in range(NSA_KV):
        imp = jnp.zeros((nq, nc), f32)
        for hh in range(NSA_HPG):
            p = _softmax_rows(jnp.where(cmask, s[rows_of(g, hh)] - SLOPES[NSA_HPG * g + hh] * cd, NEG), cmaskf)
            imp = imp + p
            ps.append(p)
        imps.append(imp)
    o_cmp = jnp.dot(jnp.concatenate(ps, axis=0).astype(bf16), vc_ref[0], preferred_element_type=f32)

    qpos_t = past_len + lax.broadcasted_iota(i32, (NSB_PAD, nq), 1)
    blk_of_key = lax.broadcasted_iota(i32, (NSB_PAD, nkeys), 1) >> SLC_SHIFT
    expand = (lax.broadcasted_iota(i32, (NSB_PAD, nkeys), 0) == blk_of_key).astype(bf16)
    sd = (qpos_i - lax.broadcasted_iota(i32, (nq, nkeys), 1)).astype(f32)
    masks = []
    for g in range(NSA_KV):
        sel_b = _untranspose(_select_t(_block_scores_t(imps[g]), qpos_t))
        selexp = jnp.dot(sel_b, expand, preferred_element_type=f32)
        masks.append((sd >= 0) & (selexp > 0.5))

    for pg in range(n_pages):
        blk = page_refs[pg][0]
        k_sc[pg * PAGE_SIZE:(pg + 1) * PAGE_SIZE, :] = blk[:, :LANES].astype(bf16)
        v_sc[pg * PAGE_SIZE:(pg + 1) * PAGE_SIZE, :] = blk[:, LANES:].astype(bf16)
    tail = jnp.zeros((PAGE_SIZE - nq, LANES), bf16)
    k_sc[n_pages * PAGE_SIZE:, :] = jnp.concatenate([kn_ref[0], tail], axis=0)
    v_sc[n_pages * PAGE_SIZE:, :] = jnp.concatenate([vn_ref[0], tail], axis=0)
    s = lax.dot_general(qs, k_sc[...], _NT, preferred_element_type=f32)
    ps = []
    for g in range(NSA_KV):
        mf = masks[g].astype(f32)
        for hh in range(NSA_HPG):
            ps.append(_softmax_rows(jnp.where(masks[g], s[rows_of(g, hh)] - SLOPES[NSA_HPG * g + hh] * sd, NEG), mf))
    o_slc = jnp.dot(jnp.concatenate(ps, axis=0).astype(bf16), v_sc[...], preferred_element_type=f32)

    wb = wst_ref.shape[1]
    wk = jnp.concatenate([wst_ref[0, :, :LANES].astype(bf16), wkn_ref[0], tail], axis=0)
    wv = jnp.concatenate([wst_ref[0, :, LANES:].astype(bf16), wvn_ref[0], tail], axis=0)
    wpos = (past_len - wb) + lax.broadcasted_iota(i32, (nq, wb + PAGE_SIZE), 1)
    wd = (qpos_i - wpos).astype(f32)
    wmask = (wd >= 0) & (wd <= WINDOW) & (wpos >= 0)
    wmaskf = wmask.astype(f32)
    s = lax.dot_general(qs, wk, _NT, preferred_element_type=f32)
    ps = []
    for g in range(NSA_KV):
        for hh in range(NSA_HPG):
            ps.append(_softmax_rows(jnp.where(wmask, s[rows_of(g, hh)] - SLOPES[NSA_HPG * g + hh] * wd, NEG), wmaskf))
    o_win = jnp.dot(jnp.concatenate(ps, axis=0).astype(bf16), wv, preferred_element_type=f32)

    for hh in range(NSA_HPG):
        halves = []
        for g in range(NSA_KV):
            r = rows_of(g, hh)
            c = 3 * (NSA_HPG * g + hh)
            halves.append(gates[:, c:c + 1] * o_cmp[r] + gates[:, c + 1:c + 2] * o_slc[r]
                          + gates[:, c + 2:c + 3] * o_win[r])
        o_ref[0, :, LANES * hh:LANES * (hh + 1)] = jnp.where(lane < HD, halves[0], halves[1])


def _nsa_sample_call(page_table, qp, gl, bg, kc, vc, kn, vn, wst, wkn, wvn, cache):
    b, nq, w = qp.shape
    n_pages = page_table.shape[1]
    nc = kc.shape[1]
    wb = wst.shape[1]
    nkeys = (n_pages + 1) * PAGE_SIZE
    assert nkeys // SLC_L <= NSB_PAD and nq <= PAGE_SIZE
    per_b = lambda bb, pt: (bb, 0, 0)
    in_specs = [pl.BlockSpec((1, nq, w), per_b), pl.BlockSpec((1, nq, LANES), per_b),
                pl.BlockSpec((1, LANES), lambda bb, pt: (0, 0)),
                pl.BlockSpec((1, nc, LANES), per_b), pl.BlockSpec((1, nc, LANES), per_b),
                pl.BlockSpec((1, nq, LANES), per_b), pl.BlockSpec((1, nq, LANES), per_b),
                pl.BlockSpec((1, wb, 2 * LANES), per_b),
                pl.BlockSpec((1, nq, LANES), per_b), pl.BlockSpec((1, nq, LANES), per_b)]
    for pg in range(n_pages):
        in_specs.append(pl.BlockSpec((1, PAGE_SIZE, 2 * LANES),
                                     functools.partial(lambda bb, pt, pg: (pt[bb, pg], 0, 1), pg=pg)))
    grid_spec = pltpu.PrefetchScalarGridSpec(
        num_scalar_prefetch=1, grid=(b,), in_specs=in_specs,
        out_specs=pl.BlockSpec((1, nq, w), per_b),
        scratch_shapes=[pltpu.VMEM((nkeys, LANES), bf16), pltpu.VMEM((nkeys, LANES), bf16)])
    return pl.pallas_call(
        functools.partial(_nsa_sample_kernel, n_pages=n_pages, past_len=n_pages * PAGE_SIZE),
        grid_spec=grid_spec,
        out_shape=jax.ShapeDtypeStruct((b, nq, w), f32),
        compiler_params=_cparams(("parallel",)),
        name="nsa_sample",
    )(page_table, qp, gl, bg, kc, vc, kn, vn, wst, wkn, wvn, *([cache] * n_pages))


def _compress_kernel(x_ref, pos_ref, w1_ref, b1_ref, w2_ref, o_ref):
    x = x_ref[0, 0, 0]
    half = CMP_S * HD
    xa = (x + pos_ref[0, :, :half]).astype(bf16)
    xb = (x + pos_ref[0, :, half:]).astype(bf16)
    a = jnp.dot(xa, w1_ref[0, :half, :], preferred_element_type=f32)
    bsec = jnp.dot(xb, w1_ref[0, half:, :], preferred_element_type=f32)
    nch = x.shape[0]
    hid = jax.nn.gelu(a + pltpu.roll(bsec, nch - 1, 0) + b1_ref[0])
    o_ref[0, 0, 0] = jnp.dot(hid.astype(bf16), w2_ref[0], preferred_element_type=f32).astype(o_ref.dtype)


def _compress_call(x2, cmp_pos, w_cmp1, b_cmp1, w_cmp2):
    b, _, g, nch, cw = x2.shape
    hid = w_cmp1.shape[-1]
    pos = cmp_pos.reshape(2, 1, CMP_L * HD)
    return pl.pallas_call(
        _compress_kernel,
        grid=(b, 2, g),
        in_specs=[pl.BlockSpec((1, 1, 1, nch, cw), lambda bb, s, gg: (bb, s, gg, 0, 0)),
                  pl.BlockSpec((1, 1, CMP_L * HD), lambda bb, s, gg: (s, 0, 0)),
                  pl.BlockSpec((1, CMP_L * HD, hid), lambda bb, s, gg: (s, 0, 0)),
                  pl.BlockSpec((1, 1, hid), lambda bb, s, gg: (s, 0, 0)),
                  pl.BlockSpec((1, hid, HD), lambda bb, s, gg: (s, 0, 0))],
        out_specs=pl.BlockSpec((1, 1, 1, nch, HD), lambda bb, s, gg: (bb, s, gg, 0, 0)),
        out_shape=jax.ShapeDtypeStruct((b, 2, g, nch, HD), bf16),
        compiler_params=_cparams(("parallel", "parallel", "parallel")),
        name="compress",
    )(x2, pos, w_cmp1.astype(bf16), b_cmp1.reshape(2, 1, hid), w_cmp2.astype(bf16))


def _compressed_kv(tok, cmp_pos, w_cmp1, b_cmp1, w_cmp2):
    b, t = tok.shape[:2]
    nch = t // CMP_S
    x2 = jnp.transpose(tok[:, :nch * CMP_S], (0, 2, 3, 1, 4)).reshape(b, 2, NSA_KV, nch, CMP_S * HD)
    ck = _compress_call(x2, cmp_pos, w_cmp1, b_cmp1, w_cmp2)
    ck = jnp.transpose(ck, (0, 1, 3, 2, 4)).reshape(b, 2, nch, NSA_KV * HD)
    return ck[:, 0], ck[:, 1]


def _pair_pack_cols(w):
    k = w.shape[0]
    return jnp.transpose(w.reshape(k, NSA_KV, NSA_HPG, HD), (0, 2, 1, 3)).reshape(k, NSA_W)


def _last_rows(a, n):
    t = a.shape[1]
    if t >= n:
        return a[:, t - n:]
    return jnp.pad(a, ((0, 0), (n - t, 0)) + ((0, 0),) * (a.ndim - 2))


def _layer_norm(x, g, b):
    mu = jnp.mean(x, axis=-1, keepdims=True)
    var = jnp.mean(jnp.square(x - mu), axis=-1, keepdims=True)
    return (x - mu) * lax.rsqrt(var + LN_EPS) * g + b


def kernel(x_prompt, x_sample, mem_prompt, state_gla, cache_nsa_kv, state_win_kv, cache_mem_kv, page_table,
           w_in_a, w_gate_a, b_gate_a, gla_norm, w_in_b, b_gate_b, w_kv_b, cmp_pos, w_cmp1, b_cmp1, w_cmp2,
           w_mem_kv, w_out, ln1_g, ln1_b, ln2_g, ln2_b, w_router, b_router, w_e1, b_e1, w_e2, b_e2):
    bp, tp, d = x_prompt.shape
    bs, ts, _ = x_sample.shape
    n_p, n_s = bp * tp, bs * ts
    wb = min(WINDOW, state_win_kv.shape[1])
    n_pool = cache_nsa_kv.shape[0]
    past_len = page_table.shape[1] * PAGE_SIZE

    mem_rows = mem_prompt.reshape(bp * MEM_TOK, d)
    w_mem = [w_mem_kv[l].astype(bf16) for l in range(DEPTH)]
    mem_kv_l = _proj(mem_rows, w_mem, _whole(w_mem))
    mem_kv_prompt = jnp.stack(mem_kv_l).reshape(DEPTH, bp, MEM_TOK, 2, MEM_HEADS, MEM_HD)
    mem_p = [m.reshape(bp, MEM_TOK, 2 * MEM_W) for m in mem_kv_l]
    mem_s = cache_mem_kv.reshape(DEPTH, bs, MEM_TOK, 2 * MEM_W)

    x = jnp.concatenate([x_prompt.reshape(n_p, d), x_sample.reshape(n_s, d)], axis=0)

    def split(a):
        w = a.shape[-1]
        return a[:n_p].reshape(bp, tp, w), a[n_p:].reshape(bs, ts, w)

    def mem_attention(mq, l):
        mq_p, mq_s = split(mq)
        om_p = _mem_attn(mq_p, mem_p[l])
        om_s = _mem_attn(mq_s, mem_s[l])
        return jnp.concatenate([om_p.reshape(n_p, MEM_W), om_s.reshape(n_s, MEM_W)], axis=0)

    hk = GLA_HEADS * GLA_DK
    cuts_a = [0, hk, 2 * hk, 2 * hk + GLA_W, 2 * hk + 2 * GLA_W, 2 * hk + 2 * GLA_W + GLA_RANK,
              2 * hk + 2 * GLA_W + GLA_RANK + MEM_W]
    mem_scale = MEM_HD ** -0.5
    q_scale = HD ** -0.5

    gla_p, gla_s = [], []
    for l in range(DEPTH):
        if l < N_A:
            ws = [w_in_a[l][:, cuts_a[i]:cuts_a[i + 1]] for i in range(6)]
            ws[5] = ws[5] * mem_scale
            ws = [w.astype(bf16) for w in ws]
            outs = _whole(ws)
            outs[5] = outs[5][:3] + (bf16,)
            q, k, v, r, glr, mq = _proj(x, ws, outs)
            parts = []
            for grp, s0 in enumerate((jnp.zeros((bp,) + state_gla.shape[2:], f32), state_gla[l])):
                o, s_new = _gla_mix(split(q)[grp], split(k)[grp], split(v)[grp], split(r)[grp], split(glr)[grp],
                                    s0, w_gate_a[l], b_gate_a[l], gla_norm[l])
                parts.append(o.reshape(-1, GLA_W))
                (gla_p if grp == 0 else gla_s).append(s_new)
            o_mix = jnp.concatenate(parts, axis=0)
            w_o = w_out[l][:GLA_W]
        else:
            j = l - N_A
            if l == N_A:
                role = NSA_KV * HD
                kv_outs = [(0, 0, 4 * role, f32), (0, 4 * role, 6 * role, f32)]
                kv_outs += [(0, r * role, (r + 1) * role, bf16) for r in range(2, 6)]
                rows, win, ks, vs, kw, vw = _proj(x, [w_kv_b.astype(bf16)], kv_outs)
                rows_p, rows_s = split(rows)
                win_p, win_s = split(win)
                ks_p, ks_s = split(ks)
                vs_p, vs_s = split(vs)
                kw_p, kw_s = split(kw)
                vw_p, vw_s = split(vw)
                kc_p, vc_p = _compressed_kv(rows_p.reshape(bp, tp, 4, NSA_KV, HD)[:, :, :2],
                                            cmp_pos, w_cmp1, b_cmp1, w_cmp2)
                cache_rows = cache_nsa_kv.reshape(n_pool, PAGE_SIZE, 4 * role)
                past_cmp = cache_rows[page_table][..., :2 * role].reshape(bs, past_len, 2, NSA_KV, HD)
                hist_cmp = jnp.concatenate([past_cmp, rows_s.reshape(bs, ts, 4, NSA_KV, HD)[:, :, :2]], axis=1)
                kc_s, vc_s = _compressed_kv(hist_cmp, cmp_pos, w_cmp1, b_cmp1, w_cmp2)
                wst = state_win_kv.reshape(bs, state_win_kv.shape[1], 2 * role)
                wkv_s = jnp.concatenate([state_win_kv, win_s.reshape(bs, ts, 2, NSA_KV, HD)], axis=1)
                nsa_out = (rows_p.reshape(bp, tp, 4, NSA_KV, HD), _last_rows(win_p.reshape(bp, tp, 2, NSA_KV, HD), wb),
                           rows_s.reshape(bs, ts, 4, NSA_KV, HD), wkv_s[:, -wb:])
            wq = _pair_pack_cols(w_in_b[j][:, :NSA_W]) * q_scale
            wg = jnp.pad(w_in_b[j][:, NSA_W:NSA_W + 3 * NSA_HEADS], ((0, 0), (0, LANES - 3 * NSA_HEADS)))
            wm = w_in_b[j][:, NSA_W + 3 * NSA_HEADS:] * mem_scale
            ws = [wq.astype(bf16), wg.astype(bf16), wm.astype(bf16)]
            qp, gl, mq = _proj(x, ws, [(0, 0, NSA_W, bf16), (1, 0, LANES, f32), (2, 0, MEM_W, bf16)])
            bg = jnp.pad(b_gate_b[j], (0, LANES - 3 * NSA_HEADS)).reshape(1, LANES)
            qp_p, qp_s = split(qp)
            gl_p, gl_s = split(gl)
            o_p = _nsa_prompt_call(qp_p, gl_p, bg, kc_p, vc_p, ks_p, vs_p, kw_p, vw_p)
            o_s = _nsa_sample_call(page_table, qp_s, gl_s, bg, kc_s, vc_s, ks_s, vs_s, wst, kw_s, vw_s, cache_rows)
            o_mix = jnp.concatenate([o_p.reshape(n_p, NSA_W), o_s.reshape(n_s, NSA_W)], axis=0)
            w_o = _pair_pack_cols(w_out[l][:NSA_W].T).T
        om = mem_attention(mq, l)
        x1, logits = _mix_ln([(o_mix, w_o.astype(bf16)), (om, w_out[l][-MEM_W:].astype(bf16))],
                             x, ln1_g[l], ln1_b[l], w_router[l], b_router[l])
        y = _moe(x1, logits, w_e1[l], b_e1[l], w_e2[l], b_e2[l])
        x = _layer_norm(DN_ALPHA * x1 + y, ln2_g[l], ln2_b[l])

    rows_p, win_p_out, rows_s, win_s_out = nsa_out
    return (x[:n_p].reshape(bp, tp, d), x[n_p:].reshape(bs, ts, d),
            jnp.stack(gla_p), jnp.stack(gla_s), rows_p, rows_s, win_p_out, win_s_out, mem_kv_prompt)
```

```python
import functools
import math

import jax
import jax.numpy as jnp
from jax import lax
from jax.experimental import pallas as pl
from jax.experimental.pallas import tpu as pltpu

D_MODEL = 1024
DEPTH = 4
PAGE_SIZE = 128
N_A = DEPTH // 2
GLA_HEADS = 4
GLA_DV = (3 * D_MODEL) // (4 * GLA_HEADS)
GLA_DK = GLA_DV // 2
GLA_RANK = 16
GLA_TAU = 16.0
GLA_CHUNK = 64
HD = 64
NSA_HEADS = (3 * D_MODEL) // (4 * HD)
NSA_KV = 2
NSA_HPG = NSA_HEADS // NSA_KV
CMP_S = 16
CMP_L = 2 * CMP_S
SLC_L = 64
SLC_K = 16
WINDOW = 512
Q_BLK = 128
MEM_TOK = 256
MEM_HEADS = 4
MEM_HD = D_MODEL // (4 * MEM_HEADS)
N_EXP = 32
TOP_K = 4
D_FF = D_MODEL
SWIGLU_LIMIT = 7.0
SWIGLU_ALPHA = 1.702
GLA_W = GLA_HEADS * GLA_DV
NSA_W = NSA_HEADS * HD
MEM_W = MEM_HEADS * MEM_HD
DN_ALPHA = (2 * DEPTH) ** 0.25
LN_EPS = 1e-5
NEG = -1e30

f32, bf16, i32 = jnp.float32, jnp.bfloat16, jnp.int32

VMEM_LIMIT_BYTES = 56 * 1024 * 1024
LANES = 128
ROW_TILE = 512
MOE_ROWS = 256
FF_CHUNK = 512
SLC_TILE = 512
WIN_KEYS = WINDOW + Q_BLK
NSB_PAD = 64
TOK_TILE = 256
_NT = (((1,), (1,)), ((), ()))


def _cparams(sem):
    return pltpu.CompilerParams(dimension_semantics=sem, vmem_limit_bytes=VMEM_LIMIT_BYTES)


def _alibi_slopes(n):
    def pow2(m):
        start = 2.0 ** (-8.0 / m)
        return [start ** (i + 1) for i in range(m)]
    if math.log2(n).is_integer():
        return pow2(n)
    c = 2 ** math.floor(math.log2(n))
    return pow2(c) + pow2(2 * c)[0::2][: n - c]


SLOPES = _alibi_slopes(NSA_HEADS)
SLC_SHIFT = int(math.log2(SLC_L))
RATIO_SHIFT = int(math.log2(SLC_L // CMP_S))


def _proj_kernel(x_ref, *refs, n_w, outs):
    xb = x_ref[...].astype(bf16)
    res = {}
    for o_ref, (wi, lo, hi) in zip(refs[n_w:], outs):
        if wi not in res:
            res[wi] = jnp.dot(xb, refs[wi][...], preferred_element_type=f32)
        o_ref[...] = res[wi][:, lo:hi].astype(o_ref.dtype)


def _proj(x, ws, outs, tm=ROW_TILE):
    m, k = x.shape
    tm = min(tm, m)
    in_specs = [pl.BlockSpec((tm, k), lambda i: (i, 0))]
    in_specs += [pl.BlockSpec(w.shape, lambda i: (0, 0)) for w in ws]
    out_specs = [pl.BlockSpec((tm, hi - lo), lambda i: (i, 0)) for _, lo, hi, _ in outs]
    out_shape = [jax.ShapeDtypeStruct((m, hi - lo), dt) for _, lo, hi, dt in outs]
    return pl.pallas_call(
        functools.partial(_proj_kernel, n_w=len(ws), outs=tuple(o[:3] for o in outs)),
        grid=(m // tm,),
        in_specs=in_specs,
        out_specs=out_specs,
        out_shape=out_shape,
        compiler_params=_cparams(("parallel",)),
        name="proj",
    )(x, *ws)


def _whole(ws, dtype=f32):
    return [(i, 0, w.shape[1], dtype) for i, w in enumerate(ws)]


def _layer_norm_rows(y, g, b):
    mu = jnp.mean(y, axis=-1, keepdims=True)
    d = y - mu
    var = jnp.mean(d * d, axis=-1, keepdims=True)
    return d * lax.rsqrt(var + LN_EPS) * g + b


def _mix_ln_kernel(*refs, n_parts):
    a_refs, w_refs = refs[:n_parts], refs[n_parts:2 * n_parts]
    x_ref, g_ref, b_ref, wr_ref, br_ref, x1_ref, lg_ref = refs[2 * n_parts:]
    mix = jnp.dot(a_refs[0][...].astype(bf16), w_refs[0][...], preferred_element_type=f32)
    for a_ref, w_ref in zip(a_refs[1:], w_refs[1:]):
        mix = mix + jnp.dot(a_ref[...].astype(bf16), w_ref[...], preferred_element_type=f32)
    x1 = _layer_norm_rows(DN_ALPHA * x_ref[...] + mix, g_ref[...], b_ref[...])
    x1_ref[...] = x1
    lg_ref[...] = jnp.dot(x1, wr_ref[...], preferred_element_type=f32,
                          precision=lax.Precision.HIGHEST) + br_ref[...]


def _mix_ln(parts, x, g, b, w_r, b_r, tm=ROW_TILE):
    m, d = x.shape
    row = lambda i: (i, 0)
    fix = lambda i: (0, 0)
    in_specs = [pl.BlockSpec((tm, a.shape[1]), row) for a, _ in parts]
    in_specs += [pl.BlockSpec(w.shape, fix) for _, w in parts]
    in_specs += [pl.BlockSpec((tm, d), row), pl.BlockSpec((1, d), fix), pl.BlockSpec((1, d), fix),
                 pl.BlockSpec((d, N_EXP), fix), pl.BlockSpec((1, N_EXP), fix)]
    return pl.pallas_call(
        functools.partial(_mix_ln_kernel, n_parts=len(parts)),
        grid=(m // tm,),
        in_specs=in_specs,
        out_specs=[pl.BlockSpec((tm, d), row), pl.BlockSpec((tm, N_EXP), row)],
        out_shape=[jax.ShapeDtypeStruct((m, d), f32), jax.ShapeDtypeStruct((m, N_EXP), f32)],
        compiler_params=_cparams(("parallel",)),
        name="mix_ln",
    )(*[a for a, _ in parts], *[w for _, w in parts], x, g.reshape(1, d), b.reshape(1, d), w_r,
      b_r.reshape(1, N_EXP))


def _moe_kernel(be_ref, nreal_ref, x_ref, w1_ref, b1_ref, w2_ref, b2_ref, o_ref, w1s, w2s):
    i = pl.program_id(0)
    real = i < nreal_ref[0]
    prev = be_ref[jnp.maximum(i - 1, 0)]
    fresh = jnp.logical_or(i == 0, be_ref[i] != prev)

    @pl.when(jnp.logical_and(real, fresh))
    def _():
        w1s[...] = w1_ref[0].astype(bf16)
        w2s[...] = w2_ref[0].astype(bf16)

    @pl.when(real)
    def _():
        x = x_ref[...].astype(bf16)
        acc = jnp.zeros((MOE_ROWS, D_MODEL), f32)
        for c in range(D_FF // FF_CHUNK):
            lo = c * FF_CHUNK
            hg = jnp.dot(x, w1s[:, lo:lo + FF_CHUNK], preferred_element_type=f32)
            hg = hg + b1_ref[0, :, lo:lo + FF_CHUNK]
            hu = jnp.dot(x, w1s[:, D_FF + lo:D_FF + lo + FF_CHUNK], preferred_element_type=f32)
            hu = hu + b1_ref[0, :, D_FF + lo:D_FF + lo + FF_CHUNK]
            g = jnp.minimum(hg, SWIGLU_LIMIT)
            u = jnp.clip(hu, -SWIGLU_LIMIT, SWIGLU_LIMIT)
            a = (u + 1.0) * g * jax.nn.sigmoid(SWIGLU_ALPHA * g)
            acc = acc + jnp.dot(a.astype(bf16), w2s[lo:lo + FF_CHUNK, :], preferred_element_type=f32)
        o_ref[...] = acc + b2_ref[0]

    @pl.when(jnp.logical_not(real))
    def _():
        o_ref[...] = jnp.zeros_like(o_ref)


def _moe_ffn_blocks(xb, blk_exp, nreal, w1, b1, w2, b2):
    p, d = xb.shape
    nblk = p // MOE_ROWS
    grid_spec = pltpu.PrefetchScalarGridSpec(
        num_scalar_prefetch=2,
        grid=(nblk,),
        in_specs=[
            pl.BlockSpec((MOE_ROWS, d), lambda i, be, nr: (i, 0)),
            pl.BlockSpec((1, d, 2 * D_FF), lambda i, be, nr: (be[i], 0, 0)),
            pl.BlockSpec((1, 1, 2 * D_FF), lambda i, be, nr: (be[i], 0, 0)),
            pl.BlockSpec((1, D_FF, d), lambda i, be, nr: (be[i], 0, 0)),
            pl.BlockSpec((1, 1, d), lambda i, be, nr: (be[i], 0, 0)),
        ],
        out_specs=pl.BlockSpec((MOE_ROWS, d), lambda i, be, nr: (i, 0)),
        scratch_shapes=[pltpu.VMEM((d, 2 * D_FF), bf16), pltpu.VMEM((D_FF, d), bf16)],
    )
    return pl.pallas_call(
        _moe_kernel,
        grid_spec=grid_spec,
        out_shape=jax.ShapeDtypeStruct((p, d), f32),
        compiler_params=_cparams(("arbitrary",)),
        name="moe_ffn",
    )(blk_exp, nreal, xb, w1, b1.reshape(N_EXP, 1, 2 * D_FF), w2, b2.reshape(N_EXP, 1, d))


def _row_copy(src, src_row, dst, dst_row, sem):
    return pltpu.make_async_copy(src.at[pl.ds(src_row, 1), :], dst.at[pl.ds(dst_row, 1), :], sem)


def _moe_dispatch_kernel(dest_ref, x_ref, xb_init_ref, xb_ref, sem):
    del xb_init_ref

    def issue(t, c):
        for k in range(TOP_K):
            _row_copy(x_ref, t, xb_ref, dest_ref[0, 0, t * TOP_K + k], sem).start(priority=k % 2)
        return c

    lax.fori_loop(0, TOK_TILE, issue, 0, unroll=2)

    def drain(t, c):
        for k in range(TOP_K):
            _row_copy(x_ref, 0, xb_ref, 0, sem).wait()
        return c

    lax.fori_loop(0, TOK_TILE, drain, 0, unroll=2)


def _moe_dispatch(x1, dest, p):
    n, d = x1.shape
    nt = n // TOK_TILE
    return pl.pallas_call(
        _moe_dispatch_kernel,
        grid=(nt,),
        in_specs=[pl.BlockSpec((1, 1, TOK_TILE * TOP_K), lambda i: (i, 0, 0), memory_space=pltpu.SMEM),
                  pl.BlockSpec((TOK_TILE, d), lambda i: (i, 0)),
                  pl.BlockSpec(memory_space=pl.ANY)],
        out_specs=pl.BlockSpec(memory_space=pl.ANY),
        out_shape=jax.ShapeDtypeStruct((p, d), f32),
        scratch_shapes=[pltpu.SemaphoreType.DMA(())],
        input_output_aliases={2: 0},
        compiler_params=_cparams(("arbitrary",)),
        name="moe_dispatch",
    )(dest.reshape(nt, 1, TOK_TILE * TOP_K), x1, jnp.zeros((p, d), f32))


def _moe_combine_kernel(dest_ref, yb_ref, gate_ref, x1_ref, g_ref, b_ref, o_ref, ybuf, sem):
    def issue(t, c):
        for k in range(TOP_K):
            _row_copy(yb_ref, dest_ref[0, 0, t * TOP_K + k], ybuf.at[k], t, sem).start(priority=k % 2)
        return c

    lax.fori_loop(0, TOK_TILE, issue, 0, unroll=2)

    def drain(t, c):
        for k in range(TOP_K):
            _row_copy(yb_ref, 0, ybuf.at[k], 0, sem).wait()
        return c

    lax.fori_loop(0, TOK_TILE, drain, 0, unroll=2)
    gate = gate_ref[...]
    y = gate[:, 0:1] * ybuf[0]
    for k in range(1, TOP_K):
        y = y + gate[:, k:k + 1] * ybuf[k]
    o_ref[...] = _layer_norm_rows(DN_ALPHA * x1_ref[...] + y, g_ref[...], b_ref[...])


def _moe_combine(yb, dest, gate, x1, g, b):
    n, d = x1.shape
    nt = n // TOK_TILE
    row = lambda i: (i, 0)
    fix = lambda i: (0, 0)
    return pl.pallas_call(
        _moe_combine_kernel,
        grid=(nt,),
        in_specs=[pl.BlockSpec((1, 1, TOK_TILE * TOP_K), lambda i: (i, 0, 0), memory_space=pltpu.SMEM),
                  pl.BlockSpec(memory_space=pl.ANY),
                  pl.BlockSpec((TOK_TILE, TOP_K), row), pl.BlockSpec((TOK_TILE, d), row),
                  pl.BlockSpec((1, d), fix), pl.BlockSpec((1, d), fix)],
        out_specs=pl.BlockSpec((TOK_TILE, d), row),
        out_shape=jax.ShapeDtypeStruct((n, d), f32),
        scratch_shapes=[pltpu.VMEM((TOP_K, TOK_TILE, d), f32), pltpu.SemaphoreType.DMA(())],
        compiler_params=_cparams(("arbitrary",)),
        name="moe_combine",
    )(dest.reshape(nt, 1, TOK_TILE * TOP_K), yb, gate, x1, g.reshape(1, d), b.reshape(1, d))


def _moe_ln(x1, logits, w1, b1, w2, b2, ln_g, ln_b):
    n, d = x1.shape
    top_v, top_i = lax.top_k(logits, TOP_K)
    gate = jax.nn.softmax(top_v, axis=-1)
    a = n * TOP_K
    flat_e = top_i.reshape(a)
    onehot = (flat_e[:, None] == jnp.arange(N_EXP, dtype=i32)[None, :]).astype(i32)
    csum = jnp.cumsum(onehot, axis=0)
    counts = csum[-1]
    rank = jnp.take_along_axis(csum, flat_e[:, None], axis=1)[:, 0] - 1
    padded = (counts + MOE_ROWS - 1) // MOE_ROWS * MOE_ROWS
    pend = jnp.cumsum(padded)
    dest = (pend - padded)[flat_e] + rank
    nblk = -(-a // MOE_ROWS) + N_EXP
    blk_start = jnp.arange(nblk, dtype=i32) * MOE_ROWS
    nreal = (pend[-1] // MOE_ROWS).astype(i32)
    blk_exp = jnp.searchsorted(pend, blk_start, side='right').astype(i32)
    last_exp = blk_exp[jnp.maximum(nreal - 1, 0)]
    blk_exp = jnp.where(blk_start < pend[-1], jnp.minimum(blk_exp, N_EXP - 1), last_exp)
    xb = _moe_dispatch(x1, dest, nblk * MOE_ROWS)
    yb = _moe_ffn_blocks(xb, blk_exp, nreal.reshape(1), w1, b1, w2, b2)
    return _moe_combine(yb, dest, gate, x1, ln_g, ln_b)


def _mem_attn_kernel(q_ref, kv_ref, o_ref):
    q = q_ref[0]
    kv = kv_ref[0].astype(bf16)
    tq = q.shape[0]
    lane = lax.broadcasted_iota(i32, (tq, LANES), 1)
    zero = jnp.zeros((tq, LANES), bf16)
    for pair in range(MEM_HEADS // 2):
        qt = q[:, LANES * pair:LANES * (pair + 1)]
        kt = kv[:, LANES * pair:LANES * (pair + 1)]
        vt = kv[:, MEM_W + LANES * pair:MEM_W + LANES * (pair + 1)]
        halves = []
        for h in range(2):
            keep = (lane < MEM_HD) if h == 0 else (lane >= MEM_HD)
            s = lax.dot_general(jnp.where(keep, qt, zero), kt, _NT, preferred_element_type=f32)
            m = jnp.max(s, axis=-1, keepdims=True)
            e = jnp.exp(s - m)
            p = e / jnp.sum(e, axis=-1, keepdims=True)
            halves.append(jnp.dot(p.astype(bf16), vt, preferred_element_type=f32))
        o_ref[0, :, LANES * pair:LANES * (pair + 1)] = jnp.where(lane < MEM_HD, halves[0], halves[1])


def _mem_attn(mq, mem_kv):
    b, t, w = mq.shape
    tq = min(t, ROW_TILE)
    return pl.pallas_call(
        _mem_attn_kernel,
        grid=(b, t // tq),
        in_specs=[pl.BlockSpec((1, tq, w), lambda bb, i: (bb, i, 0)),
                  pl.BlockSpec((1, MEM_TOK, 2 * w), lambda bb, i: (bb, 0, 0))],
        out_specs=pl.BlockSpec((1, tq, w), lambda bb, i: (bb, i, 0)),
        out_shape=jax.ShapeDtypeStruct((b, t, w), f32),
        compiler_params=_cparams(("parallel", "parallel")),
        name="mem_attn",
    )(mq, mem_kv)


def _gla_scan(q, k, v, log_a, s0):
    B, T, H, _ = q.shape
    C = GLA_CHUNK if T % GLA_CHUNK == 0 else T
    n = T // C

    def to_chunks(a):
        return jnp.moveaxis(a.astype(f32).reshape(B, n, C, *a.shape[2:]), 1, 0)

    tri = jnp.tril(jnp.ones((C, C), bool))

    def step(S, inp):
        qc, kc, vc, gc = inp
        b = jnp.cumsum(gc, axis=1)
        bl = b[:, -1]
        qt = qc * jnp.exp(b)
        kt = kc * jnp.exp(-b)
        att = jnp.where(tri, jnp.einsum('bchk,bshk->bhcs', qt, kt), 0.0)
        o = jnp.einsum('bchk,bhkv->bchv', qt, S) + jnp.einsum('bhcs,bshv->bchv', att, vc)
        kd = kc * jnp.exp(bl[:, None] - b)
        S = jnp.exp(bl)[..., None] * S + jnp.einsum('bshk,bshv->bhkv', kd, vc)
        return S, o

    sT, o = lax.scan(step, s0.astype(f32), (to_chunks(q), to_chunks(k), to_chunks(v), to_chunks(log_a)))
    return jnp.moveaxis(o, 0, 1).reshape(B, T, H, -1), sT


def _gla_mix(q, k, v, r, glr, s0, w_gate, b_gate, gla_norm):
    B, T, _ = q.shape
    q = q.reshape(B, T, GLA_HEADS, GLA_DK) * (GLA_DK ** -0.5)
    k = k.reshape(B, T, GLA_HEADS, GLA_DK)
    v = v.reshape(B, T, GLA_HEADS, GLA_DV)
    log_a = jax.nn.log_sigmoid((glr @ w_gate + b_gate).astype(f32)) / GLA_TAU
    o, s_new = _gla_scan(q, k, v, log_a.reshape(B, T, GLA_HEADS, GLA_DK), s0)
    o = o * lax.rsqrt(jnp.mean(o * o, axis=-1, keepdims=True) + LN_EPS) * gla_norm.reshape(GLA_HEADS, GLA_DV)
    return o.reshape(B, T, GLA_W) * jax.nn.silu(r), s_new


def _softmax_rows(sm, maskf):
    m = jnp.max(sm, axis=-1, keepdims=True)
    e = jnp.exp(sm - m)
    return e / jnp.sum(e, axis=-1, keepdims=True) * maskf


def _block_scores_t(imp):
    nc = imp.shape[1]
    jj = lax.broadcasted_iota(i32, (NSB_PAD, nc), 0)
    nn = lax.broadcasted_iota(i32, (NSB_PAD, nc), 1)
    mt = ((nn >> RATIO_SHIFT) == jj).astype(f32) + (((nn + 1) >> RATIO_SHIFT) == jj).astype(f32)
    return lax.dot_general(mt, imp, _NT, preferred_element_type=f32, precision=lax.Precision.HIGHEST)


def _select_t(blk_t, qpos_t):
    nq = blk_t.shape[1]
    j_t = lax.broadcasted_iota(i32, (NSB_PAD, nq), 0)
    valid = j_t * SLC_L <= qpos_t
    cur = qpos_t >> SLC_SHIFT
    forced = (j_t == 0) | (j_t == cur) | (j_t == cur - 1)
    score = jnp.where(valid, jnp.where(forced, 1e9, blk_t), -1e9)
    cnt = jnp.zeros((NSB_PAD, nq), i32)
    for k in range(NSB_PAD):
        row = score[k:k + 1, :]
        beats = (row > score) | ((row == score) & (j_t > k))
        cnt = cnt + beats.astype(i32)
    return ((cnt < SLC_K) & valid).astype(f32)


def _untranspose(sel_t):
    nq = sel_t.shape[1]
    eye = lax.broadcasted_iota(i32, (nq, nq), 0) == lax.broadcasted_iota(i32, (nq, nq), 1)
    return lax.dot_general(eye.astype(bf16), sel_t.astype(bf16), _NT, preferred_element_type=f32).astype(bf16)


def _nsa_prompt_kernel(qp_ref, gl_ref, bg_ref, kc_ref, vc_ref, ks_ref, vs_ref, kw_ref, vw_ref, o_ref,
                       p_sc, m_sc, l_sc, acc_sc, og_sc, used_sm):
    nc = kc_ref.shape[1]
    q0 = pl.program_id(1) * Q_BLK
    qpos_i = q0 + lax.broadcasted_iota(i32, (Q_BLK, 1), 0)
    lane = lax.broadcasted_iota(i32, (Q_BLK, LANES), 1)
    gates = jax.nn.sigmoid(gl_ref[0] + bg_ref[...])
    qp = qp_ref[0]
    zero_b = jnp.zeros((Q_BLK, LANES), bf16)

    for g in range(NSA_KV):
        half = (lane < HD) if g == 0 else (lane >= HD)
        qg = jnp.concatenate([jnp.where(half, qp[:, LANES * hh:LANES * (hh + 1)], zero_b)
                              for hh in range(NSA_HPG)], axis=0)
        slopes = [SLOPES[NSA_HPG * g + hh] for hh in range(NSA_HPG)]

        n_idx = lax.broadcasted_iota(i32, (Q_BLK, nc), 1)
        cd = (qpos_i - (n_idx * CMP_S + (CMP_L - 1))).astype(f32)
        cmask = cd >= 0
        cmaskf = cmask.astype(f32)
        s = lax.dot_general(qg, kc_ref[0], _NT, preferred_element_type=f32)
        imp = jnp.zeros((Q_BLK, nc), f32)
        for hh in range(NSA_HPG):
            rows = slice(hh * Q_BLK, (hh + 1) * Q_BLK)
            p = _softmax_rows(jnp.where(cmask, s[rows] - slopes[hh] * cd, NEG), cmaskf)
            imp = imp + p
            p_sc[rows, :nc] = p.astype(bf16)
        o_cmp = jnp.dot(p_sc[:, :nc], vc_ref[0], preferred_element_type=f32)

        qpos_t = q0 + lax.broadcasted_iota(i32, (NSB_PAD, Q_BLK), 1)
        sel_t = _select_t(_block_scores_t(imp), qpos_t)
        sel_b = _untranspose(sel_t)
        blocks_per_tile = SLC_TILE // SLC_L
        for t in range(NSB_PAD // blocks_per_tile):
            used_sm[t] = (jnp.max(sel_t[t * blocks_per_tile:(t + 1) * blocks_per_tile, :]) > 0.5).astype(i32)

        m_sc[...] = jnp.full(m_sc.shape, NEG, f32)
        l_sc[...] = jnp.zeros(l_sc.shape, f32)
        acc_sc[...] = jnp.zeros(acc_sc.shape, f32)

        def tile(t, carry):
            @pl.when(used_sm[t] > 0)
            def _():
                k0 = pl.multiple_of(t * SLC_TILE, SLC_TILE)
                kt = ks_ref[0, pl.ds(k0, SLC_TILE), :]
                vt = vs_ref[0, pl.ds(k0, SLC_TILE), :]
                st = lax.dot_general(qg, kt, _NT, preferred_element_type=f32)
                kpos = k0 + lax.broadcasted_iota(i32, (Q_BLK, SLC_TILE), 1)
                blk_of_key = (k0 + lax.broadcasted_iota(i32, (NSB_PAD, SLC_TILE), 1)) >> SLC_SHIFT
                expand = (lax.broadcasted_iota(i32, (NSB_PAD, SLC_TILE), 0) == blk_of_key).astype(bf16)
                selexp = jnp.dot(sel_b, expand, preferred_element_type=f32)
                addmask = jnp.where((qpos_i >= kpos) & (selexp > 0.5), 0.0, NEG)
                krel = (k0 - q0 + lax.broadcasted_iota(i32, (1, SLC_TILE), 1)).astype(f32)
                for hh in range(NSA_HPG):
                    rows = slice(hh * Q_BLK, (hh + 1) * Q_BLK)
                    sm = st[rows] + slopes[hh] * krel + addmask
                    m_old = m_sc[rows]
                    m_new = jnp.maximum(m_old, jnp.max(sm, axis=-1, keepdims=True))
                    a = jnp.exp(m_old - m_new)
                    e = jnp.exp(sm - m_new)
                    l_sc[rows] = a * l_sc[rows] + jnp.sum(e, axis=-1, keepdims=True)
                    m_sc[rows] = m_new
                    acc_sc[rows] = a * acc_sc[rows]
                    p_sc[rows, :SLC_TILE] = e.astype(bf16)
                acc_sc[...] += jnp.dot(p_sc[:, :SLC_TILE], vt, preferred_element_type=f32)
            return carry

        lax.fori_loop(0, (q0 + Q_BLK + SLC_TILE - 1) // SLC_TILE, tile, 0)
        o_slc = acc_sc[...] / l_sc[...]

        start = pl.multiple_of(jnp.maximum(q0 - WINDOW, 0), Q_BLK)
        kt = kw_ref[0, pl.ds(start, WIN_KEYS), :]
        vt = vw_ref[0, pl.ds(start, WIN_KEYS), :]
        st = lax.dot_general(qg, kt, _NT, preferred_element_type=f32)
        wd = (qpos_i - (start + lax.broadcasted_iota(i32, (Q_BLK, WIN_KEYS), 1))).astype(f32)
        wmask = (wd >= 0) & (wd <= WINDOW)
        wmaskf = wmask.astype(f32)
        for hh in range(NSA_HPG):
            rows = slice(hh * Q_BLK, (hh + 1) * Q_BLK)
            p = _softmax_rows(jnp.where(wmask, st[rows] - slopes[hh] * wd, NEG), wmaskf)
            p_sc[rows, :WIN_KEYS] = p.astype(bf16)
        o_win = jnp.dot(p_sc[:, :WIN_KEYS], vt, preferred_element_type=f32)

        for hh in range(NSA_HPG):
            rows = slice(hh * Q_BLK, (hh + 1) * Q_BLK)
            c = 3 * (NSA_HPG * g + hh)
            og_sc[g, rows] = (gates[:, c:c + 1] * o_cmp[rows] + gates[:, c + 1:c + 2] * o_slc[rows]
                              + gates[:, c + 2:c + 3] * o_win[rows])

    for hh in range(NSA_HPG):
        rows = slice(hh * Q_BLK, (hh + 1) * Q_BLK)
        o_ref[0, :, LANES * hh:LANES * (hh + 1)] = jnp.where(lane < HD, og_sc[0, rows], og_sc[1, rows])


def _nsa_prompt_call(qp, gl, bg, kc, vc, ks, vs, kw, vw):
    b, t, w = qp.shape
    nc = kc.shape[1]
    assert t % SLC_TILE == 0 and t >= WIN_KEYS and nc % LANES == 0 and t // SLC_L <= NSB_PAD
    rows = NSA_HPG * Q_BLK
    qblk = lambda bb, i: (bb, i, 0)
    whole = lambda bb, i: (bb, 0, 0)
    return pl.pallas_call(
        _nsa_prompt_kernel,
        grid=(b, t // Q_BLK),
        in_specs=[pl.BlockSpec((1, Q_BLK, w), qblk), pl.BlockSpec((1, Q_BLK, LANES), qblk),
                  pl.BlockSpec((1, LANES), lambda bb, i: (0, 0)),
                  pl.BlockSpec((1, nc, LANES), whole), pl.BlockSpec((1, nc, LANES), whole),
                  pl.BlockSpec((1, t, LANES), whole), pl.BlockSpec((1, t, LANES), whole),
                  pl.BlockSpec((1, t, LANES), whole), pl.BlockSpec((1, t, LANES), whole)],
        out_specs=pl.BlockSpec((1, Q_BLK, w), qblk),
        out_shape=jax.ShapeDtypeStruct((b, t, w), f32),
        scratch_shapes=[pltpu.VMEM((rows, WIN_KEYS), bf16), pltpu.VMEM((rows, 1), f32), pltpu.VMEM((rows, 1), f32),
                        pltpu.VMEM((rows, LANES), f32), pltpu.VMEM((NSA_KV, rows, LANES), f32),
                        pltpu.SMEM((NSB_PAD * SLC_L // SLC_TILE,), i32)],
        compiler_params=_cparams(("parallel", "parallel")),
        name="nsa_prompt",
    )(qp, gl, bg, kc, vc, ks, vs, kw, vw)


def _nsa_sample_kernel(pt_ref, qp_ref, gl_ref, bg_ref, kc_ref, vc_ref, kn_ref, vn_ref, wst_ref, wkn_ref, wvn_ref,
                       *rest, n_pages, past_len):
    page_refs = rest[:n_pages]
    o_ref, k_sc, v_sc = rest[n_pages:]
    nq = qp_ref.shape[1]
    nc = kc_ref.shape[1]
    nkeys = (n_pages + 1) * PAGE_SIZE
    rows_g = NSA_HPG * nq
    qpos_i = past_len + lax.broadcasted_iota(i32, (nq, 1), 0)
    lane = lax.broadcasted_iota(i32, (nq, LANES), 1)
    gates = jax.nn.sigmoid(gl_ref[0] + bg_ref[...])
    qp = qp_ref[0]
    zero_b = jnp.zeros((nq, LANES), bf16)
    qs = jnp.concatenate([jnp.where((lane < HD) if g == 0 else (lane >= HD),
                                    qp[:, LANES * hh:LANES * (hh + 1)], zero_b)
                          for g in range(NSA_KV) for hh in range(NSA_HPG)], axis=0)

    def rows_of(g, hh):
        r0 = g * rows_g + hh * nq
        return slice(r0, r0 + nq)

    n_idx = lax.broadcasted_iota(i32, (nq, nc), 1)
    cd = (qpos_i - (n_idx * CMP_S + (CMP_L - 1))).astype(f32)
    cmask = cd >= 0
    cmaskf = cmask.astype(f32)
    s = lax.dot_general(qs, kc_ref[0], _NT, preferred_element_type=f32)
    imps, ps = [], []
    for g in range(NSA_KV):
        imp = jnp.zeros((nq, nc), f32)
        for hh in range(NSA_HPG):
            p = _softmax_rows(jnp.where(cmask, s[rows_of(g, hh)] - SLOPES[NSA_HPG * g + hh] * cd, NEG), cmaskf)
            imp = imp + p
            ps.append(p)
        imps.append(imp)
    o_cmp = jnp.dot(jnp.concatenate(ps, axis=0).astype(bf16), vc_ref[0], preferred_element_type=f32)

    qpos_t = past_len + lax.broadcasted_iota(i32, (NSB_PAD, nq), 1)
    blk_of_key = lax.broadcasted_iota(i32, (NSB_PAD, nkeys), 1) >> SLC_SHIFT
    expand = (lax.broadcasted_iota(i32, (NSB_PAD, nkeys), 0) == blk_of_key).astype(bf16)
    sd = (qpos_i - lax.broadcasted_iota(i32, (nq, nkeys), 1)).astype(f32)
    masks = []
    for g in range(NSA_KV):
        sel_b = _untranspose(_select_t(_block_scores_t(imps[g]), qpos_t))
        selexp = jnp.dot(sel_b, expand, preferred_element_type=f32)
        masks.append((sd >= 0) & (selexp > 0.5))

    for pg in range(n_pages):
        blk = page_refs[pg][0]
        k_sc[pg * PAGE_SIZE:(pg + 1) * PAGE_SIZE, :] = blk[:, :LANES].astype(bf16)
        v_sc[pg * PAGE_SIZE:(pg + 1) * PAGE_SIZE, :] = blk[:, LANES:].astype(bf16)
    tail = jnp.zeros((PAGE_SIZE - nq, LANES), bf16)
    k_sc[n_pages * PAGE_SIZE:, :] = jnp.concatenate([kn_ref[0], tail], axis=0)
    v_sc[n_pages * PAGE_SIZE:, :] = jnp.concatenate([vn_ref[0], tail], axis=0)
    s = lax.dot_general(qs, k_sc[...], _NT, preferred_element_type=f32)
    ps = []
    for g in range(NSA_KV):
        mf = masks[g].astype(f32)
        for hh in range(NSA_HPG):
            ps.append(_softmax_rows(jnp.where(masks[g], s[rows_of(g, hh)] - SLOPES[NSA_HPG * g + hh] * sd, NEG), mf))
    o_slc = jnp.dot(jnp.concatenate(ps, axis=0).astype(bf16), v_sc[...], preferred_element_type=f32)

    wb = wst_ref.shape[1]
    wk = jnp.concatenate([wst_ref[0, :, :LANES].astype(bf16), wkn_ref[0], tail], axis=0)
    wv = jnp.concatenate([wst_ref[0, :, LANES:].astype(bf16), wvn_ref[0], tail], axis=0)
    wpos = (past_len - wb) + lax.broadcasted_iota(i32, (nq, wb + PAGE_SIZE), 1)
    wd = (qpos_i - wpos).astype(f32)
    wmask = (wd >= 0) & (wd <= WINDOW) & (wpos >= 0)
    wmaskf = wmask.astype(f32)
    s = lax.dot_general(qs, wk, _NT, preferred_element_type=f32)
    ps = []
    for g in range(NSA_KV):
        for hh in range(NSA_HPG):
            ps.append(_softmax_rows(jnp.where(wmask, s[rows_of(g, hh)] - SLOPES[NSA_HPG * g + hh] * wd, NEG), wmaskf))
    o_win = jnp.dot(jnp.concatenate(ps, axis=0).astype(bf16), wv, preferred_element_type=f32)

    for hh in range(NSA_HPG):
        halves = []
        for g in range(NSA_KV):
            r = rows_of(g, hh)
            c = 3 * (NSA_HPG * g + hh)
            halves.append(gates[:, c:c + 1] * o_cmp[r] + gates[:, c + 1:c + 2] * o_slc[r]
                          + gates[:, c + 2:c + 3] * o_win[r])
        o_ref[0, :, LANES * hh:LANES * (hh + 1)] = jnp.where(lane < HD, halves[0], halves[1])


def _nsa_sample_call(page_table, qp, gl, bg, kc, vc, kn, vn, wst, wkn, wvn, cache):
    b, nq, w = qp.shape
    n_pages = page_table.shape[1]
    nc = kc.shape[1]
    wb = wst.shape[1]
    nkeys = (n_pages + 1) * PAGE_SIZE
    assert nkeys // SLC_L <= NSB_PAD and nq <= PAGE_SIZE
    per_b = lambda bb, pt: (bb, 0, 0)
    in_specs = [pl.BlockSpec((1, nq, w), per_b), pl.BlockSpec((1, nq, LANES), per_b),
                pl.BlockSpec((1, LANES), lambda bb, pt: (0, 0)),
                pl.BlockSpec((1, nc, LANES), per_b), pl.BlockSpec((1, nc, LANES), per_b),
                pl.BlockSpec((1, nq, LANES), per_b), pl.BlockSpec((1, nq, LANES), per_b),
                pl.BlockSpec((1, wb, 2 * LANES), per_b),
                pl.BlockSpec((1, nq, LANES), per_b), pl.BlockSpec((1, nq, LANES), per_b)]
    for pg in range(n_pages):
        in_specs.append(pl.BlockSpec((1, PAGE_SIZE, 2 * LANES),
                                     functools.partial(lambda bb, pt, pg: (pt[bb, pg], 0, 1), pg=pg)))
    grid_spec = pltpu.PrefetchScalarGridSpec(
        num_scalar_prefetch=1, grid=(b,), in_specs=in_specs,
        out_specs=pl.BlockSpec((1, nq, w), per_b),
        scratch_shapes=[pltpu.VMEM((nkeys, LANES), bf16), pltpu.VMEM((nkeys, LANES), bf16)])
    return pl.pallas_call(
        functools.partial(_nsa_sample_kernel, n_pages=n_pages, past_len=n_pages * PAGE_SIZE),
        grid_spec=grid_spec,
        out_shape=jax.ShapeDtypeStruct((b, nq, w), f32),
        compiler_params=_cparams(("parallel",)),
        name="nsa_sample",
    )(page_table, qp, gl, bg, kc, vc, kn, vn, wst, wkn, wvn, *([cache] * n_pages))


def _compress_kernel(x_ref, pos_ref, w1_ref, b1_ref, w2_ref, o_ref):
    x = x_ref[0, 0, 0]
    half = CMP_S * HD
    xa = (x + pos_ref[0, :, :half]).astype(bf16)
    xb = (x + pos_ref[0, :, half:]).astype(bf16)
    a = jnp.dot(xa, w1_ref[0, :half, :], preferred_element_type=f32)
    bsec = jnp.dot(xb, w1_ref[0, half:, :], preferred_element_type=f32)
    nch = x.shape[0]
    hid = jax.nn.gelu(a + pltpu.roll(bsec, nch - 1, 0) + b1_ref[0])
    o_ref[0, 0, 0] = jnp.dot(hid.astype(bf16), w2_ref[0], preferred_element_type=f32).astype(o_ref.dtype)


def _compress_call(x2, cmp_pos, w_cmp1, b_cmp1, w_cmp2):
    b, _, g, nch, cw = x2.shape
    hid = w_cmp1.shape[-1]
    pos = cmp_pos.reshape(2, 1, CMP_L * HD)
    return pl.pallas_call(
        _compress_kernel,
        grid=(b, 2, g),
        in_specs=[pl.BlockSpec((1, 1, 1, nch, cw), lambda bb, s, gg: (bb, s, gg, 0, 0)),
                  pl.BlockSpec((1, 1, CMP_L * HD), lambda bb, s, gg: (s, 0, 0)),
                  pl.BlockSpec((1, CMP_L * HD, hid), lambda bb, s, gg: (s, 0, 0)),
                  pl.BlockSpec((1, 1, hid), lambda bb, s, gg: (s, 0, 0)),
                  pl.BlockSpec((1, hid, HD), lambda bb, s, gg: (s, 0, 0))],
        out_specs=pl.BlockSpec((1, 1, 1, nch, HD), lambda bb, s, gg: (bb, s, gg, 0, 0)),
        out_shape=jax.ShapeDtypeStruct((b, 2, g, nch, HD), bf16),
        compiler_params=_cparams(("parallel", "parallel", "parallel")),
        name="compress",
    )(x2, pos, w_cmp1.astype(bf16), b_cmp1.reshape(2, 1, hid), w_cmp2.astype(bf16))


def _compressed_kv(tok, cmp_pos, w_cmp1, b_cmp1, w_cmp2):
    b, t = tok.shape[:2]
    nch = t // CMP_S
    x2 = jnp.transpose(tok[:, :nch * CMP_S], (0, 2, 3, 1, 4)).reshape(b, 2, NSA_KV, nch, CMP_S * HD)
    ck = _compress_call(x2, cmp_pos, w_cmp1, b_cmp1, w_cmp2)
    ck = jnp.transpose(ck, (0, 1, 3, 2, 4)).reshape(b, 2, nch, NSA_KV * HD)
    return ck[:, 0], ck[:, 1]


def _pair_pack_cols(w):
    k = w.shape[0]
    return jnp.transpose(w.reshape(k, NSA_KV, NSA_HPG, HD), (0, 2, 1, 3)).reshape(k, NSA_W)


def _last_rows(a, n):
    t = a.shape[1]
    if t >= n:
        return a[:, t - n:]
    return jnp.pad(a, ((0, 0), (n - t, 0)) + ((0, 0),) * (a.ndim - 2))


def _layer_norm(x, g, b):
    mu = jnp.mean(x, axis=-1, keepdims=True)
    var = jnp.mean(jnp.square(x - mu), axis=-1, keepdims=True)
    return (x - mu) * lax.rsqrt(var + LN_EPS) * g + b


def kernel(x_prompt, x_sample, mem_prompt, state_gla, cache_nsa_kv, state_win_kv, cache_mem_kv, page_table,
           w_in_a, w_gate_a, b_gate_a, gla_norm, w_in_b, b_gate_b, w_kv_b, cmp_pos, w_cmp1, b_cmp1, w_cmp2,
           w_mem_kv, w_out, ln1_g, ln1_b, ln2_g, ln2_b, w_router, b_router, w_e1, b_e1, w_e2, b_e2):
    bp, tp, d = x_prompt.shape
    bs, ts, _ = x_sample.shape
    n_p, n_s = bp * tp, bs * ts
    wb = min(WINDOW, state_win_kv.shape[1])
    n_pool = cache_nsa_kv.shape[0]
    past_len = page_table.shape[1] * PAGE_SIZE

    mem_rows = mem_prompt.reshape(bp * MEM_TOK, d)
    w_mem = [w_mem_kv[l].astype(bf16) for l in range(DEPTH)]
    mem_kv_l = _proj(mem_rows, w_mem, _whole(w_mem))
    mem_kv_prompt = jnp.stack(mem_kv_l).reshape(DEPTH, bp, MEM_TOK, 2, MEM_HEADS, MEM_HD)
    mem_p = [m.reshape(bp, MEM_TOK, 2 * MEM_W) for m in mem_kv_l]
    mem_s = cache_mem_kv.reshape(DEPTH, bs, MEM_TOK, 2 * MEM_W)

    x = jnp.concatenate([x_prompt.reshape(n_p, d), x_sample.reshape(n_s, d)], axis=0)

    def split(a):
        w = a.shape[-1]
        return a[:n_p].reshape(bp, tp, w), a[n_p:].reshape(bs, ts, w)

    def mem_attention(mq, l):
        mq_p, mq_s = split(mq)
        om_p = _mem_attn(mq_p, mem_p[l])
        om_s = _mem_attn(mq_s, mem_s[l])
        return jnp.concatenate([om_p.reshape(n_p, MEM_W), om_s.reshape(n_s, MEM_W)], axis=0)

    hk = GLA_HEADS * GLA_DK
    cuts_a = [0, hk, 2 * hk, 2 * hk + GLA_W, 2 * hk + 2 * GLA_W, 2 * hk + 2 * GLA_W + GLA_RANK,
              2 * hk + 2 * GLA_W + GLA_RANK + MEM_W]
    mem_scale = MEM_HD ** -0.5
    q_scale = HD ** -0.5

    gla_p, gla_s = [], []
    for l in range(DEPTH):
        if l < N_A:
            ws = [w_in_a[l][:, cuts_a[i]:cuts_a[i + 1]] for i in range(6)]
            ws[5] = ws[5] * mem_scale
            ws = [w.astype(bf16) for w in ws]
            outs = _whole(ws)
            outs[5] = outs[5][:3] + (bf16,)
            q, k, v, r, glr, mq = _proj(x, ws, outs)
            parts = []
            for grp, s0 in enumerate((jnp.zeros((bp,) + state_gla.shape[2:], f32), state_gla[l])):
                o, s_new = _gla_mix(split(q)[grp], split(k)[grp], split(v)[grp], split(r)[grp], split(glr)[grp],
                                    s0, w_gate_a[l], b_gate_a[l], gla_norm[l])
                parts.append(o.reshape(-1, GLA_W))
                (gla_p if grp == 0 else gla_s).append(s_new)
            o_mix = jnp.concatenate(parts, axis=0)
            w_o = w_out[l][:GLA_W]
        else:
            j = l - N_A
            if l == N_A:
                role = NSA_KV * HD
                kv_outs = [(0, 0, 4 * role, f32), (0, 4 * role, 6 * role, f32)]
                kv_outs += [(0, r * role, (r + 1) * role, bf16) for r in range(2, 6)]
                rows, win, ks, vs, kw, vw = _proj(x, [w_kv_b.astype(bf16)], kv_outs)
                rows_p, rows_s = split(rows)
                win_p, win_s = split(win)
                ks_p, ks_s = split(ks)
                vs_p, vs_s = split(vs)
                kw_p, kw_s = split(kw)
                vw_p, vw_s = split(vw)
                kc_p, vc_p = _compressed_kv(rows_p.reshape(bp, tp, 4, NSA_KV, HD)[:, :, :2],
                                            cmp_pos, w_cmp1, b_cmp1, w_cmp2)
                cache_rows = cache_nsa_kv.reshape(n_pool, PAGE_SIZE, 4 * role)
                past_cmp = cache_rows[page_table][..., :2 * role].reshape(bs, past_len, 2, NSA_KV, HD)
                hist_cmp = jnp.concatenate([past_cmp, rows_s.reshape(bs, ts, 4, NSA_KV, HD)[:, :, :2]], axis=1)
                kc_s, vc_s = _compressed_kv(hist_cmp, cmp_pos, w_cmp1, b_cmp1, w_cmp2)
                wst = state_win_kv.reshape(bs, state_win_kv.shape[1], 2 * role)
                wkv_s = jnp.concatenate([state_win_kv, win_s.reshape(bs, ts, 2, NSA_KV, HD)], axis=1)
                nsa_out = (rows_p.reshape(bp, tp, 4, NSA_KV, HD), _last_rows(win_p.reshape(bp, tp, 2, NSA_KV, HD), wb),
                           rows_s.reshape(bs, ts, 4, NSA_KV, HD), wkv_s[:, -wb:])
            wq = _pair_pack_cols(w_in_b[j][:, :NSA_W]) * q_scale
            wg = jnp.pad(w_in_b[j][:, NSA_W:NSA_W + 3 * NSA_HEADS], ((0, 0), (0, LANES - 3 * NSA_HEADS)))
            wm = w_in_b[j][:, NSA_W + 3 * NSA_HEADS:] * mem_scale
            ws = [wq.astype(bf16), wg.astype(bf16), wm.astype(bf16)]
            qp, gl, mq = _proj(x, ws, [(0, 0, NSA_W, bf16), (1, 0, LANES, f32), (2, 0, MEM_W, bf16)])
            bg = jnp.pad(b_gate_b[j], (0, LANES - 3 * NSA_HEADS)).reshape(1, LANES)
            qp_p, qp_s = split(qp)
            gl_p, gl_s = split(gl)
            o_p = _nsa_prompt_call(qp_p, gl_p, bg, kc_p, vc_p, ks_p, vs_p, kw_p, vw_p)
            o_s = _nsa_sample_call(page_table, qp_s, gl_s, bg, kc_s, vc_s, ks_s, vs_s, wst, kw_s, vw_s, cache_rows)
            o_mix = jnp.concatenate([o_p.reshape(n_p, NSA_W), o_s.reshape(n_s, NSA_W)], axis=0)
            w_o = _pair_pack_cols(w_out[l][:NSA_W].T).T
        om = mem_attention(mq, l)
        x1, logits = _mix_ln([(o_mix, w_o.astype(bf16)), (om, w_out[l][-MEM_W:].astype(bf16))],
                             x, ln1_g[l], ln1_b[l], w_router[l], b_router[l])
        x = _moe_ln(x1, logits, w_e1[l], b_e1[l], w_e2[l], b_e2[l], ln2_g[l], ln2_b[l])

    rows_p, win_p_out, rows_s, win_s_out = nsa_out
    return (x[:n_p].reshape(bp, tp, d), x[n_p:].reshape(bs, ts, d),
            jnp.stack(gla_p), jnp.stack(gla_s), rows_p, rows_s, win_p_out, win_s_out, mem_kv_prompt)
```

```python
import functools
import math

import jax
import jax.numpy as jnp
from jax import lax
from jax.experimental import pallas as pl
from jax.experimental.pallas import tpu as pltpu

D_MODEL = 1024
DEPTH = 4
PAGE_SIZE = 128
N_A = DEPTH // 2
GLA_HEADS = 4
GLA_DV = (3 * D_MODEL) // (4 * GLA_HEADS)
GLA_DK = GLA_DV // 2
GLA_RANK = 16
GLA_TAU = 16.0
GLA_CHUNK = 64
HD = 64
NSA_HEADS = (3 * D_MODEL) // (4 * HD)
NSA_KV = 2
NSA_HPG = NSA_HEADS // NSA_KV
CMP_S = 16
CMP_L = 2 * CMP_S
SLC_L = 64
SLC_K = 16
WINDOW = 512
Q_BLK = 128
MEM_TOK = 256
MEM_HEADS = 4
MEM_HD = D_MODEL // (4 * MEM_HEADS)
N_EXP = 32
TOP_K = 4
D_FF = D_MODEL
SWIGLU_LIMIT = 7.0
SWIGLU_ALPHA = 1.702
GLA_W = GLA_HEADS * GLA_DV
NSA_W = NSA_HEADS * HD
MEM_W = MEM_HEADS * MEM_HD
DN_ALPHA = (2 * DEPTH) ** 0.25
LN_EPS = 1e-5
NEG = -1e30

f32, bf16, i32 = jnp.float32, jnp.bfloat16, jnp.int32

VMEM_LIMIT_BYTES = 56 * 1024 * 1024
LANES = 128
ROW_TILE = 512
MOE_ROWS = 256
FF_CHUNK = 512
SLC_TILE = 512
WIN_KEYS = WINDOW + Q_BLK
NSB_PAD = 64
TOK_TILE = 256
GLA_KP = 128
GLA_VP = 256
GLA_BT = 8
_NT = (((1,), (1,)), ((), ()))
_TN = (((0,), (0,)), ((), ()))


def _cparams(sem):
    return pltpu.CompilerParams(dimension_semantics=sem, vmem_limit_bytes=VMEM_LIMIT_BYTES)


def _alibi_slopes(n):
    def pow2(m):
        start = 2.0 ** (-8.0 / m)
        return [start ** (i + 1) for i in range(m)]
    if math.log2(n).is_integer():
        return pow2(n)
    c = 2 ** math.floor(math.log2(n))
    return pow2(c) + pow2(2 * c)[0::2][: n - c]


SLOPES = _alibi_slopes(NSA_HEADS)
SLC_SHIFT = int(math.log2(SLC_L))
RATIO_SHIFT = int(math.log2(SLC_L // CMP_S))


def _proj_kernel(x_ref, *refs, n_w, outs):
    xb = x_ref[...].astype(bf16)
    res = {}
    for o_ref, (wi, lo, hi) in zip(refs[n_w:], outs):
        if wi not in res:
            res[wi] = jnp.dot(xb, refs[wi][...], preferred_element_type=f32)
        o_ref[...] = res[wi][:, lo:hi].astype(o_ref.dtype)


def _proj(x, ws, outs, tm=ROW_TILE):
    m, k = x.shape
    tm = min(tm, m)
    in_specs = [pl.BlockSpec((tm, k), lambda i: (i, 0))]
    in_specs += [pl.BlockSpec(w.shape, lambda i: (0, 0)) for w in ws]
    out_specs = [pl.BlockSpec((tm, hi - lo), lambda i: (i, 0)) for _, lo, hi, _ in outs]
    out_shape = [jax.ShapeDtypeStruct((m, hi - lo), dt) for _, lo, hi, dt in outs]
    return pl.pallas_call(
        functools.partial(_proj_kernel, n_w=len(ws), outs=tuple(o[:3] for o in outs)),
        grid=(m // tm,),
        in_specs=in_specs,
        out_specs=out_specs,
        out_shape=out_shape,
        compiler_params=_cparams(("parallel",)),
        name="proj",
    )(x, *ws)


def _whole(ws, dtype=f32):
    return [(i, 0, w.shape[1], dtype) for i, w in enumerate(ws)]


def _layer_norm_rows(y, g, b):
    mu = jnp.mean(y, axis=-1, keepdims=True)
    d = y - mu
    var = jnp.mean(d * d, axis=-1, keepdims=True)
    return d * lax.rsqrt(var + LN_EPS) * g + b


def _mix_ln_kernel(*refs, n_parts):
    a_refs, w_refs = refs[:n_parts], refs[n_parts:2 * n_parts]
    x_ref, g_ref, b_ref, wr_ref, br_ref, x1_ref, gate_ref, idx_ref = refs[2 * n_parts:]
    mix = jnp.dot(a_refs[0][...].astype(bf16), w_refs[0][...], preferred_element_type=f32)
    for a_ref, w_ref in zip(a_refs[1:], w_refs[1:]):
        mix = mix + jnp.dot(a_ref[...].astype(bf16), w_ref[...], preferred_element_type=f32)
    x1 = _layer_norm_rows(DN_ALPHA * x_ref[...] + mix, g_ref[...], b_ref[...])
    x1_ref[...] = x1
    lg = jnp.dot(x1, wr_ref[...], preferred_element_type=f32, precision=lax.Precision.HIGHEST) + br_ref[...]
    lane = lax.broadcasted_iota(i32, lg.shape, 1)
    vals, idxs = [], []
    for _ in range(TOP_K):
        m = jnp.max(lg, axis=-1, keepdims=True)
        i = jnp.min(jnp.where(lg == m, lane, N_EXP), axis=-1, keepdims=True)
        vals.append(m)
        idxs.append(i)
        lg = jnp.where(lane == i, -jnp.inf, lg)
    es = [jnp.exp(v - vals[0]) for v in vals]
    tot = es[0] + es[1] + es[2] + es[3]
    col = lax.broadcasted_iota(i32, (lg.shape[0], TOP_K), 1)
    gate = jnp.zeros((lg.shape[0], TOP_K), f32)
    idx = jnp.zeros((lg.shape[0], TOP_K), i32)
    for k in range(TOP_K):
        gate = jnp.where(col == k, es[k] / tot, gate)
        idx = jnp.where(col == k, idxs[k], idx)
    gate_ref[...] = gate
    idx_ref[...] = idx


def _mix_ln(parts, x, g, b, w_r, b_r, tm=ROW_TILE):
    m, d = x.shape
    row = lambda i: (i, 0)
    fix = lambda i: (0, 0)
    in_specs = [pl.BlockSpec((tm, a.shape[1]), row) for a, _ in parts]
    in_specs += [pl.BlockSpec(w.shape, fix) for _, w in parts]
    in_specs += [pl.BlockSpec((tm, d), row), pl.BlockSpec((1, d), fix), pl.BlockSpec((1, d), fix),
                 pl.BlockSpec((d, N_EXP), fix), pl.BlockSpec((1, N_EXP), fix)]
    return pl.pallas_call(
        functools.partial(_mix_ln_kernel, n_parts=len(parts)),
        grid=(m // tm,),
        in_specs=in_specs,
        out_specs=[pl.BlockSpec((tm, d), row), pl.BlockSpec((tm, TOP_K), row), pl.BlockSpec((tm, TOP_K), row)],
        out_shape=[jax.ShapeDtypeStruct((m, d), f32), jax.ShapeDtypeStruct((m, TOP_K), f32),
                   jax.ShapeDtypeStruct((m, TOP_K), i32)],
        compiler_params=_cparams(("parallel",)),
        name="mix_ln",
    )(*[a for a, _ in parts], *[w for _, w in parts], x, g.reshape(1, d), b.reshape(1, d), w_r,
      b_r.reshape(1, N_EXP))


def _moe_kernel(be_ref, nreal_ref, x_ref, w1_ref, b1_ref, w2_ref, b2_ref, o_ref, w1s, w2s):
    i = pl.program_id(0)
    real = i < nreal_ref[0]
    prev = be_ref[jnp.maximum(i - 1, 0)]
    fresh = jnp.logical_or(i == 0, be_ref[i] != prev)

    @pl.when(jnp.logical_and(real, fresh))
    def _():
        w1s[...] = w1_ref[0].astype(bf16)
        w2s[...] = w2_ref[0].astype(bf16)

    @pl.when(real)
    def _():
        x = x_ref[...].astype(bf16)
        acc = jnp.zeros((MOE_ROWS, D_MODEL), f32)
        for c in range(D_FF // FF_CHUNK):
            lo = c * FF_CHUNK
            hg = jnp.dot(x, w1s[:, lo:lo + FF_CHUNK], preferred_element_type=f32)
            hg = hg + b1_ref[0, :, lo:lo + FF_CHUNK]
            hu = jnp.dot(x, w1s[:, D_FF + lo:D_FF + lo + FF_CHUNK], preferred_element_type=f32)
            hu = hu + b1_ref[0, :, D_FF + lo:D_FF + lo + FF_CHUNK]
            g = jnp.minimum(hg, SWIGLU_LIMIT)
            u = jnp.clip(hu, -SWIGLU_LIMIT, SWIGLU_LIMIT)
            a = (u + 1.0) * g * jax.nn.sigmoid(SWIGLU_ALPHA * g)
            acc = acc + jnp.dot(a.astype(bf16), w2s[lo:lo + FF_CHUNK, :], preferred_element_type=f32)
        o_ref[...] = acc + b2_ref[0]

    @pl.when(jnp.logical_not(real))
    def _():
        o_ref[...] = jnp.zeros_like(o_ref)


def _moe_ffn_blocks(xb, blk_exp, nreal, w1, b1, w2, b2):
    p, d = xb.shape
    nblk = p // MOE_ROWS
    grid_spec = pltpu.PrefetchScalarGridSpec(
        num_scalar_prefetch=2,
        grid=(nblk,),
        in_specs=[
            pl.BlockSpec((MOE_ROWS, d), lambda i, be, nr: (i, 0)),
            pl.BlockSpec((1, d, 2 * D_FF), lambda i, be, nr: (be[i], 0, 0)),
            pl.BlockSpec((1, 1, 2 * D_FF), lambda i, be, nr: (be[i], 0, 0)),
            pl.BlockSpec((1, D_FF, d), lambda i, be, nr: (be[i], 0, 0)),
            pl.BlockSpec((1, 1, d), lambda i, be, nr: (be[i], 0, 0)),
        ],
        out_specs=pl.BlockSpec((MOE_ROWS, d), lambda i, be, nr: (i, 0)),
        scratch_shapes=[pltpu.VMEM((d, 2 * D_FF), bf16), pltpu.VMEM((D_FF, d), bf16)],
    )
    return pl.pallas_call(
        _moe_kernel,
        grid_spec=grid_spec,
        out_shape=jax.ShapeDtypeStruct((p, d), f32),
        compiler_params=_cparams(("arbitrary",)),
        name="moe_ffn",
    )(blk_exp, nreal, xb, w1, b1.reshape(N_EXP, 1, 2 * D_FF), w2, b2.reshape(N_EXP, 1, d))


def _row_copy(src, src_row, dst, dst_row, sem):
    return pltpu.make_async_copy(src.at[pl.ds(src_row, 1), :], dst.at[pl.ds(dst_row, 1), :], sem)


def _moe_dispatch_kernel(dest_ref, x_ref, xb_init_ref, xb_ref, sem):
    del xb_init_ref

    def issue(t, c):
        for k in range(TOP_K):
            _row_copy(x_ref, t, xb_ref, dest_ref[0, 0, t * TOP_K + k], sem).start(priority=k % 2)
        return c

    lax.fori_loop(0, TOK_TILE, issue, 0, unroll=2)

    def drain(t, c):
        for k in range(TOP_K):
            _row_copy(x_ref, 0, xb_ref, 0, sem).wait()
        return c

    lax.fori_loop(0, TOK_TILE, drain, 0, unroll=2)


def _moe_dispatch(x1, dest, xb):
    n, d = x1.shape
    nt = n // TOK_TILE
    return pl.pallas_call(
        _moe_dispatch_kernel,
        grid=(nt,),
        in_specs=[pl.BlockSpec((1, 1, TOK_TILE * TOP_K), lambda i: (i, 0, 0), memory_space=pltpu.SMEM),
                  pl.BlockSpec((TOK_TILE, d), lambda i: (i, 0)),
                  pl.BlockSpec(memory_space=pl.ANY)],
        out_specs=pl.BlockSpec(memory_space=pl.ANY),
        out_shape=jax.ShapeDtypeStruct(xb.shape, f32),
        scratch_shapes=[pltpu.SemaphoreType.DMA(())],
        input_output_aliases={2: 0},
        compiler_params=_cparams(("arbitrary",)),
        name="moe_dispatch",
    )(dest.reshape(nt, 1, TOK_TILE * TOP_K), x1, xb)


def _moe_combine_kernel(dest_ref, yb_ref, gate_ref, x1_ref, g_ref, b_ref, o_ref, ybuf, sem):
    def issue(t, c):
        for k in range(TOP_K):
            _row_copy(yb_ref, dest_ref[0, 0, t * TOP_K + k], ybuf.at[k], t, sem).start(priority=k % 2)
        return c

    lax.fori_loop(0, TOK_TILE, issue, 0, unroll=2)

    def drain(t, c):
        for k in range(TOP_K):
            _row_copy(yb_ref, 0, ybuf.at[k], 0, sem).wait()
        return c

    lax.fori_loop(0, TOK_TILE, drain, 0, unroll=2)
    gate = gate_ref[...]
    y = gate[:, 0:1] * ybuf[0]
    for k in range(1, TOP_K):
        y = y + gate[:, k:k + 1] * ybuf[k]
    o_ref[...] = _layer_norm_rows(DN_ALPHA * x1_ref[...] + y, g_ref[...], b_ref[...])


def _moe_combine(yb, dest, gate, x1, g, b):
    n, d = x1.shape
    nt = n // TOK_TILE
    row = lambda i: (i, 0)
    fix = lambda i: (0, 0)
    return pl.pallas_call(
        _moe_combine_kernel,
        grid=(nt,),
        in_specs=[pl.BlockSpec((1, 1, TOK_TILE * TOP_K), lambda i: (i, 0, 0), memory_space=pltpu.SMEM),
                  pl.BlockSpec(memory_space=pl.ANY),
                  pl.BlockSpec((TOK_TILE, TOP_K), row), pl.BlockSpec((TOK_TILE, d), row),
                  pl.BlockSpec((1, d), fix), pl.BlockSpec((1, d), fix)],
        out_specs=pl.BlockSpec((TOK_TILE, d), row),
        out_shape=jax.ShapeDtypeStruct((n, d), f32),
        scratch_shapes=[pltpu.VMEM((TOP_K, TOK_TILE, d), f32), pltpu.SemaphoreType.DMA(())],
        compiler_params=_cparams(("arbitrary",)),
        name="moe_combine",
    )(dest.reshape(nt, 1, TOK_TILE * TOP_K), yb, gate, x1, g.reshape(1, d), b.reshape(1, d))


def _moe_blocks(a):
    return -(-a // MOE_ROWS) + N_EXP


def _moe_slots(flat_e):
    a = flat_e.shape[0]
    onehot = (flat_e[:, None] == jnp.arange(N_EXP, dtype=i32)[None, :]).astype(i32)
    csum = jnp.cumsum(onehot, axis=0)
    counts = csum[-1]
    rank = jnp.take_along_axis(csum, flat_e[:, None], axis=1)[:, 0] - 1
    padded = (counts + MOE_ROWS - 1) // MOE_ROWS * MOE_ROWS
    pend = jnp.cumsum(padded)
    dest = (pend - padded)[flat_e] + rank
    blk_start = jnp.arange(_moe_blocks(a), dtype=i32) * MOE_ROWS
    nreal = (pend[-1] // MOE_ROWS).astype(i32)
    blk_exp = jnp.searchsorted(pend, blk_start, side='right').astype(i32)
    last_exp = blk_exp[jnp.maximum(nreal - 1, 0)]
    blk_exp = jnp.where(blk_start < pend[-1], jnp.minimum(blk_exp, N_EXP - 1), last_exp)
    return dest, blk_exp, nreal.reshape(1)


def _mem_attn_kernel(q_ref, kv_ref, o_ref):
    q = q_ref[0]
    kv = kv_ref[0].astype(bf16)
    tq = q.shape[0]
    lane = lax.broadcasted_iota(i32, (tq, LANES), 1)
    zero = jnp.zeros((tq, LANES), bf16)
    for pair in range(MEM_HEADS // 2):
        qt = q[:, LANES * pair:LANES * (pair + 1)]
        kt = kv[:, LANES * pair:LANES * (pair + 1)]
        vt = kv[:, MEM_W + LANES * pair:MEM_W + LANES * (pair + 1)]
        halves = []
        for h in range(2):
            keep = (lane < MEM_HD) if h == 0 else (lane >= MEM_HD)
            s = lax.dot_general(jnp.where(keep, qt, zero), kt, _NT, preferred_element_type=f32)
            m = jnp.max(s, axis=-1, keepdims=True)
            e = jnp.exp(s - m)
            p = e / jnp.sum(e, axis=-1, keepdims=True)
            halves.append(jnp.dot(p.astype(bf16), vt, preferred_element_type=f32))
        o_ref[0, :, LANES * pair:LANES * (pair + 1)] = jnp.where(lane < MEM_HD, halves[0], halves[1])


def _mem_attn(mq, mem_kv):
    b, t, w = mq.shape
    tq = min(t, ROW_TILE)
    return pl.pallas_call(
        _mem_attn_kernel,
        grid=(b, t // tq),
        in_specs=[pl.BlockSpec((1, tq, w), lambda bb, i: (bb, i, 0)),
                  pl.BlockSpec((1, MEM_TOK, 2 * w), lambda bb, i: (bb, 0, 0))],
        out_specs=pl.BlockSpec((1, tq, w), lambda bb, i: (bb, i, 0)),
        out_shape=jax.ShapeDtypeStruct((b, t, w), f32),
        compiler_params=_cparams(("parallel", "parallel")),
        name="mem_attn",
    )(mq, mem_kv)


def _pad_state(s):
    s = jnp.concatenate([s, jnp.zeros((GLA_KP - GLA_DK, GLA_DV), f32)], axis=0)
    return jnp.concatenate([s, jnp.zeros((GLA_KP, GLA_VP - GLA_DV), f32)], axis=1)


def _gla_kernel(q_ref, k_ref, v_ref, r_ref, glr_ref, wg_ref, bg_ref, gn_ref, s0_ref, o_ref, s_out_ref, st_sc,
                *, bt, chunk, zero_init):
    c = pl.program_id(1)

    @pl.when(c == 0)
    def _():
        if zero_init:
            st_sc[...] = jnp.zeros(st_sc.shape, f32)
        else:
            for b in range(bt):
                for h in range(GLA_HEADS):
                    st_sc[b, h] = jnp.transpose(_pad_state(s0_ref[b, h]))

    tri = lax.broadcasted_iota(i32, (chunk, chunk), 0) >= lax.broadcasted_iota(i32, (chunk, chunk), 1)
    trif = tri.astype(f32)
    for b in range(bt):
        glr = glr_ref[b].astype(bf16)
        for h in range(GLA_HEADS):
            ks = slice(GLA_KP * h, GLA_KP * (h + 1))
            vs = slice(GLA_VP * h, GLA_VP * (h + 1))
            z = jnp.dot(glr, wg_ref[:, ks], preferred_element_type=f32) + bg_ref[:, ks]
            log_a = jax.nn.log_sigmoid(z) / GLA_TAU
            bc = jnp.dot(trif, log_a, preferred_element_type=f32, precision=lax.Precision.HIGHEST)
            bl = bc[chunk - 1:chunk, :]
            q = q_ref[b, :, ks] * (GLA_DK ** -0.5)
            k = k_ref[b, :, ks]
            v = v_ref[b, :, vs].astype(bf16)
            qt = (q * jnp.exp(bc)).astype(bf16)
            kt = (k * jnp.exp(-bc)).astype(bf16)
            kd = (k * jnp.exp(bl - bc)).astype(bf16)
            att = jnp.where(tri, lax.dot_general(qt, kt, _NT, preferred_element_type=f32), 0.0)
            st = st_sc[b, h]
            o = lax.dot_general(qt, st.astype(bf16), _NT, preferred_element_type=f32)
            o = o + jnp.dot(att.astype(bf16), v, preferred_element_type=f32)
            st_sc[b, h] = jnp.exp(bl) * st + lax.dot_general(v, kd, _TN, preferred_element_type=f32)
            ms = jnp.sum(o * o, axis=-1, keepdims=True) * (1.0 / GLA_DV)
            o = o * lax.rsqrt(ms + LN_EPS) * gn_ref[:, vs]
            o_ref[b, :, vs] = o * jax.nn.silu(r_ref[b, :, vs])

    @pl.when(c == pl.num_programs(1) - 1)
    def _():
        for b in range(bt):
            for h in range(GLA_HEADS):
                s_out_ref[b, h] = jnp.transpose(st_sc[b, h])[:GLA_DK, :GLA_DV]


def _gla(q, k, v, r, glr, wg, bg, gn, s0, *, bt):
    nb, t, _ = q.shape
    chunk = GLA_CHUNK if t % GLA_CHUNK == 0 else t
    zero_init = s0 is None
    if zero_init:
        s0 = jnp.zeros((1, 1, 8, LANES), f32)
    tok = lambda w: pl.BlockSpec((bt, chunk, w), lambda i, c: (i, c, 0))
    fix = lambda a: pl.BlockSpec(a.shape, lambda i, c: (0,) * a.ndim)
    state = pl.BlockSpec((bt, GLA_HEADS, GLA_DK, GLA_DV), lambda i, c: (i, 0, 0, 0))
    return pl.pallas_call(
        functools.partial(_gla_kernel, bt=bt, chunk=chunk, zero_init=zero_init),
        grid=(nb // bt, t // chunk),
        in_specs=[tok(GLA_HEADS * GLA_KP), tok(GLA_HEADS * GLA_KP), tok(GLA_HEADS * GLA_VP), tok(GLA_HEADS * GLA_VP),
                  tok(LANES), fix(wg), fix(bg), fix(gn), fix(s0) if zero_init else state],
        out_specs=[tok(GLA_HEADS * GLA_VP), state],
        out_shape=[jax.ShapeDtypeStruct((nb, t, GLA_HEADS * GLA_VP), f32),
                   jax.ShapeDtypeStruct((nb, GLA_HEADS, GLA_DK, GLA_DV), f32)],
        scratch_shapes=[pltpu.VMEM((bt, GLA_HEADS, GLA_VP, GLA_KP), f32)],
        compiler_params=_cparams(("parallel", "arbitrary")),
        name="gla",
    )(q, k, v, r, glr, wg, bg, gn, s0)


def _pad_heads(a, w, wp):
    lead = a.shape[:-1]
    a = jnp.pad(a.reshape(*lead, GLA_HEADS, w), [(0, 0)] * (len(lead) + 1) + [(0, wp - w)])
    return a.reshape(*lead, GLA_HEADS * wp)


def _softmax_rows(sm, maskf):
    m = jnp.max(sm, axis=-1, keepdims=True)
    e = jnp.exp(sm - m)
    return e / jnp.sum(e, axis=-1, keepdims=True) * maskf


def _block_scores_t(imp):
    nc = imp.shape[1]
    jj = lax.broadcasted_iota(i32, (NSB_PAD, nc), 0)
    nn = lax.broadcasted_iota(i32, (NSB_PAD, nc), 1)
    mt = ((nn >> RATIO_SHIFT) == jj).astype(f32) + (((nn + 1) >> RATIO_SHIFT) == jj).astype(f32)
    return lax.dot_general(mt, imp, _NT, preferred_element_type=f32, precision=lax.Precision.HIGHEST)


def _select_t(blk_t, qpos_t):
    nq = blk_t.shape[1]
    j_t = lax.broadcasted_iota(i32, (NSB_PAD, nq), 0)
    valid = j_t * SLC_L <= qpos_t
    cur = qpos_t >> SLC_SHIFT
    forced = (j_t == 0) | (j_t == cur) | (j_t == cur - 1)
    score = jnp.where(valid, jnp.where(forced, 1e9, blk_t), -1e9)
    cnt = jnp.zeros((NSB_PAD, nq), i32)
    for k in range(NSB_PAD):
        row = score[k:k + 1, :]
        beats = (row > score) | ((row == score) & (j_t > k))
        cnt = cnt + beats.astype(i32)
    return ((cnt < SLC_K) & valid).astype(f32)


def _untranspose(sel_t):
    nq = sel_t.shape[1]
    eye = lax.broadcasted_iota(i32, (nq, nq), 0) == lax.broadcasted_iota(i32, (nq, nq), 1)
    return lax.dot_general(eye.astype(bf16), sel_t.astype(bf16), _NT, preferred_element_type=f32).astype(bf16)


def _nsa_prompt_kernel(qp_ref, gl_ref, bg_ref, kc_ref, vc_ref, ks_ref, vs_ref, kw_ref, vw_ref, o_ref,
                       p_sc, m_sc, l_sc, acc_sc, og_sc, used_sm):
    nc = kc_ref.shape[1]
    q0 = pl.program_id(1) * Q_BLK
    qpos_i = q0 + lax.broadcasted_iota(i32, (Q_BLK, 1), 0)
    lane = lax.broadcasted_iota(i32, (Q_BLK, LANES), 1)
    gates = jax.nn.sigmoid(gl_ref[0] + bg_ref[...])
    qp = qp_ref[0]
    zero_b = jnp.zeros((Q_BLK, LANES), bf16)

    for g in range(NSA_KV):
        half = (lane < HD) if g == 0 else (lane >= HD)
        qg = jnp.concatenate([jnp.where(half, qp[:, LANES * hh:LANES * (hh + 1)], zero_b)
                              for hh in range(NSA_HPG)], axis=0)
        slopes = [SLOPES[NSA_HPG * g + hh] for hh in range(NSA_HPG)]

        n_idx = lax.broadcasted_iota(i32, (Q_BLK, nc), 1)
        cd = (qpos_i - (n_idx * CMP_S + (CMP_L - 1))).astype(f32)
        cmask = cd >= 0
        cmaskf = cmask.astype(f32)
        s = lax.dot_general(qg, kc_ref[0], _NT, preferred_element_type=f32)
        imp = jnp.zeros((Q_BLK, nc), f32)
        for hh in range(NSA_HPG):
            rows = slice(hh * Q_BLK, (hh + 1) * Q_BLK)
            p = _softmax_rows(jnp.where(cmask, s[rows] - slopes[hh] * cd, NEG), cmaskf)
            imp = imp + p
            p_sc[rows, :nc] = p.astype(bf16)
        o_cmp = jnp.dot(p_sc[:, :nc], vc_ref[0], preferred_element_type=f32)

        qpos_t = q0 + lax.broadcasted_iota(i32, (NSB_PAD, Q_BLK), 1)
        sel_t = _select_t(_block_scores_t(imp), qpos_t)
        sel_b = _untranspose(sel_t)
        blocks_per_tile = SLC_TILE // SLC_L
        for t in range(NSB_PAD // blocks_per_tile):
            used_sm[t] = (jnp.max(sel_t[t * blocks_per_tile:(t + 1) * blocks_per_tile, :]) > 0.5).astype(i32)

        m_sc[...] = jnp.full(m_sc.shape, NEG, f32)
        l_sc[...] = jnp.zeros(l_sc.shape, f32)
        acc_sc[...] = jnp.zeros(acc_sc.shape, f32)

        def tile(t, carry):
            @pl.when(used_sm[t] > 0)
            def _():
                k0 = pl.multiple_of(t * SLC_TILE, SLC_TILE)
                kt = ks_ref[0, pl.ds(k0, SLC_TILE), :]
                vt = vs_ref[0, pl.ds(k0, SLC_TILE), :]
                st = lax.dot_general(qg, kt, _NT, preferred_element_type=f32)
                kpos = k0 + lax.broadcasted_iota(i32, (Q_BLK, SLC_TILE), 1)
                blk_of_key = (k0 + lax.broadcasted_iota(i32, (NSB_PAD, SLC_TILE), 1)) >> SLC_SHIFT
                expand = (lax.broadcasted_iota(i32, (NSB_PAD, SLC_TILE), 0) == blk_of_key).astype(bf16)
                selexp = jnp.dot(sel_b, expand, preferred_element_type=f32)
                addmask = jnp.where((qpos_i >= kpos) & (selexp > 0.5), 0.0, NEG)
                krel = (k0 - q0 + lax.broadcasted_iota(i32, (1, SLC_TILE), 1)).astype(f32)
                for hh in range(NSA_HPG):
                    rows = slice(hh * Q_BLK, (hh + 1) * Q_BLK)
                    sm = st[rows] + slopes[hh] * krel + addmask
                    m_old = m_sc[rows]
                    m_new = jnp.maximum(m_old, jnp.max(sm, axis=-1, keepdims=True))
                    a = jnp.exp(m_old - m_new)
                    e = jnp.exp(sm - m_new)
                    l_sc[rows] = a * l_sc[rows] + jnp.sum(e, axis=-1, keepdims=True)
                    m_sc[rows] = m_new
                    acc_sc[rows] = a * acc_sc[rows]
                    p_sc[rows, :SLC_TILE] = e.astype(bf16)
                acc_sc[...] += jnp.dot(p_sc[:, :SLC_TILE], vt, preferred_element_type=f32)
            return carry

        lax.fori_loop(0, (q0 + Q_BLK + SLC_TILE - 1) // SLC_TILE, tile, 0)
        o_slc = acc_sc[...] / l_sc[...]

        start = pl.multiple_of(jnp.maximum(q0 - WINDOW, 0), Q_BLK)
        kt = kw_ref[0, pl.ds(start, WIN_KEYS), :]
        vt = vw_ref[0, pl.ds(start, WIN_KEYS), :]
        st = lax.dot_general(qg, kt, _NT, preferred_element_type=f32)
        wd = (qpos_i - (start + lax.broadcasted_iota(i32, (Q_BLK, WIN_KEYS), 1))).astype(f32)
        wmask = (wd >= 0) & (wd <= WINDOW)
        wmaskf = wmask.astype(f32)
        for hh in range(NSA_HPG):
            rows = slice(hh * Q_BLK, (hh + 1) * Q_BLK)
            p = _softmax_rows(jnp.where(wmask, st[rows] - slopes[hh] * wd, NEG), wmaskf)
            p_sc[rows, :WIN_KEYS] = p.astype(bf16)
        o_win = jnp.dot(p_sc[:, :WIN_KEYS], vt, preferred_element_type=f32)

        for hh in range(NSA_HPG):
            rows = slice(hh * Q_BLK, (hh + 1) * Q_BLK)
            c = 3 * (NSA_HPG * g + hh)
            og_sc[g, rows] = (gates[:, c:c + 1] * o_cmp[rows] + gates[:, c + 1:c + 2] * o_slc[rows]
                              + gates[:, c + 2:c + 3] * o_win[rows])

    for hh in range(NSA_HPG):
        rows = slice(hh * Q_BLK, (hh + 1) * Q_BLK)
        o_ref[0, :, LANES * hh:LANES * (hh + 1)] = jnp.where(lane < HD, og_sc[0, rows], og_sc[1, rows])


def _nsa_prompt_call(qp, gl, bg, kc, vc, ks, vs, kw, vw):
    b, t, w = qp.shape
    nc = kc.shape[1]
    assert t % SLC_TILE == 0 and t >= WIN_KEYS and nc % LANES == 0 and t // SLC_L <= NSB_PAD
    rows = NSA_HPG * Q_BLK
    qblk = lambda bb, i: (bb, i, 0)
    whole = lambda bb, i: (bb, 0, 0)
    return pl.pallas_call(
        _nsa_prompt_kernel,
        grid=(b, t // Q_BLK),
        in_specs=[pl.BlockSpec((1, Q_BLK, w), qblk), pl.BlockSpec((1, Q_BLK, LANES), qblk),
                  pl.BlockSpec((1, LANES), lambda bb, i: (0, 0)),
                  pl.BlockSpec((1, nc, LANES), whole), pl.BlockSpec((1, nc, LANES), whole),
                  pl.BlockSpec((1, t, LANES), whole), pl.BlockSpec((1, t, LANES), whole),
                  pl.BlockSpec((1, t, LANES), whole), pl.BlockSpec((1, t, LANES), whole)],
        out_specs=pl.BlockSpec((1, Q_BLK, w), qblk),
        out_shape=jax.ShapeDtypeStruct((b, t, w), f32),
        scratch_shapes=[pltpu.VMEM((rows, WIN_KEYS), bf16), pltpu.VMEM((rows, 1), f32), pltpu.VMEM((rows, 1), f32),
                        pltpu.VMEM((rows, LANES), f32), pltpu.VMEM((NSA_KV, rows, LANES), f32),
                        pltpu.SMEM((NSB_PAD * SLC_L // SLC_TILE,), i32)],
        compiler_params=_cparams(("parallel", "parallel")),
        name="nsa_prompt",
    )(qp, gl, bg, kc, vc, ks, vs, kw, vw)


def _nsa_sample_kernel(pt_ref, qp_ref, gl_ref, bg_ref, kc_ref, vc_ref, kn_ref, vn_ref, wst_ref, wkn_ref, wvn_ref,
                       *rest, n_pages, past_len):
    page_refs = rest[:n_pages]
    o_ref, k_sc, v_sc = rest[n_pages:]
    nq = qp_ref.shape[1]
    nc = kc_ref.shape[1]
    nkeys = (n_pages + 1) * PAGE_SIZE
    rows_g = NSA_HPG * nq
    qpos_i = past_len + lax.broadcasted_iota(i32, (nq, 1), 0)
    lane = lax.broadcasted_iota(i32, (nq, LANES), 1)
    gates = jax.nn.sigmoid(gl_ref[0] + bg_ref[...])
    qp = qp_ref[0]
    zero_b = jnp.zeros((nq, LANES), bf16)
    qs = jnp.concatenate([jnp.where((lane < HD) if g == 0 else (lane >= HD),
                                    qp[:, LANES * hh:LANES * (hh + 1)], zero_b)
                          for g in range(NSA_KV) for hh in range(NSA_HPG)], axis=0)

    def rows_of(g, hh):
        r0 = g * rows_g + hh * nq
        return slice(r0, r0 + nq)

    n_idx = lax.broadcasted_iota(i32, (nq, nc), 1)
    cd = (qpos_i - (n_idx * CMP_S + (CMP_L - 1))).astype(f32)
    cmask = cd >= 0
    cmaskf = cmask.astype(f32)
    s = lax.dot_general(qs, kc_ref[0], _NT, preferred_element_type=f32)
    imps, ps = [], []
    for g in range(NSA_KV):
        imp = jnp.zeros((nq, nc), f32)
        for hh in range(NSA_HPG):
            p = _softmax_rows(jnp.where(cmask, s[rows_of(g, hh)] - SLOPES[NSA_HPG * g + hh] * cd, NEG), cmaskf)
            imp = imp + p
            ps.append(p)
        imps.append(imp)
    o_cmp = jnp.dot(jnp.concatenate(ps, axis=0).astype(bf16), vc_ref[0], preferred_element_type=f32)

    qpos_t = past_len + lax.broadcasted_iota(i32, (NSB_PAD, nq), 1)
    blk_of_key = lax.broadcasted_iota(i32, (NSB_PAD, nkeys), 1) >> SLC_SHIFT
    expand = (lax.broadcasted_iota(i32, (NSB_PAD, nkeys), 0) == blk_of_key).astype(bf16)
    sd = (qpos_i - lax.broadcasted_iota(i32, (nq, nkeys), 1)).astype(f32)
    masks = []
    for g in range(NSA_KV):
        sel_b = _untranspose(_select_t(_block_scores_t(imps[g]), qpos_t))
        selexp = jnp.dot(sel_b, expand, preferred_element_type=f32)
        masks.append((sd >= 0) & (selexp > 0.5))

    for pg in range(n_pages):
        blk = page_refs[pg][0]
        k_sc[pg * PAGE_SIZE:(pg + 1) * PAGE_SIZE, :] = blk[:, :LANES].astype(bf16)
        v_sc[pg * PAGE_SIZE:(pg + 1) * PAGE_SIZE, :] = blk[:, LANES:].astype(bf16)
    tail = jnp.zeros((PAGE_SIZE - nq, LANES), bf16)
    k_sc[n_pages * PAGE_SIZE:, :] = jnp.concatenate([kn_ref[0], tail], axis=0)
    v_sc[n_pages * PAGE_SIZE:, :] = jnp.concatenate([vn_ref[0], tail], axis=0)
    s = lax.dot_general(qs, k_sc[...], _NT, preferred_element_type=f32)
    ps = []
    for g in range(NSA_KV):
        mf = masks[g].astype(f32)
        for hh in range(NSA_HPG):
            ps.append(_softmax_rows(jnp.where(masks[g], s[rows_of(g, hh)] - SLOPES[NSA_HPG * g + hh] * sd, NEG), mf))
    o_slc = jnp.dot(jnp.concatenate(ps, axis=0).astype(bf16), v_sc[...], preferred_element_type=f32)

    wb = wst_ref.shape[1]
    wk = jnp.concatenate([wst_ref[0, :, :LANES].astype(bf16), wkn_ref[0], tail], axis=0)
    wv = jnp.concatenate([wst_ref[0, :, LANES:].astype(bf16), wvn_ref[0], tail], axis=0)
    wpos = (past_len - wb) + lax.broadcasted_iota(i32, (nq, wb + PAGE_SIZE), 1)
    wd = (qpos_i - wpos).astype(f32)
    wmask = (wd >= 0) & (wd <= WINDOW) & (wpos >= 0)
    wmaskf = wmask.astype(f32)
    s = lax.dot_general(qs, wk, _NT, preferred_element_type=f32)
    ps = []
    for g in range(NSA_KV):
        for hh in range(NSA_HPG):
            ps.append(_softmax_rows(jnp.where(wmask, s[rows_of(g, hh)] - SLOPES[NSA_HPG * g + hh] * wd, NEG), wmaskf))
    o_win = jnp.dot(jnp.concatenate(ps, axis=0).astype(bf16), wv, preferred_element_type=f32)

    for hh in range(NSA_HPG):
        halves = []
        for g in range(NSA_KV):
            r = rows_of(g, hh)
            c = 3 * (NSA_HPG * g + hh)
            halves.append(gates[:, c:c + 1] * o_cmp[r] + gates[:, c + 1:c + 2] * o_slc[r]
                          + gates[:, c + 2:c + 3] * o_win[r])
        o_ref[0, :, LANES * hh:LANES * (hh + 1)] = jnp.where(lane < HD, halves[0], halves[1])


def _nsa_sample_call(page_table, qp, gl, bg, kc, vc, kn, vn, wst, wkn, wvn, cache):
    b, nq, w = qp.shape
    n_pages = page_table.shape[1]
    nc = kc.shape[1]
    wb = wst.shape[1]
    nkeys = (n_pages + 1) * PAGE_SIZE
    assert nkeys // SLC_L <= NSB_PAD and nq <= PAGE_SIZE
    per_b = lambda bb, pt: (bb, 0, 0)
    in_specs = [pl.BlockSpec((1, nq, w), per_b), pl.BlockSpec((1, nq, LANES), per_b),
                pl.BlockSpec((1, LANES), lambda bb, pt: (0, 0)),
                pl.BlockSpec((1, nc, LANES), per_b), pl.BlockSpec((1, nc, LANES), per_b),
                pl.BlockSpec((1, nq, LANES), per_b), pl.BlockSpec((1, nq, LANES), per_b),
                pl.BlockSpec((1, wb, 2 * LANES), per_b),
                pl.BlockSpec((1, nq, LANES), per_b), pl.BlockSpec((1, nq, LANES), per_b)]
    for pg in range(n_pages):
        in_specs.append(pl.BlockSpec((1, PAGE_SIZE, 2 * LANES),
                                     functools.partial(lambda bb, pt, pg: (pt[bb, pg], 0, 1), pg=pg)))
    grid_spec = pltpu.PrefetchScalarGridSpec(
        num_scalar_prefetch=1, grid=(b,), in_specs=in_specs,
        out_specs=pl.BlockSpec((1, nq, w), per_b),
        scratch_shapes=[pltpu.VMEM((nkeys, LANES), bf16), pltpu.VMEM((nkeys, LANES), bf16)])
    return pl.pallas_call(
        functools.partial(_nsa_sample_kernel, n_pages=n_pages, past_len=n_pages * PAGE_SIZE),
        grid_spec=grid_spec,
        out_shape=jax.ShapeDtypeStruct((b, nq, w), f32),
        compiler_params=_cparams(("parallel",)),
        name="nsa_sample",
    )(page_table, qp, gl, bg, kc, vc, kn, vn, wst, wkn, wvn, *([cache] * n_pages))


def _compress_kernel(x_ref, pos_ref, w1_ref, b1_ref, w2_ref, o_ref):
    x = x_ref[0, 0, 0]
    half = CMP_S * HD
    xa = (x + pos_ref[0, :, :half]).astype(bf16)
    xb = (x + pos_ref[0, :, half:]).astype(bf16)
    a = jnp.dot(xa, w1_ref[0, :half, :], preferred_element_type=f32)
    bsec = jnp.dot(xb, w1_ref[0, half:, :], preferred_element_type=f32)
    nch = x.shape[0]
    hid = jax.nn.gelu(a + pltpu.roll(bsec, nch - 1, 0) + b1_ref[0])
    o_ref[0, 0, 0] = jnp.dot(hid.astype(bf16), w2_ref[0], preferred_element_type=f32).astype(o_ref.dtype)


def _compress_call(x2, cmp_pos, w_cmp1, b_cmp1, w_cmp2):
    b, _, g, nch, cw = x2.shape
    hid = w_cmp1.shape[-1]
    pos = cmp_pos.reshape(2, 1, CMP_L * HD)
    return pl.pallas_call(
        _compress_kernel,
        grid=(b, 2, g),
        in_specs=[pl.BlockSpec((1, 1, 1, nch, cw), lambda bb, s, gg: (bb, s, gg, 0, 0)),
                  pl.BlockSpec((1, 1, CMP_L * HD), lambda bb, s, gg: (s, 0, 0)),
                  pl.BlockSpec((1, CMP_L * HD, hid), lambda bb, s, gg: (s, 0, 0)),
                  pl.BlockSpec((1, 1, hid), lambda bb, s, gg: (s, 0, 0)),
                  pl.BlockSpec((1, hid, HD), lambda bb, s, gg: (s, 0, 0))],
        out_specs=pl.BlockSpec((1, 1, 1, nch, HD), lambda bb, s, gg: (bb, s, gg, 0, 0)),
        out_shape=jax.ShapeDtypeStruct((b, 2, g, nch, HD), bf16),
        compiler_params=_cparams(("parallel", "parallel", "parallel")),
        name="compress",
    )(x2, pos, w_cmp1.astype(bf16), b_cmp1.reshape(2, 1, hid), w_cmp2.astype(bf16))


def _compressed_kv(tok, cmp_pos, w_cmp1, b_cmp1, w_cmp2):
    b, t = tok.shape[:2]
    nch = t // CMP_S
    x2 = jnp.transpose(tok[:, :nch * CMP_S], (0, 2, 3, 1, 4)).reshape(b, 2, NSA_KV, nch, CMP_S * HD)
    ck = _compress_call(x2, cmp_pos, w_cmp1, b_cmp1, w_cmp2)
    ck = jnp.transpose(ck, (0, 1, 3, 2, 4)).reshape(b, 2, nch, NSA_KV * HD)
    return ck[:, 0], ck[:, 1]


def _pair_pack_cols(w):
    k = w.shape[0]
    return jnp.transpose(w.reshape(k, NSA_KV, NSA_HPG, HD), (0, 2, 1, 3)).reshape(k, NSA_W)


def _last_rows(a, n):
    t = a.shape[1]
    if t >= n:
        return a[:, t - n:]
    return jnp.pad(a, ((0, 0), (n - t, 0)) + ((0, 0),) * (a.ndim - 2))


def kernel(x_prompt, x_sample, mem_prompt, state_gla, cache_nsa_kv, state_win_kv, cache_mem_kv, page_table,
           w_in_a, w_gate_a, b_gate_a, gla_norm, w_in_b, b_gate_b, w_kv_b, cmp_pos, w_cmp1, b_cmp1, w_cmp2,
           w_mem_kv, w_out, ln1_g, ln1_b, ln2_g, ln2_b, w_router, b_router, w_e1, b_e1, w_e2, b_e2):
    bp, tp, d = x_prompt.shape
    bs, ts, _ = x_sample.shape
    n_p, n_s = bp * tp, bs * ts
    wb = min(WINDOW, state_win_kv.shape[1])
    n_pool = cache_nsa_kv.shape[0]
    past_len = page_table.shape[1] * PAGE_SIZE

    mem_rows = mem_prompt.reshape(bp * MEM_TOK, d)
    w_mem = [w_mem_kv[l].astype(bf16) for l in range(DEPTH)]
    mem_kv_l = _proj(mem_rows, w_mem, _whole(w_mem))
    mem_kv_prompt = jnp.stack(mem_kv_l).reshape(DEPTH, bp, MEM_TOK, 2, MEM_HEADS, MEM_HD)
    mem_p = [m.reshape(bp, MEM_TOK, 2 * MEM_W) for m in mem_kv_l]
    mem_s = cache_mem_kv.reshape(DEPTH, bs, MEM_TOK, 2 * MEM_W)

    x_p, x_s = x_prompt.reshape(n_p, d), x_sample.reshape(n_s, d)
    as_p = lambda a: a.reshape(bp, tp, a.shape[-1])
    as_s = lambda a: a.reshape(bs, ts, a.shape[-1])

    hk = GLA_HEADS * GLA_DK
    cuts_a = [0, hk, 2 * hk, 2 * hk + GLA_W, 2 * hk + 2 * GLA_W, 2 * hk + 2 * GLA_W + GLA_RANK,
              2 * hk + 2 * GLA_W + GLA_RANK + MEM_W]
    mem_scale = MEM_HD ** -0.5
    q_scale = HD ** -0.5
    xb = jnp.zeros((_moe_blocks((n_p + n_s) * TOP_K) * MOE_ROWS, d), f32)

    gla_p, gla_s = [], []
    for l in range(DEPTH):
        if l < N_A:
            wq, wk, wv, wr, wgl, wm = [w_in_a[l][:, cuts_a[i]:cuts_a[i + 1]] for i in range(6)]
            ws = [_pad_heads(wq, GLA_DK, GLA_KP), _pad_heads(wk, GLA_DK, GLA_KP), _pad_heads(wv, GLA_DV, GLA_VP),
                  _pad_heads(wr, GLA_DV, GLA_VP), jnp.pad(wgl, ((0, 0), (0, LANES - GLA_RANK))), wm * mem_scale]
            ws = [w.astype(bf16) for w in ws]
            outs = _whole(ws)
            outs[5] = outs[5][:3] + (bf16,)
            wg = _pad_heads(jnp.pad(w_gate_a[l], ((0, LANES - GLA_RANK), (0, 0))), GLA_DK, GLA_KP).astype(bf16)
            bg = _pad_heads(b_gate_a[l], GLA_DK, GLA_KP).reshape(1, GLA_HEADS * GLA_KP)
            gn = _pad_heads(gla_norm[l], GLA_DV, GLA_VP).reshape(1, GLA_HEADS * GLA_VP)
            q, k, v, r, glr, mq_p = _proj(x_p, ws, outs)
            o_p, s_new = _gla(as_p(q), as_p(k), as_p(v), as_p(r), as_p(glr), wg, bg, gn, None, bt=min(bp, GLA_BT))
            gla_p.append(s_new)
            q, k, v, r, glr, mq_s = _proj(x_s, ws, outs)
            o_s, s_new = _gla(as_s(q), as_s(k), as_s(v), as_s(r), as_s(glr), wg, bg, gn, state_gla[l],
                              bt=min(bs, GLA_BT))
            gla_s.append(s_new)
            w_o = _pad_heads(w_out[l][:GLA_W].T, GLA_DV, GLA_VP).T
        else:
            j = l - N_A
            if l == N_A:
                role = NSA_KV * HD
                kv_outs = [(0, 0, 4 * role, f32), (0, 4 * role, 6 * role, f32)]
                kv_outs += [(0, r * role, (r + 1) * role, bf16) for r in range(2, 6)]
                wkv = [w_kv_b.astype(bf16)]
                rows_p, win_p, ks_p, vs_p, kw_p, vw_p = [as_p(a) for a in _proj(x_p, wkv, kv_outs)]
                rows_s, win_s, ks_s, vs_s, kw_s, vw_s = [as_s(a) for a in _proj(x_s, wkv, kv_outs)]
                kc_p, vc_p = _compressed_kv(rows_p.reshape(bp, tp, 4, NSA_KV, HD)[:, :, :2],
                                            cmp_pos, w_cmp1, b_cmp1, w_cmp2)
                cache_rows = cache_nsa_kv.reshape(n_pool, PAGE_SIZE, 4 * role)
                past_cmp = cache_rows[page_table][..., :2 * role].reshape(bs, past_len, 2, NSA_KV, HD)
                hist_cmp = jnp.concatenate([past_cmp, rows_s.reshape(bs, ts, 4, NSA_KV, HD)[:, :, :2]], axis=1)
                kc_s, vc_s = _compressed_kv(hist_cmp, cmp_pos, w_cmp1, b_cmp1, w_cmp2)
                wst = state_win_kv.reshape(bs, state_win_kv.shape[1], 2 * role)
                wkv_s = jnp.concatenate([state_win_kv, win_s.reshape(bs, ts, 2, NSA_KV, HD)], axis=1)
                nsa_out = (rows_p.reshape(bp, tp, 4, NSA_KV, HD), _last_rows(win_p.reshape(bp, tp, 2, NSA_KV, HD), wb),
                           rows_s.reshape(bs, ts, 4, NSA_KV, HD), wkv_s[:, -wb:])
            wq = _pair_pack_cols(w_in_b[j][:, :NSA_W]) * q_scale
            wg = jnp.pad(w_in_b[j][:, NSA_W:NSA_W + 3 * NSA_HEADS], ((0, 0), (0, LANES - 3 * NSA_HEADS)))
            wm = w_in_b[j][:, NSA_W + 3 * NSA_HEADS:] * mem_scale
            ws = [wq.astype(bf16), wg.astype(bf16), wm.astype(bf16)]
            outs = [(0, 0, NSA_W, bf16), (1, 0, LANES, f32), (2, 0, MEM_W, bf16)]
            bg = jnp.pad(b_gate_b[j], (0, LANES - 3 * NSA_HEADS)).reshape(1, LANES)
            qp, gl, mq_p = _proj(x_p, ws, outs)
            o_p = _nsa_prompt_call(as_p(qp), as_p(gl), bg, kc_p, vc_p, ks_p, vs_p, kw_p, vw_p)
            qp, gl, mq_s = _proj(x_s, ws, outs)
            o_s = _nsa_sample_call(page_table, as_s(qp), as_s(gl), bg, kc_s, vc_s, ks_s, vs_s, wst, kw_s, vw_s,
                                   cache_rows)
            w_o = _pair_pack_cols(w_out[l][:NSA_W].T).T
        om_p = _mem_attn(as_p(mq_p), mem_p[l])
        om_s = _mem_attn(as_s(mq_s), mem_s[l])
        w_o, w_om = w_o.astype(bf16), w_out[l][-MEM_W:].astype(bf16)
        flat = lambda a: a.reshape(-1, a.shape[-1])
        x1_p, gate_p, idx_p = _mix_ln([(flat(o_p), w_o), (flat(om_p), w_om)], x_p, ln1_g[l], ln1_b[l],
                                      w_router[l], b_router[l])
        x1_s, gate_s, idx_s = _mix_ln([(flat(o_s), w_o), (flat(om_s), w_om)], x_s, ln1_g[l], ln1_b[l],
                                      w_router[l], b_router[l])
        dest, blk_exp, nreal = _moe_slots(jnp.concatenate([idx_p.reshape(-1), idx_s.reshape(-1)]))
        dest_p, dest_s = dest[:n_p * TOP_K], dest[n_p * TOP_K:]
        xb = _moe_dispatch(x1_s, dest_s, _moe_dispatch(x1_p, dest_p, xb))
        yb = _moe_ffn_blocks(xb, blk_exp, nreal, w_e1[l], b_e1[l], w_e2[l], b_e2[l])
        x_p = _moe_combine(yb, dest_p, gate_p, x1_p, ln2_g[l], ln2_b[l])
        x_s = _moe_combine(yb, dest_s, gate_s, x1_s, ln2_g[l], ln2_b[l])

    rows_p, win_p_out, rows_s, win_s_out = nsa_out
    return (x_p.reshape(bp, tp, d), x_s.reshape(bs, ts, d),
            jnp.stack(gla_p), jnp.stack(gla_s), rows_p, rows_s, win_p_out, win_s_out, mem_kv_prompt)
```

```python
import functools
import math

import jax
import jax.numpy as jnp
from jax import lax
from jax.experimental import pallas as pl
from jax.experimental.pallas import tpu as pltpu

D_MODEL = 1024
DEPTH = 4
PAGE_SIZE = 128
N_A = DEPTH // 2
GLA_HEADS = 4
GLA_DV = (3 * D_MODEL) // (4 * GLA_HEADS)
GLA_DK = GLA_DV // 2
GLA_RANK = 16
GLA_TAU = 16.0
GLA_CHUNK = 64
HD = 64
NSA_HEADS = (3 * D_MODEL) // (4 * HD)
NSA_KV = 2
NSA_HPG = NSA_HEADS // NSA_KV
CMP_S = 16
CMP_L = 2 * CMP_S
SLC_L = 64
SLC_K = 16
WINDOW = 512
Q_BLK = 128
MEM_TOK = 256
MEM_HEADS = 4
MEM_HD = D_MODEL // (4 * MEM_HEADS)
N_EXP = 32
TOP_K = 4
D_FF = D_MODEL
SWIGLU_LIMIT = 7.0
SWIGLU_ALPHA = 1.702
GLA_W = GLA_HEADS * GLA_DV
NSA_W = NSA_HEADS * HD
MEM_W = MEM_HEADS * MEM_HD
DN_ALPHA = (2 * DEPTH) ** 0.25
LN_EPS = 1e-5
NEG = -1e30

f32, bf16, i32 = jnp.float32, jnp.bfloat16, jnp.int32

VMEM_LIMIT_BYTES = 56 * 1024 * 1024
LANES = 128
ROW_TILE = 512
MOE_ROWS = 256
FF_CHUNK = 512
SLC_TILE = 512
WIN_KEYS = WINDOW + Q_BLK
NSB_PAD = 64
TOK_TILE = 256
GLA_KP = 128
GLA_VP = 256
GLA_BT = 8
_NT = (((1,), (1,)), ((), ()))
_TN = (((0,), (0,)), ((), ()))


def _cparams(sem):
    return pltpu.CompilerParams(dimension_semantics=sem, vmem_limit_bytes=VMEM_LIMIT_BYTES)


def _alibi_slopes(n):
    def pow2(m):
        start = 2.0 ** (-8.0 / m)
        return [start ** (i + 1) for i in range(m)]
    if math.log2(n).is_integer():
        return pow2(n)
    c = 2 ** math.floor(math.log2(n))
    return pow2(c) + pow2(2 * c)[0::2][: n - c]


SLOPES = _alibi_slopes(NSA_HEADS)
SLC_SHIFT = int(math.log2(SLC_L))
RATIO_SHIFT = int(math.log2(SLC_L // CMP_S))


def _proj_kernel(x_ref, *refs, n_w, outs):
    xb = x_ref[...].astype(bf16)
    res = {}
    for o_ref, (wi, lo, hi) in zip(refs[n_w:], outs):
        if wi not in res:
            res[wi] = jnp.dot(xb, refs[wi][...], preferred_element_type=f32)
        o_ref[...] = res[wi][:, lo:hi].astype(o_ref.dtype)


def _proj(x, ws, outs, tm=ROW_TILE):
    m, k = x.shape
    tm = min(tm, m)
    in_specs = [pl.BlockSpec((tm, k), lambda i: (i, 0))]
    in_specs += [pl.BlockSpec(w.shape, lambda i: (0, 0)) for w in ws]
    out_specs = [pl.BlockSpec((tm, hi - lo), lambda i: (i, 0)) for _, lo, hi, _ in outs]
    out_shape = [jax.ShapeDtypeStruct((m, hi - lo), dt) for _, lo, hi, dt in outs]
    return pl.pallas_call(
        functools.partial(_proj_kernel, n_w=len(ws), outs=tuple(o[:3] for o in outs)),
        grid=(m // tm,),
        in_specs=in_specs,
        out_specs=out_specs,
        out_shape=out_shape,
        compiler_params=_cparams(("parallel",)),
        name="proj",
    )(x, *ws)


def _whole(ws, dtype=f32):
    return [(i, 0, w.shape[1], dtype) for i, w in enumerate(ws)]


def _layer_norm_rows(y, g, b):
    mu = jnp.mean(y, axis=-1, keepdims=True)
    d = y - mu
    var = jnp.mean(d * d, axis=-1, keepdims=True)
    return d * lax.rsqrt(var + LN_EPS) * g + b


def _mix_ln_kernel(*refs, n_parts):
    a_refs, w_refs = refs[:n_parts], refs[n_parts:2 * n_parts]
    x_ref, g_ref, b_ref, wr_ref, br_ref, x1_ref, gate_ref, idx_ref = refs[2 * n_parts:]
    mix = jnp.dot(a_refs[0][...].astype(bf16), w_refs[0][...], preferred_element_type=f32)
    for a_ref, w_ref in zip(a_refs[1:], w_refs[1:]):
        mix = mix + jnp.dot(a_ref[...].astype(bf16), w_ref[...], preferred_element_type=f32)
    x1 = _layer_norm_rows(DN_ALPHA * x_ref[...] + mix, g_ref[...], b_ref[...])
    x1_ref[...] = x1
    lg = jnp.dot(x1, wr_ref[...], preferred_element_type=f32, precision=lax.Precision.HIGHEST) + br_ref[...]
    lane = lax.broadcasted_iota(i32, lg.shape, 1)
    vals, idxs = [], []
    for _ in range(TOP_K):
        m = jnp.max(lg, axis=-1, keepdims=True)
        i = jnp.min(jnp.where(lg == m, lane, N_EXP), axis=-1, keepdims=True)
        vals.append(m)
        idxs.append(i)
        lg = jnp.where(lane == i, -jnp.inf, lg)
    es = [jnp.exp(v - vals[0]) for v in vals]
    tot = es[0] + es[1] + es[2] + es[3]
    col = lax.broadcasted_iota(i32, (lg.shape[0], TOP_K), 1)
    gate = jnp.zeros((lg.shape[0], TOP_K), f32)
    idx = jnp.zeros((lg.shape[0], TOP_K), i32)
    for k in range(TOP_K):
        gate = jnp.where(col == k, es[k] / tot, gate)
        idx = jnp.where(col == k, idxs[k], idx)
    gate_ref[...] = gate
    idx_ref[...] = idx


def _mix_ln(parts, x, g, b, w_r, b_r, tm=ROW_TILE):
    m, d = x.shape
    row = lambda i: (i, 0)
    fix = lambda i: (0, 0)
    in_specs = [pl.BlockSpec((tm, a.shape[1]), row) for a, _ in parts]
    in_specs += [pl.BlockSpec(w.shape, fix) for _, w in parts]
    in_specs += [pl.BlockSpec((tm, d), row), pl.BlockSpec((1, d), fix), pl.BlockSpec((1, d), fix),
                 pl.BlockSpec((d, N_EXP), fix), pl.BlockSpec((1, N_EXP), fix)]
    return pl.pallas_call(
        functools.partial(_mix_ln_kernel, n_parts=len(parts)),
        grid=(m // tm,),
        in_specs=in_specs,
        out_specs=[pl.BlockSpec((tm, d), row), pl.BlockSpec((tm, TOP_K), row), pl.BlockSpec((tm, TOP_K), row)],
        out_shape=[jax.ShapeDtypeStruct((m, d), f32), jax.ShapeDtypeStruct((m, TOP_K), f32),
                   jax.ShapeDtypeStruct((m, TOP_K), i32)],
        compiler_params=_cparams(("parallel",)),
        name="mix_ln",
    )(*[a for a, _ in parts], *[w for _, w in parts], x, g.reshape(1, d), b.reshape(1, d), w_r,
      b_r.reshape(1, N_EXP))


def _moe_kernel(be_ref, nreal_ref, x_ref, w1_ref, b1_ref, w2_ref, b2_ref, o_ref, w1s, w2s):
    i = pl.program_id(0)
    real = i < nreal_ref[0]
    prev = be_ref[jnp.maximum(i - 1, 0)]
    fresh = jnp.logical_or(i == 0, be_ref[i] != prev)

    @pl.when(jnp.logical_and(real, fresh))
    def _():
        w1s[...] = w1_ref[0, 0].astype(bf16)
        w2s[...] = w2_ref[0, 0].astype(bf16)

    @pl.when(real)
    def _():
        x = x_ref[...].astype(bf16)
        acc = jnp.zeros((MOE_ROWS, D_MODEL), f32)
        for c in range(D_FF // FF_CHUNK):
            lo = c * FF_CHUNK
            hg = jnp.dot(x, w1s[:, lo:lo + FF_CHUNK], preferred_element_type=f32)
            hg = hg + b1_ref[0, 0, :, lo:lo + FF_CHUNK]
            hu = jnp.dot(x, w1s[:, D_FF + lo:D_FF + lo + FF_CHUNK], preferred_element_type=f32)
            hu = hu + b1_ref[0, 0, :, D_FF + lo:D_FF + lo + FF_CHUNK]
            g = jnp.minimum(hg, SWIGLU_LIMIT)
            u = jnp.clip(hu, -SWIGLU_LIMIT, SWIGLU_LIMIT)
            a = (u + 1.0) * g * jax.nn.sigmoid(SWIGLU_ALPHA * g)
            acc = acc + jnp.dot(a.astype(bf16), w2s[lo:lo + FF_CHUNK, :], preferred_element_type=f32)
        o_ref[...] = acc + b2_ref[0, 0]

    @pl.when(jnp.logical_not(real))
    def _():
        o_ref[...] = jnp.zeros_like(o_ref)


def _moe_ffn_blocks(xb, blk_exp, nreal, w1, b1, w2, b2, layer):
    p, d = xb.shape
    nblk = p // MOE_ROWS
    n_layers = w1.shape[0]
    per_expert = lambda i, be, nr: (layer, be[i], 0, 0)
    grid_spec = pltpu.PrefetchScalarGridSpec(
        num_scalar_prefetch=2,
        grid=(nblk,),
        in_specs=[
            pl.BlockSpec((MOE_ROWS, d), lambda i, be, nr: (i, 0)),
            pl.BlockSpec((1, 1, d, 2 * D_FF), per_expert),
            pl.BlockSpec((1, 1, 1, 2 * D_FF), per_expert),
            pl.BlockSpec((1, 1, D_FF, d), per_expert),
            pl.BlockSpec((1, 1, 1, d), per_expert),
        ],
        out_specs=pl.BlockSpec((MOE_ROWS, d), lambda i, be, nr: (i, 0)),
        scratch_shapes=[pltpu.VMEM((d, 2 * D_FF), bf16), pltpu.VMEM((D_FF, d), bf16)],
    )
    return pl.pallas_call(
        _moe_kernel,
        grid_spec=grid_spec,
        out_shape=jax.ShapeDtypeStruct((p, d), f32),
        compiler_params=_cparams(("arbitrary",)),
        name="moe_ffn",
    )(blk_exp, nreal, xb, w1, b1.reshape(n_layers, N_EXP, 1, 2 * D_FF), w2, b2.reshape(n_layers, N_EXP, 1, d))


def _row_copy(src, src_row, dst, dst_row, sem):
    return pltpu.make_async_copy(src.at[pl.ds(src_row, 1), :], dst.at[pl.ds(dst_row, 1), :], sem)


def _moe_dispatch_kernel(dest_ref, x_ref, xb_init_ref, xb_ref, sem):
    del xb_init_ref

    def issue(t, c):
        for k in range(TOP_K):
            _row_copy(x_ref, t, xb_ref, dest_ref[0, 0, t * TOP_K + k], sem).start(priority=k % 2)
        return c

    lax.fori_loop(0, TOK_TILE, issue, 0, unroll=2)

    def drain(t, c):
        for k in range(TOP_K):
            _row_copy(x_ref, 0, xb_ref, 0, sem).wait()
        return c

    lax.fori_loop(0, TOK_TILE, drain, 0, unroll=2)


def _moe_dispatch(x1, dest, xb):
    n, d = x1.shape
    nt = n // TOK_TILE
    return pl.pallas_call(
        _moe_dispatch_kernel,
        grid=(nt,),
        in_specs=[pl.BlockSpec((1, 1, TOK_TILE * TOP_K), lambda i: (i, 0, 0), memory_space=pltpu.SMEM),
                  pl.BlockSpec((TOK_TILE, d), lambda i: (i, 0)),
                  pl.BlockSpec(memory_space=pl.ANY)],
        out_specs=pl.BlockSpec(memory_space=pl.ANY),
        out_shape=jax.ShapeDtypeStruct(xb.shape, f32),
        scratch_shapes=[pltpu.SemaphoreType.DMA(())],
        input_output_aliases={2: 0},
        compiler_params=_cparams(("arbitrary",)),
        name="moe_dispatch",
    )(dest.reshape(nt, 1, TOK_TILE * TOP_K), x1, xb)


def _moe_combine_kernel(dest_ref, yb_ref, gate_ref, x1_ref, g_ref, b_ref, o_ref, ybuf, sem):
    def issue(t, c):
        for k in range(TOP_K):
            _row_copy(yb_ref, dest_ref[0, 0, t * TOP_K + k], ybuf.at[k], t, sem).start(priority=k % 2)
        return c

    lax.fori_loop(0, TOK_TILE, issue, 0, unroll=2)

    def drain(t, c):
        for k in range(TOP_K):
            _row_copy(yb_ref, 0, ybuf.at[k], 0, sem).wait()
        return c

    lax.fori_loop(0, TOK_TILE, drain, 0, unroll=2)
    gate = gate_ref[...]
    y = gate[:, 0:1] * ybuf[0]
    for k in range(1, TOP_K):
        y = y + gate[:, k:k + 1] * ybuf[k]
    o_ref[...] = _layer_norm_rows(DN_ALPHA * x1_ref[...] + y, g_ref[...], b_ref[...])


def _moe_combine(yb, dest, gate, x1, g, b):
    n, d = x1.shape
    nt = n // TOK_TILE
    row = lambda i: (i, 0)
    fix = lambda i: (0, 0)
    return pl.pallas_call(
        _moe_combine_kernel,
        grid=(nt,),
        in_specs=[pl.BlockSpec((1, 1, TOK_TILE * TOP_K), lambda i: (i, 0, 0), memory_space=pltpu.SMEM),
                  pl.BlockSpec(memory_space=pl.ANY),
                  pl.BlockSpec((TOK_TILE, TOP_K), row), pl.BlockSpec((TOK_TILE, d), row),
                  pl.BlockSpec((1, d), fix), pl.BlockSpec((1, d), fix)],
        out_specs=pl.BlockSpec((TOK_TILE, d), row),
        out_shape=jax.ShapeDtypeStruct((n, d), f32),
        scratch_shapes=[pltpu.VMEM((TOP_K, TOK_TILE, d), f32), pltpu.SemaphoreType.DMA(())],
        compiler_params=_cparams(("arbitrary",)),
        name="moe_combine",
    )(dest.reshape(nt, 1, TOK_TILE * TOP_K), yb, gate, x1, g.reshape(1, d), b.reshape(1, d))


def _moe_blocks(a):
    return -(-a // MOE_ROWS) + N_EXP


def _moe_slots(flat_e):
    a = flat_e.shape[0]
    onehot = (flat_e[:, None] == jnp.arange(N_EXP, dtype=i32)[None, :]).astype(i32)
    csum = jnp.cumsum(onehot, axis=0)
    counts = csum[-1]
    rank = jnp.take_along_axis(csum, flat_e[:, None], axis=1)[:, 0] - 1
    padded = (counts + MOE_ROWS - 1) // MOE_ROWS * MOE_ROWS
    pend = jnp.cumsum(padded)
    dest = (pend - padded)[flat_e] + rank
    blk_start = jnp.arange(_moe_blocks(a), dtype=i32) * MOE_ROWS
    nreal = (pend[-1] // MOE_ROWS).astype(i32)
    blk_exp = jnp.searchsorted(pend, blk_start, side='right').astype(i32)
    last_exp = blk_exp[jnp.maximum(nreal - 1, 0)]
    blk_exp = jnp.where(blk_start < pend[-1], jnp.minimum(blk_exp, N_EXP - 1), last_exp)
    return dest, blk_exp, nreal.reshape(1)


def _mem_attn_kernel(q_ref, kv_ref, o_ref):
    q = q_ref[0]
    kv = kv_ref[0].astype(bf16)
    tq = q.shape[0]
    lane = lax.broadcasted_iota(i32, (tq, LANES), 1)
    zero = jnp.zeros((tq, LANES), bf16)
    for pair in range(MEM_HEADS // 2):
        qt = q[:, LANES * pair:LANES * (pair + 1)]
        kt = kv[:, LANES * pair:LANES * (pair + 1)]
        vt = kv[:, MEM_W + LANES * pair:MEM_W + LANES * (pair + 1)]
        halves = []
        for h in range(2):
            keep = (lane < MEM_HD) if h == 0 else (lane >= MEM_HD)
            s = lax.dot_general(jnp.where(keep, qt, zero), kt, _NT, preferred_element_type=f32)
            m = jnp.max(s, axis=-1, keepdims=True)
            e = jnp.exp(s - m)
            p = e / jnp.sum(e, axis=-1, keepdims=True)
            halves.append(jnp.dot(p.astype(bf16), vt, preferred_element_type=f32))
        o_ref[0, :, LANES * pair:LANES * (pair + 1)] = jnp.where(lane < MEM_HD, halves[0], halves[1])


def _mem_attn(mq, mem_kv):
    b, t, w = mq.shape
    tq = min(t, ROW_TILE)
    return pl.pallas_call(
        _mem_attn_kernel,
        grid=(b, t // tq),
        in_specs=[pl.BlockSpec((1, tq, w), lambda bb, i: (bb, i, 0)),
                  pl.BlockSpec((1, MEM_TOK, 2 * w), lambda bb, i: (bb, 0, 0))],
        out_specs=pl.BlockSpec((1, tq, w), lambda bb, i: (bb, i, 0)),
        out_shape=jax.ShapeDtypeStruct((b, t, w), f32),
        compiler_params=_cparams(("parallel", "parallel")),
        name="mem_attn",
    )(mq, mem_kv)


def _pad_state(s):
    s = jnp.concatenate([s, jnp.zeros((GLA_KP - GLA_DK, GLA_DV), f32)], axis=0)
    return jnp.concatenate([s, jnp.zeros((GLA_KP, GLA_VP - GLA_DV), f32)], axis=1)


def _gla_kernel(q_ref, k_ref, v_ref, r_ref, glr_ref, wg_ref, bg_ref, gn_ref, s0_ref, o_ref, s_out_ref, st_sc,
                *, bt, chunk, zero_init):
    c = pl.program_id(1)

    @pl.when(c == 0)
    def _():
        if zero_init:
            st_sc[...] = jnp.zeros(st_sc.shape, f32)
        else:
            for b in range(bt):
                for h in range(GLA_HEADS):
                    st_sc[b, h] = jnp.transpose(_pad_state(s0_ref[b, h]))

    tri = lax.broadcasted_iota(i32, (chunk, chunk), 0) >= lax.broadcasted_iota(i32, (chunk, chunk), 1)
    trif = tri.astype(f32)
    problems = [(b, h) for b in range(bt) for h in range(GLA_HEADS)]
    ksl = lambda h: slice(GLA_KP * h, GLA_KP * (h + 1))
    vsl = lambda h: slice(GLA_VP * h, GLA_VP * (h + 1))
    qts, kts, kds, ebl = [], [], [], []
    for b in range(bt):
        z = jnp.dot(glr_ref[b].astype(bf16), wg_ref[...], preferred_element_type=f32) + bg_ref[...]
        log_a = jax.nn.log_sigmoid(z) / GLA_TAU
        bc = jnp.dot(trif, log_a, preferred_element_type=f32, precision=lax.Precision.HIGHEST)
        bl = bc[chunk - 1:chunk, :]
        k = k_ref[b]
        qts.append((q_ref[b] * (GLA_DK ** -0.5) * jnp.exp(bc)).astype(bf16))
        kts.append((k * jnp.exp(-bc)).astype(bf16))
        kds.append((k * jnp.exp(bl - bc)).astype(bf16))
        ebl.append(jnp.exp(bl))
    vbs = [v_ref[b].astype(bf16) for b in range(bt)]
    atts = [jnp.where(tri, lax.dot_general(qts[b][:, ksl(h)], kts[b][:, ksl(h)], _NT, preferred_element_type=f32),
                      0.0).astype(bf16) for b, h in problems]
    sts = [st_sc[b, h] for b, h in problems]
    outs = [lax.dot_general(qts[b][:, ksl(h)], st.astype(bf16), _NT, preferred_element_type=f32)
            + jnp.dot(att, vbs[b][:, vsl(h)], preferred_element_type=f32)
            for (b, h), st, att in zip(problems, sts, atts)]
    for (b, h), st in zip(problems, sts):
        st_sc[b, h] = ebl[b][:, ksl(h)] * st + lax.dot_general(vbs[b][:, vsl(h)], kds[b][:, ksl(h)], _TN,
                                                              preferred_element_type=f32)
    for (b, h), o in zip(problems, outs):
        ms = jnp.sum(o * o, axis=-1, keepdims=True) * (1.0 / GLA_DV)
        o = o * lax.rsqrt(ms + LN_EPS) * gn_ref[:, vsl(h)]
        o_ref[b, :, vsl(h)] = o * jax.nn.silu(r_ref[b, :, vsl(h)])

    @pl.when(c == pl.num_programs(1) - 1)
    def _():
        for b in range(bt):
            for h in range(GLA_HEADS):
                s_out_ref[b, h] = jnp.transpose(st_sc[b, h])[:GLA_DK, :GLA_DV]


def _gla(q, k, v, r, glr, wg, bg, gn, s0, *, bt):
    nb, t, _ = q.shape
    chunk = GLA_CHUNK if t % GLA_CHUNK == 0 else t
    zero_init = s0 is None
    if zero_init:
        s0 = jnp.zeros((1, 1, 8, LANES), f32)
    tok = lambda w: pl.BlockSpec((bt, chunk, w), lambda i, c: (i, c, 0))
    fix = lambda a: pl.BlockSpec(a.shape, lambda i, c: (0,) * a.ndim)
    state = pl.BlockSpec((bt, GLA_HEADS, GLA_DK, GLA_DV), lambda i, c: (i, 0, 0, 0))
    return pl.pallas_call(
        functools.partial(_gla_kernel, bt=bt, chunk=chunk, zero_init=zero_init),
        grid=(nb // bt, t // chunk),
        in_specs=[tok(GLA_HEADS * GLA_KP), tok(GLA_HEADS * GLA_KP), tok(GLA_HEADS * GLA_VP), tok(GLA_HEADS * GLA_VP),
                  tok(LANES), fix(wg), fix(bg), fix(gn), fix(s0) if zero_init else state],
        out_specs=[tok(GLA_HEADS * GLA_VP), state],
        out_shape=[jax.ShapeDtypeStruct((nb, t, GLA_HEADS * GLA_VP), f32),
                   jax.ShapeDtypeStruct((nb, GLA_HEADS, GLA_DK, GLA_DV), f32)],
        scratch_shapes=[pltpu.VMEM((bt, GLA_HEADS, GLA_VP, GLA_KP), f32)],
        compiler_params=_cparams(("parallel", "arbitrary")),
        name="gla",
    )(q, k, v, r, glr, wg, bg, gn, s0)


def _pad_heads(a, w, wp):
    lead = a.shape[:-1]
    a = jnp.pad(a.reshape(*lead, GLA_HEADS, w), [(0, 0)] * (len(lead) + 1) + [(0, wp - w)])
    return a.reshape(*lead, GLA_HEADS * wp)


def _softmax_rows(sm, maskf):
    m = jnp.max(sm, axis=-1, keepdims=True)
    e = jnp.exp(sm - m)
    return e / jnp.sum(e, axis=-1, keepdims=True) * maskf


def _block_scores_t(imp):
    nc = imp.shape[1]
    jj = lax.broadcasted_iota(i32, (NSB_PAD, nc), 0)
    nn = lax.broadcasted_iota(i32, (NSB_PAD, nc), 1)
    mt = ((nn >> RATIO_SHIFT) == jj).astype(f32) + (((nn + 1) >> RATIO_SHIFT) == jj).astype(f32)
    return lax.dot_general(mt, imp, _NT, preferred_element_type=f32, precision=lax.Precision.HIGHEST)


def _select_t(blk_t, qpos_t):
    nq = blk_t.shape[1]
    j_t = lax.broadcasted_iota(i32, (NSB_PAD, nq), 0)
    valid = j_t * SLC_L <= qpos_t
    cur = qpos_t >> SLC_SHIFT
    forced = (j_t == 0) | (j_t == cur) | (j_t == cur - 1)
    score = jnp.where(valid, jnp.where(forced, 1e9, blk_t), -1e9)
    cnt = jnp.zeros((NSB_PAD, nq), i32)
    for k in range(NSB_PAD):
        row = score[k:k + 1, :]
        beats = (row > score) | ((row == score) & (j_t > k))
        cnt = cnt + beats.astype(i32)
    return ((cnt < SLC_K) & valid).astype(f32)


def _untranspose(sel_t):
    nq = sel_t.shape[1]
    eye = lax.broadcasted_iota(i32, (nq, nq), 0) == lax.broadcasted_iota(i32, (nq, nq), 1)
    return lax.dot_general(eye.astype(bf16), sel_t.astype(bf16), _NT, preferred_element_type=f32).astype(bf16)


def _nsa_prompt_kernel(qp_ref, gl_ref, bg_ref, kc_ref, vc_ref, ks_ref, vs_ref, kw_ref, vw_ref, o_ref,
                       p_sc, m_sc, l_sc, acc_sc, og_sc, used_sm):
    nc = kc_ref.shape[1]
    q0 = pl.program_id(1) * Q_BLK
    qpos_i = q0 + lax.broadcasted_iota(i32, (Q_BLK, 1), 0)
    lane = lax.broadcasted_iota(i32, (Q_BLK, LANES), 1)
    gates = jax.nn.sigmoid(gl_ref[0] + bg_ref[...])
    qp = qp_ref[0]
    zero_b = jnp.zeros((Q_BLK, LANES), bf16)

    for g in range(NSA_KV):
        half = (lane < HD) if g == 0 else (lane >= HD)
        qg = jnp.concatenate([jnp.where(half, qp[:, LANES * hh:LANES * (hh + 1)], zero_b)
                              for hh in range(NSA_HPG)], axis=0)
        slopes = [SLOPES[NSA_HPG * g + hh] for hh in range(NSA_HPG)]

        n_idx = lax.broadcasted_iota(i32, (Q_BLK, nc), 1)
        cd = (qpos_i - (n_idx * CMP_S + (CMP_L - 1))).astype(f32)
        cmask = cd >= 0
        cmaskf = cmask.astype(f32)
        s = lax.dot_general(qg, kc_ref[0], _NT, preferred_element_type=f32)
        imp = jnp.zeros((Q_BLK, nc), f32)
        for hh in range(NSA_HPG):
            rows = slice(hh * Q_BLK, (hh + 1) * Q_BLK)
            p = _softmax_rows(jnp.where(cmask, s[rows] - slopes[hh] * cd, NEG), cmaskf)
            imp = imp + p
            p_sc[rows, :nc] = p.astype(bf16)
        o_cmp = jnp.dot(p_sc[:, :nc], vc_ref[0], preferred_element_type=f32)

        qpos_t = q0 + lax.broadcasted_iota(i32, (NSB_PAD, Q_BLK), 1)
        sel_t = _select_t(_block_scores_t(imp), qpos_t)
        sel_b = _untranspose(sel_t)
        blocks_per_tile = SLC_TILE // SLC_L
        for t in range(NSB_PAD // blocks_per_tile):
            used_sm[t] = (jnp.max(sel_t[t * blocks_per_tile:(t + 1) * blocks_per_tile, :]) > 0.5).astype(i32)

        m_sc[...] = jnp.full(m_sc.shape, NEG, f32)
        l_sc[...] = jnp.zeros(l_sc.shape, f32)
        acc_sc[...] = jnp.zeros(acc_sc.shape, f32)

        def tile(t, carry):
            @pl.when(used_sm[t] > 0)
            def _():
                k0 = pl.multiple_of(t * SLC_TILE, SLC_TILE)
                kt = ks_ref[0, pl.ds(k0, SLC_TILE), :]
                vt = vs_ref[0, pl.ds(k0, SLC_TILE), :]
                st = lax.dot_general(qg, kt, _NT, preferred_element_type=f32)
                kpos = k0 + lax.broadcasted_iota(i32, (Q_BLK, SLC_TILE), 1)
                blk_of_key = (k0 + lax.broadcasted_iota(i32, (NSB_PAD, SLC_TILE), 1)) >> SLC_SHIFT
                expand = (lax.broadcasted_iota(i32, (NSB_PAD, SLC_TILE), 0) == blk_of_key).astype(bf16)
                selexp = jnp.dot(sel_b, expand, preferred_element_type=f32)
                addmask = jnp.where((qpos_i >= kpos) & (selexp > 0.5), 0.0, NEG)
                krel = (k0 - q0 + lax.broadcasted_iota(i32, (1, SLC_TILE), 1)).astype(f32)
                for hh in range(NSA_HPG):
                    rows = slice(hh * Q_BLK, (hh + 1) * Q_BLK)
                    sm = st[rows] + slopes[hh] * krel + addmask
                    m_old = m_sc[rows]
                    m_new = jnp.maximum(m_old, jnp.max(sm, axis=-1, keepdims=True))
                    a = jnp.exp(m_old - m_new)
                    e = jnp.exp(sm - m_new)
                    l_sc[rows] = a * l_sc[rows] + jnp.sum(e, axis=-1, keepdims=True)
                    m_sc[rows] = m_new
                    acc_sc[rows] = a * acc_sc[rows]
                    p_sc[rows, :SLC_TILE] = e.astype(bf16)
                acc_sc[...] += jnp.dot(p_sc[:, :SLC_TILE], vt, preferred_element_type=f32)
            return carry

        lax.fori_loop(0, (q0 + Q_BLK + SLC_TILE - 1) // SLC_TILE, tile, 0)
        o_slc = acc_sc[...] / l_sc[...]

        start = pl.multiple_of(jnp.maximum(q0 - WINDOW, 0), Q_BLK)
        kt = kw_ref[0, pl.ds(start, WIN_KEYS), :]
        vt = vw_ref[0, pl.ds(start, WIN_KEYS), :]
        st = lax.dot_general(qg, kt, _NT, preferred_element_type=f32)
        wd = (qpos_i - (start + lax.broadcasted_iota(i32, (Q_BLK, WIN_KEYS), 1))).astype(f32)
        wmask = (wd >= 0) & (wd <= WINDOW)
        wmaskf = wmask.astype(f32)
        for hh in range(NSA_HPG):
            rows = slice(hh * Q_BLK, (hh + 1) * Q_BLK)
            p = _softmax_rows(jnp.where(wmask, st[rows] - slopes[hh] * wd, NEG), wmaskf)
            p_sc[rows, :WIN_KEYS] = p.astype(bf16)
        o_win = jnp.dot(p_sc[:, :WIN_KEYS], vt, preferred_element_type=f32)

        for hh in range(NSA_HPG):
            rows = slice(hh * Q_BLK, (hh + 1) * Q_BLK)
            c = 3 * (NSA_HPG * g + hh)
            og_sc[g, rows] = (gates[:, c:c + 1] * o_cmp[rows] + gates[:, c + 1:c + 2] * o_slc[rows]
                              + gates[:, c + 2:c + 3] * o_win[rows])

    for hh in range(NSA_HPG):
        rows = slice(hh * Q_BLK, (hh + 1) * Q_BLK)
        o_ref[0, :, LANES * hh:LANES * (hh + 1)] = jnp.where(lane < HD, og_sc[0, rows], og_sc[1, rows])


def _nsa_prompt_call(qp, gl, bg, kc, vc, ks, vs, kw, vw):
    b, t, w = qp.shape
    nc = kc.shape[1]
    assert t % SLC_TILE == 0 and t >= WIN_KEYS and nc % LANES == 0 and t // SLC_L <= NSB_PAD
    rows = NSA_HPG * Q_BLK
    qblk = lambda bb, i: (bb, i, 0)
    whole = lambda bb, i: (bb, 0, 0)
    return pl.pallas_call(
        _nsa_prompt_kernel,
        grid=(b, t // Q_BLK),
        in_specs=[pl.BlockSpec((1, Q_BLK, w), qblk), pl.BlockSpec((1, Q_BLK, LANES), qblk),
                  pl.BlockSpec((1, LANES), lambda bb, i: (0, 0)),
                  pl.BlockSpec((1, nc, LANES), whole), pl.BlockSpec((1, nc, LANES), whole),
                  pl.BlockSpec((1, t, LANES), whole), pl.BlockSpec((1, t, LANES), whole),
                  pl.BlockSpec((1, t, LANES), whole), pl.BlockSpec((1, t, LANES), whole)],
        out_specs=pl.BlockSpec((1, Q_BLK, w), qblk),
        out_shape=jax.ShapeDtypeStruct((b, t, w), f32),
        scratch_shapes=[pltpu.VMEM((rows, WIN_KEYS), bf16), pltpu.VMEM((rows, 1), f32), pltpu.VMEM((rows, 1), f32),
                        pltpu.VMEM((rows, LANES), f32), pltpu.VMEM((NSA_KV, rows, LANES), f32),
                        pltpu.SMEM((NSB_PAD * SLC_L // SLC_TILE,), i32)],
        compiler_params=_cparams(("parallel", "parallel")),
        name="nsa_prompt",
    )(qp, gl, bg, kc, vc, ks, vs, kw, vw)


def _nsa_sample_kernel(pt_ref, qp_ref, gl_ref, bg_ref, kc_ref, vc_ref, kn_ref, vn_ref, wst_ref, wkn_ref, wvn_ref,
                       *rest, n_pages, past_len):
    page_refs = rest[:n_pages]
    o_ref, k_sc, v_sc = rest[n_pages:]
    nq = qp_ref.shape[1]
    nc = kc_ref.shape[1]
    nkeys = (n_pages + 1) * PAGE_SIZE
    rows_g = NSA_HPG * nq
    qpos_i = past_len + lax.broadcasted_iota(i32, (nq, 1), 0)
    lane = lax.broadcasted_iota(i32, (nq, LANES), 1)
    gates = jax.nn.sigmoid(gl_ref[0] + bg_ref[...])
    qp = qp_ref[0]
    zero_b = jnp.zeros((nq, LANES), bf16)
    qs = jnp.concatenate([jnp.where((lane < HD) if g == 0 else (lane >= HD),
                                    qp[:, LANES * hh:LANES * (hh + 1)], zero_b)
                          for g in range(NSA_KV) for hh in range(NSA_HPG)], axis=0)

    def rows_of(g, hh):
        r0 = g * rows_g + hh * nq
        return slice(r0, r0 + nq)

    n_idx = lax.broadcasted_iota(i32, (nq, nc), 1)
    cd = (qpos_i - (n_idx * CMP_S + (CMP_L - 1))).astype(f32)
    cmask = cd >= 0
    cmaskf = cmask.astype(f32)
    s = lax.dot_general(qs, kc_ref[0], _NT, preferred_element_type=f32)
    imps, ps = [], []
    for g in range(NSA_KV):
        imp = jnp.zeros((nq, nc), f32)
        for hh in range(NSA_HPG):
            p = _softmax_rows(jnp.where(cmask, s[rows_of(g, hh)] - SLOPES[NSA_HPG * g + hh] * cd, NEG), cmaskf)
            imp = imp + p
            ps.append(p)
        imps.append(imp)
    o_cmp = jnp.dot(jnp.concatenate(ps, axis=0).astype(bf16), vc_ref[0], preferred_element_type=f32)

    qpos_t = past_len + lax.broadcasted_iota(i32, (NSB_PAD, nq), 1)
    blk_of_key = lax.broadcasted_iota(i32, (NSB_PAD, nkeys), 1) >> SLC_SHIFT
    expand = (lax.broadcasted_iota(i32, (NSB_PAD, nkeys), 0) == blk_of_key).astype(bf16)
    sd = (qpos_i - lax.broadcasted_iota(i32, (nq, nkeys), 1)).astype(f32)
    masks = []
    for g in range(NSA_KV):
        sel_b = _untranspose(_select_t(_block_scores_t(imps[g]), qpos_t))
        selexp = jnp.dot(sel_b, expand, preferred_element_type=f32)
        masks.append((sd >= 0) & (selexp > 0.5))

    for pg in range(n_pages):
        blk = page_refs[pg][0]
        k_sc[:, pg * PAGE_SIZE:(pg + 1) * PAGE_SIZE] = blk[:LANES].astype(bf16)
        v_sc[:, pg * PAGE_SIZE:(pg + 1) * PAGE_SIZE] = blk[LANES:].astype(bf16)
    tail = jnp.zeros((PAGE_SIZE - nq, LANES), bf16)
    new_t = lambda ref: jnp.transpose(jnp.concatenate([ref[0], tail], axis=0).astype(f32)).astype(bf16)
    k_sc[:, n_pages * PAGE_SIZE:] = new_t(kn_ref)
    v_sc[:, n_pages * PAGE_SIZE:] = new_t(vn_ref)
    s = jnp.dot(qs, k_sc[...], preferred_element_type=f32)
    ps = []
    for g in range(NSA_KV):
        mf = masks[g].astype(f32)
        for hh in range(NSA_HPG):
            ps.append(_softmax_rows(jnp.where(masks[g], s[rows_of(g, hh)] - SLOPES[NSA_HPG * g + hh] * sd, NEG), mf))
    o_slc = lax.dot_general(jnp.concatenate(ps, axis=0).astype(bf16), v_sc[...], _NT, preferred_element_type=f32)

    wb = wst_ref.shape[1]
    wk = jnp.concatenate([wst_ref[0, :, :LANES].astype(bf16), wkn_ref[0], tail], axis=0)
    wv = jnp.concatenate([wst_ref[0, :, LANES:].astype(bf16), wvn_ref[0], tail], axis=0)
    wpos = (past_len - wb) + lax.broadcasted_iota(i32, (nq, wb + PAGE_SIZE), 1)
    wd = (qpos_i - wpos).astype(f32)
    wmask = (wd >= 0) & (wd <= WINDOW) & (wpos >= 0)
    wmaskf = wmask.astype(f32)
    s = lax.dot_general(qs, wk, _NT, preferred_element_type=f32)
    ps = []
    for g in range(NSA_KV):
        for hh in range(NSA_HPG):
            ps.append(_softmax_rows(jnp.where(wmask, s[rows_of(g, hh)] - SLOPES[NSA_HPG * g + hh] * wd, NEG), wmaskf))
    o_win = jnp.dot(jnp.concatenate(ps, axis=0).astype(bf16), wv, preferred_element_type=f32)

    for hh in range(NSA_HPG):
        halves = []
        for g in range(NSA_KV):
            r = rows_of(g, hh)
            c = 3 * (NSA_HPG * g + hh)
            halves.append(gates[:, c:c + 1] * o_cmp[r] + gates[:, c + 1:c + 2] * o_slc[r]
                          + gates[:, c + 2:c + 3] * o_win[r])
        o_ref[0, :, LANES * hh:LANES * (hh + 1)] = jnp.where(lane < HD, halves[0], halves[1])


def _nsa_sample_call(page_table, qp, gl, bg, kc, vc, kn, vn, wst, wkn, wvn, cache):
    b, nq, w = qp.shape
    n_pages = page_table.shape[1]
    nc = kc.shape[1]
    wb = wst.shape[1]
    nkeys = (n_pages + 1) * PAGE_SIZE
    assert nkeys // SLC_L <= NSB_PAD and nq <= PAGE_SIZE
    per_b = lambda bb, pt: (bb, 0, 0)
    in_specs = [pl.BlockSpec((1, nq, w), per_b), pl.BlockSpec((1, nq, LANES), per_b),
                pl.BlockSpec((1, LANES), lambda bb, pt: (0, 0)),
                pl.BlockSpec((1, nc, LANES), per_b), pl.BlockSpec((1, nc, LANES), per_b),
                pl.BlockSpec((1, nq, LANES), per_b), pl.BlockSpec((1, nq, LANES), per_b),
                pl.BlockSpec((1, wb, 2 * LANES), per_b),
                pl.BlockSpec((1, nq, LANES), per_b), pl.BlockSpec((1, nq, LANES), per_b)]
    for pg in range(n_pages):
        in_specs.append(pl.BlockSpec((1, 2 * LANES, PAGE_SIZE),
                                     functools.partial(lambda bb, pt, pg: (pt[bb, pg], 1, 0), pg=pg)))
    grid_spec = pltpu.PrefetchScalarGridSpec(
        num_scalar_prefetch=1, grid=(b,), in_specs=in_specs,
        out_specs=pl.BlockSpec((1, nq, w), per_b),
        scratch_shapes=[pltpu.VMEM((LANES, nkeys), bf16), pltpu.VMEM((LANES, nkeys), bf16)])
    return pl.pallas_call(
        functools.partial(_nsa_sample_kernel, n_pages=n_pages, past_len=n_pages * PAGE_SIZE),
        grid_spec=grid_spec,
        out_shape=jax.ShapeDtypeStruct((b, nq, w), f32),
        compiler_params=_cparams(("parallel",)),
        name="nsa_sample",
    )(page_table, qp, gl, bg, kc, vc, kn, vn, wst, wkn, wvn, *([cache] * n_pages))


def _compress_kernel(x_ref, pos_ref, w1_ref, b1_ref, w2_ref, o_ref):
    x = x_ref[0, 0, 0]
    half = CMP_S * HD
    xa = (x + pos_ref[0, :, :half]).astype(bf16)
    xb = (x + pos_ref[0, :, half:]).astype(bf16)
    a = jnp.dot(xa, w1_ref[0, :half, :], preferred_element_type=f32)
    bsec = jnp.dot(xb, w1_ref[0, half:, :], preferred_element_type=f32)
    nch = x.shape[0]
    hid = jax.nn.gelu(a + pltpu.roll(bsec, nch - 1, 0) + b1_ref[0])
    o_ref[0, 0, 0] = jnp.dot(hid.astype(bf16), w2_ref[0], preferred_element_type=f32).astype(o_ref.dtype)


def _compress_call(x2, cmp_pos, w_cmp1, b_cmp1, w_cmp2):
    b, _, g, nch, cw = x2.shape
    hid = w_cmp1.shape[-1]
    pos = cmp_pos.reshape(2, 1, CMP_L * HD)
    return pl.pallas_call(
        _compress_kernel,
        grid=(b, 2, g),
        in_specs=[pl.BlockSpec((1, 1, 1, nch, cw), lambda bb, s, gg: (bb, s, gg, 0, 0)),
                  pl.BlockSpec((1, 1, CMP_L * HD), lambda bb, s, gg: (s, 0, 0)),
                  pl.BlockSpec((1, CMP_L * HD, hid), lambda bb, s, gg: (s, 0, 0)),
                  pl.BlockSpec((1, 1, hid), lambda bb, s, gg: (s, 0, 0)),
                  pl.BlockSpec((1, hid, HD), lambda bb, s, gg: (s, 0, 0))],
        out_specs=pl.BlockSpec((1, 1, 1, nch, HD), lambda bb, s, gg: (bb, s, gg, 0, 0)),
        out_shape=jax.ShapeDtypeStruct((b, 2, g, nch, HD), bf16),
        compiler_params=_cparams(("parallel", "parallel", "parallel")),
        name="compress",
    )(x2, pos, w_cmp1.astype(bf16), b_cmp1.reshape(2, 1, hid), w_cmp2.astype(bf16))


def _compressed_kv(tok, cmp_pos, w_cmp1, b_cmp1, w_cmp2):
    b, t = tok.shape[:2]
    nch = t // CMP_S
    x2 = jnp.transpose(tok[:, :nch * CMP_S], (0, 2, 3, 1, 4)).reshape(b, 2, NSA_KV, nch, CMP_S * HD)
    ck = _compress_call(x2, cmp_pos, w_cmp1, b_cmp1, w_cmp2)
    ck = jnp.transpose(ck, (0, 1, 3, 2, 4)).reshape(b, 2, nch, NSA_KV * HD)
    return ck[:, 0], ck[:, 1]


def _compress_paged_kernel(pt_ref, pos_ref, w1a_ref, w1b_ref, b1_ref, w2_ref, *rest, n_pages):
    page_refs = rest[:n_pages]
    o_ref, x_sc, xcat_sc = rest[n_pages:]
    nch = n_pages * (PAGE_SIZE // CMP_S)
    role = NSA_KV * HD
    for pg in range(n_pages):
        xt = jnp.transpose(page_refs[pg][0])
        for s in range(2):
            x_sc[s, pg * PAGE_SIZE:(pg + 1) * PAGE_SIZE, :] = xt[:, s * role:(s + 1) * role]
    for s in range(2):
        for t in range(CMP_S):
            xcat_sc[:, t * role:(t + 1) * role] = x_sc[s, pl.ds(t, nch, stride=CMP_S), :]
        xc = xcat_sc[...]
        xa = (xc + pos_ref[s, 0:1, :]).astype(bf16)
        xb = (xc + pos_ref[s, 1:2, :]).astype(bf16)
        a = jnp.dot(xa, w1a_ref[s], preferred_element_type=f32)
        bsec = jnp.dot(xb, w1b_ref[s], preferred_element_type=f32)
        hid = jax.nn.gelu(a + pltpu.roll(bsec, nch - 1, 0) + b1_ref[s])
        o_ref[0, s] = jnp.dot(hid.astype(bf16), w2_ref[s], preferred_element_type=f32).astype(o_ref.dtype)


def _compress_paged(page_table, cache_t, cmp_pos, w_cmp1, b_cmp1, w_cmp2):
    b, n_pages = page_table.shape
    nch = n_pages * (PAGE_SIZE // CMP_S)
    hid = w_cmp1.shape[-1]
    half = CMP_S * HD
    eye = jnp.eye(NSA_KV, dtype=f32)

    def widen_rows(w):
        w = w.reshape(2, CMP_S, HD, hid)
        return jnp.einsum('stdh,ag->stadgh', w, eye).reshape(2, CMP_S * NSA_KV * HD, NSA_KV * hid)

    w1a = widen_rows(w_cmp1[:, :half]).astype(bf16)
    w1b = widen_rows(w_cmp1[:, half:]).astype(bf16)
    w2 = jnp.einsum('shd,ag->sahgd', w_cmp2, eye).reshape(2, NSA_KV * hid, NSA_KV * HD).astype(bf16)
    b1 = jnp.tile(b_cmp1, (1, NSA_KV)).reshape(2, 1, NSA_KV * hid)
    pos = jnp.broadcast_to(cmp_pos.reshape(2, 2, CMP_S, 1, HD), (2, 2, CMP_S, NSA_KV, HD))
    pos = pos.reshape(2, 2, CMP_S * NSA_KV * HD)
    fix = lambda a: pl.BlockSpec(a.shape, lambda bb, pt: (0,) * a.ndim)
    in_specs = [fix(pos), fix(w1a), fix(w1b), fix(b1), fix(w2)]
    for pg in range(n_pages):
        in_specs.append(pl.BlockSpec((1, 2 * NSA_KV * HD, PAGE_SIZE),
                                     functools.partial(lambda bb, pt, pg: (pt[bb, pg], 0, 0), pg=pg)))
    grid_spec = pltpu.PrefetchScalarGridSpec(
        num_scalar_prefetch=1, grid=(b,), in_specs=in_specs,
        out_specs=pl.BlockSpec((1, 2, nch, NSA_KV * HD), lambda bb, pt: (bb, 0, 0, 0)),
        scratch_shapes=[pltpu.VMEM((2, n_pages * PAGE_SIZE, NSA_KV * HD), f32),
                        pltpu.VMEM((nch, CMP_S * NSA_KV * HD), f32)])
    return pl.pallas_call(
        functools.partial(_compress_paged_kernel, n_pages=n_pages),
        grid_spec=grid_spec,
        out_shape=jax.ShapeDtypeStruct((b, 2, nch, NSA_KV * HD), bf16),
        compiler_params=_cparams(("parallel",)),
        name="compress_paged",
    )(page_table, pos, w1a, w1b, b1, w2, *([cache_t] * n_pages))


def _pair_pack_cols(w):
    k = w.shape[0]
    return jnp.transpose(w.reshape(k, NSA_KV, NSA_HPG, HD), (0, 2, 1, 3)).reshape(k, NSA_W)


def _last_rows(a, n):
    t = a.shape[1]
    if t >= n:
        return a[:, t - n:]
    return jnp.pad(a, ((0, 0), (n - t, 0)) + ((0, 0),) * (a.ndim - 2))


def kernel(x_prompt, x_sample, mem_prompt, state_gla, cache_nsa_kv, state_win_kv, cache_mem_kv, page_table,
           w_in_a, w_gate_a, b_gate_a, gla_norm, w_in_b, b_gate_b, w_kv_b, cmp_pos, w_cmp1, b_cmp1, w_cmp2,
           w_mem_kv, w_out, ln1_g, ln1_b, ln2_g, ln2_b, w_router, b_router, w_e1, b_e1, w_e2, b_e2):
    bp, tp, d = x_prompt.shape
    bs, ts, _ = x_sample.shape
    n_p, n_s = bp * tp, bs * ts
    wb = min(WINDOW, state_win_kv.shape[1])
    n_pool = cache_nsa_kv.shape[0]
    past_len = page_table.shape[1] * PAGE_SIZE

    mem_rows = mem_prompt.reshape(bp * MEM_TOK, d)
    w_mem = [w_mem_kv[l].astype(bf16) for l in range(DEPTH)]
    mem_kv_l = _proj(mem_rows, w_mem, _whole(w_mem))
    mem_kv_prompt = jnp.stack(mem_kv_l).reshape(DEPTH, bp, MEM_TOK, 2, MEM_HEADS, MEM_HD)
    mem_p = [m.reshape(bp, MEM_TOK, 2 * MEM_W) for m in mem_kv_l]
    mem_s = cache_mem_kv.reshape(DEPTH, bs, MEM_TOK, 2 * MEM_W)

    x_p, x_s = x_prompt.reshape(n_p, d), x_sample.reshape(n_s, d)
    as_p = lambda a: a.reshape(bp, tp, a.shape[-1])
    as_s = lambda a: a.reshape(bs, ts, a.shape[-1])

    hk = GLA_HEADS * GLA_DK
    cuts_a = [0, hk, 2 * hk, 2 * hk + GLA_W, 2 * hk + 2 * GLA_W, 2 * hk + 2 * GLA_W + GLA_RANK,
              2 * hk + 2 * GLA_W + GLA_RANK + MEM_W]
    mem_scale = MEM_HD ** -0.5
    q_scale = HD ** -0.5
    xb = jnp.zeros((_moe_blocks((n_p + n_s) * TOP_K) * MOE_ROWS, d), f32)

    gla_p, gla_s = [], []
    for l in range(DEPTH):
        if l < N_A:
            wq, wk, wv, wr, wgl, wm = [w_in_a[l][:, cuts_a[i]:cuts_a[i + 1]] for i in range(6)]
            ws = [_pad_heads(wq, GLA_DK, GLA_KP), _pad_heads(wk, GLA_DK, GLA_KP), _pad_heads(wv, GLA_DV, GLA_VP),
                  _pad_heads(wr, GLA_DV, GLA_VP), jnp.pad(wgl, ((0, 0), (0, LANES - GLA_RANK))), wm * mem_scale]
            ws = [w.astype(bf16) for w in ws]
            outs = _whole(ws)
            outs[5] = outs[5][:3] + (bf16,)
            wg = _pad_heads(jnp.pad(w_gate_a[l], ((0, LANES - GLA_RANK), (0, 0))), GLA_DK, GLA_KP).astype(bf16)
            bg = _pad_heads(b_gate_a[l], GLA_DK, GLA_KP).reshape(1, GLA_HEADS * GLA_KP)
            gn = _pad_heads(gla_norm[l], GLA_DV, GLA_VP).reshape(1, GLA_HEADS * GLA_VP)
            q, k, v, r, glr, mq_p = _proj(x_p, ws, outs)
            o_p, s_new = _gla(as_p(q), as_p(k), as_p(v), as_p(r), as_p(glr), wg, bg, gn, None, bt=min(bp, GLA_BT))
            gla_p.append(s_new)
            q, k, v, r, glr, mq_s = _proj(x_s, ws, outs)
            o_s, s_new = _gla(as_s(q), as_s(k), as_s(v), as_s(r), as_s(glr), wg, bg, gn, state_gla[l],
                              bt=min(bs, GLA_BT))
            gla_s.append(s_new)
            w_o = _pad_heads(w_out[l][:GLA_W].T, GLA_DV, GLA_VP).T
        else:
            j = l - N_A
            if l == N_A:
                role = NSA_KV * HD
                kv_outs = [(0, 0, 4 * role, f32), (0, 4 * role, 6 * role, f32)]
                kv_outs += [(0, r * role, (r + 1) * role, bf16) for r in range(2, 6)]
                wkv = [w_kv_b.astype(bf16)]
                rows_p, win_p, ks_p, vs_p, kw_p, vw_p = [as_p(a) for a in _proj(x_p, wkv, kv_outs)]
                rows_s, win_s, ks_s, vs_s, kw_s, vw_s = [as_s(a) for a in _proj(x_s, wkv, kv_outs)]
                kc_p, vc_p = _compressed_kv(rows_p.reshape(bp, tp, 4, NSA_KV, HD)[:, :, :2],
                                            cmp_pos, w_cmp1, b_cmp1, w_cmp2)
                assert (past_len + ts) // CMP_S == past_len // CMP_S
                cache_t = jnp.transpose(cache_nsa_kv, (0, 2, 3, 4, 1)).reshape(n_pool, 4 * role, PAGE_SIZE)
                ck_s = _compress_paged(page_table, cache_t, cmp_pos, w_cmp1, b_cmp1, w_cmp2)
                kc_s, vc_s = ck_s[:, 0], ck_s[:, 1]
                wst = state_win_kv.reshape(bs, state_win_kv.shape[1], 2 * role)
                wkv_s = jnp.concatenate([state_win_kv, win_s.reshape(bs, ts, 2, NSA_KV, HD)], axis=1)
                nsa_out = (rows_p.reshape(bp, tp, 4, NSA_KV, HD), _last_rows(win_p.reshape(bp, tp, 2, NSA_KV, HD), wb),
                           rows_s.reshape(bs, ts, 4, NSA_KV, HD), wkv_s[:, -wb:])
            wq = _pair_pack_cols(w_in_b[j][:, :NSA_W]) * q_scale
            wg = jnp.pad(w_in_b[j][:, NSA_W:NSA_W + 3 * NSA_HEADS], ((0, 0), (0, LANES - 3 * NSA_HEADS)))
            wm = w_in_b[j][:, NSA_W + 3 * NSA_HEADS:] * mem_scale
            ws = [wq.astype(bf16), wg.astype(bf16), wm.astype(bf16)]
            outs = [(0, 0, NSA_W, bf16), (1, 0, LANES, f32), (2, 0, MEM_W, bf16)]
            bg = jnp.pad(b_gate_b[j], (0, LANES - 3 * NSA_HEADS)).reshape(1, LANES)
            qp, gl, mq_p = _proj(x_p, ws, outs)
            o_p = _nsa_prompt_call(as_p(qp), as_p(gl), bg, kc_p, vc_p, ks_p, vs_p, kw_p, vw_p)
            qp, gl, mq_s = _proj(x_s, ws, outs)
            o_s = _nsa_sample_call(page_table, as_s(qp), as_s(gl), bg, kc_s, vc_s, ks_s, vs_s, wst, kw_s, vw_s,
                                   cache_t)
            w_o = _pair_pack_cols(w_out[l][:NSA_W].T).T
        om_p = _mem_attn(as_p(mq_p), mem_p[l])
        om_s = _mem_attn(as_s(mq_s), mem_s[l])
        w_o, w_om = w_o.astype(bf16), w_out[l][-MEM_W:].astype(bf16)
        flat = lambda a: a.reshape(-1, a.shape[-1])
        x1_p, gate_p, idx_p = _mix_ln([(flat(o_p), w_o), (flat(om_p), w_om)], x_p, ln1_g[l], ln1_b[l],
                                      w_router[l], b_router[l])
        x1_s, gate_s, idx_s = _mix_ln([(flat(o_s), w_o), (flat(om_s), w_om)], x_s, ln1_g[l], ln1_b[l],
                                      w_router[l], b_router[l])
        dest, blk_exp, nreal = _moe_slots(jnp.concatenate([idx_p.reshape(-1), idx_s.reshape(-1)]))
        dest_p, dest_s = dest[:n_p * TOP_K], dest[n_p * TOP_K:]
        xb = _moe_dispatch(x1_s, dest_s, _moe_dispatch(x1_p, dest_p, xb))
        yb = _moe_ffn_blocks(xb, blk_exp, nreal, w_e1, b_e1, w_e2, b_e2, l)
        x_p = _moe_combine(yb, dest_p, gate_p, x1_p, ln2_g[l], ln2_b[l])
        x_s = _moe_combine(yb, dest_s, gate_s, x1_s, ln2_g[l], ln2_b[l])

    rows_p, win_p_out, rows_s, win_s_out = nsa_out
    return (x_p.reshape(bp, tp, d), x_s.reshape(bs, ts, d),
            jnp.stack(gla_p), jnp.stack(gla_s), rows_p, rows_s, win_p_out, win_s_out, mem_kv_prompt)
```

```python
import functools
import math

import jax
import jax.numpy as jnp
from jax import lax
from jax.experimental import pallas as pl
from jax.experimental.pallas import tpu as pltpu

D_MODEL = 1024
DEPTH = 4
PAGE_SIZE = 128
N_A = DEPTH // 2
GLA_HEADS = 4
GLA_DV = (3 * D_MODEL) // (4 * GLA_HEADS)
GLA_DK = GLA_DV // 2
GLA_RANK = 16
GLA_TAU = 16.0
GLA_CHUNK = 64
HD = 64
NSA_HEADS = (3 * D_MODEL) // (4 * HD)
NSA_KV = 2
NSA_HPG = NSA_HEADS // NSA_KV
CMP_S = 16
CMP_L = 2 * CMP_S
SLC_L = 64
SLC_K = 16
WINDOW = 512
Q_BLK = 128
MEM_TOK = 256
MEM_HEADS = 4
MEM_HD = D_MODEL // (4 * MEM_HEADS)
N_EXP = 32
TOP_K = 4
D_FF = D_MODEL
SWIGLU_LIMIT = 7.0
SWIGLU_ALPHA = 1.702
GLA_W = GLA_HEADS * GLA_DV
NSA_W = NSA_HEADS * HD
MEM_W = MEM_HEADS * MEM_HD
DN_ALPHA = (2 * DEPTH) ** 0.25
LN_EPS = 1e-5
NEG = -1e30

f32, bf16, i32 = jnp.float32, jnp.bfloat16, jnp.int32

VMEM_LIMIT_BYTES = 56 * 1024 * 1024
LANES = 128
ROW_TILE = 512
MOE_ROWS = 256
FF_CHUNK = 512
SLC_TILE = 512
WIN_KEYS = WINDOW + Q_BLK
NSB_PAD = 64
TOK_TILE = 256
RANK_TILE = 512
MEM_ROWS = 64
GLA_KP = 128
GLA_VP = 256
GLA_BT = 8
_NT = (((1,), (1,)), ((), ()))
_TN = (((0,), (0,)), ((), ()))


def _cparams(sem):
    return pltpu.CompilerParams(dimension_semantics=sem, vmem_limit_bytes=VMEM_LIMIT_BYTES)


def _alibi_slopes(n):
    def pow2(m):
        start = 2.0 ** (-8.0 / m)
        return [start ** (i + 1) for i in range(m)]
    if math.log2(n).is_integer():
        return pow2(n)
    c = 2 ** math.floor(math.log2(n))
    return pow2(c) + pow2(2 * c)[0::2][: n - c]


SLOPES = _alibi_slopes(NSA_HEADS)
SLC_SHIFT = int(math.log2(SLC_L))
RATIO_SHIFT = int(math.log2(SLC_L // CMP_S))


def _proj_kernel(x_ref, *refs, n_w, outs):
    xb = x_ref[...].astype(bf16)
    res = {}
    for o_ref, (wi, lo, hi) in zip(refs[n_w:], outs):
        if wi not in res:
            res[wi] = jnp.dot(xb, refs[wi][...], preferred_element_type=f32)
        o_ref[...] = res[wi][:, lo:hi].astype(o_ref.dtype)


def _proj(x, ws, outs, tm=ROW_TILE):
    m, k = x.shape
    tm = min(tm, m)
    in_specs = [pl.BlockSpec((tm, k), lambda i: (i, 0))]
    in_specs += [pl.BlockSpec(w.shape, lambda i: (0, 0)) for w in ws]
    out_specs = [pl.BlockSpec((tm, hi - lo), lambda i: (i, 0)) for _, lo, hi, _ in outs]
    out_shape = [jax.ShapeDtypeStruct((m, hi - lo), dt) for _, lo, hi, dt in outs]
    return pl.pallas_call(
        functools.partial(_proj_kernel, n_w=len(ws), outs=tuple(o[:3] for o in outs)),
        grid=(m // tm,),
        in_specs=in_specs,
        out_specs=out_specs,
        out_shape=out_shape,
        compiler_params=_cparams(("parallel",)),
        name="proj",
    )(x, *ws)


def _whole(ws, dtype=f32):
    return [(i, 0, w.shape[1], dtype) for i, w in enumerate(ws)]


def _layer_norm_rows(y, g, b):
    mu = jnp.mean(y, axis=-1, keepdims=True)
    d = y - mu
    var = jnp.mean(d * d, axis=-1, keepdims=True)
    return d * lax.rsqrt(var + LN_EPS) * g + b


def _mix_ln_kernel(*refs, n_parts):
    a_refs, w_refs = refs[:n_parts], refs[n_parts:2 * n_parts]
    x_ref, g_ref, b_ref, wr_ref, br_ref, x1_ref, gate_ref, idx_ref = refs[2 * n_parts:]
    tm = x_ref.shape[0]
    sub = tm // 2 if tm % 16 == 0 else tm
    for r0 in range(0, tm, sub):
        rows = slice(r0, r0 + sub)
        mix = jnp.dot(a_refs[0][rows, :].astype(bf16), w_refs[0][...], preferred_element_type=f32)
        for a_ref, w_ref in zip(a_refs[1:], w_refs[1:]):
            mix = mix + jnp.dot(a_ref[rows, :].astype(bf16), w_ref[...], preferred_element_type=f32)
        x1 = _layer_norm_rows(DN_ALPHA * x_ref[rows, :] + mix, g_ref[...], b_ref[...])
        x1_ref[rows, :] = x1
        lg = jnp.dot(x1, wr_ref[...], preferred_element_type=f32, precision=lax.Precision.HIGHEST) + br_ref[...]
        lane = lax.broadcasted_iota(i32, lg.shape, 1)
        vals, idxs = [], []
        for _ in range(TOP_K):
            m = jnp.max(lg, axis=-1, keepdims=True)
            i = jnp.min(jnp.where(lg == m, lane, N_EXP), axis=-1, keepdims=True)
            vals.append(m)
            idxs.append(i)
            lg = jnp.where(lane == i, -jnp.inf, lg)
        es = [jnp.exp(v - vals[0]) for v in vals]
        tot = es[0] + es[1] + es[2] + es[3]
        col = lax.broadcasted_iota(i32, (sub, TOP_K), 1)
        gate = jnp.zeros((sub, TOP_K), f32)
        idx = jnp.zeros((sub, TOP_K), i32)
        for k in range(TOP_K):
            gate = jnp.where(col == k, es[k] / tot, gate)
            idx = jnp.where(col == k, idxs[k], idx)
        gate_ref[rows, :] = gate
        idx_ref[rows, :] = idx


def _mix_ln(parts, x, g, b, w_r, b_r, tm=ROW_TILE):
    m, d = x.shape
    row = lambda i: (i, 0)
    fix = lambda i: (0, 0)
    in_specs = [pl.BlockSpec((tm, a.shape[1]), row) for a, _ in parts]
    in_specs += [pl.BlockSpec(w.shape, fix) for _, w in parts]
    in_specs += [pl.BlockSpec((tm, d), row), pl.BlockSpec((1, d), fix), pl.BlockSpec((1, d), fix),
                 pl.BlockSpec((d, N_EXP), fix), pl.BlockSpec((1, N_EXP), fix)]
    return pl.pallas_call(
        functools.partial(_mix_ln_kernel, n_parts=len(parts)),
        grid=(m // tm,),
        in_specs=in_specs,
        out_specs=[pl.BlockSpec((tm, d), row), pl.BlockSpec((tm, TOP_K), row), pl.BlockSpec((tm, TOP_K), row)],
        out_shape=[jax.ShapeDtypeStruct((m, d), f32), jax.ShapeDtypeStruct((m, TOP_K), f32),
                   jax.ShapeDtypeStruct((m, TOP_K), i32)],
        compiler_params=_cparams(("parallel",)),
        name="mix_ln",
    )(*[a for a, _ in parts], *[w for _, w in parts], x, g.reshape(1, d), b.reshape(1, d), w_r,
      b_r.reshape(1, N_EXP))


def _moe_kernel(be_ref, nreal_ref, x_ref, w1_ref, b1_ref, w2_ref, b2_ref, o_ref, w1s, w2s):
    i = pl.program_id(0)
    real = i < nreal_ref[0]
    prev = be_ref[jnp.maximum(i - 1, 0)]
    fresh = jnp.logical_or(i == 0, be_ref[i] != prev)

    @pl.when(jnp.logical_and(real, fresh))
    def _():
        w1s[...] = w1_ref[0, 0].astype(bf16)
        w2s[...] = w2_ref[0, 0].astype(bf16)

    @pl.when(real)
    def _():
        x = x_ref[...].astype(bf16)
        acc = jnp.zeros((MOE_ROWS, D_MODEL), f32)
        for c in range(D_FF // FF_CHUNK):
            lo = c * FF_CHUNK
            hg = jnp.dot(x, w1s[:, lo:lo + FF_CHUNK], preferred_element_type=f32)
            hg = hg + b1_ref[0, 0, :, lo:lo + FF_CHUNK]
            hu = jnp.dot(x, w1s[:, D_FF + lo:D_FF + lo + FF_CHUNK], preferred_element_type=f32)
            hu = hu + b1_ref[0, 0, :, D_FF + lo:D_FF + lo + FF_CHUNK]
            g = jnp.minimum(hg, SWIGLU_LIMIT)
            u = jnp.clip(hu, -SWIGLU_LIMIT, SWIGLU_LIMIT)
            a = (u + 1.0) * g * jax.nn.sigmoid(SWIGLU_ALPHA * g)
            acc = acc + jnp.dot(a.astype(bf16), w2s[lo:lo + FF_CHUNK, :], preferred_element_type=f32)
        o_ref[...] = acc + b2_ref[0, 0]

    @pl.when(jnp.logical_not(real))
    def _():
        o_ref[...] = jnp.zeros_like(o_ref)


def _moe_ffn_blocks(xb, blk_exp, nreal, w1, b1, w2, b2, layer):
    p, d = xb.shape
    nblk = p // MOE_ROWS
    n_layers = w1.shape[0]
    per_expert = lambda i, be, nr: (layer, be[i], 0, 0)
    grid_spec = pltpu.PrefetchScalarGridSpec(
        num_scalar_prefetch=2,
        grid=(nblk,),
        in_specs=[
            pl.BlockSpec((MOE_ROWS, d), lambda i, be, nr: (i, 0)),
            pl.BlockSpec((1, 1, d, 2 * D_FF), per_expert),
            pl.BlockSpec((1, 1, 1, 2 * D_FF), per_expert),
            pl.BlockSpec((1, 1, D_FF, d), per_expert),
            pl.BlockSpec((1, 1, 1, d), per_expert),
        ],
        out_specs=pl.BlockSpec((MOE_ROWS, d), lambda i, be, nr: (i, 0)),
        scratch_shapes=[pltpu.VMEM((d, 2 * D_FF), bf16), pltpu.VMEM((D_FF, d), bf16)],
    )
    return pl.pallas_call(
        _moe_kernel,
        grid_spec=grid_spec,
        out_shape=jax.ShapeDtypeStruct((p, d), f32),
        compiler_params=_cparams(("arbitrary",)),
        name="moe_ffn",
    )(blk_exp, nreal, xb, w1, b1.reshape(n_layers, N_EXP, 1, 2 * D_FF), w2, b2.reshape(n_layers, N_EXP, 1, d))


def _row_copy(src, src_row, dst, dst_row, sem):
    return pltpu.make_async_copy(src.at[pl.ds(src_row, 1), :], dst.at[pl.ds(dst_row, 1), :], sem)


def _moe_dispatch_kernel(dest_ref, x_ref, xb_init_ref, xb_ref, sem):
    del xb_init_ref

    def issue(t, c):
        for k in range(TOP_K):
            _row_copy(x_ref, t, xb_ref, dest_ref[0, 0, t * TOP_K + k], sem).start(priority=k % 2)
        return c

    lax.fori_loop(0, TOK_TILE, issue, 0, unroll=2)

    def drain(t, c):
        for k in range(TOP_K):
            _row_copy(x_ref, 0, xb_ref, 0, sem).wait()
        return c

    lax.fori_loop(0, TOK_TILE, drain, 0, unroll=2)


def _moe_dispatch(x1, dest, xb):
    n, d = x1.shape
    nt = n // TOK_TILE
    return pl.pallas_call(
        _moe_dispatch_kernel,
        grid=(nt,),
        in_specs=[pl.BlockSpec((1, 1, TOK_TILE * TOP_K), lambda i: (i, 0, 0), memory_space=pltpu.SMEM),
                  pl.BlockSpec((TOK_TILE, d), lambda i: (i, 0)),
                  pl.BlockSpec(memory_space=pl.ANY)],
        out_specs=pl.BlockSpec(memory_space=pl.ANY),
        out_shape=jax.ShapeDtypeStruct(xb.shape, f32),
        scratch_shapes=[pltpu.SemaphoreType.DMA(())],
        input_output_aliases={2: 0},
        compiler_params=_cparams(("arbitrary",)),
        name="moe_dispatch",
    )(dest.reshape(nt, 1, TOK_TILE * TOP_K), x1, xb)


def _moe_combine_kernel(dest_ref, yb_ref, gate_ref, x1_ref, g_ref, b_ref, o_ref, ybuf, sem):
    def issue(t, c):
        for k in range(TOP_K):
            _row_copy(yb_ref, dest_ref[0, 0, t * TOP_K + k], ybuf.at[k], t, sem).start(priority=k % 2)
        return c

    lax.fori_loop(0, TOK_TILE, issue, 0, unroll=2)

    def drain(t, c):
        for k in range(TOP_K):
            _row_copy(yb_ref, 0, ybuf.at[k], 0, sem).wait()
        return c

    lax.fori_loop(0, TOK_TILE, drain, 0, unroll=2)
    gate = gate_ref[...]
    y = gate[:, 0:1] * ybuf[0]
    for k in range(1, TOP_K):
        y = y + gate[:, k:k + 1] * ybuf[k]
    o_ref[...] = _layer_norm_rows(DN_ALPHA * x1_ref[...] + y, g_ref[...], b_ref[...])


def _moe_combine(yb, dest, gate, x1, g, b):
    n, d = x1.shape
    nt = n // TOK_TILE
    row = lambda i: (i, 0)
    fix = lambda i: (0, 0)
    return pl.pallas_call(
        _moe_combine_kernel,
        grid=(nt,),
        in_specs=[pl.BlockSpec((1, 1, TOK_TILE * TOP_K), lambda i: (i, 0, 0), memory_space=pltpu.SMEM),
                  pl.BlockSpec(memory_space=pl.ANY),
                  pl.BlockSpec((TOK_TILE, TOP_K), row), pl.BlockSpec((TOK_TILE, d), row),
                  pl.BlockSpec((1, d), fix), pl.BlockSpec((1, d), fix)],
        out_specs=pl.BlockSpec((TOK_TILE, d), row),
        out_shape=jax.ShapeDtypeStruct((n, d), f32),
        scratch_shapes=[pltpu.VMEM((TOP_K, TOK_TILE, d), f32), pltpu.SemaphoreType.DMA(())],
        compiler_params=_cparams(("arbitrary",)),
        name="moe_combine",
    )(dest.reshape(nt, 1, TOK_TILE * TOP_K), yb, gate, x1, g.reshape(1, d), b.reshape(1, d))


def _rank_kernel(e_ref, rank_ref, cnt_ref, carry):
    @pl.when(pl.program_id(0) == 0)
    def _():
        carry[...] = jnp.zeros(carry.shape, f32)

    e = e_ref[0]
    onehot = (lax.broadcasted_iota(i32, (N_EXP, RANK_TILE), 0) == e).astype(bf16)
    upper = (lax.broadcasted_iota(i32, (RANK_TILE, RANK_TILE), 0)
             < lax.broadcasted_iota(i32, (RANK_TILE, RANK_TILE), 1)).astype(bf16)
    before = jnp.dot(onehot, upper, preferred_element_type=f32) + carry[:, 0:1]
    hot = onehot.astype(f32)
    rank_ref[0] = jnp.sum(hot * before, axis=0, keepdims=True).astype(i32)
    carry[...] = carry[...] + jnp.sum(hot, axis=1, keepdims=True)
    cnt_ref[...] = carry[...].astype(i32)


def _expert_ranks(flat_e):
    a = flat_e.shape[0]
    nt = a // RANK_TILE
    rank, cnt = pl.pallas_call(
        _rank_kernel,
        grid=(nt,),
        in_specs=[pl.BlockSpec((1, 1, RANK_TILE), lambda i: (i, 0, 0))],
        out_specs=[pl.BlockSpec((1, 1, RANK_TILE), lambda i: (i, 0, 0)),
                   pl.BlockSpec((N_EXP, LANES), lambda i: (0, 0))],
        out_shape=[jax.ShapeDtypeStruct((nt, 1, RANK_TILE), i32), jax.ShapeDtypeStruct((N_EXP, LANES), i32)],
        scratch_shapes=[pltpu.VMEM((N_EXP, LANES), f32)],
        compiler_params=_cparams(("arbitrary",)),
        name="expert_ranks",
    )(flat_e.reshape(nt, 1, RANK_TILE))
    return rank.reshape(a), cnt[:, 0]


def _moe_blocks(a):
    return -(-a // MOE_ROWS) + N_EXP


def _moe_slots(flat_e):
    a = flat_e.shape[0]
    rank, counts = _expert_ranks(flat_e)
    padded = (counts + MOE_ROWS - 1) // MOE_ROWS * MOE_ROWS
    pend = jnp.cumsum(padded)
    dest = (pend - padded)[flat_e] + rank
    blk_start = jnp.arange(_moe_blocks(a), dtype=i32) * MOE_ROWS
    nreal = (pend[-1] // MOE_ROWS).astype(i32)
    blk_exp = jnp.searchsorted(pend, blk_start, side='right').astype(i32)
    last_exp = blk_exp[jnp.maximum(nreal - 1, 0)]
    blk_exp = jnp.where(blk_start < pend[-1], jnp.minimum(blk_exp, N_EXP - 1), last_exp)
    return dest, blk_exp, nreal.reshape(1)


def _mem_attn_kernel(q_ref, kv_ref, o_ref):
    bt, tq, _ = q_ref.shape
    lane = lax.broadcasted_iota(i32, (tq, LANES), 1)
    zero = jnp.zeros((tq, LANES), bf16)
    for pair in range(MEM_HEADS // 2):
        cols = slice(LANES * pair, LANES * (pair + 1))
        qts = [q_ref[b, :, cols] for b in range(bt)]
        kts = [kv_ref[b, :, cols].astype(bf16) for b in range(bt)]
        vts = [kv_ref[b, :, MEM_W + LANES * pair:MEM_W + LANES * (pair + 1)].astype(bf16) for b in range(bt)]
        halves = []
        for h in range(2):
            keep = (lane < MEM_HD) if h == 0 else (lane >= MEM_HD)
            ss = [lax.dot_general(jnp.where(keep, qt, zero), kt, _NT, preferred_element_type=f32)
                  for qt, kt in zip(qts, kts)]
            es = [jnp.exp(s - jnp.max(s, axis=-1, keepdims=True)) for s in ss]
            ps = [(e * (1.0 / jnp.sum(e, axis=-1, keepdims=True))).astype(bf16) for e in es]
            halves.append([jnp.dot(p, vt, preferred_element_type=f32) for p, vt in zip(ps, vts)])
        for b in range(bt):
            o_ref[b, :, cols] = jnp.where(lane < MEM_HD, halves[0][b], halves[1][b])


def _mem_attn(mq, mem_kv):
    b, t, w = mq.shape
    tq = min(t, ROW_TILE)
    bt = max(1, min(b, MEM_ROWS // tq))
    return pl.pallas_call(
        _mem_attn_kernel,
        grid=(b // bt, t // tq),
        in_specs=[pl.BlockSpec((bt, tq, w), lambda bb, i: (bb, i, 0)),
                  pl.BlockSpec((bt, MEM_TOK, 2 * w), lambda bb, i: (bb, 0, 0))],
        out_specs=pl.BlockSpec((bt, tq, w), lambda bb, i: (bb, i, 0)),
        out_shape=jax.ShapeDtypeStruct((b, t, w), f32),
        compiler_params=_cparams(("parallel", "parallel")),
        name="mem_attn",
    )(mq, mem_kv)


def _pad_state(s):
    s = jnp.concatenate([s, jnp.zeros((GLA_KP - GLA_DK, GLA_DV), f32)], axis=0)
    return jnp.concatenate([s, jnp.zeros((GLA_KP, GLA_VP - GLA_DV), f32)], axis=1)


def _gla_kernel(q_ref, k_ref, v_ref, r_ref, glr_ref, wg_ref, bg_ref, gn_ref, s0_ref, o_ref, s_out_ref, st_sc,
                *, bt, chunk, zero_init):
    c = pl.program_id(1)

    @pl.when(c == 0)
    def _():
        if zero_init:
            st_sc[...] = jnp.zeros(st_sc.shape, f32)
        else:
            for b in range(bt):
                for h in range(GLA_HEADS):
                    st_sc[b, h] = jnp.transpose(_pad_state(s0_ref[b, h]))

    tri = lax.broadcasted_iota(i32, (chunk, chunk), 0) >= lax.broadcasted_iota(i32, (chunk, chunk), 1)
    trif = tri.astype(f32)
    problems = [(b, h) for b in range(bt) for h in range(GLA_HEADS)]
    ksl = lambda h: slice(GLA_KP * h, GLA_KP * (h + 1))
    vsl = lambda h: slice(GLA_VP * h, GLA_VP * (h + 1))
    qts, kts, kds, ebl = [], [], [], []
    for b in range(bt):
        z = jnp.dot(glr_ref[b].astype(bf16), wg_ref[...], preferred_element_type=f32) + bg_ref[...]
        log_a = jax.nn.log_sigmoid(z) / GLA_TAU
        bc = jnp.dot(trif, log_a, preferred_element_type=f32, precision=lax.Precision.HIGHEST)
        bl = bc[chunk - 1:chunk, :]
        k = k_ref[b]
        qts.append((q_ref[b] * (GLA_DK ** -0.5) * jnp.exp(bc)).astype(bf16))
        kts.append((k * jnp.exp(-bc)).astype(bf16))
        kds.append((k * jnp.exp(bl - bc)).astype(bf16))
        ebl.append(jnp.exp(bl))
    vbs = [v_ref[b].astype(bf16) for b in range(bt)]
    atts = [jnp.where(tri, lax.dot_general(qts[b][:, ksl(h)], kts[b][:, ksl(h)], _NT, preferred_element_type=f32),
                      0.0).astype(bf16) for b, h in problems]
    sts = [st_sc[b, h] for b, h in problems]
    outs = [lax.dot_general(qts[b][:, ksl(h)], st.astype(bf16), _NT, preferred_element_type=f32)
            + jnp.dot(att, vbs[b][:, vsl(h)], preferred_element_type=f32)
            for (b, h), st, att in zip(problems, sts, atts)]
    for (b, h), st in zip(problems, sts):
        st_sc[b, h] = ebl[b][:, ksl(h)] * st + lax.dot_general(vbs[b][:, vsl(h)], kds[b][:, ksl(h)], _TN,
                                                              preferred_element_type=f32)
    for (b, h), o in zip(problems, outs):
        ms = jnp.sum(o * o, axis=-1, keepdims=True) * (1.0 / GLA_DV)
        o = o * lax.rsqrt(ms + LN_EPS) * gn_ref[:, vsl(h)]
        o_ref[b, :, vsl(h)] = o * jax.nn.silu(r_ref[b, :, vsl(h)])

    @pl.when(c == pl.num_programs(1) - 1)
    def _():
        for b in range(bt):
            for h in range(GLA_HEADS):
                s_out_ref[b, h] = jnp.transpose(st_sc[b, h])[:GLA_DK, :GLA_DV]


def _gla(q, k, v, r, glr, wg, bg, gn, s0, *, bt):
    nb, t, _ = q.shape
    chunk = GLA_CHUNK if t % GLA_CHUNK == 0 else t
    zero_init = s0 is None
    if zero_init:
        s0 = jnp.zeros((1, 1, 8, LANES), f32)
    tok = lambda w: pl.BlockSpec((bt, chunk, w), lambda i, c: (i, c, 0))
    fix = lambda a: pl.BlockSpec(a.shape, lambda i, c: (0,) * a.ndim)
    state = pl.BlockSpec((bt, GLA_HEADS, GLA_DK, GLA_DV), lambda i, c: (i, 0, 0, 0))
    return pl.pallas_call(
        functools.partial(_gla_kernel, bt=bt, chunk=chunk, zero_init=zero_init),
        grid=(nb // bt, t // chunk),
        in_specs=[tok(GLA_HEADS * GLA_KP), tok(GLA_HEADS * GLA_KP), tok(GLA_HEADS * GLA_VP), tok(GLA_HEADS * GLA_VP),
                  tok(LANES), fix(wg), fix(bg), fix(gn), fix(s0) if zero_init else state],
        out_specs=[tok(GLA_HEADS * GLA_VP), state],
        out_shape=[jax.ShapeDtypeStruct((nb, t, GLA_HEADS * GLA_VP), f32),
                   jax.ShapeDtypeStruct((nb, GLA_HEADS, GLA_DK, GLA_DV), f32)],
        scratch_shapes=[pltpu.VMEM((bt, GLA_HEADS, GLA_VP, GLA_KP), f32)],
        compiler_params=_cparams(("parallel", "arbitrary")),
        name="gla",
    )(q, k, v, r, glr, wg, bg, gn, s0)


def _pad_heads(a, w, wp):
    lead = a.shape[:-1]
    a = jnp.pad(a.reshape(*lead, GLA_HEADS, w), [(0, 0)] * (len(lead) + 1) + [(0, wp - w)])
    return a.reshape(*lead, GLA_HEADS * wp)


def _softmax_rows(sm, maskf):
    m = jnp.max(sm, axis=-1, keepdims=True)
    e = jnp.exp(sm - m)
    return e * (1.0 / jnp.sum(e, axis=-1, keepdims=True)) * maskf


def _block_scores_t(imp):
    nc = imp.shape[1]
    jj = lax.broadcasted_iota(i32, (NSB_PAD, nc), 0)
    nn = lax.broadcasted_iota(i32, (NSB_PAD, nc), 1)
    mt = ((nn >> RATIO_SHIFT) == jj).astype(f32) + (((nn + 1) >> RATIO_SHIFT) == jj).astype(f32)
    return lax.dot_general(mt, imp, _NT, preferred_element_type=f32, precision=lax.Precision.HIGHEST)


def _select_t(blk_t, qpos_t):
    nq = blk_t.shape[1]
    j_t = lax.broadcasted_iota(i32, (NSB_PAD, nq), 0)
    valid = j_t * SLC_L <= qpos_t
    cur = qpos_t >> SLC_SHIFT
    forced = (j_t == 0) | (j_t == cur) | (j_t == cur - 1)
    score = jnp.where(valid, jnp.where(forced, 1e9, blk_t), -1e9)
    cnt = jnp.zeros((NSB_PAD, nq), i32)
    for k in range(NSB_PAD):
        row = score[k:k + 1, :]
        beats = (row > score) | ((row == score) & (j_t > k))
        cnt = cnt + beats.astype(i32)
    return ((cnt < SLC_K) & valid).astype(f32)


def _untranspose(sel_t):
    nq = sel_t.shape[1]
    eye = lax.broadcasted_iota(i32, (nq, nq), 0) == lax.broadcasted_iota(i32, (nq, nq), 1)
    return lax.dot_general(eye.astype(bf16), sel_t.astype(bf16), _NT, preferred_element_type=f32).astype(bf16)


def _nsa_prompt_kernel(qp_ref, gl_ref, bg_ref, kc_ref, vc_ref, ks_ref, vs_ref, kw_ref, vw_ref, o_ref,
                       p_sc, m_sc, l_sc, acc_sc, og_sc, used_sm):
    nc = kc_ref.shape[1]
    q0 = pl.program_id(1) * Q_BLK
    qpos_i = q0 + lax.broadcasted_iota(i32, (Q_BLK, 1), 0)
    lane = lax.broadcasted_iota(i32, (Q_BLK, LANES), 1)
    gates = jax.nn.sigmoid(gl_ref[0] + bg_ref[...])
    qp = qp_ref[0]
    zero_b = jnp.zeros((Q_BLK, LANES), bf16)

    cpos = lax.broadcasted_iota(i32, (1, nc), 1) * CMP_S + (CMP_L - 1)
    c_add = jnp.where(qpos_i >= cpos, 0.0, NEG)
    c_rel = (cpos - q0).astype(f32)
    c_any = (qpos_i >= CMP_L - 1).astype(f32)
    w_start = pl.multiple_of(jnp.maximum(q0 - WINDOW, 0), Q_BLK)
    wpos = w_start + lax.broadcasted_iota(i32, (1, WIN_KEYS), 1)
    w_add = jnp.where((qpos_i >= wpos) & (qpos_i - wpos <= WINDOW), 0.0, NEG)
    w_rel = (wpos - q0).astype(f32)

    for g in range(NSA_KV):
        half = (lane < HD) if g == 0 else (lane >= HD)
        qg = jnp.concatenate([jnp.where(half, qp[:, LANES * hh:LANES * (hh + 1)], zero_b)
                              for hh in range(NSA_HPG)], axis=0)
        slopes = [SLOPES[NSA_HPG * g + hh] for hh in range(NSA_HPG)]

        s = lax.dot_general(qg, kc_ref[0], _NT, preferred_element_type=f32)
        imp = jnp.zeros((Q_BLK, nc), f32)
        for hh in range(NSA_HPG):
            rows = slice(hh * Q_BLK, (hh + 1) * Q_BLK)
            sm = s[rows] + slopes[hh] * c_rel + c_add
            e = jnp.exp(sm - jnp.max(sm, axis=-1, keepdims=True))
            p = e * (c_any / jnp.sum(e, axis=-1, keepdims=True))
            imp = imp + p
            p_sc[rows, :nc] = p.astype(bf16)
        o_cmp = jnp.dot(p_sc[:, :nc], vc_ref[0], preferred_element_type=f32)

        qpos_t = q0 + lax.broadcasted_iota(i32, (NSB_PAD, Q_BLK), 1)
        sel_t = _select_t(_block_scores_t(imp), qpos_t)
        sel_b = _untranspose(sel_t)
        blocks_per_tile = SLC_TILE // SLC_L
        for t in range(NSB_PAD // blocks_per_tile):
            used_sm[t] = (jnp.max(sel_t[t * blocks_per_tile:(t + 1) * blocks_per_tile, :]) > 0.5).astype(i32)

        m_sc[...] = jnp.full(m_sc.shape, NEG, f32)
        l_sc[...] = jnp.zeros(l_sc.shape, f32)
        acc_sc[...] = jnp.zeros(acc_sc.shape, f32)

        def tile(t, carry):
            @pl.when(used_sm[t] > 0)
            def _():
                k0 = pl.multiple_of(t * SLC_TILE, SLC_TILE)
                kt = ks_ref[0, pl.ds(k0, SLC_TILE), :]
                vt = vs_ref[0, pl.ds(k0, SLC_TILE), :]
                st = lax.dot_general(qg, kt, _NT, preferred_element_type=f32)
                kpos = k0 + lax.broadcasted_iota(i32, (Q_BLK, SLC_TILE), 1)
                blk_of_key = (k0 + lax.broadcasted_iota(i32, (NSB_PAD, SLC_TILE), 1)) >> SLC_SHIFT
                expand = (lax.broadcasted_iota(i32, (NSB_PAD, SLC_TILE), 0) == blk_of_key).astype(bf16)
                selexp = jnp.dot(sel_b, expand, preferred_element_type=f32)
                addmask = jnp.where((qpos_i >= kpos) & (selexp > 0.5), 0.0, NEG)
                krel = (k0 - q0 + lax.broadcasted_iota(i32, (1, SLC_TILE), 1)).astype(f32)
                for hh in range(NSA_HPG):
                    rows = slice(hh * Q_BLK, (hh + 1) * Q_BLK)
                    sm = st[rows] + slopes[hh] * krel + addmask
                    m_old = m_sc[rows]
                    m_new = jnp.maximum(m_old, jnp.max(sm, axis=-1, keepdims=True))
                    a = jnp.exp(m_old - m_new)
                    e = jnp.exp(sm - m_new)
                    l_sc[rows] = a * l_sc[rows] + jnp.sum(e, axis=-1, keepdims=True)
                    m_sc[rows] = m_new
                    acc_sc[rows] = a * acc_sc[rows]
                    p_sc[rows, :SLC_TILE] = e.astype(bf16)
                acc_sc[...] += jnp.dot(p_sc[:, :SLC_TILE], vt, preferred_element_type=f32)
            return carry

        lax.fori_loop(0, (q0 + Q_BLK + SLC_TILE - 1) // SLC_TILE, tile, 0)

        kt = kw_ref[0, pl.ds(w_start, WIN_KEYS), :]
        vt = vw_ref[0, pl.ds(w_start, WIN_KEYS), :]
        st = lax.dot_general(qg, kt, _NT, preferred_element_type=f32)
        for hh in range(NSA_HPG):
            rows = slice(hh * Q_BLK, (hh + 1) * Q_BLK)
            c = 3 * (NSA_HPG * g + hh) + 2
            sm = st[rows] + slopes[hh] * w_rel + w_add
            e = jnp.exp(sm - jnp.max(sm, axis=-1, keepdims=True))
            p_sc[rows, :WIN_KEYS] = (e * (gates[:, c:c + 1] / jnp.sum(e, axis=-1, keepdims=True))).astype(bf16)
        o_win = jnp.dot(p_sc[:, :WIN_KEYS], vt, preferred_element_type=f32)

        for hh in range(NSA_HPG):
            rows = slice(hh * Q_BLK, (hh + 1) * Q_BLK)
            c = 3 * (NSA_HPG * g + hh)
            og_sc[g, rows] = (gates[:, c:c + 1] * o_cmp[rows]
                              + (gates[:, c + 1:c + 2] / l_sc[rows]) * acc_sc[rows] + o_win[rows])

    for hh in range(NSA_HPG):
        rows = slice(hh * Q_BLK, (hh + 1) * Q_BLK)
        o_ref[0, :, LANES * hh:LANES * (hh + 1)] = jnp.where(lane < HD, og_sc[0, rows], og_sc[1, rows])


def _nsa_prompt_call(qp, gl, bg, kc, vc, ks, vs, kw, vw):
    b, t, w = qp.shape
    nc = kc.shape[1]
    assert t % SLC_TILE == 0 and t >= WIN_KEYS and nc % LANES == 0 and t // SLC_L <= NSB_PAD
    rows = NSA_HPG * Q_BLK
    qblk = lambda bb, i: (bb, i, 0)
    whole = lambda bb, i: (bb, 0, 0)
    return pl.pallas_call(
        _nsa_prompt_kernel,
        grid=(b, t // Q_BLK),
        in_specs=[pl.BlockSpec((1, Q_BLK, w), qblk), pl.BlockSpec((1, Q_BLK, LANES), qblk),
                  pl.BlockSpec((1, LANES), lambda bb, i: (0, 0)),
                  pl.BlockSpec((1, nc, LANES), whole), pl.BlockSpec((1, nc, LANES), whole),
                  pl.BlockSpec((1, t, LANES), whole), pl.BlockSpec((1, t, LANES), whole),
                  pl.BlockSpec((1, t, LANES), whole), pl.BlockSpec((1, t, LANES), whole)],
        out_specs=pl.BlockSpec((1, Q_BLK, w), qblk),
        out_shape=jax.ShapeDtypeStruct((b, t, w), f32),
        scratch_shapes=[pltpu.VMEM((rows, WIN_KEYS), bf16), pltpu.VMEM((rows, 1), f32), pltpu.VMEM((rows, 1), f32),
                        pltpu.VMEM((rows, LANES), f32), pltpu.VMEM((NSA_KV, rows, LANES), f32),
                        pltpu.SMEM((NSB_PAD * SLC_L // SLC_TILE,), i32)],
        compiler_params=_cparams(("parallel", "parallel")),
        name="nsa_prompt",
    )(qp, gl, bg, kc, vc, ks, vs, kw, vw)


def _nsa_sample_kernel(pt_ref, qp_ref, gl_ref, bg_ref, kc_ref, vc_ref, kn_ref, vn_ref, wst_ref, wkn_ref, wvn_ref,
                       *rest, n_pages, past_len):
    page_refs = rest[:n_pages]
    o_ref, k_sc, v_sc = rest[n_pages:]
    nq = qp_ref.shape[1]
    nc = kc_ref.shape[1]
    nkeys = (n_pages + 1) * PAGE_SIZE
    rows_g = NSA_HPG * nq
    qpos_i = past_len + lax.broadcasted_iota(i32, (nq, 1), 0)
    lane = lax.broadcasted_iota(i32, (nq, LANES), 1)
    gates = jax.nn.sigmoid(gl_ref[0] + bg_ref[...])
    qp = qp_ref[0]
    zero_b = jnp.zeros((nq, LANES), bf16)
    qs = jnp.concatenate([jnp.where((lane < HD) if g == 0 else (lane >= HD),
                                    qp[:, LANES * hh:LANES * (hh + 1)], zero_b)
                          for g in range(NSA_KV) for hh in range(NSA_HPG)], axis=0)

    def rows_of(g, hh):
        r0 = g * rows_g + hh * nq
        return slice(r0, r0 + nq)

    n_idx = lax.broadcasted_iota(i32, (nq, nc), 1)
    cd = (qpos_i - (n_idx * CMP_S + (CMP_L - 1))).astype(f32)
    cmask = cd >= 0
    cmaskf = cmask.astype(f32)
    s = lax.dot_general(qs, kc_ref[0], _NT, preferred_element_type=f32)
    imps, ps = [], []
    for g in range(NSA_KV):
        imp = jnp.zeros((nq, nc), f32)
        for hh in range(NSA_HPG):
            p = _softmax_rows(jnp.where(cmask, s[rows_of(g, hh)] - SLOPES[NSA_HPG * g + hh] * cd, NEG), cmaskf)
            imp = imp + p
            ps.append(p)
        imps.append(imp)
    o_cmp = jnp.dot(jnp.concatenate(ps, axis=0).astype(bf16), vc_ref[0], preferred_element_type=f32)

    qpos_t = past_len + lax.broadcasted_iota(i32, (NSB_PAD, nq), 1)
    blk_of_key = lax.broadcasted_iota(i32, (NSB_PAD, nkeys), 1) >> SLC_SHIFT
    expand = (lax.broadcasted_iota(i32, (NSB_PAD, nkeys), 0) == blk_of_key).astype(bf16)
    sd = (qpos_i - lax.broadcasted_iota(i32, (nq, nkeys), 1)).astype(f32)
    masks = []
    for g in range(NSA_KV):
        sel_b = _untranspose(_select_t(_block_scores_t(imps[g]), qpos_t))
        selexp = jnp.dot(sel_b, expand, preferred_element_type=f32)
        masks.append((sd >= 0) & (selexp > 0.5))

    for pg in range(n_pages):
        blk = page_refs[pg][0]
        k_sc[:, pg * PAGE_SIZE:(pg + 1) * PAGE_SIZE] = blk[:LANES].astype(bf16)
        v_sc[:, pg * PAGE_SIZE:(pg + 1) * PAGE_SIZE] = blk[LANES:].astype(bf16)
    tail = jnp.zeros((PAGE_SIZE - nq, LANES), bf16)
    new_t = lambda ref: jnp.transpose(jnp.concatenate([ref[0], tail], axis=0).astype(f32)).astype(bf16)
    k_sc[:, n_pages * PAGE_SIZE:] = new_t(kn_ref)
    v_sc[:, n_pages * PAGE_SIZE:] = new_t(vn_ref)
    s = jnp.dot(qs, k_sc[...], preferred_element_type=f32)
    ps = []
    for g in range(NSA_KV):
        mf = masks[g].astype(f32)
        for hh in range(NSA_HPG):
            ps.append(_softmax_rows(jnp.where(masks[g], s[rows_of(g, hh)] - SLOPES[NSA_HPG * g + hh] * sd, NEG), mf))
    o_slc = lax.dot_general(jnp.concatenate(ps, axis=0).astype(bf16), v_sc[...], _NT, preferred_element_type=f32)

    wb = wst_ref.shape[1]
    wk = jnp.concatenate([wst_ref[0, :, :LANES].astype(bf16), wkn_ref[0], tail], axis=0)
    wv = jnp.concatenate([wst_ref[0, :, LANES:].astype(bf16), wvn_ref[0], tail], axis=0)
    wpos = (past_len - wb) + lax.broadcasted_iota(i32, (nq, wb + PAGE_SIZE), 1)
    wd = (qpos_i - wpos).astype(f32)
    wmask = (wd >= 0) & (wd <= WINDOW) & (wpos >= 0)
    wmaskf = wmask.astype(f32)
    s = lax.dot_general(qs, wk, _NT, preferred_element_type=f32)
    ps = []
    for g in range(NSA_KV):
        for hh in range(NSA_HPG):
            ps.append(_softmax_rows(jnp.where(wmask, s[rows_of(g, hh)] - SLOPES[NSA_HPG * g + hh] * wd, NEG), wmaskf))
    o_win = jnp.dot(jnp.concatenate(ps, axis=0).astype(bf16), wv, preferred_element_type=f32)

    for hh in range(NSA_HPG):
        halves = []
        for g in range(NSA_KV):
            r = rows_of(g, hh)
            c = 3 * (NSA_HPG * g + hh)
            halves.append(gates[:, c:c + 1] * o_cmp[r] + gates[:, c + 1:c + 2] * o_slc[r]
                          + gates[:, c + 2:c + 3] * o_win[r])
        o_ref[0, :, LANES * hh:LANES * (hh + 1)] = jnp.where(lane < HD, halves[0], halves[1])


def _nsa_sample_call(page_table, qp, gl, bg, kc, vc, kn, vn, wst, wkn, wvn, cache):
    b, nq, w = qp.shape
    n_pages = page_table.shape[1]
    nc = kc.shape[1]
    wb = wst.shape[1]
    nkeys = (n_pages + 1) * PAGE_SIZE
    assert nkeys // SLC_L <= NSB_PAD and nq <= PAGE_SIZE
    per_b = lambda bb, pt: (bb, 0, 0)
    in_specs = [pl.BlockSpec((1, nq, w), per_b), pl.BlockSpec((1, nq, LANES), per_b),
                pl.BlockSpec((1, LANES), lambda bb, pt: (0, 0)),
                pl.BlockSpec((1, nc, LANES), per_b), pl.BlockSpec((1, nc, LANES), per_b),
                pl.BlockSpec((1, nq, LANES), per_b), pl.BlockSpec((1, nq, LANES), per_b),
                pl.BlockSpec((1, wb, 2 * LANES), per_b),
                pl.BlockSpec((1, nq, LANES), per_b), pl.BlockSpec((1, nq, LANES), per_b)]
    for pg in range(n_pages):
        in_specs.append(pl.BlockSpec((1, 2 * LANES, PAGE_SIZE),
                                     functools.partial(lambda bb, pt, pg: (pt[bb, pg], 1, 0), pg=pg)))
    grid_spec = pltpu.PrefetchScalarGridSpec(
        num_scalar_prefetch=1, grid=(b,), in_specs=in_specs,
        out_specs=pl.BlockSpec((1, nq, w), per_b),
        scratch_shapes=[pltpu.VMEM((LANES, nkeys), bf16), pltpu.VMEM((LANES, nkeys), bf16)])
    return pl.pallas_call(
        functools.partial(_nsa_sample_kernel, n_pages=n_pages, past_len=n_pages * PAGE_SIZE),
        grid_spec=grid_spec,
        out_shape=jax.ShapeDtypeStruct((b, nq, w), f32),
        compiler_params=_cparams(("parallel",)),
        name="nsa_sample",
    )(page_table, qp, gl, bg, kc, vc, kn, vn, wst, wkn, wvn, *([cache] * n_pages))


def _compress_kernel(x_ref, pos_ref, w1_ref, b1_ref, w2_ref, o_ref):
    x = x_ref[0, 0, 0]
    half = CMP_S * HD
    xa = (x + pos_ref[0, :, :half]).astype(bf16)
    xb = (x + pos_ref[0, :, half:]).astype(bf16)
    a = jnp.dot(xa, w1_ref[0, :half, :], preferred_element_type=f32)
    bsec = jnp.dot(xb, w1_ref[0, half:, :], preferred_element_type=f32)
    nch = x.shape[0]
    hid = jax.nn.gelu(a + pltpu.roll(bsec, nch - 1, 0) + b1_ref[0])
    o_ref[0, 0, 0] = jnp.dot(hid.astype(bf16), w2_ref[0], preferred_element_type=f32).astype(o_ref.dtype)


def _compress_call(x2, cmp_pos, w_cmp1, b_cmp1, w_cmp2):
    b, _, g, nch, cw = x2.shape
    hid = w_cmp1.shape[-1]
    pos = cmp_pos.reshape(2, 1, CMP_L * HD)
    return pl.pallas_call(
        _compress_kernel,
        grid=(b, 2, g),
        in_specs=[pl.BlockSpec((1, 1, 1, nch, cw), lambda bb, s, gg: (bb, s, gg, 0, 0)),
                  pl.BlockSpec((1, 1, CMP_L * HD), lambda bb, s, gg: (s, 0, 0)),
                  pl.BlockSpec((1, CMP_L * HD, hid), lambda bb, s, gg: (s, 0, 0)),
                  pl.BlockSpec((1, 1, hid), lambda bb, s, gg: (s, 0, 0)),
                  pl.BlockSpec((1, hid, HD), lambda bb, s, gg: (s, 0, 0))],
        out_specs=pl.BlockSpec((1, 1, 1, nch, HD), lambda bb, s, gg: (bb, s, gg, 0, 0)),
        out_shape=jax.ShapeDtypeStruct((b, 2, g, nch, HD), bf16),
        compiler_params=_cparams(("parallel", "parallel", "parallel")),
        name="compress",
    )(x2, pos, w_cmp1.astype(bf16), b_cmp1.reshape(2, 1, hid), w_cmp2.astype(bf16))


def _compressed_kv(tok, cmp_pos, w_cmp1, b_cmp1, w_cmp2):
    b, t = tok.shape[:2]
    nch = t // CMP_S
    x2 = jnp.transpose(tok[:, :nch * CMP_S], (0, 2, 3, 1, 4)).reshape(b, 2, NSA_KV, nch, CMP_S * HD)
    ck = _compress_call(x2, cmp_pos, w_cmp1, b_cmp1, w_cmp2)
    ck = jnp.transpose(ck, (0, 1, 3, 2, 4)).reshape(b, 2, nch, NSA_KV * HD)
    return ck[:, 0], ck[:, 1]


def _compress_paged_kernel(pt_ref, pos_ref, w1a_ref, w1b_ref, b1_ref, w2_ref, *rest, n_pages):
    page_refs = rest[:n_pages]
    o_ref, x_sc, xcat_sc = rest[n_pages:]
    nch = n_pages * (PAGE_SIZE // CMP_S)
    role = NSA_KV * HD
    for pg in range(n_pages):
        xt = jnp.transpose(page_refs[pg][0])
        for s in range(2):
            x_sc[s, pg * PAGE_SIZE:(pg + 1) * PAGE_SIZE, :] = xt[:, s * role:(s + 1) * role]
    for s in range(2):
        for t in range(CMP_S):
            xcat_sc[:, t * role:(t + 1) * role] = x_sc[s, pl.ds(t, nch, stride=CMP_S), :]
        xc = xcat_sc[...]
        xa = (xc + pos_ref[s, 0:1, :]).astype(bf16)
        xb = (xc + pos_ref[s, 1:2, :]).astype(bf16)
        a = jnp.dot(xa, w1a_ref[s], preferred_element_type=f32)
        bsec = jnp.dot(xb, w1b_ref[s], preferred_element_type=f32)
        hid = jax.nn.gelu(a + pltpu.roll(bsec, nch - 1, 0) + b1_ref[s])
        o_ref[0, s] = jnp.dot(hid.astype(bf16), w2_ref[s], preferred_element_type=f32).astype(o_ref.dtype)


def _compress_paged(page_table, cache_t, cmp_pos, w_cmp1, b_cmp1, w_cmp2):
    b, n_pages = page_table.shape
    nch = n_pages * (PAGE_SIZE // CMP_S)
    hid = w_cmp1.shape[-1]
    half = CMP_S * HD
    eye = jnp.eye(NSA_KV, dtype=f32)

    def widen_rows(w):
        w = w.reshape(2, CMP_S, HD, hid)
        return jnp.einsum('stdh,ag->stadgh', w, eye).reshape(2, CMP_S * NSA_KV * HD, NSA_KV * hid)

    w1a = widen_rows(w_cmp1[:, :half]).astype(bf16)
    w1b = widen_rows(w_cmp1[:, half:]).astype(bf16)
    w2 = jnp.einsum('shd,ag->sahgd', w_cmp2, eye).reshape(2, NSA_KV * hid, NSA_KV * HD).astype(bf16)
    b1 = jnp.tile(b_cmp1, (1, NSA_KV)).reshape(2, 1, NSA_KV * hid)
    pos = jnp.broadcast_to(cmp_pos.reshape(2, 2, CMP_S, 1, HD), (2, 2, CMP_S, NSA_KV, HD))
    pos = pos.reshape(2, 2, CMP_S * NSA_KV * HD)
    fix = lambda a: pl.BlockSpec(a.shape, lambda bb, pt: (0,) * a.ndim)
    in_specs = [fix(pos), fix(w1a), fix(w1b), fix(b1), fix(w2)]
    for pg in range(n_pages):
        in_specs.append(pl.BlockSpec((1, 2 * NSA_KV * HD, PAGE_SIZE),
                                     functools.partial(lambda bb, pt, pg: (pt[bb, pg], 0, 0), pg=pg)))
    grid_spec = pltpu.PrefetchScalarGridSpec(
        num_scalar_prefetch=1, grid=(b,), in_specs=in_specs,
        out_specs=pl.BlockSpec((1, 2, nch, NSA_KV * HD), lambda bb, pt: (bb, 0, 0, 0)),
        scratch_shapes=[pltpu.VMEM((2, n_pages * PAGE_SIZE, NSA_KV * HD), f32),
                        pltpu.VMEM((nch, CMP_S * NSA_KV * HD), f32)])
    return pl.pallas_call(
        functools.partial(_compress_paged_kernel, n_pages=n_pages),
        grid_spec=grid_spec,
        out_shape=jax.ShapeDtypeStruct((b, 2, nch, NSA_KV * HD), bf16),
        compiler_params=_cparams(("parallel",)),
        name="compress_paged",
    )(page_table, pos, w1a, w1b, b1, w2, *([cache_t] * n_pages))


def _pair_pack_cols(w):
    k = w.shape[0]
    return jnp.transpose(w.reshape(k, NSA_KV, NSA_HPG, HD), (0, 2, 1, 3)).reshape(k, NSA_W)


def _last_rows(a, n):
    t = a.shape[1]
    if t >= n:
        return a[:, t - n:]
    return jnp.pad(a, ((0, 0), (n - t, 0)) + ((0, 0),) * (a.ndim - 2))


def kernel(x_prompt, x_sample, mem_prompt, state_gla, cache_nsa_kv, state_win_kv, cache_mem_kv, page_table,
           w_in_a, w_gate_a, b_gate_a, gla_norm, w_in_b, b_gate_b, w_kv_b, cmp_pos, w_cmp1, b_cmp1, w_cmp2,
           w_mem_kv, w_out, ln1_g, ln1_b, ln2_g, ln2_b, w_router, b_router, w_e1, b_e1, w_e2, b_e2):
    bp, tp, d = x_prompt.shape
    bs, ts, _ = x_sample.shape
    n_p, n_s = bp * tp, bs * ts
    wb = min(WINDOW, state_win_kv.shape[1])
    n_pool = cache_nsa_kv.shape[0]
    past_len = page_table.shape[1] * PAGE_SIZE

    mem_rows = mem_prompt.reshape(bp * MEM_TOK, d)
    w_mem = [w_mem_kv[l].astype(bf16) for l in range(DEPTH)]
    mem_kv_l = _proj(mem_rows, w_mem, _whole(w_mem))
    mem_kv_prompt = jnp.stack(mem_kv_l).reshape(DEPTH, bp, MEM_TOK, 2, MEM_HEADS, MEM_HD)
    mem_p = [m.reshape(bp, MEM_TOK, 2 * MEM_W) for m in mem_kv_l]
    mem_s = cache_mem_kv.reshape(DEPTH, bs, MEM_TOK, 2 * MEM_W)

    x_p, x_s = x_prompt.reshape(n_p, d), x_sample.reshape(n_s, d)
    as_p = lambda a: a.reshape(bp, tp, a.shape[-1])
    as_s = lambda a: a.reshape(bs, ts, a.shape[-1])

    hk = GLA_HEADS * GLA_DK
    cuts_a = [0, hk, 2 * hk, 2 * hk + GLA_W, 2 * hk + 2 * GLA_W, 2 * hk + 2 * GLA_W + GLA_RANK,
              2 * hk + 2 * GLA_W + GLA_RANK + MEM_W]
    mem_scale = MEM_HD ** -0.5
    q_scale = HD ** -0.5
    xb = jnp.zeros((_moe_blocks((n_p + n_s) * TOP_K) * MOE_ROWS, d), f32)

    gla_p, gla_s = [], []
    for l in range(DEPTH):
        if l < N_A:
            wq, wk, wv, wr, wgl, wm = [w_in_a[l][:, cuts_a[i]:cuts_a[i + 1]] for i in range(6)]
            ws = [_pad_heads(wq, GLA_DK, GLA_KP), _pad_heads(wk, GLA_DK, GLA_KP), _pad_heads(wv, GLA_DV, GLA_VP),
                  _pad_heads(wr, GLA_DV, GLA_VP), jnp.pad(wgl, ((0, 0), (0, LANES - GLA_RANK))), wm * mem_scale]
            ws = [w.astype(bf16) for w in ws]
            outs = _whole(ws)
            outs[5] = outs[5][:3] + (bf16,)
            wg = _pad_heads(jnp.pad(w_gate_a[l], ((0, LANES - GLA_RANK), (0, 0))), GLA_DK, GLA_KP).astype(bf16)
            bg = _pad_heads(b_gate_a[l], GLA_DK, GLA_KP).reshape(1, GLA_HEADS * GLA_KP)
            gn = _pad_heads(gla_norm[l], GLA_DV, GLA_VP).reshape(1, GLA_HEADS * GLA_VP)
            q, k, v, r, glr, mq_p = _proj(x_p, ws, outs)
            o_p, s_new = _gla(as_p(q), as_p(k), as_p(v), as_p(r), as_p(glr), wg, bg, gn, None, bt=min(bp, GLA_BT))
            gla_p.append(s_new)
            q, k, v, r, glr, mq_s = _proj(x_s, ws, outs)
            o_s, s_new = _gla(as_s(q), as_s(k), as_s(v), as_s(r), as_s(glr), wg, bg, gn, state_gla[l],
                              bt=min(bs, GLA_BT))
            gla_s.append(s_new)
            w_o = _pad_heads(w_out[l][:GLA_W].T, GLA_DV, GLA_VP).T
        else:
            j = l - N_A
            if l == N_A:
                role = NSA_KV * HD
                kv_outs = [(0, 0, 4 * role, f32), (0, 4 * role, 6 * role, f32)]
                kv_outs += [(0, r * role, (r + 1) * role, bf16) for r in range(2, 6)]
                wkv = [w_kv_b.astype(bf16)]
                rows_p, win_p, ks_p, vs_p, kw_p, vw_p = [as_p(a) for a in _proj(x_p, wkv, kv_outs)]
                rows_s, win_s, ks_s, vs_s, kw_s, vw_s = [as_s(a) for a in _proj(x_s, wkv, kv_outs)]
                kc_p, vc_p = _compressed_kv(rows_p.reshape(bp, tp, 4, NSA_KV, HD)[:, :, :2],
                                            cmp_pos, w_cmp1, b_cmp1, w_cmp2)
                assert (past_len + ts) // CMP_S == past_len // CMP_S
                cache_t = jnp.transpose(cache_nsa_kv, (0, 2, 3, 4, 1)).reshape(n_pool, 4 * role, PAGE_SIZE)
                ck_s = _compress_paged(page_table, cache_t, cmp_pos, w_cmp1, b_cmp1, w_cmp2)
                kc_s, vc_s = ck_s[:, 0], ck_s[:, 1]
                wst = state_win_kv.reshape(bs, state_win_kv.shape[1], 2 * role)
                wkv_s = jnp.concatenate([state_win_kv, win_s.reshape(bs, ts, 2, NSA_KV, HD)], axis=1)
                nsa_out = (rows_p.reshape(bp, tp, 4, NSA_KV, HD), _last_rows(win_p.reshape(bp, tp, 2, NSA_KV, HD), wb),
                           rows_s.reshape(bs, ts, 4, NSA_KV, HD), wkv_s[:, -wb:])
            wq = _pair_pack_cols(w_in_b[j][:, :NSA_W]) * q_scale
            wg = jnp.pad(w_in_b[j][:, NSA_W:NSA_W + 3 * NSA_HEADS], ((0, 0), (0, LANES - 3 * NSA_HEADS)))
            wm = w_in_b[j][:, NSA_W + 3 * NSA_HEADS:] * mem_scale
            ws = [wq.astype(bf16), wg.astype(bf16), wm.astype(bf16)]
            outs = [(0, 0, NSA_W, bf16), (1, 0, LANES, f32), (2, 0, MEM_W, bf16)]
            bg = jnp.pad(b_gate_b[j], (0, LANES - 3 * NSA_HEADS)).reshape(1, LANES)
            qp, gl, mq_p = _proj(x_p, ws, outs)
            o_p = _nsa_prompt_call(as_p(qp), as_p(gl), bg, kc_p, vc_p, ks_p, vs_p, kw_p, vw_p)
            qp, gl, mq_s = _proj(x_s, ws, outs)
            o_s = _nsa_sample_call(page_table, as_s(qp), as_s(gl), bg, kc_s, vc_s, ks_s, vs_s, wst, kw_s, vw_s,
                                   cache_t)
            w_o = _pair_pack_cols(w_out[l][:NSA_W].T).T
        om_p = _mem_attn(as_p(mq_p), mem_p[l])
        om_s = _mem_attn(as_s(mq_s), mem_s[l])
        w_o, w_om = w_o.astype(bf16), w_out[l][-MEM_W:].astype(bf16)
        flat = lambda a: a.reshape(-1, a.shape[-1])
        x1_p, gate_p, idx_p = _mix_ln([(flat(o_p), w_o), (flat(om_p), w_om)], x_p, ln1_g[l], ln1_b[l],
                                      w_router[l], b_router[l])
        x1_s, gate_s, idx_s = _mix_ln([(flat(o_s), w_o), (flat(om_s), w_om)], x_s, ln1_g[l], ln1_b[l],
                                      w_router[l], b_router[l])
        dest, blk_exp, nreal = _moe_slots(jnp.concatenate([idx_p.reshape(-1), idx_s.reshape(-1)]))
        dest_p, dest_s = dest[:n_p * TOP_K], dest[n_p * TOP_K:]
        xb = _moe_dispatch(x1_s, dest_s, _moe_dispatch(x1_p, dest_p, xb))
        yb = _moe_ffn_blocks(xb, blk_exp, nreal, w_e1, b_e1, w_e2, b_e2, l)
        x_p = _moe_combine(yb, dest_p, gate_p, x1_p, ln2_g[l], ln2_b[l])
        x_s = _moe_combine(yb, dest_s, gate_s, x1_s, ln2_g[l], ln2_b[l])

    rows_p, win_p_out, rows_s, win_s_out = nsa_out
    return (x_p.reshape(bp, tp, d), x_s.reshape(bs, ts, d),
            jnp.stack(gla_p), jnp.stack(gla_s), rows_p, rows_s, win_p_out, win_s_out, mem_kv_prompt)
```

```python
import functools
import math

import jax
import jax.numpy as jnp
from jax import lax
from jax.experimental import pallas as pl
from jax.experimental.pallas import tpu as pltpu

D_MODEL = 1024
DEPTH = 4
PAGE_SIZE = 128
N_A = DEPTH // 2
GLA_HEADS = 4
GLA_DV = (3 * D_MODEL) // (4 * GLA_HEADS)
GLA_DK = GLA_DV // 2
GLA_RANK = 16
GLA_TAU = 16.0
GLA_CHUNK = 64
HD = 64
NSA_HEADS = (3 * D_MODEL) // (4 * HD)
NSA_KV = 2
NSA_HPG = NSA_HEADS // NSA_KV
CMP_S = 16
CMP_L = 2 * CMP_S
SLC_L = 64
SLC_K = 16
WINDOW = 512
Q_BLK = 128
MEM_TOK = 256
MEM_HEADS = 4
MEM_HD = D_MODEL // (4 * MEM_HEADS)
N_EXP = 32
TOP_K = 4
D_FF = D_MODEL
SWIGLU_LIMIT = 7.0
SWIGLU_ALPHA = 1.702
GLA_W = GLA_HEADS * GLA_DV
NSA_W = NSA_HEADS * HD
MEM_W = MEM_HEADS * MEM_HD
DN_ALPHA = (2 * DEPTH) ** 0.25
LN_EPS = 1e-5
NEG = -1e30

f32, bf16, i32 = jnp.float32, jnp.bfloat16, jnp.int32

VMEM_LIMIT_BYTES = 56 * 1024 * 1024
LANES = 128
ROW_TILE = 512
MOE_ROWS = 256
FF_CHUNK = 512
SLC_TILE = 512
WIN_KEYS = WINDOW + Q_BLK
NSB_PAD = 64
TOK_TILE = 256
RANK_TILE = 512
MEM_ROWS = 64
GLA_KP = 128
GLA_VP = 256
GLA_BT = 8
_NT = (((1,), (1,)), ((), ()))
_TN = (((0,), (0,)), ((), ()))


def _cparams(sem):
    return pltpu.CompilerParams(dimension_semantics=sem, vmem_limit_bytes=VMEM_LIMIT_BYTES)


def _alibi_slopes(n):
    def pow2(m):
        start = 2.0 ** (-8.0 / m)
        return [start ** (i + 1) for i in range(m)]
    if math.log2(n).is_integer():
        return pow2(n)
    c = 2 ** math.floor(math.log2(n))
    return pow2(c) + pow2(2 * c)[0::2][: n - c]


SLOPES = _alibi_slopes(NSA_HEADS)
SLC_SHIFT = int(math.log2(SLC_L))
RATIO_SHIFT = int(math.log2(SLC_L // CMP_S))


def _proj_kernel(x_ref, *refs, n_w, outs):
    xb = x_ref[...].astype(bf16)
    res = {}
    for o_ref, (wi, lo, hi) in zip(refs[n_w:], outs):
        if wi not in res:
            res[wi] = jnp.dot(xb, refs[wi][...], preferred_element_type=f32)
        o_ref[...] = res[wi][:, lo:hi].astype(o_ref.dtype)


def _proj(x, ws, outs, tm=ROW_TILE):
    m, k = x.shape
    tm = min(tm, m)
    in_specs = [pl.BlockSpec((tm, k), lambda i: (i, 0))]
    in_specs += [pl.BlockSpec(w.shape, lambda i: (0, 0)) for w in ws]
    out_specs = [pl.BlockSpec((tm, hi - lo), lambda i: (i, 0)) for _, lo, hi, _ in outs]
    out_shape = [jax.ShapeDtypeStruct((m, hi - lo), dt) for _, lo, hi, dt in outs]
    return pl.pallas_call(
        functools.partial(_proj_kernel, n_w=len(ws), outs=tuple(o[:3] for o in outs)),
        grid=(m // tm,),
        in_specs=in_specs,
        out_specs=out_specs,
        out_shape=out_shape,
        compiler_params=_cparams(("parallel",)),
        name="proj",
    )(x, *ws)


def _whole(ws, dtype=f32):
    return [(i, 0, w.shape[1], dtype) for i, w in enumerate(ws)]


def _layer_norm_rows(y, g, b):
    mu = jnp.mean(y, axis=-1, keepdims=True)
    d = y - mu
    var = jnp.mean(d * d, axis=-1, keepdims=True)
    return d * lax.rsqrt(var + LN_EPS) * g + b


def _mix_ln_kernel(*refs, n_parts):
    a_refs, w_refs = refs[:n_parts], refs[n_parts:2 * n_parts]
    x_ref, g_ref, b_ref, wr_ref, br_ref, x1_ref, gate_ref, idx_ref = refs[2 * n_parts:]
    tm = x_ref.shape[0]
    sub = tm // 2 if tm % 16 == 0 else tm
    for r0 in range(0, tm, sub):
        rows = slice(r0, r0 + sub)
        mix = jnp.dot(a_refs[0][rows, :].astype(bf16), w_refs[0][...], preferred_element_type=f32)
        for a_ref, w_ref in zip(a_refs[1:], w_refs[1:]):
            mix = mix + jnp.dot(a_ref[rows, :].astype(bf16), w_ref[...], preferred_element_type=f32)
        x1 = _layer_norm_rows(DN_ALPHA * x_ref[rows, :] + mix, g_ref[...], b_ref[...])
        x1_ref[rows, :] = x1
        lg = jnp.dot(x1, wr_ref[...], preferred_element_type=f32, precision=lax.Precision.HIGHEST) + br_ref[...]
        lane = lax.broadcasted_iota(i32, lg.shape, 1)
        vals, idxs = [], []
        for _ in range(TOP_K):
            m = jnp.max(lg, axis=-1, keepdims=True)
            i = jnp.min(jnp.where(lg == m, lane, N_EXP), axis=-1, keepdims=True)
            vals.append(m)
            idxs.append(i)
            lg = jnp.where(lane == i, -jnp.inf, lg)
        es = [jnp.exp(v - vals[0]) for v in vals]
        tot = es[0] + es[1] + es[2] + es[3]
        col = lax.broadcasted_iota(i32, (sub, TOP_K), 1)
        gate = jnp.zeros((sub, TOP_K), f32)
        idx = jnp.zeros((sub, TOP_K), i32)
        for k in range(TOP_K):
            gate = jnp.where(col == k, es[k] / tot, gate)
            idx = jnp.where(col == k, idxs[k], idx)
        gate_ref[rows, :] = gate
        idx_ref[rows, :] = idx


def _mix_ln(parts, x, g, b, w_r, b_r, tm=ROW_TILE):
    m, d = x.shape
    row = lambda i: (i, 0)
    fix = lambda i: (0, 0)
    in_specs = [pl.BlockSpec((tm, a.shape[1]), row) for a, _ in parts]
    in_specs += [pl.BlockSpec(w.shape, fix) for _, w in parts]
    in_specs += [pl.BlockSpec((tm, d), row), pl.BlockSpec((1, d), fix), pl.BlockSpec((1, d), fix),
                 pl.BlockSpec((d, N_EXP), fix), pl.BlockSpec((1, N_EXP), fix)]
    return pl.pallas_call(
        functools.partial(_mix_ln_kernel, n_parts=len(parts)),
        grid=(m // tm,),
        in_specs=in_specs,
        out_specs=[pl.BlockSpec((tm, d), row), pl.BlockSpec((tm, TOP_K), row), pl.BlockSpec((tm, TOP_K), row)],
        out_shape=[jax.ShapeDtypeStruct((m, d), f32), jax.ShapeDtypeStruct((m, TOP_K), f32),
                   jax.ShapeDtypeStruct((m, TOP_K), i32)],
        compiler_params=_cparams(("parallel",)),
        name="mix_ln",
    )(*[a for a, _ in parts], *[w for _, w in parts], x, g.reshape(1, d), b.reshape(1, d), w_r,
      b_r.reshape(1, N_EXP))


def _moe_kernel(be_ref, nreal_ref, x_ref, w1_ref, b1_ref, w2_ref, b2_ref, o_ref, w1s, w2s):
    i = pl.program_id(0)
    real = i < nreal_ref[0]
    prev = be_ref[jnp.maximum(i - 1, 0)]
    fresh = jnp.logical_or(i == 0, be_ref[i] != prev)

    @pl.when(jnp.logical_and(real, fresh))
    def _():
        w1s[...] = w1_ref[0, 0].astype(bf16)
        w2s[...] = w2_ref[0, 0].astype(bf16)

    @pl.when(real)
    def _():
        x = x_ref[...].astype(bf16)
        acc = jnp.zeros((MOE_ROWS, D_MODEL), f32)
        for c in range(D_FF // FF_CHUNK):
            lo = c * FF_CHUNK
            hg = jnp.dot(x, w1s[:, lo:lo + FF_CHUNK], preferred_element_type=f32)
            hg = hg + b1_ref[0, 0, :, lo:lo + FF_CHUNK]
            hu = jnp.dot(x, w1s[:, D_FF + lo:D_FF + lo + FF_CHUNK], preferred_element_type=f32)
            hu = hu + b1_ref[0, 0, :, D_FF + lo:D_FF + lo + FF_CHUNK]
            g = jnp.minimum(hg, SWIGLU_LIMIT)
            u = jnp.clip(hu, -SWIGLU_LIMIT, SWIGLU_LIMIT)
            a = (u + 1.0) * g * jax.nn.sigmoid(SWIGLU_ALPHA * g)
            acc = acc + jnp.dot(a.astype(bf16), w2s[lo:lo + FF_CHUNK, :], preferred_element_type=f32)
        o_ref[...] = acc + b2_ref[0, 0]

    @pl.when(jnp.logical_not(real))
    def _():
        o_ref[...] = jnp.zeros_like(o_ref)


def _moe_ffn_blocks(xb, blk_exp, nreal, w1, b1, w2, b2, layer):
    p, d = xb.shape
    nblk = p // MOE_ROWS
    n_layers = w1.shape[0]
    per_expert = lambda i, be, nr: (layer, be[i], 0, 0)
    grid_spec = pltpu.PrefetchScalarGridSpec(
        num_scalar_prefetch=2,
        grid=(nblk,),
        in_specs=[
            pl.BlockSpec((MOE_ROWS, d), lambda i, be, nr: (i, 0)),
            pl.BlockSpec((1, 1, d, 2 * D_FF), per_expert),
            pl.BlockSpec((1, 1, 1, 2 * D_FF), per_expert),
            pl.BlockSpec((1, 1, D_FF, d), per_expert),
            pl.BlockSpec((1, 1, 1, d), per_expert),
        ],
        out_specs=pl.BlockSpec((MOE_ROWS, d), lambda i, be, nr: (i, 0)),
        scratch_shapes=[pltpu.VMEM((d, 2 * D_FF), bf16), pltpu.VMEM((D_FF, d), bf16)],
    )
    return pl.pallas_call(
        _moe_kernel,
        grid_spec=grid_spec,
        out_shape=jax.ShapeDtypeStruct((p, d), f32),
        compiler_params=_cparams(("arbitrary",)),
        name="moe_ffn",
    )(blk_exp, nreal, xb, w1, b1.reshape(n_layers, N_EXP, 1, 2 * D_FF), w2, b2.reshape(n_layers, N_EXP, 1, d))


def _row_copy(src, src_row, dst, dst_row, sem):
    return pltpu.make_async_copy(src.at[pl.ds(src_row, 1), :], dst.at[pl.ds(dst_row, 1), :], sem)


def _moe_dispatch_kernel(dest_ref, x_ref, xb_init_ref, xb_ref, sem):
    del xb_init_ref

    def issue(t, c):
        for k in range(TOP_K):
            _row_copy(x_ref, t, xb_ref, dest_ref[0, 0, t * TOP_K + k], sem).start(priority=k % 2)
        return c

    lax.fori_loop(0, TOK_TILE, issue, 0, unroll=2)

    def drain(t, c):
        for k in range(TOP_K):
            _row_copy(x_ref, 0, xb_ref, 0, sem).wait()
        return c

    lax.fori_loop(0, TOK_TILE, drain, 0, unroll=2)


def _moe_dispatch(x1, dest, xb):
    n, d = x1.shape
    nt = n // TOK_TILE
    return pl.pallas_call(
        _moe_dispatch_kernel,
        grid=(nt,),
        in_specs=[pl.BlockSpec((1, 1, TOK_TILE * TOP_K), lambda i: (i, 0, 0), memory_space=pltpu.SMEM),
                  pl.BlockSpec((TOK_TILE, d), lambda i: (i, 0)),
                  pl.BlockSpec(memory_space=pl.ANY)],
        out_specs=pl.BlockSpec(memory_space=pl.ANY),
        out_shape=jax.ShapeDtypeStruct(xb.shape, f32),
        scratch_shapes=[pltpu.SemaphoreType.DMA(())],
        input_output_aliases={2: 0},
        compiler_params=_cparams(("arbitrary",)),
        name="moe_dispatch",
    )(dest.reshape(nt, 1, TOK_TILE * TOP_K), x1, xb)


def _moe_combine_kernel(dest_ref, yb_ref, gate_ref, x1_ref, g_ref, b_ref, o_ref, ybuf, sem):
    def issue(t, c):
        for k in range(TOP_K):
            _row_copy(yb_ref, dest_ref[0, 0, t * TOP_K + k], ybuf.at[k], t, sem).start(priority=k % 2)
        return c

    lax.fori_loop(0, TOK_TILE, issue, 0, unroll=2)

    def drain(t, c):
        for k in range(TOP_K):
            _row_copy(yb_ref, 0, ybuf.at[k], 0, sem).wait()
        return c

    lax.fori_loop(0, TOK_TILE, drain, 0, unroll=2)
    gate = gate_ref[...]
    y = gate[:, 0:1] * ybuf[0]
    for k in range(1, TOP_K):
        y = y + gate[:, k:k + 1] * ybuf[k]
    o_ref[...] = _layer_norm_rows(DN_ALPHA * x1_ref[...] + y, g_ref[...], b_ref[...])


def _moe_combine(yb, dest, gate, x1, g, b):
    n, d = x1.shape
    nt = n // TOK_TILE
    row = lambda i: (i, 0)
    fix = lambda i: (0, 0)
    return pl.pallas_call(
        _moe_combine_kernel,
        grid=(nt,),
        in_specs=[pl.BlockSpec((1, 1, TOK_TILE * TOP_K), lambda i: (i, 0, 0), memory_space=pltpu.SMEM),
                  pl.BlockSpec(memory_space=pl.ANY),
                  pl.BlockSpec((TOK_TILE, TOP_K), row), pl.BlockSpec((TOK_TILE, d), row),
                  pl.BlockSpec((1, d), fix), pl.BlockSpec((1, d), fix)],
        out_specs=pl.BlockSpec((TOK_TILE, d), row),
        out_shape=jax.ShapeDtypeStruct((n, d), f32),
        scratch_shapes=[pltpu.VMEM((TOP_K, TOK_TILE, d), f32), pltpu.SemaphoreType.DMA(())],
        compiler_params=_cparams(("arbitrary",)),
        name="moe_combine",
    )(dest.reshape(nt, 1, TOK_TILE * TOP_K), yb, gate, x1, g.reshape(1, d), b.reshape(1, d))


def _rank_kernel(e_ref, rank_ref, cnt_ref, carry):
    @pl.when(pl.program_id(0) == 0)
    def _():
        carry[...] = jnp.zeros(carry.shape, f32)

    e = e_ref[0]
    onehot = (lax.broadcasted_iota(i32, (N_EXP, RANK_TILE), 0) == e).astype(bf16)
    upper = (lax.broadcasted_iota(i32, (RANK_TILE, RANK_TILE), 0)
             < lax.broadcasted_iota(i32, (RANK_TILE, RANK_TILE), 1)).astype(bf16)
    before = jnp.dot(onehot, upper, preferred_element_type=f32) + carry[:, 0:1]
    hot = onehot.astype(f32)
    rank_ref[0] = jnp.sum(hot * before, axis=0, keepdims=True).astype(i32)
    carry[...] = carry[...] + jnp.sum(hot, axis=1, keepdims=True)
    cnt_ref[...] = carry[...].astype(i32)


def _expert_ranks(flat_e):
    a = flat_e.shape[0]
    nt = a // RANK_TILE
    rank, cnt = pl.pallas_call(
        _rank_kernel,
        grid=(nt,),
        in_specs=[pl.BlockSpec((1, 1, RANK_TILE), lambda i: (i, 0, 0))],
        out_specs=[pl.BlockSpec((1, 1, RANK_TILE), lambda i: (i, 0, 0)),
                   pl.BlockSpec((N_EXP, LANES), lambda i: (0, 0))],
        out_shape=[jax.ShapeDtypeStruct((nt, 1, RANK_TILE), i32), jax.ShapeDtypeStruct((N_EXP, LANES), i32)],
        scratch_shapes=[pltpu.VMEM((N_EXP, LANES), f32)],
        compiler_params=_cparams(("arbitrary",)),
        name="expert_ranks",
    )(flat_e.reshape(nt, 1, RANK_TILE))
    return rank.reshape(a), cnt[:, 0]


def _moe_blocks(a):
    return -(-a // MOE_ROWS) + N_EXP


def _moe_slots(flat_e):
    a = flat_e.shape[0]
    rank, counts = _expert_ranks(flat_e)
    padded = (counts + MOE_ROWS - 1) // MOE_ROWS * MOE_ROWS
    eidx = jnp.arange(N_EXP, dtype=i32)
    pend = jnp.sum(jnp.where(eidx[None, :] <= eidx[:, None], padded[None, :], 0), axis=1)
    dest = (pend - padded)[flat_e] + rank
    blk_start = jnp.arange(_moe_blocks(a), dtype=i32) * MOE_ROWS
    nreal = (pend[-1] // MOE_ROWS).astype(i32)
    blk_exp = jnp.sum((pend[None, :] <= blk_start[:, None]).astype(i32), axis=1)
    last_exp = blk_exp[jnp.maximum(nreal - 1, 0)]
    blk_exp = jnp.where(blk_start < pend[-1], jnp.minimum(blk_exp, N_EXP - 1), last_exp)
    return dest, blk_exp, nreal.reshape(1)


def _mem_attn_kernel(q_ref, kv_ref, o_ref, *, feat_major):
    bt, tq, _ = q_ref.shape
    lane = lax.broadcasted_iota(i32, (tq, LANES), 1)
    zero = jnp.zeros((tq, LANES), bf16)

    def kv(b, lo):
        blk = kv_ref[0, b, lo:lo + LANES, :] if feat_major else kv_ref[b, :, lo:lo + LANES]
        return blk.astype(bf16)

    for pair in range(MEM_HEADS // 2):
        cols = slice(LANES * pair, LANES * (pair + 1))
        qts = [q_ref[b, :, cols] for b in range(bt)]
        kts = [kv(b, LANES * pair) for b in range(bt)]
        vts = [kv(b, MEM_W + LANES * pair) for b in range(bt)]
        halves = []
        for h in range(2):
            keep = (lane < MEM_HD) if h == 0 else (lane >= MEM_HD)
            qms = [jnp.where(keep, qt, zero) for qt in qts]
            if feat_major:
                ss = [jnp.dot(qm, kt, preferred_element_type=f32) for qm, kt in zip(qms, kts)]
            else:
                ss = [lax.dot_general(qm, kt, _NT, preferred_element_type=f32) for qm, kt in zip(qms, kts)]
            es = [jnp.exp(s - jnp.max(s, axis=-1, keepdims=True)) for s in ss]
            ps = [(e * (1.0 / jnp.sum(e, axis=-1, keepdims=True))).astype(bf16) for e in es]
            if feat_major:
                halves.append([lax.dot_general(p, vt, _NT, preferred_element_type=f32) for p, vt in zip(ps, vts)])
            else:
                halves.append([jnp.dot(p, vt, preferred_element_type=f32) for p, vt in zip(ps, vts)])
        for b in range(bt):
            o_ref[b, :, cols] = jnp.where(lane < MEM_HD, halves[0][b], halves[1][b])


def _mem_attn(mq, mem_kv, layer=None):
    b, t, w = mq.shape
    tq = min(t, ROW_TILE)
    bt = max(1, min(b, MEM_ROWS // tq))
    if layer is None:
        kv_spec = pl.BlockSpec((bt, MEM_TOK, 2 * w), lambda bb, i: (bb, 0, 0))
    else:
        kv_spec = pl.BlockSpec((1, bt, 2 * w, MEM_TOK), lambda bb, i: (layer, bb, 0, 0))
    return pl.pallas_call(
        functools.partial(_mem_attn_kernel, feat_major=layer is not None),
        grid=(b // bt, t // tq),
        in_specs=[pl.BlockSpec((bt, tq, w), lambda bb, i: (bb, i, 0)), kv_spec],
        out_specs=pl.BlockSpec((bt, tq, w), lambda bb, i: (bb, i, 0)),
        out_shape=jax.ShapeDtypeStruct((b, t, w), f32),
        compiler_params=_cparams(("parallel", "parallel")),
        name="mem_attn",
    )(mq, mem_kv)


def _pad_state(s):
    s = jnp.concatenate([s, jnp.zeros((GLA_KP - GLA_DK, GLA_DV), f32)], axis=0)
    return jnp.concatenate([s, jnp.zeros((GLA_KP, GLA_VP - GLA_DV), f32)], axis=1)


def _gla_kernel(q_ref, k_ref, v_ref, r_ref, glr_ref, wg_ref, bg_ref, gn_ref, s0_ref, o_ref, s_out_ref, st_sc,
                *, bt, chunk, zero_init):
    c = pl.program_id(1)

    @pl.when(c == 0)
    def _():
        if zero_init:
            st_sc[...] = jnp.zeros(st_sc.shape, f32)
        else:
            for b in range(bt):
                for h in range(GLA_HEADS):
                    st_sc[b, h] = jnp.transpose(_pad_state(s0_ref[b, h]))

    tri = lax.broadcasted_iota(i32, (chunk, chunk), 0) >= lax.broadcasted_iota(i32, (chunk, chunk), 1)
    trif = tri.astype(f32)
    problems = [(b, h) for b in range(bt) for h in range(GLA_HEADS)]
    ksl = lambda h: slice(GLA_KP * h, GLA_KP * (h + 1))
    vsl = lambda h: slice(GLA_VP * h, GLA_VP * (h + 1))
    qts, kts, kds, ebl = [], [], [], []
    for b in range(bt):
        z = jnp.dot(glr_ref[b].astype(bf16), wg_ref[...], preferred_element_type=f32) + bg_ref[...]
        log_a = jax.nn.log_sigmoid(z) / GLA_TAU
        bc = jnp.dot(trif, log_a, preferred_element_type=f32, precision=lax.Precision.HIGHEST)
        bl = bc[chunk - 1:chunk, :]
        k = k_ref[b]
        qts.append((q_ref[b] * (GLA_DK ** -0.5) * jnp.exp(bc)).astype(bf16))
        kts.append((k * jnp.exp(-bc)).astype(bf16))
        kds.append((k * jnp.exp(bl - bc)).astype(bf16))
        ebl.append(jnp.exp(bl))
    vbs = [v_ref[b].astype(bf16) for b in range(bt)]
    atts = [jnp.where(tri, lax.dot_general(qts[b][:, ksl(h)], kts[b][:, ksl(h)], _NT, preferred_element_type=f32),
                      0.0).astype(bf16) for b, h in problems]
    sts = [st_sc[b, h] for b, h in problems]
    outs = [lax.dot_general(qts[b][:, ksl(h)], st.astype(bf16), _NT, preferred_element_type=f32)
            + jnp.dot(att, vbs[b][:, vsl(h)], preferred_element_type=f32)
            for (b, h), st, att in zip(problems, sts, atts)]
    for (b, h), st in zip(problems, sts):
        st_sc[b, h] = ebl[b][:, ksl(h)] * st + lax.dot_general(vbs[b][:, vsl(h)], kds[b][:, ksl(h)], _TN,
                                                              preferred_element_type=f32)
    for (b, h), o in zip(problems, outs):
        ms = jnp.sum(o * o, axis=-1, keepdims=True) * (1.0 / GLA_DV)
        o = o * lax.rsqrt(ms + LN_EPS) * gn_ref[:, vsl(h)]
        o_ref[b, :, vsl(h)] = o * jax.nn.silu(r_ref[b, :, vsl(h)])

    @pl.when(c == pl.num_programs(1) - 1)
    def _():
        for b in range(bt):
            for h in range(GLA_HEADS):
                s_out_ref[b, h] = jnp.transpose(st_sc[b, h])[:GLA_DK, :GLA_DV]


def _gla(q, k, v, r, glr, wg, bg, gn, s0, *, bt):
    nb, t, _ = q.shape
    chunk = GLA_CHUNK if t % GLA_CHUNK == 0 else t
    zero_init = s0 is None
    if zero_init:
        s0 = jnp.zeros((1, 1, 8, LANES), f32)
    tok = lambda w: pl.BlockSpec((bt, chunk, w), lambda i, c: (i, c, 0))
    fix = lambda a: pl.BlockSpec(a.shape, lambda i, c: (0,) * a.ndim)
    state = pl.BlockSpec((bt, GLA_HEADS, GLA_DK, GLA_DV), lambda i, c: (i, 0, 0, 0))
    return pl.pallas_call(
        functools.partial(_gla_kernel, bt=bt, chunk=chunk, zero_init=zero_init),
        grid=(nb // bt, t // chunk),
        in_specs=[tok(GLA_HEADS * GLA_KP), tok(GLA_HEADS * GLA_KP), tok(GLA_HEADS * GLA_VP), tok(GLA_HEADS * GLA_VP),
                  tok(LANES), fix(wg), fix(bg), fix(gn), fix(s0) if zero_init else state],
        out_specs=[tok(GLA_HEADS * GLA_VP), state],
        out_shape=[jax.ShapeDtypeStruct((nb, t, GLA_HEADS * GLA_VP), f32),
                   jax.ShapeDtypeStruct((nb, GLA_HEADS, GLA_DK, GLA_DV), f32)],
        scratch_shapes=[pltpu.VMEM((bt, GLA_HEADS, GLA_VP, GLA_KP), f32)],
        compiler_params=_cparams(("parallel", "arbitrary")),
        name="gla",
    )(q, k, v, r, glr, wg, bg, gn, s0)


def _pad_heads(a, w, wp):
    lead = a.shape[:-1]
    a = jnp.pad(a.reshape(*lead, GLA_HEADS, w), [(0, 0)] * (len(lead) + 1) + [(0, wp - w)])
    return a.reshape(*lead, GLA_HEADS * wp)


def _softmax_rows(sm, maskf):
    m = jnp.max(sm, axis=-1, keepdims=True)
    e = jnp.exp(sm - m)
    return e * (1.0 / jnp.sum(e, axis=-1, keepdims=True)) * maskf


def _block_scores_t(imp):
    nc = imp.shape[1]
    jj = lax.broadcasted_iota(i32, (NSB_PAD, nc), 0)
    nn = lax.broadcasted_iota(i32, (NSB_PAD, nc), 1)
    mt = ((nn >> RATIO_SHIFT) == jj).astype(f32) + (((nn + 1) >> RATIO_SHIFT) == jj).astype(f32)
    return lax.dot_general(mt, imp, _NT, preferred_element_type=f32, precision=lax.Precision.HIGHEST)


def _select_t(blk_t, qpos_t):
    nq = blk_t.shape[1]
    j_t = lax.broadcasted_iota(i32, (NSB_PAD, nq), 0)
    valid = j_t * SLC_L <= qpos_t
    cur = qpos_t >> SLC_SHIFT
    forced = (j_t == 0) | (j_t == cur) | (j_t == cur - 1)
    score = jnp.where(valid, jnp.where(forced, 1e9, blk_t), -1e9)
    cnt = jnp.zeros((NSB_PAD, nq), i32)
    for k in range(NSB_PAD):
        row = score[k:k + 1, :]
        beats = (row > score) | ((row == score) & (j_t > k))
        cnt = cnt + beats.astype(i32)
    return ((cnt < SLC_K) & valid).astype(f32)


def _untranspose(sel_t):
    nq = sel_t.shape[1]
    eye = lax.broadcasted_iota(i32, (nq, nq), 0) == lax.broadcasted_iota(i32, (nq, nq), 1)
    return lax.dot_general(eye.astype(bf16), sel_t.astype(bf16), _NT, preferred_element_type=f32).astype(bf16)


def _nsa_prompt_kernel(qp_ref, gl_ref, bg_ref, kc_ref, vc_ref, ks_ref, vs_ref, kw_ref, vw_ref, o_ref,
                       p_sc, m_sc, l_sc, acc_sc, og_sc, used_sm):
    nc = kc_ref.shape[1]
    q0 = pl.program_id(1) * Q_BLK
    qpos_i = q0 + lax.broadcasted_iota(i32, (Q_BLK, 1), 0)
    lane = lax.broadcasted_iota(i32, (Q_BLK, LANES), 1)
    gates = jax.nn.sigmoid(gl_ref[0] + bg_ref[...])
    qp = qp_ref[0]
    zero_b = jnp.zeros((Q_BLK, LANES), bf16)

    for g in range(NSA_KV):
        half = (lane < HD) if g == 0 else (lane >= HD)
        qg = jnp.concatenate([jnp.where(half, qp[:, LANES * hh:LANES * (hh + 1)], zero_b)
                              for hh in range(NSA_HPG)], axis=0)
        slopes = [SLOPES[NSA_HPG * g + hh] for hh in range(NSA_HPG)]

        n_idx = lax.broadcasted_iota(i32, (Q_BLK, nc), 1)
        cd = (qpos_i - (n_idx * CMP_S + (CMP_L - 1))).astype(f32)
        cmask = cd >= 0
        cmaskf = cmask.astype(f32)
        s = lax.dot_general(qg, kc_ref[0], _NT, preferred_element_type=f32)
        imp = jnp.zeros((Q_BLK, nc), f32)
        for hh in range(NSA_HPG):
            rows = slice(hh * Q_BLK, (hh + 1) * Q_BLK)
            p = _softmax_rows(jnp.where(cmask, s[rows] - slopes[hh] * cd, NEG), cmaskf)
            imp = imp + p
            p_sc[rows, :nc] = p.astype(bf16)
        o_cmp = jnp.dot(p_sc[:, :nc], vc_ref[0], preferred_element_type=f32)

        qpos_t = q0 + lax.broadcasted_iota(i32, (NSB_PAD, Q_BLK), 1)
        sel_t = _select_t(_block_scores_t(imp), qpos_t)
        sel_b = _untranspose(sel_t)
        blocks_per_tile = SLC_TILE // SLC_L
        for t in range(NSB_PAD // blocks_per_tile):
            used_sm[t] = (jnp.max(sel_t[t * blocks_per_tile:(t + 1) * blocks_per_tile, :]) > 0.5).astype(i32)

        m_sc[...] = jnp.full(m_sc.shape, NEG, f32)
        l_sc[...] = jnp.zeros(l_sc.shape, f32)
        acc_sc[...] = jnp.zeros(acc_sc.shape, f32)

        def tile(t, carry):
            @pl.when(used_sm[t] > 0)
            def _():
                k0 = pl.multiple_of(t * SLC_TILE, SLC_TILE)
                kt = ks_ref[0, pl.ds(k0, SLC_TILE), :]
                vt = vs_ref[0, pl.ds(k0, SLC_TILE), :]
                st = lax.dot_general(qg, kt, _NT, preferred_element_type=f32)
                kpos = k0 + lax.broadcasted_iota(i32, (Q_BLK, SLC_TILE), 1)
                blk_of_key = (k0 + lax.broadcasted_iota(i32, (NSB_PAD, SLC_TILE), 1)) >> SLC_SHIFT
                expand = (lax.broadcasted_iota(i32, (NSB_PAD, SLC_TILE), 0) == blk_of_key).astype(bf16)
                selexp = jnp.dot(sel_b, expand, preferred_element_type=f32)
                addmask = jnp.where((qpos_i >= kpos) & (selexp > 0.5), 0.0, NEG)
                krel = (k0 - q0 + lax.broadcasted_iota(i32, (1, SLC_TILE), 1)).astype(f32)
                for hh in range(NSA_HPG):
                    rows = slice(hh * Q_BLK, (hh + 1) * Q_BLK)
                    sm = st[rows] + slopes[hh] * krel + addmask
                    m_old = m_sc[rows]
                    m_new = jnp.maximum(m_old, jnp.max(sm, axis=-1, keepdims=True))
                    a = jnp.exp(m_old - m_new)
                    e = jnp.exp(sm - m_new)
                    l_sc[rows] = a * l_sc[rows] + jnp.sum(e, axis=-1, keepdims=True)
                    m_sc[rows] = m_new
                    acc_sc[rows] = a * acc_sc[rows]
                    p_sc[rows, :SLC_TILE] = e.astype(bf16)
                acc_sc[...] += jnp.dot(p_sc[:, :SLC_TILE], vt, preferred_element_type=f32)
            return carry

        lax.fori_loop(0, (q0 + Q_BLK + SLC_TILE - 1) // SLC_TILE, tile, 0)
        o_slc = acc_sc[...] / l_sc[...]

        start = pl.multiple_of(jnp.maximum(q0 - WINDOW, 0), Q_BLK)
        kt = kw_ref[0, pl.ds(start, WIN_KEYS), :]
        vt = vw_ref[0, pl.ds(start, WIN_KEYS), :]
        st = lax.dot_general(qg, kt, _NT, preferred_element_type=f32)
        wd = (qpos_i - (start + lax.broadcasted_iota(i32, (Q_BLK, WIN_KEYS), 1))).astype(f32)
        wmask = (wd >= 0) & (wd <= WINDOW)
        wmaskf = wmask.astype(f32)
        for hh in range(NSA_HPG):
            rows = slice(hh * Q_BLK, (hh + 1) * Q_BLK)
            p = _softmax_rows(jnp.where(wmask, st[rows] - slopes[hh] * wd, NEG), wmaskf)
            p_sc[rows, :WIN_KEYS] = p.astype(bf16)
        o_win = jnp.dot(p_sc[:, :WIN_KEYS], vt, preferred_element_type=f32)

        for hh in range(NSA_HPG):
            rows = slice(hh * Q_BLK, (hh + 1) * Q_BLK)
            c = 3 * (NSA_HPG * g + hh)
            og_sc[g, rows] = (gates[:, c:c + 1] * o_cmp[rows] + gates[:, c + 1:c + 2] * o_slc[rows]
                              + gates[:, c + 2:c + 3] * o_win[rows])

    for hh in range(NSA_HPG):
        rows = slice(hh * Q_BLK, (hh + 1) * Q_BLK)
        o_ref[0, :, LANES * hh:LANES * (hh + 1)] = jnp.where(lane < HD, og_sc[0, rows], og_sc[1, rows])


def _nsa_prompt_call(qp, gl, bg, kc, vc, ks, vs, kw, vw):
    b, t, w = qp.shape
    nc = kc.shape[1]
    assert t % SLC_TILE == 0 and t >= WIN_KEYS and nc % LANES == 0 and t // SLC_L <= NSB_PAD
    rows = NSA_HPG * Q_BLK
    qblk = lambda bb, i: (bb, i, 0)
    whole = lambda bb, i: (bb, 0, 0)
    return pl.pallas_call(
        _nsa_prompt_kernel,
        grid=(b, t // Q_BLK),
        in_specs=[pl.BlockSpec((1, Q_BLK, w), qblk), pl.BlockSpec((1, Q_BLK, LANES), qblk),
                  pl.BlockSpec((1, LANES), lambda bb, i: (0, 0)),
                  pl.BlockSpec((1, nc, LANES), whole), pl.BlockSpec((1, nc, LANES), whole),
                  pl.BlockSpec((1, t, LANES), whole), pl.BlockSpec((1, t, LANES), whole),
                  pl.BlockSpec((1, t, LANES), whole), pl.BlockSpec((1, t, LANES), whole)],
        out_specs=pl.BlockSpec((1, Q_BLK, w), qblk),
        out_shape=jax.ShapeDtypeStruct((b, t, w), f32),
        scratch_shapes=[pltpu.VMEM((rows, WIN_KEYS), bf16), pltpu.VMEM((rows, 1), f32), pltpu.VMEM((rows, 1), f32),
                        pltpu.VMEM((rows, LANES), f32), pltpu.VMEM((NSA_KV, rows, LANES), f32),
                        pltpu.SMEM((NSB_PAD * SLC_L // SLC_TILE,), i32)],
        compiler_params=_cparams(("parallel", "parallel")),
        name="nsa_prompt",
    )(qp, gl, bg, kc, vc, ks, vs, kw, vw)


def _nsa_sample_kernel(pt_ref, qp_ref, gl_ref, bg_ref, kc_ref, vc_ref, kn_ref, vn_ref, wst_ref, wkn_ref, wvn_ref,
                       *rest, n_pages, past_len):
    page_refs = rest[:n_pages]
    o_ref, k_sc, v_sc = rest[n_pages:]
    nq = qp_ref.shape[1]
    nc = kc_ref.shape[1]
    nkeys = (n_pages + 1) * PAGE_SIZE
    rows_g = NSA_HPG * nq
    qpos_i = past_len + lax.broadcasted_iota(i32, (nq, 1), 0)
    lane = lax.broadcasted_iota(i32, (nq, LANES), 1)
    gates = jax.nn.sigmoid(gl_ref[0] + bg_ref[...])
    qp = qp_ref[0]
    zero_b = jnp.zeros((nq, LANES), bf16)
    qs = jnp.concatenate([jnp.where((lane < HD) if g == 0 else (lane >= HD),
                                    qp[:, LANES * hh:LANES * (hh + 1)], zero_b)
                          for g in range(NSA_KV) for hh in range(NSA_HPG)], axis=0)

    def rows_of(g, hh):
        r0 = g * rows_g + hh * nq
        return slice(r0, r0 + nq)

    n_idx = lax.broadcasted_iota(i32, (nq, nc), 1)
    cd = (qpos_i - (n_idx * CMP_S + (CMP_L - 1))).astype(f32)
    cmask = cd >= 0
    cmaskf = cmask.astype(f32)
    s = lax.dot_general(qs, kc_ref[0], _NT, preferred_element_type=f32)
    imps, ps = [], []
    for g in range(NSA_KV):
        imp = jnp.zeros((nq, nc), f32)
        for hh in range(NSA_HPG):
            p = _softmax_rows(jnp.where(cmask, s[rows_of(g, hh)] - SLOPES[NSA_HPG * g + hh] * cd, NEG), cmaskf)
            imp = imp + p
            ps.append(p)
        imps.append(imp)
    o_cmp = jnp.dot(jnp.concatenate(ps, axis=0).astype(bf16), vc_ref[0], preferred_element_type=f32)

    qpos_t = past_len + lax.broadcasted_iota(i32, (NSB_PAD, nq), 1)
    blk_of_key = lax.broadcasted_iota(i32, (NSB_PAD, nkeys), 1) >> SLC_SHIFT
    expand = (lax.broadcasted_iota(i32, (NSB_PAD, nkeys), 0) == blk_of_key).astype(bf16)
    sd = (qpos_i - lax.broadcasted_iota(i32, (nq, nkeys), 1)).astype(f32)
    masks = []
    for g in range(NSA_KV):
        sel_b = _untranspose(_select_t(_block_scores_t(imps[g]), qpos_t))
        selexp = jnp.dot(sel_b, expand, preferred_element_type=f32)
        masks.append((sd >= 0) & (selexp > 0.5))

    for pg in range(n_pages):
        blk = page_refs[pg][0]
        k_sc[:, pg * PAGE_SIZE:(pg + 1) * PAGE_SIZE] = blk[:LANES].astype(bf16)
        v_sc[:, pg * PAGE_SIZE:(pg + 1) * PAGE_SIZE] = blk[LANES:].astype(bf16)
    tail = jnp.zeros((PAGE_SIZE - nq, LANES), bf16)
    new_t = lambda ref: jnp.transpose(jnp.concatenate([ref[0], tail], axis=0).astype(f32)).astype(bf16)
    k_sc[:, n_pages * PAGE_SIZE:] = new_t(kn_ref)
    v_sc[:, n_pages * PAGE_SIZE:] = new_t(vn_ref)
    s = jnp.dot(qs, k_sc[...], preferred_element_type=f32)
    ps = []
    for g in range(NSA_KV):
        mf = masks[g].astype(f32)
        for hh in range(NSA_HPG):
            ps.append(_softmax_rows(jnp.where(masks[g], s[rows_of(g, hh)] - SLOPES[NSA_HPG * g + hh] * sd, NEG), mf))
    o_slc = lax.dot_general(jnp.concatenate(ps, axis=0).astype(bf16), v_sc[...], _NT, preferred_element_type=f32)

    wb = wst_ref.shape[1]
    wk = jnp.concatenate([wst_ref[0, :, :LANES].astype(bf16), wkn_ref[0], tail], axis=0)
    wv = jnp.concatenate([wst_ref[0, :, LANES:].astype(bf16), wvn_ref[0], tail], axis=0)
    wpos = (past_len - wb) + lax.broadcasted_iota(i32, (nq, wb + PAGE_SIZE), 1)
    wd = (qpos_i - wpos).astype(f32)
    wmask = (wd >= 0) & (wd <= WINDOW) & (wpos >= 0)
    wmaskf = wmask.astype(f32)
    s = lax.dot_general(qs, wk, _NT, preferred_element_type=f32)
    ps = []
    for g in range(NSA_KV):
        for hh in range(NSA_HPG):
            ps.append(_softmax_rows(jnp.where(wmask, s[rows_of(g, hh)] - SLOPES[NSA_HPG * g + hh] * wd, NEG), wmaskf))
    o_win = jnp.dot(jnp.concatenate(ps, axis=0).astype(bf16), wv, preferred_element_type=f32)

    for hh in range(NSA_HPG):
        halves = []
        for g in range(NSA_KV):
            r = rows_of(g, hh)
            c = 3 * (NSA_HPG * g + hh)
            halves.append(gates[:, c:c + 1] * o_cmp[r] + gates[:, c + 1:c + 2] * o_slc[r]
                          + gates[:, c + 2:c + 3] * o_win[r])
        o_ref[0, :, LANES * hh:LANES * (hh + 1)] = jnp.where(lane < HD, halves[0], halves[1])


def _nsa_sample_call(page_table, qp, gl, bg, kc, vc, kn, vn, wst, wkn, wvn, cache):
    b, nq, w = qp.shape
    n_pages = page_table.shape[1]
    nc = kc.shape[1]
    wb = wst.shape[1]
    nkeys = (n_pages + 1) * PAGE_SIZE
    assert nkeys // SLC_L <= NSB_PAD and nq <= PAGE_SIZE
    per_b = lambda bb, pt: (bb, 0, 0)
    in_specs = [pl.BlockSpec((1, nq, w), per_b), pl.BlockSpec((1, nq, LANES), per_b),
                pl.BlockSpec((1, LANES), lambda bb, pt: (0, 0)),
                pl.BlockSpec((1, nc, LANES), per_b), pl.BlockSpec((1, nc, LANES), per_b),
                pl.BlockSpec((1, nq, LANES), per_b), pl.BlockSpec((1, nq, LANES), per_b),
                pl.BlockSpec((1, wb, 2 * LANES), per_b),
                pl.BlockSpec((1, nq, LANES), per_b), pl.BlockSpec((1, nq, LANES), per_b)]
    for pg in range(n_pages):
        in_specs.append(pl.BlockSpec((1, 2 * LANES, PAGE_SIZE),
                                     functools.partial(lambda bb, pt, pg: (pt[bb, pg], 1, 0), pg=pg)))
    grid_spec = pltpu.PrefetchScalarGridSpec(
        num_scalar_prefetch=1, grid=(b,), in_specs=in_specs,
        out_specs=pl.BlockSpec((1, nq, w), per_b),
        scratch_shapes=[pltpu.VMEM((LANES, nkeys), bf16), pltpu.VMEM((LANES, nkeys), bf16)])
    return pl.pallas_call(
        functools.partial(_nsa_sample_kernel, n_pages=n_pages, past_len=n_pages * PAGE_SIZE),
        grid_spec=grid_spec,
        out_shape=jax.ShapeDtypeStruct((b, nq, w), f32),
        compiler_params=_cparams(("parallel",)),
        name="nsa_sample",
    )(page_table, qp, gl, bg, kc, vc, kn, vn, wst, wkn, wvn, *([cache] * n_pages))


def _compress_kernel(x_ref, pos_ref, w1_ref, b1_ref, w2_ref, o_ref):
    x = x_ref[0, 0, 0]
    half = CMP_S * HD
    xa = (x + pos_ref[0, :, :half]).astype(bf16)
    xb = (x + pos_ref[0, :, half:]).astype(bf16)
    a = jnp.dot(xa, w1_ref[0, :half, :], preferred_element_type=f32)
    bsec = jnp.dot(xb, w1_ref[0, half:, :], preferred_element_type=f32)
    nch = x.shape[0]
    hid = jax.nn.gelu(a + pltpu.roll(bsec, nch - 1, 0) + b1_ref[0])
    o_ref[0, 0, 0] = jnp.dot(hid.astype(bf16), w2_ref[0], preferred_element_type=f32).astype(o_ref.dtype)


def _compress_call(x2, cmp_pos, w_cmp1, b_cmp1, w_cmp2):
    b, _, g, nch, cw = x2.shape
    hid = w_cmp1.shape[-1]
    pos = cmp_pos.reshape(2, 1, CMP_L * HD)
    return pl.pallas_call(
        _compress_kernel,
        grid=(b, 2, g),
        in_specs=[pl.BlockSpec((1, 1, 1, nch, cw), lambda bb, s, gg: (bb, s, gg, 0, 0)),
                  pl.BlockSpec((1, 1, CMP_L * HD), lambda bb, s, gg: (s, 0, 0)),
                  pl.BlockSpec((1, CMP_L * HD, hid), lambda bb, s, gg: (s, 0, 0)),
                  pl.BlockSpec((1, 1, hid), lambda bb, s, gg: (s, 0, 0)),
                  pl.BlockSpec((1, hid, HD), lambda bb, s, gg: (s, 0, 0))],
        out_specs=pl.BlockSpec((1, 1, 1, nch, HD), lambda bb, s, gg: (bb, s, gg, 0, 0)),
        out_shape=jax.ShapeDtypeStruct((b, 2, g, nch, HD), bf16),
        compiler_params=_cparams(("parallel", "parallel", "parallel")),
        name="compress",
    )(x2, pos, w_cmp1.astype(bf16), b_cmp1.reshape(2, 1, hid), w_cmp2.astype(bf16))


def _compressed_kv(tok, cmp_pos, w_cmp1, b_cmp1, w_cmp2):
    b, t = tok.shape[:2]
    nch = t // CMP_S
    x2 = jnp.transpose(tok[:, :nch * CMP_S], (0, 2, 3, 1, 4)).reshape(b, 2, NSA_KV, nch, CMP_S * HD)
    ck = _compress_call(x2, cmp_pos, w_cmp1, b_cmp1, w_cmp2)
    ck = jnp.transpose(ck, (0, 1, 3, 2, 4)).reshape(b, 2, nch, NSA_KV * HD)
    return ck[:, 0], ck[:, 1]


def _compress_paged_kernel(pt_ref, pos_ref, w1a_ref, w1b_ref, b1_ref, w2_ref, *rest, n_pages):
    page_refs = rest[:n_pages]
    o_ref, x_sc, xcat_sc = rest[n_pages:]
    nch = n_pages * (PAGE_SIZE // CMP_S)
    role = NSA_KV * HD
    for pg in range(n_pages):
        xt = jnp.transpose(page_refs[pg][0])
        for s in range(2):
            x_sc[s, pg * PAGE_SIZE:(pg + 1) * PAGE_SIZE, :] = xt[:, s * role:(s + 1) * role]
    for s in range(2):
        for t in range(CMP_S):
            xcat_sc[:, t * role:(t + 1) * role] = x_sc[s, pl.ds(t, nch, stride=CMP_S), :]
        xc = xcat_sc[...]
        xa = (xc + pos_ref[s, 0:1, :]).astype(bf16)
        xb = (xc + pos_ref[s, 1:2, :]).astype(bf16)
        a = jnp.dot(xa, w1a_ref[s], preferred_element_type=f32)
        bsec = jnp.dot(xb, w1b_ref[s], preferred_element_type=f32)
        hid = jax.nn.gelu(a + pltpu.roll(bsec, nch - 1, 0) + b1_ref[s])
        o_ref[0, s] = jnp.dot(hid.astype(bf16), w2_ref[s], preferred_element_type=f32).astype(o_ref.dtype)


def _compress_paged(page_table, cache_t, cmp_pos, w_cmp1, b_cmp1, w_cmp2):
    b, n_pages = page_table.shape
    nch = n_pages * (PAGE_SIZE // CMP_S)
    hid = w_cmp1.shape[-1]
    half = CMP_S * HD
    eye = jnp.eye(NSA_KV, dtype=f32)

    def widen_rows(w):
        w = w.reshape(2, CMP_S, HD, hid)
        return jnp.einsum('stdh,ag->stadgh', w, eye).reshape(2, CMP_S * NSA_KV * HD, NSA_KV * hid)

    w1a = widen_rows(w_cmp1[:, :half]).astype(bf16)
    w1b = widen_rows(w_cmp1[:, half:]).astype(bf16)
    w2 = jnp.einsum('shd,ag->sahgd', w_cmp2, eye).reshape(2, NSA_KV * hid, NSA_KV * HD).astype(bf16)
    b1 = jnp.tile(b_cmp1, (1, NSA_KV)).reshape(2, 1, NSA_KV * hid)
    pos = jnp.broadcast_to(cmp_pos.reshape(2, 2, CMP_S, 1, HD), (2, 2, CMP_S, NSA_KV, HD))
    pos = pos.reshape(2, 2, CMP_S * NSA_KV * HD)
    fix = lambda a: pl.BlockSpec(a.shape, lambda bb, pt: (0,) * a.ndim)
    in_specs = [fix(pos), fix(w1a), fix(w1b), fix(b1), fix(w2)]
    for pg in range(n_pages):
        in_specs.append(pl.BlockSpec((1, 2 * NSA_KV * HD, PAGE_SIZE),
                                     functools.partial(lambda bb, pt, pg: (pt[bb, pg], 0, 0), pg=pg)))
    grid_spec = pltpu.PrefetchScalarGridSpec(
        num_scalar_prefetch=1, grid=(b,), in_specs=in_specs,
        out_specs=pl.BlockSpec((1, 2, nch, NSA_KV * HD), lambda bb, pt: (bb, 0, 0, 0)),
        scratch_shapes=[pltpu.VMEM((2, n_pages * PAGE_SIZE, NSA_KV * HD), f32),
                        pltpu.VMEM((nch, CMP_S * NSA_KV * HD), f32)])
    return pl.pallas_call(
        functools.partial(_compress_paged_kernel, n_pages=n_pages),
        grid_spec=grid_spec,
        out_shape=jax.ShapeDtypeStruct((b, 2, nch, NSA_KV * HD), bf16),
        compiler_params=_cparams(("parallel",)),
        name="compress_paged",
    )(page_table, pos, w1a, w1b, b1, w2, *([cache_t] * n_pages))


def _pair_pack_cols(w):
    k = w.shape[0]
    return jnp.transpose(w.reshape(k, NSA_KV, NSA_HPG, HD), (0, 2, 1, 3)).reshape(k, NSA_W)


def _last_rows(a, n):
    t = a.shape[1]
    if t >= n:
        return a[:, t - n:]
    return jnp.pad(a, ((0, 0), (n - t, 0)) + ((0, 0),) * (a.ndim - 2))


def kernel(x_prompt, x_sample, mem_prompt, state_gla, cache_nsa_kv, state_win_kv, cache_mem_kv, page_table,
           w_in_a, w_gate_a, b_gate_a, gla_norm, w_in_b, b_gate_b, w_kv_b, cmp_pos, w_cmp1, b_cmp1, w_cmp2,
           w_mem_kv, w_out, ln1_g, ln1_b, ln2_g, ln2_b, w_router, b_router, w_e1, b_e1, w_e2, b_e2):
    bp, tp, d = x_prompt.shape
    bs, ts, _ = x_sample.shape
    n_p, n_s = bp * tp, bs * ts
    wb = min(WINDOW, state_win_kv.shape[1])
    n_pool = cache_nsa_kv.shape[0]
    past_len = page_table.shape[1] * PAGE_SIZE

    mem_rows = mem_prompt.reshape(bp * MEM_TOK, d)
    w_mem = [w_mem_kv[l].astype(bf16) for l in range(DEPTH)]
    mem_kv_l = _proj(mem_rows, w_mem, _whole(w_mem))
    mem_kv_prompt = jnp.stack(mem_kv_l).reshape(DEPTH, bp, MEM_TOK, 2, MEM_HEADS, MEM_HD)
    mem_p = [m.reshape(bp, MEM_TOK, 2 * MEM_W) for m in mem_kv_l]
    mem_s = jnp.transpose(cache_mem_kv, (0, 1, 3, 4, 5, 2)).reshape(DEPTH, bs, 2 * MEM_W, MEM_TOK)

    x_p, x_s = x_prompt.reshape(n_p, d), x_sample.reshape(n_s, d)
    as_p = lambda a: a.reshape(bp, tp, a.shape[-1])
    as_s = lambda a: a.reshape(bs, ts, a.shape[-1])

    hk = GLA_HEADS * GLA_DK
    cuts_a = [0, hk, 2 * hk, 2 * hk + GLA_W, 2 * hk + 2 * GLA_W, 2 * hk + 2 * GLA_W + GLA_RANK,
              2 * hk + 2 * GLA_W + GLA_RANK + MEM_W]
    mem_scale = MEM_HD ** -0.5
    q_scale = HD ** -0.5
    xb = jnp.zeros((_moe_blocks((n_p + n_s) * TOP_K) * MOE_ROWS, d), f32)

    gla_p, gla_s = [], []
    for l in range(DEPTH):
        if l < N_A:
            wq, wk, wv, wr, wgl, wm = [w_in_a[l][:, cuts_a[i]:cuts_a[i + 1]] for i in range(6)]
            ws = [_pad_heads(wq, GLA_DK, GLA_KP), _pad_heads(wk, GLA_DK, GLA_KP), _pad_heads(wv, GLA_DV, GLA_VP),
                  _pad_heads(wr, GLA_DV, GLA_VP), jnp.pad(wgl, ((0, 0), (0, LANES - GLA_RANK))), wm * mem_scale]
            ws = [w.astype(bf16) for w in ws]
            outs = _whole(ws)
            outs[5] = outs[5][:3] + (bf16,)
            wg = _pad_heads(jnp.pad(w_gate_a[l], ((0, LANES - GLA_RANK), (0, 0))), GLA_DK, GLA_KP).astype(bf16)
            bg = _pad_heads(b_gate_a[l], GLA_DK, GLA_KP).reshape(1, GLA_HEADS * GLA_KP)
            gn = _pad_heads(gla_norm[l], GLA_DV, GLA_VP).reshape(1, GLA_HEADS * GLA_VP)
            q, k, v, r, glr, mq_p = _proj(x_p, ws, outs)
            o_p, s_new = _gla(as_p(q), as_p(k), as_p(v), as_p(r), as_p(glr), wg, bg, gn, None, bt=min(bp, GLA_BT))
            gla_p.append(s_new)
            q, k, v, r, glr, mq_s = _proj(x_s, ws, outs)
            o_s, s_new = _gla(as_s(q), as_s(k), as_s(v), as_s(r), as_s(glr), wg, bg, gn, state_gla[l],
                              bt=min(bs, GLA_BT))
            gla_s.append(s_new)
            w_o = _pad_heads(w_out[l][:GLA_W].T, GLA_DV, GLA_VP).T
        else:
            j = l - N_A
            if l == N_A:
                role = NSA_KV * HD
                kv_outs = [(0, 0, 4 * role, f32), (0, 4 * role, 6 * role, f32)]
                kv_outs += [(0, r * role, (r + 1) * role, bf16) for r in range(2, 6)]
                wkv = [w_kv_b.astype(bf16)]
                rows_p, win_p, ks_p, vs_p, kw_p, vw_p = [as_p(a) for a in _proj(x_p, wkv, kv_outs)]
                rows_s, win_s, ks_s, vs_s, kw_s, vw_s = [as_s(a) for a in _proj(x_s, wkv, kv_outs)]
                kc_p, vc_p = _compressed_kv(rows_p.reshape(bp, tp, 4, NSA_KV, HD)[:, :, :2],
                                            cmp_pos, w_cmp1, b_cmp1, w_cmp2)
                assert (past_len + ts) // CMP_S == past_len // CMP_S
                cache_t = jnp.transpose(cache_nsa_kv, (0, 2, 3, 4, 1)).reshape(n_pool, 4 * role, PAGE_SIZE)
                ck_s = _compress_paged(page_table, cache_t, cmp_pos, w_cmp1, b_cmp1, w_cmp2)
                kc_s, vc_s = ck_s[:, 0], ck_s[:, 1]
                wst = state_win_kv.reshape(bs, state_win_kv.shape[1], 2 * role)
                wkv_s = jnp.concatenate([state_win_kv, win_s.reshape(bs, ts, 2, NSA_KV, HD)], axis=1)
                nsa_out = (rows_p.reshape(bp, tp, 4, NSA_KV, HD), _last_rows(win_p.reshape(bp, tp, 2, NSA_KV, HD), wb),
                           rows_s.reshape(bs, ts, 4, NSA_KV, HD), wkv_s[:, -wb:])
            wq = _pair_pack_cols(w_in_b[j][:, :NSA_W]) * q_scale
            wg = jnp.pad(w_in_b[j][:, NSA_W:NSA_W + 3 * NSA_HEADS], ((0, 0), (0, LANES - 3 * NSA_HEADS)))
            wm = w_in_b[j][:, NSA_W + 3 * NSA_HEADS:] * mem_scale
            ws = [wq.astype(bf16), wg.astype(bf16), wm.astype(bf16)]
            outs = [(0, 0, NSA_W, bf16), (1, 0, LANES, f32), (2, 0, MEM_W, bf16)]
            bg = jnp.pad(b_gate_b[j], (0, LANES - 3 * NSA_HEADS)).reshape(1, LANES)
            qp, gl, mq_p = _proj(x_p, ws, outs)
            o_p = _nsa_prompt_call(as_p(qp), as_p(gl), bg, kc_p, vc_p, ks_p, vs_p, kw_p, vw_p)
            qp, gl, mq_s = _proj(x_s, ws, outs)
            o_s = _nsa_sample_call(page_table, as_s(qp), as_s(gl), bg, kc_s, vc_s, ks_s, vs_s, wst, kw_s, vw_s,
                                   cache_t)
            w_o = _pair_pack_cols(w_out[l][:NSA_W].T).T
        om_p = _mem_attn(as_p(mq_p), mem_p[l])
        om_s = _mem_attn(as_s(mq_s), mem_s, layer=l)
        w_o, w_om = w_o.astype(bf16), w_out[l][-MEM_W:].astype(bf16)
        flat = lambda a: a.reshape(-1, a.shape[-1])
        x1_p, gate_p, idx_p = _mix_ln([(flat(o_p), w_o), (flat(om_p), w_om)], x_p, ln1_g[l], ln1_b[l],
                                      w_router[l], b_router[l])
        x1_s, gate_s, idx_s = _mix_ln([(flat(o_s), w_o), (flat(om_s), w_om)], x_s, ln1_g[l], ln1_b[l],
                                      w_router[l], b_router[l])
        dest, blk_exp, nreal = _moe_slots(jnp.concatenate([idx_p.reshape(-1), idx_s.reshape(-1)]))
        dest_p, dest_s = dest[:n_p * TOP_K], dest[n_p * TOP_K:]
        xb = _moe_dispatch(x1_s, dest_s, _moe_dispatch(x1_p, dest_p, xb))
        yb = _moe_ffn_blocks(xb, blk_exp, nreal, w_e1, b_e1, w_e2, b_e2, l)
        x_p = _moe_combine(yb, dest_p, gate_p, x1_p, ln2_g[l], ln2_b[l])
        x_s = _moe_combine(yb, dest_s, gate_s, x1_s, ln2_g[l], ln2_b[l])

    rows_p, win_p_out, rows_s, win_s_out = nsa_out
    return (x_p.reshape(bp, tp, d), x_s.reshape(bs, ts, d),
            jnp.stack(gla_p), jnp.stack(gla_s), rows_p, rows_s, win_p_out, win_s_out, mem_kv_prompt)
```

```python
import functools
import math

import jax
import jax.numpy as jnp
from jax import lax
from jax.experimental import pallas as pl
from jax.experimental.pallas import tpu as pltpu

D_MODEL = 1024
DEPTH = 4
PAGE_SIZE = 128
N_A = DEPTH // 2
GLA_HEADS = 4
GLA_DV = (3 * D_MODEL) // (4 * GLA_HEADS)
GLA_DK = GLA_DV // 2
GLA_RANK = 16
GLA_TAU = 16.0
GLA_CHUNK = 64
HD = 64
NSA_HEADS = (3 * D_MODEL) // (4 * HD)
NSA_KV = 2
NSA_HPG = NSA_HEADS // NSA_KV
CMP_S = 16
CMP_L = 2 * CMP_S
SLC_L = 64
SLC_K = 16
WINDOW = 512
Q_BLK = 128
MEM_TOK = 256
MEM_HEADS = 4
MEM_HD = D_MODEL // (4 * MEM_HEADS)
N_EXP = 32
TOP_K = 4
D_FF = D_MODEL
SWIGLU_LIMIT = 7.0
SWIGLU_ALPHA = 1.702
GLA_W = GLA_HEADS * GLA_DV
NSA_W = NSA_HEADS * HD
MEM_W = MEM_HEADS * MEM_HD
DN_ALPHA = (2 * DEPTH) ** 0.25
LN_EPS = 1e-5
NEG = -1e30

f32, bf16, i32 = jnp.float32, jnp.bfloat16, jnp.int32

VMEM_LIMIT_BYTES = 56 * 1024 * 1024
LANES = 128
ROW_TILE = 512
MOE_ROWS = 256
FF_CHUNK = 512
SLC_TILE = 512
WIN_KEYS = WINDOW + Q_BLK
NSB_PAD = 64
TOK_TILE = 256
DMA_UNROLL = 4
RANK_TILE = 1024
MEM_ROWS = 64
NSA_SAMPLE_BT = 2
GLA_KP = 128
GLA_VP = 256
GLA_BT = 8
_NT = (((1,), (1,)), ((), ()))
_TN = (((0,), (0,)), ((), ()))


def _cparams(sem):
    return pltpu.CompilerParams(dimension_semantics=sem, vmem_limit_bytes=VMEM_LIMIT_BYTES)


def _alibi_slopes(n):
    def pow2(m):
        start = 2.0 ** (-8.0 / m)
        return [start ** (i + 1) for i in range(m)]
    if math.log2(n).is_integer():
        return pow2(n)
    c = 2 ** math.floor(math.log2(n))
    return pow2(c) + pow2(2 * c)[0::2][: n - c]


SLOPES = _alibi_slopes(NSA_HEADS)
SLC_SHIFT = int(math.log2(SLC_L))
RATIO_SHIFT = int(math.log2(SLC_L // CMP_S))


def _proj_kernel(x_ref, *refs, n_w, outs):
    xb = x_ref[...].astype(bf16)
    res = {}
    for o_ref, (wi, lo, hi) in zip(refs[n_w:], outs):
        if wi not in res:
            res[wi] = jnp.dot(xb, refs[wi][...], preferred_element_type=f32)
        o_ref[...] = res[wi][:, lo:hi].astype(o_ref.dtype)


def _proj(x, ws, outs, tm=ROW_TILE):
    m, k = x.shape
    tm = min(tm, m)
    in_specs = [pl.BlockSpec((tm, k), lambda i: (i, 0))]
    in_specs += [pl.BlockSpec(w.shape, lambda i: (0, 0)) for w in ws]
    out_specs = [pl.BlockSpec((tm, hi - lo), lambda i: (i, 0)) for _, lo, hi, _ in outs]
    out_shape = [jax.ShapeDtypeStruct((m, hi - lo), dt) for _, lo, hi, dt in outs]
    return pl.pallas_call(
        functools.partial(_proj_kernel, n_w=len(ws), outs=tuple(o[:3] for o in outs)),
        grid=(m // tm,),
        in_specs=in_specs,
        out_specs=out_specs,
        out_shape=out_shape,
        compiler_params=_cparams(("parallel",)),
        name="proj",
    )(x, *ws)


def _whole(ws, dtype=f32):
    return [(i, 0, w.shape[1], dtype) for i, w in enumerate(ws)]


def _layer_norm_rows(y, g, b):
    mu = jnp.mean(y, axis=-1, keepdims=True)
    d = y - mu
    var = jnp.mean(d * d, axis=-1, keepdims=True)
    return d * lax.rsqrt(var + LN_EPS) * g + b


def _mix_ln_kernel(*refs, n_parts):
    a_refs, w_refs = refs[:n_parts], refs[n_parts:2 * n_parts]
    x_ref, g_ref, b_ref, wr_ref, br_ref, x1_ref, gate_ref, idx_ref = refs[2 * n_parts:]
    tm = x_ref.shape[0]
    sub = tm // 2 if tm % 16 == 0 else tm
    for r0 in range(0, tm, sub):
        rows = slice(r0, r0 + sub)
        mix = jnp.dot(a_refs[0][rows, :].astype(bf16), w_refs[0][...], preferred_element_type=f32)
        for a_ref, w_ref in zip(a_refs[1:], w_refs[1:]):
            mix = mix + jnp.dot(a_ref[rows, :].astype(bf16), w_ref[...], preferred_element_type=f32)
        x1 = _layer_norm_rows(DN_ALPHA * x_ref[rows, :] + mix, g_ref[...], b_ref[...])
        x1_ref[rows, :] = x1
        lg = jnp.dot(x1, wr_ref[...], preferred_element_type=f32, precision=lax.Precision.HIGHEST) + br_ref[...]
        lane = lax.broadcasted_iota(i32, lg.shape, 1)
        vals, idxs = [], []
        for _ in range(TOP_K):
            m = jnp.max(lg, axis=-1, keepdims=True)
            i = jnp.min(jnp.where(lg == m, lane, N_EXP), axis=-1, keepdims=True)
            vals.append(m)
            idxs.append(i)
            lg = jnp.where(lane == i, -jnp.inf, lg)
        es = [jnp.exp(v - vals[0]) for v in vals]
        tot = es[0] + es[1] + es[2] + es[3]
        col = lax.broadcasted_iota(i32, (sub, TOP_K), 1)
        gate = jnp.zeros((sub, TOP_K), f32)
        idx = jnp.zeros((sub, TOP_K), i32)
        for k in range(TOP_K):
            gate = jnp.where(col == k, es[k] / tot, gate)
            idx = jnp.where(col == k, idxs[k], idx)
        gate_ref[rows, :] = gate
        idx_ref[rows, :] = idx


def _mix_ln(parts, x, g, b, w_r, b_r, tm=ROW_TILE):
    m, d = x.shape
    row = lambda i: (i, 0)
    fix = lambda i: (0, 0)
    in_specs = [pl.BlockSpec((tm, a.shape[1]), row) for a, _ in parts]
    in_specs += [pl.BlockSpec(w.shape, fix) for _, w in parts]
    in_specs += [pl.BlockSpec((tm, d), row), pl.BlockSpec((1, d), fix), pl.BlockSpec((1, d), fix),
                 pl.BlockSpec((d, N_EXP), fix), pl.BlockSpec((1, N_EXP), fix)]
    return pl.pallas_call(
        functools.partial(_mix_ln_kernel, n_parts=len(parts)),
        grid=(m // tm,),
        in_specs=in_specs,
        out_specs=[pl.BlockSpec((tm, d), row), pl.BlockSpec((tm, TOP_K), row), pl.BlockSpec((tm, TOP_K), row)],
        out_shape=[jax.ShapeDtypeStruct((m, d), f32), jax.ShapeDtypeStruct((m, TOP_K), f32),
                   jax.ShapeDtypeStruct((m, TOP_K), i32)],
        compiler_params=_cparams(("parallel",)),
        name="mix_ln",
    )(*[a for a, _ in parts], *[w for _, w in parts], x, g.reshape(1, d), b.reshape(1, d), w_r,
      b_r.reshape(1, N_EXP))


def _moe_kernel(be_ref, nreal_ref, x_ref, w1_ref, b1_ref, w2_ref, b2_ref, o_ref, w1s, w2s):
    i = pl.program_id(0)
    real = i < nreal_ref[0]
    prev = be_ref[jnp.maximum(i - 1, 0)]
    fresh = jnp.logical_or(i == 0, be_ref[i] != prev)

    @pl.when(jnp.logical_and(real, fresh))
    def _():
        w1s[...] = w1_ref[0, 0].astype(bf16)
        w2s[...] = w2_ref[0, 0].astype(bf16)

    @pl.when(real)
    def _():
        x = x_ref[...].astype(bf16)
        acc = jnp.zeros((MOE_ROWS, D_MODEL), f32)
        for c in range(D_FF // FF_CHUNK):
            lo = c * FF_CHUNK
            hg = jnp.dot(x, w1s[:, lo:lo + FF_CHUNK], preferred_element_type=f32)
            hg = hg + b1_ref[0, 0, :, lo:lo + FF_CHUNK]
            hu = jnp.dot(x, w1s[:, D_FF + lo:D_FF + lo + FF_CHUNK], preferred_element_type=f32)
            hu = hu + b1_ref[0, 0, :, D_FF + lo:D_FF + lo + FF_CHUNK]
            g = jnp.minimum(hg, SWIGLU_LIMIT)
            u = jnp.clip(hu, -SWIGLU_LIMIT, SWIGLU_LIMIT)
            a = (u + 1.0) * g * jax.nn.sigmoid(SWIGLU_ALPHA * g)
            acc = acc + jnp.dot(a.astype(bf16), w2s[lo:lo + FF_CHUNK, :], preferred_element_type=f32)
        o_ref[...] = acc + b2_ref[0, 0]

    @pl.when(jnp.logical_not(real))
    def _():
        o_ref[...] = jnp.zeros_like(o_ref)


def _moe_ffn_blocks(xb, blk_exp, nreal, w1, b1, w2, b2, layer):
    p, d = xb.shape
    nblk = p // MOE_ROWS
    n_layers = w1.shape[0]
    per_expert = lambda i, be, nr: (layer, be[i], 0, 0)
    grid_spec = pltpu.PrefetchScalarGridSpec(
        num_scalar_prefetch=2,
        grid=(nblk,),
        in_specs=[
            pl.BlockSpec((MOE_ROWS, d), lambda i, be, nr: (i, 0)),
            pl.BlockSpec((1, 1, d, 2 * D_FF), per_expert),
            pl.BlockSpec((1, 1, 1, 2 * D_FF), per_expert),
            pl.BlockSpec((1, 1, D_FF, d), per_expert),
            pl.BlockSpec((1, 1, 1, d), per_expert),
        ],
        out_specs=pl.BlockSpec((MOE_ROWS, d), lambda i, be, nr: (i, 0)),
        scratch_shapes=[pltpu.VMEM((d, 2 * D_FF), bf16), pltpu.VMEM((D_FF, d), bf16)],
    )
    return pl.pallas_call(
        _moe_kernel,
        grid_spec=grid_spec,
        out_shape=jax.ShapeDtypeStruct((p, d), f32),
        compiler_params=_cparams(("arbitrary",)),
        name="moe_ffn",
    )(blk_exp, nreal, xb, w1, b1.reshape(n_layers, N_EXP, 1, 2 * D_FF), w2, b2.reshape(n_layers, N_EXP, 1, d))


def _row_copy(src, src_row, dst, dst_row, sem):
    return pltpu.make_async_copy(src.at[pl.ds(src_row, 1), :], dst.at[pl.ds(dst_row, 1), :], sem)


def _moe_dispatch_kernel(dest_ref, x_ref, xb_init_ref, xb_ref, sem):
    del xb_init_ref

    def issue(t, c):
        for k in range(TOP_K):
            _row_copy(x_ref, t, xb_ref, dest_ref[0, 0, t * TOP_K + k], sem).start(priority=k % 2)
        return c

    lax.fori_loop(0, TOK_TILE, issue, 0, unroll=DMA_UNROLL)

    def drain(t, c):
        for k in range(TOP_K):
            _row_copy(x_ref, 0, xb_ref, 0, sem).wait()
        return c

    lax.fori_loop(0, TOK_TILE, drain, 0, unroll=DMA_UNROLL)


def _moe_dispatch(x1, dest, xb):
    n, d = x1.shape
    nt = n // TOK_TILE
    return pl.pallas_call(
        _moe_dispatch_kernel,
        grid=(nt,),
        in_specs=[pl.BlockSpec((1, 1, TOK_TILE * TOP_K), lambda i: (i, 0, 0), memory_space=pltpu.SMEM),
                  pl.BlockSpec((TOK_TILE, d), lambda i: (i, 0)),
                  pl.BlockSpec(memory_space=pl.ANY)],
        out_specs=pl.BlockSpec(memory_space=pl.ANY),
        out_shape=jax.ShapeDtypeStruct(xb.shape, f32),
        scratch_shapes=[pltpu.SemaphoreType.DMA(())],
        input_output_aliases={2: 0},
        compiler_params=_cparams(("arbitrary",)),
        name="moe_dispatch",
    )(dest.reshape(nt, 1, TOK_TILE * TOP_K), x1, xb)


def _moe_combine_kernel(dest_ref, yb_ref, gate_ref, x1_ref, g_ref, b_ref, o_ref, ybuf, sem):
    def issue(t, c):
        for k in range(TOP_K):
            _row_copy(yb_ref, dest_ref[0, 0, t * TOP_K + k], ybuf.at[k], t, sem).start(priority=k % 2)
        return c

    lax.fori_loop(0, TOK_TILE, issue, 0, unroll=DMA_UNROLL)

    def drain(t, c):
        for k in range(TOP_K):
            _row_copy(yb_ref, 0, ybuf.at[k], 0, sem).wait()
        return c

    lax.fori_loop(0, TOK_TILE, drain, 0, unroll=DMA_UNROLL)
    gate = gate_ref[...]
    y = gate[:, 0:1] * ybuf[0]
    for k in range(1, TOP_K):
        y = y + gate[:, k:k + 1] * ybuf[k]
    o_ref[...] = _layer_norm_rows(DN_ALPHA * x1_ref[...] + y, g_ref[...], b_ref[...])


def _moe_combine(yb, dest, gate, x1, g, b):
    n, d = x1.shape
    nt = n // TOK_TILE
    row = lambda i: (i, 0)
    fix = lambda i: (0, 0)
    return pl.pallas_call(
        _moe_combine_kernel,
        grid=(nt,),
        in_specs=[pl.BlockSpec((1, 1, TOK_TILE * TOP_K), lambda i: (i, 0, 0), memory_space=pltpu.SMEM),
                  pl.BlockSpec(memory_space=pl.ANY),
                  pl.BlockSpec((TOK_TILE, TOP_K), row), pl.BlockSpec((TOK_TILE, d), row),
                  pl.BlockSpec((1, d), fix), pl.BlockSpec((1, d), fix)],
        out_specs=pl.BlockSpec((TOK_TILE, d), row),
        out_shape=jax.ShapeDtypeStruct((n, d), f32),
        scratch_shapes=[pltpu.VMEM((TOP_K, TOK_TILE, d), f32), pltpu.SemaphoreType.DMA(())],
        compiler_params=_cparams(("arbitrary",)),
        name="moe_combine",
    )(dest.reshape(nt, 1, TOK_TILE * TOP_K), yb, gate, x1, g.reshape(1, d), b.reshape(1, d))


def _rank_kernel(e_ref, rank_ref, cnt_ref, carry, upper):
    @pl.when(pl.program_id(0) == 0)
    def _():
        carry[...] = jnp.zeros(carry.shape, f32)
        upper[...] = (lax.broadcasted_iota(i32, (RANK_TILE, RANK_TILE), 0)
                      < lax.broadcasted_iota(i32, (RANK_TILE, RANK_TILE), 1)).astype(bf16)

    e = e_ref[0]
    onehot = (lax.broadcasted_iota(i32, (N_EXP, RANK_TILE), 0) == e).astype(bf16)
    before = jnp.dot(onehot, upper[...], preferred_element_type=f32) + carry[:, 0:1]
    hot = onehot.astype(f32)
    rank_ref[0] = jnp.sum(hot * before, axis=0, keepdims=True).astype(i32)
    carry[...] = carry[...] + jnp.sum(hot, axis=1, keepdims=True)
    cnt_ref[...] = carry[...].astype(i32)


def _expert_ranks(flat_e):
    a = flat_e.shape[0]
    nt = a // RANK_TILE
    rank, cnt = pl.pallas_call(
        _rank_kernel,
        grid=(nt,),
        in_specs=[pl.BlockSpec((1, 1, RANK_TILE), lambda i: (i, 0, 0))],
        out_specs=[pl.BlockSpec((1, 1, RANK_TILE), lambda i: (i, 0, 0)),
                   pl.BlockSpec((N_EXP, LANES), lambda i: (0, 0))],
        out_shape=[jax.ShapeDtypeStruct((nt, 1, RANK_TILE), i32), jax.ShapeDtypeStruct((N_EXP, LANES), i32)],
        scratch_shapes=[pltpu.VMEM((N_EXP, LANES), f32), pltpu.VMEM((RANK_TILE, RANK_TILE), bf16)],
        compiler_params=_cparams(("arbitrary",)),
        name="expert_ranks",
    )(flat_e.reshape(nt, 1, RANK_TILE))
    return rank.reshape(a), cnt[:, 0]


def _moe_blocks(a):
    return -(-a // MOE_ROWS) + N_EXP


def _moe_slots(flat_e):
    a = flat_e.shape[0]
    rank, counts = _expert_ranks(flat_e)
    padded = (counts + MOE_ROWS - 1) // MOE_ROWS * MOE_ROWS
    eidx = jnp.arange(N_EXP, dtype=i32)
    pend = jnp.sum(jnp.where(eidx[None, :] <= eidx[:, None], padded[None, :], 0), axis=1)
    dest = (pend - padded)[flat_e] + rank
    blk_start = jnp.arange(_moe_blocks(a), dtype=i32) * MOE_ROWS
    nreal = (pend[-1] // MOE_ROWS).astype(i32)
    blk_exp = jnp.sum((pend[None, :] <= blk_start[:, None]).astype(i32), axis=1)
    last_exp = blk_exp[jnp.maximum(nreal - 1, 0)]
    blk_exp = jnp.where(blk_start < pend[-1], jnp.minimum(blk_exp, N_EXP - 1), last_exp)
    return dest, blk_exp, nreal.reshape(1)


def _mem_attn_kernel(q_ref, kv_ref, o_ref, *, feat_major):
    bt, tq, _ = q_ref.shape
    lane = lax.broadcasted_iota(i32, (tq, LANES), 1)
    zero = jnp.zeros((tq, LANES), bf16)

    def kv(b, lo):
        blk = kv_ref[0, b, lo:lo + LANES, :] if feat_major else kv_ref[b, :, lo:lo + LANES]
        return blk.astype(bf16)

    for pair in range(MEM_HEADS // 2):
        cols = slice(LANES * pair, LANES * (pair + 1))
        qts = [q_ref[b, :, cols] for b in range(bt)]
        kts = [kv(b, LANES * pair) for b in range(bt)]
        vts = [kv(b, MEM_W + LANES * pair) for b in range(bt)]
        halves = []
        for h in range(2):
            keep = (lane < MEM_HD) if h == 0 else (lane >= MEM_HD)
            qms = [jnp.where(keep, qt, zero) for qt in qts]
            if feat_major:
                ss = [jnp.dot(qm, kt, preferred_element_type=f32) for qm, kt in zip(qms, kts)]
            else:
                ss = [lax.dot_general(qm, kt, _NT, preferred_element_type=f32) for qm, kt in zip(qms, kts)]
            es = [jnp.exp(s - jnp.max(s, axis=-1, keepdims=True)) for s in ss]
            ps = [(e * (1.0 / jnp.sum(e, axis=-1, keepdims=True))).astype(bf16) for e in es]
            if feat_major:
                halves.append([lax.dot_general(p, vt, _NT, preferred_element_type=f32) for p, vt in zip(ps, vts)])
            else:
                halves.append([jnp.dot(p, vt, preferred_element_type=f32) for p, vt in zip(ps, vts)])
        for b in range(bt):
            o_ref[b, :, cols] = jnp.where(lane < MEM_HD, halves[0][b], halves[1][b])


def _mem_attn(mq, mem_kv, layer=None):
    b, t, w = mq.shape
    tq = min(t, ROW_TILE)
    bt = max(1, min(b, MEM_ROWS // tq))
    if layer is None:
        kv_spec = pl.BlockSpec((bt, MEM_TOK, 2 * w), lambda bb, i: (bb, 0, 0))
    else:
        kv_spec = pl.BlockSpec((1, bt, 2 * w, MEM_TOK), lambda bb, i: (layer, bb, 0, 0))
    return pl.pallas_call(
        functools.partial(_mem_attn_kernel, feat_major=layer is not None),
        grid=(b // bt, t // tq),
        in_specs=[pl.BlockSpec((bt, tq, w), lambda bb, i: (bb, i, 0)), kv_spec],
        out_specs=pl.BlockSpec((bt, tq, w), lambda bb, i: (bb, i, 0)),
        out_shape=jax.ShapeDtypeStruct((b, t, w), f32),
        compiler_params=_cparams(("parallel", "parallel")),
        name="mem_attn",
    )(mq, mem_kv)


def _pad_state(s):
    s = jnp.concatenate([s, jnp.zeros((GLA_KP - GLA_DK, GLA_DV), f32)], axis=0)
    return jnp.concatenate([s, jnp.zeros((GLA_KP, GLA_VP - GLA_DV), f32)], axis=1)


def _gla_kernel(q_ref, k_ref, v_ref, r_ref, glr_ref, wg_ref, bg_ref, gn_ref, s0_ref, o_ref, s_out_ref, st_sc,
                *, bt, chunk, zero_init):
    c = pl.program_id(1)

    @pl.when(c == 0)
    def _():
        if zero_init:
            st_sc[...] = jnp.zeros(st_sc.shape, f32)
        else:
            for b in range(bt):
                for h in range(GLA_HEADS):
                    st_sc[b, h] = jnp.transpose(_pad_state(s0_ref[b, h]))

    tri = lax.broadcasted_iota(i32, (chunk, chunk), 0) >= lax.broadcasted_iota(i32, (chunk, chunk), 1)
    trif = tri.astype(f32)
    problems = [(b, h) for b in range(bt) for h in range(GLA_HEADS)]
    ksl = lambda h: slice(GLA_KP * h, GLA_KP * (h + 1))
    vsl = lambda h: slice(GLA_VP * h, GLA_VP * (h + 1))
    qts, kts, kds, ebl = [], [], [], []
    for b in range(bt):
        z = jnp.dot(glr_ref[b].astype(bf16), wg_ref[...], preferred_element_type=f32) + bg_ref[...]
        log_a = jax.nn.log_sigmoid(z) / GLA_TAU
        bc = jnp.dot(trif, log_a, preferred_element_type=f32, precision=lax.Precision.HIGHEST)
        bl = bc[chunk - 1:chunk, :]
        k = k_ref[b]
        qts.append((q_ref[b] * (GLA_DK ** -0.5) * jnp.exp(bc)).astype(bf16))
        kts.append((k * jnp.exp(-bc)).astype(bf16))
        kds.append((k * jnp.exp(bl - bc)).astype(bf16))
        ebl.append(jnp.exp(bl))
    vbs = [v_ref[b].astype(bf16) for b in range(bt)]
    atts = [jnp.where(tri, lax.dot_general(qts[b][:, ksl(h)], kts[b][:, ksl(h)], _NT, preferred_element_type=f32),
                      0.0).astype(bf16) for b, h in problems]
    sts = [st_sc[b, h] for b, h in problems]
    outs = [lax.dot_general(qts[b][:, ksl(h)], st.astype(bf16), _NT, preferred_element_type=f32)
            + jnp.dot(att, vbs[b][:, vsl(h)], preferred_element_type=f32)
            for (b, h), st, att in zip(problems, sts, atts)]
    for (b, h), st in zip(problems, sts):
        st_sc[b, h] = ebl[b][:, ksl(h)] * st + lax.dot_general(vbs[b][:, vsl(h)], kds[b][:, ksl(h)], _TN,
                                                              preferred_element_type=f32)
    for (b, h), o in zip(problems, outs):
        ms = jnp.sum(o * o, axis=-1, keepdims=True) * (1.0 / GLA_DV)
        o = o * lax.rsqrt(ms + LN_EPS) * gn_ref[:, vsl(h)]
        o_ref[b, :, vsl(h)] = o * jax.nn.silu(r_ref[b, :, vsl(h)])

    @pl.when(c == pl.num_programs(1) - 1)
    def _():
        for b in range(bt):
            for h in range(GLA_HEADS):
                s_out_ref[b, h] = jnp.transpose(st_sc[b, h])[:GLA_DK, :GLA_DV]


def _gla(q, k, v, r, glr, wg, bg, gn, s0, *, bt):
    nb, t, _ = q.shape
    chunk = GLA_CHUNK if t % GLA_CHUNK == 0 else t
    zero_init = s0 is None
    if zero_init:
        s0 = jnp.zeros((1, 1, 8, LANES), f32)
    tok = lambda w: pl.BlockSpec((bt, chunk, w), lambda i, c: (i, c, 0))
    fix = lambda a: pl.BlockSpec(a.shape, lambda i, c: (0,) * a.ndim)
    state = pl.BlockSpec((bt, GLA_HEADS, GLA_DK, GLA_DV), lambda i, c: (i, 0, 0, 0))
    return pl.pallas_call(
        functools.partial(_gla_kernel, bt=bt, chunk=chunk, zero_init=zero_init),
        grid=(nb // bt, t // chunk),
        in_specs=[tok(GLA_HEADS * GLA_KP), tok(GLA_HEADS * GLA_KP), tok(GLA_HEADS * GLA_VP), tok(GLA_HEADS * GLA_VP),
                  tok(LANES), fix(wg), fix(bg), fix(gn), fix(s0) if zero_init else state],
        out_specs=[tok(GLA_HEADS * GLA_VP), state],
        out_shape=[jax.ShapeDtypeStruct((nb, t, GLA_HEADS * GLA_VP), f32),
                   jax.ShapeDtypeStruct((nb, GLA_HEADS, GLA_DK, GLA_DV), f32)],
        scratch_shapes=[pltpu.VMEM((bt, GLA_HEADS, GLA_VP, GLA_KP), f32)],
        compiler_params=_cparams(("parallel", "arbitrary")),
        name="gla",
    )(q, k, v, r, glr, wg, bg, gn, s0)


def _pad_heads(a, w, wp):
    lead = a.shape[:-1]
    a = jnp.pad(a.reshape(*lead, GLA_HEADS, w), [(0, 0)] * (len(lead) + 1) + [(0, wp - w)])
    return a.reshape(*lead, GLA_HEADS * wp)


def _softmax_rows(sm, maskf):
    m = jnp.max(sm, axis=-1, keepdims=True)
    e = jnp.exp(sm - m)
    return e * (1.0 / jnp.sum(e, axis=-1, keepdims=True)) * maskf


def _block_scores_t(imp):
    nc = imp.shape[1]
    jj = lax.broadcasted_iota(i32, (NSB_PAD, nc), 0)
    nn = lax.broadcasted_iota(i32, (NSB_PAD, nc), 1)
    mt = ((nn >> RATIO_SHIFT) == jj).astype(f32) + (((nn + 1) >> RATIO_SHIFT) == jj).astype(f32)
    return lax.dot_general(mt, imp, _NT, preferred_element_type=f32, precision=lax.Precision.HIGHEST)


def _select_t(blk_t, qpos_t):
    nq = blk_t.shape[1]
    j_t = lax.broadcasted_iota(i32, (NSB_PAD, nq), 0)
    valid = j_t * SLC_L <= qpos_t
    cur = qpos_t >> SLC_SHIFT
    forced = (j_t == 0) | (j_t == cur) | (j_t == cur - 1)
    score = jnp.where(valid, jnp.where(forced, 1e9, blk_t), -1e9)
    cnt = jnp.zeros((NSB_PAD, nq), i32)
    for k in range(NSB_PAD):
        row = score[k:k + 1, :]
        beats = (row > score) | ((row == score) & (j_t > k))
        cnt = cnt + beats.astype(i32)
    return ((cnt < SLC_K) & valid).astype(f32)


def _untranspose(sel_t):
    nq = sel_t.shape[1]
    eye = lax.broadcasted_iota(i32, (nq, nq), 0) == lax.broadcasted_iota(i32, (nq, nq), 1)
    return lax.dot_general(eye.astype(bf16), sel_t.astype(bf16), _NT, preferred_element_type=f32).astype(bf16)


def _nsa_prompt_kernel(qp_ref, gl_ref, bg_ref, kc_ref, vc_ref, ks_ref, vs_ref, kw_ref, vw_ref, o_ref,
                       p_sc, m_sc, l_sc, acc_sc, og_sc, used_sm):
    nc = kc_ref.shape[1]
    q0 = pl.program_id(1) * Q_BLK
    qpos_i = q0 + lax.broadcasted_iota(i32, (Q_BLK, 1), 0)
    lane = lax.broadcasted_iota(i32, (Q_BLK, LANES), 1)
    gates = jax.nn.sigmoid(gl_ref[0] + bg_ref[...])
    qp = qp_ref[0]
    zero_b = jnp.zeros((Q_BLK, LANES), bf16)

    for g in range(NSA_KV):
        half = (lane < HD) if g == 0 else (lane >= HD)
        qg = jnp.concatenate([jnp.where(half, qp[:, LANES * hh:LANES * (hh + 1)], zero_b)
                              for hh in range(NSA_HPG)], axis=0)
        slopes = [SLOPES[NSA_HPG * g + hh] for hh in range(NSA_HPG)]

        n_idx = lax.broadcasted_iota(i32, (Q_BLK, nc), 1)
        cd = (qpos_i - (n_idx * CMP_S + (CMP_L - 1))).astype(f32)
        cmask = cd >= 0
        cmaskf = cmask.astype(f32)
        s = lax.dot_general(qg, kc_ref[0], _NT, preferred_element_type=f32)
        imp = jnp.zeros((Q_BLK, nc), f32)
        for hh in range(NSA_HPG):
            rows = slice(hh * Q_BLK, (hh + 1) * Q_BLK)
            p = _softmax_rows(jnp.where(cmask, s[rows] - slopes[hh] * cd, NEG), cmaskf)
            imp = imp + p
            p_sc[rows, :nc] = p.astype(bf16)
        o_cmp = jnp.dot(p_sc[:, :nc], vc_ref[0], preferred_element_type=f32)

        qpos_t = q0 + lax.broadcasted_iota(i32, (NSB_PAD, Q_BLK), 1)
        sel_t = _select_t(_block_scores_t(imp), qpos_t)
        sel_b = _untranspose(sel_t)
        blocks_per_tile = SLC_TILE // SLC_L
        for t in range(NSB_PAD // blocks_per_tile):
            used_sm[t] = (jnp.max(sel_t[t * blocks_per_tile:(t + 1) * blocks_per_tile, :]) > 0.5).astype(i32)

        m_sc[...] = jnp.full(m_sc.shape, NEG, f32)
        l_sc[...] = jnp.zeros(l_sc.shape, f32)
        acc_sc[...] = jnp.zeros(acc_sc.shape, f32)

        def tile(t, carry):
            @pl.when(used_sm[t] > 0)
            def _():
                k0 = pl.multiple_of(t * SLC_TILE, SLC_TILE)
                kt = ks_ref[0, pl.ds(k0, SLC_TILE), :]
                vt = vs_ref[0, pl.ds(k0, SLC_TILE), :]
                st = lax.dot_general(qg, kt, _NT, preferred_element_type=f32)
                kpos = k0 + lax.broadcasted_iota(i32, (Q_BLK, SLC_TILE), 1)
                blk_of_key = (k0 + lax.broadcasted_iota(i32, (NSB_PAD, SLC_TILE), 1)) >> SLC_SHIFT
                expand = (lax.broadcasted_iota(i32, (NSB_PAD, SLC_TILE), 0) == blk_of_key).astype(bf16)
                selexp = jnp.dot(sel_b, expand, preferred_element_type=f32)
                addmask = jnp.where((qpos_i >= kpos) & (selexp > 0.5), 0.0, NEG)
                krel = (k0 - q0 + lax.broadcasted_iota(i32, (1, SLC_TILE), 1)).astype(f32)
                for hh in range(NSA_HPG):
                    rows = slice(hh * Q_BLK, (hh + 1) * Q_BLK)
                    sm = st[rows] + slopes[hh] * krel + addmask
                    m_old = m_sc[rows]
                    m_new = jnp.maximum(m_old, jnp.max(sm, axis=-1, keepdims=True))
                    a = jnp.exp(m_old - m_new)
                    e = jnp.exp(sm - m_new)
                    l_sc[rows] = a * l_sc[rows] + jnp.sum(e, axis=-1, keepdims=True)
                    m_sc[rows] = m_new
                    acc_sc[rows] = a * acc_sc[rows]
                    p_sc[rows, :SLC_TILE] = e.astype(bf16)
                acc_sc[...] += jnp.dot(p_sc[:, :SLC_TILE], vt, preferred_element_type=f32)
            return carry

        lax.fori_loop(0, (q0 + Q_BLK + SLC_TILE - 1) // SLC_TILE, tile, 0)
        o_slc = acc_sc[...] / l_sc[...]

        start = pl.multiple_of(jnp.maximum(q0 - WINDOW, 0), Q_BLK)
        kt = kw_ref[0, pl.ds(start, WIN_KEYS), :]
        vt = vw_ref[0, pl.ds(start, WIN_KEYS), :]
        st = lax.dot_general(qg, kt, _NT, preferred_element_type=f32)
        wd = (qpos_i - (start + lax.broadcasted_iota(i32, (Q_BLK, WIN_KEYS), 1))).astype(f32)
        wmask = (wd >= 0) & (wd <= WINDOW)
        wmaskf = wmask.astype(f32)
        for hh in range(NSA_HPG):
            rows = slice(hh * Q_BLK, (hh + 1) * Q_BLK)
            p = _softmax_rows(jnp.where(wmask, st[rows] - slopes[hh] * wd, NEG), wmaskf)
            p_sc[rows, :WIN_KEYS] = p.astype(bf16)
        o_win = jnp.dot(p_sc[:, :WIN_KEYS], vt, preferred_element_type=f32)

        for hh in range(NSA_HPG):
            rows = slice(hh * Q_BLK, (hh + 1) * Q_BLK)
            c = 3 * (NSA_HPG * g + hh)
            og_sc[g, rows] = (gates[:, c:c + 1] * o_cmp[rows] + gates[:, c + 1:c + 2] * o_slc[rows]
                              + gates[:, c + 2:c + 3] * o_win[rows])

    for hh in range(NSA_HPG):
        rows = slice(hh * Q_BLK, (hh + 1) * Q_BLK)
        o_ref[0, :, LANES * hh:LANES * (hh + 1)] = jnp.where(lane < HD, og_sc[0, rows], og_sc[1, rows])


def _nsa_prompt_call(qp, gl, bg, kc, vc, ks, vs, kw, vw):
    b, t, w = qp.shape
    nc = kc.shape[1]
    assert t % SLC_TILE == 0 and t >= WIN_KEYS and nc % LANES == 0 and t // SLC_L <= NSB_PAD
    rows = NSA_HPG * Q_BLK
    qblk = lambda bb, i: (bb, i, 0)
    whole = lambda bb, i: (bb, 0, 0)
    return pl.pallas_call(
        _nsa_prompt_kernel,
        grid=(b, t // Q_BLK),
        in_specs=[pl.BlockSpec((1, Q_BLK, w), qblk), pl.BlockSpec((1, Q_BLK, LANES), qblk),
                  pl.BlockSpec((1, LANES), lambda bb, i: (0, 0)),
                  pl.BlockSpec((1, nc, LANES), whole), pl.BlockSpec((1, nc, LANES), whole),
                  pl.BlockSpec((1, t, LANES), whole), pl.BlockSpec((1, t, LANES), whole),
                  pl.BlockSpec((1, t, LANES), whole), pl.BlockSpec((1, t, LANES), whole)],
        out_specs=pl.BlockSpec((1, Q_BLK, w), qblk),
        out_shape=jax.ShapeDtypeStruct((b, t, w), f32),
        scratch_shapes=[pltpu.VMEM((rows, WIN_KEYS), bf16), pltpu.VMEM((rows, 1), f32), pltpu.VMEM((rows, 1), f32),
                        pltpu.VMEM((rows, LANES), f32), pltpu.VMEM((NSA_KV, rows, LANES), f32),
                        pltpu.SMEM((NSB_PAD * SLC_L // SLC_TILE,), i32)],
        compiler_params=_cparams(("parallel", "parallel")),
        name="nsa_prompt",
    )(qp, gl, bg, kc, vc, ks, vs, kw, vw)


def _nsa_sample_kernel(pt_ref, qp_ref, gl_ref, bg_ref, kc_ref, vc_ref, kn_ref, vn_ref, wst_ref, wkn_ref, wvn_ref,
                       *rest, n_pages, past_len, bt):
    page_refs = rest[:n_pages * bt]
    o_ref, k_sc, v_sc = rest[n_pages * bt:]
    nq = qp_ref.shape[1]
    nc = kc_ref.shape[1]
    wb = wst_ref.shape[1]
    nkeys = (n_pages + 1) * PAGE_SIZE
    rows_g = NSA_HPG * nq
    qpos_i = past_len + lax.broadcasted_iota(i32, (nq, 1), 0)
    lane = lax.broadcasted_iota(i32, (nq, LANES), 1)
    zero_b = jnp.zeros((nq, LANES), bf16)
    tail = jnp.zeros((PAGE_SIZE - nq, LANES), bf16)

    def rows_of(g, hh):
        r0 = g * rows_g + hh * nq
        return slice(r0, r0 + nq)

    n_idx = lax.broadcasted_iota(i32, (nq, nc), 1)
    cd = (qpos_i - (n_idx * CMP_S + (CMP_L - 1))).astype(f32)
    cmask = cd >= 0
    cmaskf = cmask.astype(f32)
    qpos_t = past_len + lax.broadcasted_iota(i32, (NSB_PAD, nq), 1)
    blk_of_key = lax.broadcasted_iota(i32, (NSB_PAD, nkeys), 1) >> SLC_SHIFT
    expand = (lax.broadcasted_iota(i32, (NSB_PAD, nkeys), 0) == blk_of_key).astype(bf16)
    sd = (qpos_i - lax.broadcasted_iota(i32, (nq, nkeys), 1)).astype(f32)
    wpos = (past_len - wb) + lax.broadcasted_iota(i32, (nq, wb + PAGE_SIZE), 1)
    wd = (qpos_i - wpos).astype(f32)
    wmask = (wd >= 0) & (wd <= WINDOW) & (wpos >= 0)
    wmaskf = wmask.astype(f32)

    def sequence(j):
        gates = jax.nn.sigmoid(gl_ref[j] + bg_ref[...])
        qp = qp_ref[j]
        qs = jnp.concatenate([jnp.where((lane < HD) if g == 0 else (lane >= HD),
                                        qp[:, LANES * hh:LANES * (hh + 1)], zero_b)
                              for g in range(NSA_KV) for hh in range(NSA_HPG)], axis=0)

        s = lax.dot_general(qs, kc_ref[j], _NT, preferred_element_type=f32)
        yield
        imps, ps = [], []
        for g in range(NSA_KV):
            imp = jnp.zeros((nq, nc), f32)
            for hh in range(NSA_HPG):
                p = _softmax_rows(jnp.where(cmask, s[rows_of(g, hh)] - SLOPES[NSA_HPG * g + hh] * cd, NEG), cmaskf)
                imp = imp + p
                ps.append(p)
            imps.append(imp)
        o_cmp = jnp.dot(jnp.concatenate(ps, axis=0).astype(bf16), vc_ref[j], preferred_element_type=f32)
        yield

        masks = []
        for g in range(NSA_KV):
            sel_b = _untranspose(_select_t(_block_scores_t(imps[g]), qpos_t))
            selexp = jnp.dot(sel_b, expand, preferred_element_type=f32)
            masks.append((sd >= 0) & (selexp > 0.5))
        yield

        for pg in range(n_pages):
            blk = page_refs[j * n_pages + pg][0]
            k_sc[j, :, pg * PAGE_SIZE:(pg + 1) * PAGE_SIZE] = blk[:LANES].astype(bf16)
            v_sc[j, :, pg * PAGE_SIZE:(pg + 1) * PAGE_SIZE] = blk[LANES:].astype(bf16)
        new_t = lambda ref: jnp.transpose(jnp.concatenate([ref[j], tail], axis=0).astype(f32)).astype(bf16)
        k_sc[j, :, n_pages * PAGE_SIZE:] = new_t(kn_ref)
        v_sc[j, :, n_pages * PAGE_SIZE:] = new_t(vn_ref)
        s = jnp.dot(qs, k_sc[j], preferred_element_type=f32)
        yield
        ps = []
        for g in range(NSA_KV):
            mf = masks[g].astype(f32)
            for hh in range(NSA_HPG):
                ps.append(_softmax_rows(jnp.where(masks[g], s[rows_of(g, hh)] - SLOPES[NSA_HPG * g + hh] * sd, NEG),
                                        mf))
        o_slc = lax.dot_general(jnp.concatenate(ps, axis=0).astype(bf16), v_sc[j], _NT,
                                preferred_element_type=f32)
        yield

        wk = jnp.concatenate([wst_ref[j, :, :LANES].astype(bf16), wkn_ref[j], tail], axis=0)
        wv = jnp.concatenate([wst_ref[j, :, LANES:].astype(bf16), wvn_ref[j], tail], axis=0)
        s = lax.dot_general(qs, wk, _NT, preferred_element_type=f32)
        yield
        ps = []
        for g in range(NSA_KV):
            for hh in range(NSA_HPG):
                ps.append(_softmax_rows(jnp.where(wmask, s[rows_of(g, hh)] - SLOPES[NSA_HPG * g + hh] * wd, NEG),
                                        wmaskf))
        o_win = jnp.dot(jnp.concatenate(ps, axis=0).astype(bf16), wv, preferred_element_type=f32)
        yield

        for hh in range(NSA_HPG):
            halves = []
            for g in range(NSA_KV):
                r = rows_of(g, hh)
                c = 3 * (NSA_HPG * g + hh)
                halves.append(gates[:, c:c + 1] * o_cmp[r] + gates[:, c + 1:c + 2] * o_slc[r]
                              + gates[:, c + 2:c + 3] * o_win[r])
            o_ref[j, :, LANES * hh:LANES * (hh + 1)] = jnp.where(lane < HD, halves[0], halves[1])

    running = [sequence(j) for j in range(bt)]
    done = object()
    while running:
        running = [seq for seq in running if next(seq, done) is not done]


def _nsa_sample_call(page_table, qp, gl, bg, kc, vc, kn, vn, wst, wkn, wvn, cache):
    b, nq, w = qp.shape
    n_pages = page_table.shape[1]
    nc = kc.shape[1]
    wb = wst.shape[1]
    nkeys = (n_pages + 1) * PAGE_SIZE
    assert nkeys // SLC_L <= NSB_PAD and nq <= PAGE_SIZE
    bt = NSA_SAMPLE_BT if b % NSA_SAMPLE_BT == 0 else 1
    per_b = lambda bb, pt: (bb, 0, 0)
    in_specs = [pl.BlockSpec((bt, nq, w), per_b), pl.BlockSpec((bt, nq, LANES), per_b),
                pl.BlockSpec((1, LANES), lambda bb, pt: (0, 0)),
                pl.BlockSpec((bt, nc, LANES), per_b), pl.BlockSpec((bt, nc, LANES), per_b),
                pl.BlockSpec((bt, nq, LANES), per_b), pl.BlockSpec((bt, nq, LANES), per_b),
                pl.BlockSpec((bt, wb, 2 * LANES), per_b),
                pl.BlockSpec((bt, nq, LANES), per_b), pl.BlockSpec((bt, nq, LANES), per_b)]
    for j in range(bt):
        for pg in range(n_pages):
            in_specs.append(pl.BlockSpec(
                (1, 2 * LANES, PAGE_SIZE),
                functools.partial(lambda bb, pt, j, pg: (pt[bb * bt + j, pg], 1, 0), j=j, pg=pg)))
    grid_spec = pltpu.PrefetchScalarGridSpec(
        num_scalar_prefetch=1, grid=(b // bt,), in_specs=in_specs,
        out_specs=pl.BlockSpec((bt, nq, w), per_b),
        scratch_shapes=[pltpu.VMEM((bt, LANES, nkeys), bf16), pltpu.VMEM((bt, LANES, nkeys), bf16)])
    return pl.pallas_call(
        functools.partial(_nsa_sample_kernel, n_pages=n_pages, past_len=n_pages * PAGE_SIZE, bt=bt),
        grid_spec=grid_spec,
        out_shape=jax.ShapeDtypeStruct((b, nq, w), f32),
        compiler_params=_cparams(("parallel",)),
        name="nsa_sample",
    )(page_table, qp, gl, bg, kc, vc, kn, vn, wst, wkn, wvn, *([cache] * (n_pages * bt)))


def _compress_kernel(x_ref, pos_ref, w1_ref, b1_ref, w2_ref, o_ref):
    x = x_ref[0, 0, 0]
    half = CMP_S * HD
    xa = (x + pos_ref[0, :, :half]).astype(bf16)
    xb = (x + pos_ref[0, :, half:]).astype(bf16)
    a = jnp.dot(xa, w1_ref[0, :half, :], preferred_element_type=f32)
    bsec = jnp.dot(xb, w1_ref[0, half:, :], preferred_element_type=f32)
    nch = x.shape[0]
    hid = jax.nn.gelu(a + pltpu.roll(bsec, nch - 1, 0) + b1_ref[0])
    o_ref[0, 0, 0] = jnp.dot(hid.astype(bf16), w2_ref[0], preferred_element_type=f32).astype(o_ref.dtype)


def _compress_call(x2, cmp_pos, w_cmp1, b_cmp1, w_cmp2):
    b, _, g, nch, cw = x2.shape
    hid = w_cmp1.shape[-1]
    pos = cmp_pos.reshape(2, 1, CMP_L * HD)
    return pl.pallas_call(
        _compress_kernel,
        grid=(b, 2, g),
        in_specs=[pl.BlockSpec((1, 1, 1, nch, cw), lambda bb, s, gg: (bb, s, gg, 0, 0)),
                  pl.BlockSpec((1, 1, CMP_L * HD), lambda bb, s, gg: (s, 0, 0)),
                  pl.BlockSpec((1, CMP_L * HD, hid), lambda bb, s, gg: (s, 0, 0)),
                  pl.BlockSpec((1, 1, hid), lambda bb, s, gg: (s, 0, 0)),
                  pl.BlockSpec((1, hid, HD), lambda bb, s, gg: (s, 0, 0))],
        out_specs=pl.BlockSpec((1, 1, 1, nch, HD), lambda bb, s, gg: (bb, s, gg, 0, 0)),
        out_shape=jax.ShapeDtypeStruct((b, 2, g, nch, HD), bf16),
        compiler_params=_cparams(("parallel", "parallel", "parallel")),
        name="compress",
    )(x2, pos, w_cmp1.astype(bf16), b_cmp1.reshape(2, 1, hid), w_cmp2.astype(bf16))


def _compressed_kv(tok, cmp_pos, w_cmp1, b_cmp1, w_cmp2):
    b, t = tok.shape[:2]
    nch = t // CMP_S
    x2 = jnp.transpose(tok[:, :nch * CMP_S], (0, 2, 3, 1, 4)).reshape(b, 2, NSA_KV, nch, CMP_S * HD)
    ck = _compress_call(x2, cmp_pos, w_cmp1, b_cmp1, w_cmp2)
    ck = jnp.transpose(ck, (0, 1, 3, 2, 4)).reshape(b, 2, nch, NSA_KV * HD)
    return ck[:, 0], ck[:, 1]


def _compress_paged_kernel(pt_ref, pos_ref, w1a_ref, w1b_ref, b1_ref, w2_ref, *rest, n_pages):
    page_refs = rest[:n_pages]
    o_ref, x_sc, xcat_sc = rest[n_pages:]
    nch = n_pages * (PAGE_SIZE // CMP_S)
    role = NSA_KV * HD
    for pg in range(n_pages):
        xt = jnp.transpose(page_refs[pg][0])
        for s in range(2):
            x_sc[s, pg * PAGE_SIZE:(pg + 1) * PAGE_SIZE, :] = xt[:, s * role:(s + 1) * role]
    for s in range(2):
        for t in range(CMP_S):
            xcat_sc[:, t * role:(t + 1) * role] = x_sc[s, pl.ds(t, nch, stride=CMP_S), :]
        xc = xcat_sc[...]
        xa = (xc + pos_ref[s, 0:1, :]).astype(bf16)
        xb = (xc + pos_ref[s, 1:2, :]).astype(bf16)
        a = jnp.dot(xa, w1a_ref[s], preferred_element_type=f32)
        bsec = jnp.dot(xb, w1b_ref[s], preferred_element_type=f32)
        hid = jax.nn.gelu(a + pltpu.roll(bsec, nch - 1, 0) + b1_ref[s])
        o_ref[0, s] = jnp.dot(hid.astype(bf16), w2_ref[s], preferred_element_type=f32).astype(o_ref.dtype)


def _compress_paged(page_table, cache_t, cmp_pos, w_cmp1, b_cmp1, w_cmp2):
    b, n_pages = page_table.shape
    nch = n_pages * (PAGE_SIZE // CMP_S)
    hid = w_cmp1.shape[-1]
    half = CMP_S * HD
    eye = jnp.eye(NSA_KV, dtype=f32)

    def widen_rows(w):
        w = w.reshape(2, CMP_S, HD, hid)
        return jnp.einsum('stdh,ag->stadgh', w, eye).reshape(2, CMP_S * NSA_KV * HD, NSA_KV * hid)

    w1a = widen_rows(w_cmp1[:, :half]).astype(bf16)
    w1b = widen_rows(w_cmp1[:, half:]).astype(bf16)
    w2 = jnp.einsum('shd,ag->sahgd', w_cmp2, eye).reshape(2, NSA_KV * hid, NSA_KV * HD).astype(bf16)
    b1 = jnp.tile(b_cmp1, (1, NSA_KV)).reshape(2, 1, NSA_KV * hid)
    pos = jnp.broadcast_to(cmp_pos.reshape(2, 2, CMP_S, 1, HD), (2, 2, CMP_S, NSA_KV, HD))
    pos = pos.reshape(2, 2, CMP_S * NSA_KV * HD)
    fix = lambda a: pl.BlockSpec(a.shape, lambda bb, pt: (0,) * a.ndim)
    in_specs = [fix(pos), fix(w1a), fix(w1b), fix(b1), fix(w2)]
    for pg in range(n_pages):
        in_specs.append(pl.BlockSpec((1, 2 * NSA_KV * HD, PAGE_SIZE),
                                     functools.partial(lambda bb, pt, pg: (pt[bb, pg], 0, 0), pg=pg)))
    grid_spec = pltpu.PrefetchScalarGridSpec(
        num_scalar_prefetch=1, grid=(b,), in_specs=in_specs,
        out_specs=pl.BlockSpec((1, 2, nch, NSA_KV * HD), lambda bb, pt: (bb, 0, 0, 0)),
        scratch_shapes=[pltpu.VMEM((2, n_pages * PAGE_SIZE, NSA_KV * HD), f32),
                        pltpu.VMEM((nch, CMP_S * NSA_KV * HD), f32)])
    return pl.pallas_call(
        functools.partial(_compress_paged_kernel, n_pages=n_pages),
        grid_spec=grid_spec,
        out_shape=jax.ShapeDtypeStruct((b, 2, nch, NSA_KV * HD), bf16),
        compiler_params=_cparams(("parallel",)),
        name="compress_paged",
    )(page_table, pos, w1a, w1b, b1, w2, *([cache_t] * n_pages))


def _pair_pack_cols(w):
    k = w.shape[0]
    return jnp.transpose(w.reshape(k, NSA_KV, NSA_HPG, HD), (0, 2, 1, 3)).reshape(k, NSA_W)


def _last_rows(a, n):
    t = a.shape[1]
    if t >= n:
        return a[:, t - n:]
    return jnp.pad(a, ((0, 0), (n - t, 0)) + ((0, 0),) * (a.ndim - 2))


def kernel(x_prompt, x_sample, mem_prompt, state_gla, cache_nsa_kv, state_win_kv, cache_mem_kv, page_table,
           w_in_a, w_gate_a, b_gate_a, gla_norm, w_in_b, b_gate_b, w_kv_b, cmp_pos, w_cmp1, b_cmp1, w_cmp2,
           w_mem_kv, w_out, ln1_g, ln1_b, ln2_g, ln2_b, w_router, b_router, w_e1, b_e1, w_e2, b_e2):
    bp, tp, d = x_prompt.shape
    bs, ts, _ = x_sample.shape
    n_p, n_s = bp * tp, bs * ts
    wb = min(WINDOW, state_win_kv.shape[1])
    n_pool = cache_nsa_kv.shape[0]
    past_len = page_table.shape[1] * PAGE_SIZE

    mem_rows = mem_prompt.reshape(bp * MEM_TOK, d)
    w_mem = [w_mem_kv[l].astype(bf16) for l in range(DEPTH)]
    mem_kv_l = _proj(mem_rows, w_mem, _whole(w_mem))
    mem_kv_prompt = jnp.stack(mem_kv_l).reshape(DEPTH, bp, MEM_TOK, 2, MEM_HEADS, MEM_HD)
    mem_p = [m.reshape(bp, MEM_TOK, 2 * MEM_W) for m in mem_kv_l]
    mem_s = jnp.transpose(cache_mem_kv, (0, 1, 3, 4, 5, 2)).reshape(DEPTH, bs, 2 * MEM_W, MEM_TOK)

    x_p, x_s = x_prompt.reshape(n_p, d), x_sample.reshape(n_s, d)
    as_p = lambda a: a.reshape(bp, tp, a.shape[-1])
    as_s = lambda a: a.reshape(bs, ts, a.shape[-1])

    hk = GLA_HEADS * GLA_DK
    cuts_a = [0, hk, 2 * hk, 2 * hk + GLA_W, 2 * hk + 2 * GLA_W, 2 * hk + 2 * GLA_W + GLA_RANK,
              2 * hk + 2 * GLA_W + GLA_RANK + MEM_W]
    mem_scale = MEM_HD ** -0.5
    q_scale = HD ** -0.5
    xb = jnp.zeros((_moe_blocks((n_p + n_s) * TOP_K) * MOE_ROWS, d), f32)

    gla_p, gla_s = [], []
    for l in range(DEPTH):
        if l < N_A:
            wq, wk, wv, wr, wgl, wm = [w_in_a[l][:, cuts_a[i]:cuts_a[i + 1]] for i in range(6)]
            ws = [_pad_heads(wq, GLA_DK, GLA_KP), _pad_heads(wk, GLA_DK, GLA_KP), _pad_heads(wv, GLA_DV, GLA_VP),
                  _pad_heads(wr, GLA_DV, GLA_VP), jnp.pad(wgl, ((0, 0), (0, LANES - GLA_RANK))), wm * mem_scale]
            ws = [w.astype(bf16) for w in ws]
            outs = _whole(ws)
            outs[5] = outs[5][:3] + (bf16,)
            wg = _pad_heads(jnp.pad(w_gate_a[l], ((0, LANES - GLA_RANK), (0, 0))), GLA_DK, GLA_KP).astype(bf16)
            bg = _pad_heads(b_gate_a[l], GLA_DK, GLA_KP).reshape(1, GLA_HEADS * GLA_KP)
            gn = _pad_heads(gla_norm[l], GLA_DV, GLA_VP).reshape(1, GLA_HEADS * GLA_VP)
            q, k, v, r, glr, mq_p = _proj(x_p, ws, outs)
            o_p, s_new = _gla(as_p(q), as_p(k), as_p(v), as_p(r), as_p(glr), wg, bg, gn, None, bt=min(bp, GLA_BT))
            gla_p.append(s_new)
            q, k, v, r, glr, mq_s = _proj(x_s, ws, outs)
            o_s, s_new = _gla(as_s(q), as_s(k), as_s(v), as_s(r), as_s(glr), wg, bg, gn, state_gla[l],
                              bt=min(bs, GLA_BT))
            gla_s.append(s_new)
            w_o = _pad_heads(w_out[l][:GLA_W].T, GLA_DV, GLA_VP).T
        else:
            j = l - N_A
            if l == N_A:
                role = NSA_KV * HD
                kv_outs = [(0, 0, 4 * role, f32), (0, 4 * role, 6 * role, f32)]
                kv_outs += [(0, r * role, (r + 1) * role, bf16) for r in range(2, 6)]
                wkv = [w_kv_b.astype(bf16)]
                rows_p, win_p, ks_p, vs_p, kw_p, vw_p = [as_p(a) for a in _proj(x_p, wkv, kv_outs)]
                rows_s, win_s, ks_s, vs_s, kw_s, vw_s = [as_s(a) for a in _proj(x_s, wkv, kv_outs)]
                kc_p, vc_p = _compressed_kv(rows_p.reshape(bp, tp, 4, NSA_KV, HD)[:, :, :2],
                                            cmp_pos, w_cmp1, b_cmp1, w_cmp2)
                assert (past_len + ts) // CMP_S == past_len // CMP_S
                cache_t = jnp.transpose(cache_nsa_kv, (0, 2, 3, 4, 1)).reshape(n_pool, 4 * role, PAGE_SIZE)
                ck_s = _compress_paged(page_table, cache_t, cmp_pos, w_cmp1, b_cmp1, w_cmp2)
                kc_s, vc_s = ck_s[:, 0], ck_s[:, 1]
                wst = state_win_kv.reshape(bs, state_win_kv.shape[1], 2 * role)
                wkv_s = jnp.concatenate([state_win_kv, win_s.reshape(bs, ts, 2, NSA_KV, HD)], axis=1)
                nsa_out = (rows_p.reshape(bp, tp, 4, NSA_KV, HD), _last_rows(win_p.reshape(bp, tp, 2, NSA_KV, HD), wb),
                           rows_s.reshape(bs, ts, 4, NSA_KV, HD), wkv_s[:, -wb:])
            wq = _pair_pack_cols(w_in_b[j][:, :NSA_W]) * q_scale
            wg = jnp.pad(w_in_b[j][:, NSA_W:NSA_W + 3 * NSA_HEADS], ((0, 0), (0, LANES - 3 * NSA_HEADS)))
            wm = w_in_b[j][:, NSA_W + 3 * NSA_HEADS:] * mem_scale
            ws = [wq.astype(bf16), wg.astype(bf16), wm.astype(bf16)]
            outs = [(0, 0, NSA_W, bf16), (1, 0, LANES, f32), (2, 0, MEM_W, bf16)]
            bg = jnp.pad(b_gate_b[j], (0, LANES - 3 * NSA_HEADS)).reshape(1, LANES)
            qp, gl, mq_p = _proj(x_p, ws, outs)
            o_p = _nsa_prompt_call(as_p(qp), as_p(gl), bg, kc_p, vc_p, ks_p, vs_p, kw_p, vw_p)
            qp, gl, mq_s = _proj(x_s, ws, outs)
            o_s = _nsa_sample_call(page_table, as_s(qp), as_s(gl), bg, kc_s, vc_s, ks_s, vs_s, wst, kw_s, vw_s,
                                   cache_t)
            w_o = _pair_pack_cols(w_out[l][:NSA_W].T).T
        om_p = _mem_attn(as_p(mq_p), mem_p[l])
        om_s = _mem_attn(as_s(mq_s), mem_s, layer=l)
        w_o, w_om = w_o.astype(bf16), w_out[l][-MEM_W:].astype(bf16)
        flat = lambda a: a.reshape(-1, a.shape[-1])
        x1_p, gate_p, idx_p = _mix_ln([(flat(o_p), w_o), (flat(om_p), w_om)], x_p, ln1_g[l], ln1_b[l],
                                      w_router[l], b_router[l])
        x1_s, gate_s, idx_s = _mix_ln([(flat(o_s), w_o), (flat(om_s), w_om)], x_s, ln1_g[l], ln1_b[l],
                                      w_router[l], b_router[l])
        dest, blk_exp, nreal = _moe_slots(jnp.concatenate([idx_p.reshape(-1), idx_s.reshape(-1)]))
        dest_p, dest_s = dest[:n_p * TOP_K], dest[n_p * TOP_K:]
        xb = _moe_dispatch(x1_s, dest_s, _moe_dispatch(x1_p, dest_p, xb))
        yb = _moe_ffn_blocks(xb, blk_exp, nreal, w_e1, b_e1, w_e2, b_e2, l)
        x_p = _moe_combine(yb, dest_p, gate_p, x1_p, ln2_g[l], ln2_b[l])
        x_s = _moe_combine(yb, dest_s, gate_s, x1_s, ln2_g[l], ln2_b[l])

    rows_p, win_p_out, rows_s, win_s_out = nsa_out
    return (x_p.reshape(bp, tp, d), x_s.reshape(bs, ts, d),
            jnp.stack(gla_p), jnp.stack(gla_s), rows_p, rows_s, win_p_out, win_s_out, mem_kv_prompt)
```

```python
import functools
import math

import jax
import jax.numpy as jnp
from jax import lax
from jax.experimental import pallas as pl
from jax.experimental.pallas import tpu as pltpu

D_MODEL = 1024
DEPTH = 4
PAGE_SIZE = 128
N_A = DEPTH // 2
GLA_HEADS = 4
GLA_DV = (3 * D_MODEL) // (4 * GLA_HEADS)
GLA_DK = GLA_DV // 2
GLA_RANK = 16
GLA_TAU = 16.0
GLA_CHUNK = 64
HD = 64
NSA_HEADS = (3 * D_MODEL) // (4 * HD)
NSA_KV = 2
NSA_HPG = NSA_HEADS // NSA_KV
CMP_S = 16
CMP_L = 2 * CMP_S
SLC_L = 64
SLC_K = 16
WINDOW = 512
Q_BLK = 128
MEM_TOK = 256
MEM_HEADS = 4
MEM_HD = D_MODEL // (4 * MEM_HEADS)
N_EXP = 32
TOP_K = 4
D_FF = D_MODEL
SWIGLU_LIMIT = 7.0
SWIGLU_ALPHA = 1.702
GLA_W = GLA_HEADS * GLA_DV
NSA_W = NSA_HEADS * HD
MEM_W = MEM_HEADS * MEM_HD
DN_ALPHA = (2 * DEPTH) ** 0.25
LN_EPS = 1e-5
NEG = -1e30

f32, bf16, i32 = jnp.float32, jnp.bfloat16, jnp.int32

VMEM_LIMIT_BYTES = 56 * 1024 * 1024
LANES = 128
ROW_TILE = 512
MOE_ROWS = 512
FF_CHUNK = 512
SLC_TILE = 512
WIN_KEYS = WINDOW + Q_BLK
NSB_PAD = 64
TOK_TILE = 256
DMA_UNROLL = 4
RANK_TILE = 1024
MEM_ROWS = 64
NSA_SAMPLE_BT = 2
GLA_KP = 128
GLA_VP = 256
GLA_BT = 8
_NT = (((1,), (1,)), ((), ()))
_TN = (((0,), (0,)), ((), ()))


def _cparams(sem):
    return pltpu.CompilerParams(dimension_semantics=sem, vmem_limit_bytes=VMEM_LIMIT_BYTES)


def _alibi_slopes(n):
    def pow2(m):
        start = 2.0 ** (-8.0 / m)
        return [start ** (i + 1) for i in range(m)]
    if math.log2(n).is_integer():
        return pow2(n)
    c = 2 ** math.floor(math.log2(n))
    return pow2(c) + pow2(2 * c)[0::2][: n - c]


SLOPES = _alibi_slopes(NSA_HEADS)
SLC_SHIFT = int(math.log2(SLC_L))
RATIO_SHIFT = int(math.log2(SLC_L // CMP_S))


def _proj_kernel(x_ref, *refs, n_w, outs):
    xb = x_ref[...].astype(bf16)
    res = {}
    for o_ref, (wi, lo, hi) in zip(refs[n_w:], outs):
        if wi not in res:
            res[wi] = jnp.dot(xb, refs[wi][...], preferred_element_type=f32)
        o_ref[...] = res[wi][:, lo:hi].astype(o_ref.dtype)


def _proj(x, ws, outs, tm=ROW_TILE):
    m, k = x.shape
    tm = min(tm, m)
    in_specs = [pl.BlockSpec((tm, k), lambda i: (i, 0))]
    in_specs += [pl.BlockSpec(w.shape, lambda i: (0, 0)) for w in ws]
    out_specs = [pl.BlockSpec((tm, hi - lo), lambda i: (i, 0)) for _, lo, hi, _ in outs]
    out_shape = [jax.ShapeDtypeStruct((m, hi - lo), dt) for _, lo, hi, dt in outs]
    return pl.pallas_call(
        functools.partial(_proj_kernel, n_w=len(ws), outs=tuple(o[:3] for o in outs)),
        grid=(m // tm,),
        in_specs=in_specs,
        out_specs=out_specs,
        out_shape=out_shape,
        compiler_params=_cparams(("parallel",)),
        name="proj",
    )(x, *ws)


def _whole(ws, dtype=f32):
    return [(i, 0, w.shape[1], dtype) for i, w in enumerate(ws)]


def _layer_norm_rows(y, g, b):
    mu = jnp.mean(y, axis=-1, keepdims=True)
    d = y - mu
    var = jnp.mean(d * d, axis=-1, keepdims=True)
    return d * lax.rsqrt(var + LN_EPS) * g + b


def _mix_ln_kernel(*refs, n_parts):
    a_refs, w_refs = refs[:n_parts], refs[n_parts:2 * n_parts]
    x_ref, g_ref, b_ref, wr_ref, br_ref, x1_ref, gate_ref, idx_ref = refs[2 * n_parts:]
    tm = x_ref.shape[0]
    sub = tm // 2 if tm % 16 == 0 else tm
    for r0 in range(0, tm, sub):
        rows = slice(r0, r0 + sub)
        mix = jnp.dot(a_refs[0][rows, :].astype(bf16), w_refs[0][...], preferred_element_type=f32)
        for a_ref, w_ref in zip(a_refs[1:], w_refs[1:]):
            mix = mix + jnp.dot(a_ref[rows, :].astype(bf16), w_ref[...], preferred_element_type=f32)
        x1 = _layer_norm_rows(DN_ALPHA * x_ref[rows, :] + mix, g_ref[...], b_ref[...])
        x1_ref[rows, :] = x1
        lg = jnp.dot(x1, wr_ref[...], preferred_element_type=f32, precision=lax.Precision.HIGHEST) + br_ref[...]
        lane = lax.broadcasted_iota(i32, lg.shape, 1)
        vals, idxs = [], []
        for _ in range(TOP_K):
            m = jnp.max(lg, axis=-1, keepdims=True)
            i = jnp.min(jnp.where(lg == m, lane, N_EXP), axis=-1, keepdims=True)
            vals.append(m)
            idxs.append(i)
            lg = jnp.where(lane == i, -jnp.inf, lg)
        es = [jnp.exp(v - vals[0]) for v in vals]
        tot = es[0] + es[1] + es[2] + es[3]
        col = lax.broadcasted_iota(i32, (sub, TOP_K), 1)
        gate = jnp.zeros((sub, TOP_K), f32)
        idx = jnp.zeros((sub, TOP_K), i32)
        for k in range(TOP_K):
            gate = jnp.where(col == k, es[k] / tot, gate)
            idx = jnp.where(col == k, idxs[k], idx)
        gate_ref[rows, :] = gate
        idx_ref[rows, :] = idx


def _mix_ln(parts, x, g, b, w_r, b_r, tm=ROW_TILE):
    m, d = x.shape
    row = lambda i: (i, 0)
    fix = lambda i: (0, 0)
    in_specs = [pl.BlockSpec((tm, a.shape[1]), row) for a, _ in parts]
    in_specs += [pl.BlockSpec(w.shape, fix) for _, w in parts]
    in_specs += [pl.BlockSpec((tm, d), row), pl.BlockSpec((1, d), fix), pl.BlockSpec((1, d), fix),
                 pl.BlockSpec((d, N_EXP), fix), pl.BlockSpec((1, N_EXP), fix)]
    return pl.pallas_call(
        functools.partial(_mix_ln_kernel, n_parts=len(parts)),
        grid=(m // tm,),
        in_specs=in_specs,
        out_specs=[pl.BlockSpec((tm, d), row), pl.BlockSpec((tm, TOP_K), row), pl.BlockSpec((tm, TOP_K), row)],
        out_shape=[jax.ShapeDtypeStruct((m, d), f32), jax.ShapeDtypeStruct((m, TOP_K), f32),
                   jax.ShapeDtypeStruct((m, TOP_K), i32)],
        compiler_params=_cparams(("parallel",)),
        name="mix_ln",
    )(*[a for a, _ in parts], *[w for _, w in parts], x, g.reshape(1, d), b.reshape(1, d), w_r,
      b_r.reshape(1, N_EXP))


def _moe_kernel(be_ref, nreal_ref, x_ref, w1_ref, b1_ref, w2_ref, b2_ref, o_ref, w1s, w2s):
    i = pl.program_id(0)
    real = i < nreal_ref[0]
    prev = be_ref[jnp.maximum(i - 1, 0)]
    fresh = jnp.logical_or(i == 0, be_ref[i] != prev)

    @pl.when(jnp.logical_and(real, fresh))
    def _():
        w1s[...] = w1_ref[0, 0].astype(bf16)
        w2s[...] = w2_ref[0, 0].astype(bf16)

    @pl.when(real)
    def _():
        x = x_ref[...].astype(bf16)
        acc = jnp.zeros((MOE_ROWS, D_MODEL), f32)
        for c in range(D_FF // FF_CHUNK):
            lo = c * FF_CHUNK
            hg = jnp.dot(x, w1s[:, lo:lo + FF_CHUNK], preferred_element_type=f32)
            hg = hg + b1_ref[0, 0, :, lo:lo + FF_CHUNK]
            hu = jnp.dot(x, w1s[:, D_FF + lo:D_FF + lo + FF_CHUNK], preferred_element_type=f32)
            hu = hu + b1_ref[0, 0, :, D_FF + lo:D_FF + lo + FF_CHUNK]
            g = jnp.minimum(hg, SWIGLU_LIMIT)
            u = jnp.clip(hu, -SWIGLU_LIMIT, SWIGLU_LIMIT)
            a = (u + 1.0) * g * jax.nn.sigmoid(SWIGLU_ALPHA * g)
            acc = acc + jnp.dot(a.astype(bf16), w2s[lo:lo + FF_CHUNK, :], preferred_element_type=f32)
        o_ref[...] = acc + b2_ref[0, 0]

    @pl.when(jnp.logical_not(real))
    def _():
        o_ref[...] = jnp.zeros_like(o_ref)


def _moe_ffn_blocks(xb, blk_exp, nreal, w1, b1, w2, b2, layer):
    p, d = xb.shape
    nblk = p // MOE_ROWS
    n_layers = w1.shape[0]
    per_expert = lambda i, be, nr: (layer, be[i], 0, 0)
    grid_spec = pltpu.PrefetchScalarGridSpec(
        num_scalar_prefetch=2,
        grid=(nblk,),
        in_specs=[
            pl.BlockSpec((MOE_ROWS, d), lambda i, be, nr: (i, 0)),
            pl.BlockSpec((1, 1, d, 2 * D_FF), per_expert),
            pl.BlockSpec((1, 1, 1, 2 * D_FF), per_expert),
            pl.BlockSpec((1, 1, D_FF, d), per_expert),
            pl.BlockSpec((1, 1, 1, d), per_expert),
        ],
        out_specs=pl.BlockSpec((MOE_ROWS, d), lambda i, be, nr: (i, 0)),
        scratch_shapes=[pltpu.VMEM((d, 2 * D_FF), bf16), pltpu.VMEM((D_FF, d), bf16)],
    )
    return pl.pallas_call(
        _moe_kernel,
        grid_spec=grid_spec,
        out_shape=jax.ShapeDtypeStruct((p, d), f32),
        compiler_params=_cparams(("arbitrary",)),
        name="moe_ffn",
    )(blk_exp, nreal, xb, w1, b1.reshape(n_layers, N_EXP, 1, 2 * D_FF), w2, b2.reshape(n_layers, N_EXP, 1, d))


def _row_copy(src, src_row, dst, dst_row, sem):
    return pltpu.make_async_copy(src.at[pl.ds(src_row, 1), :], dst.at[pl.ds(dst_row, 1), :], sem)


def _moe_dispatch_kernel(dest_ref, x_ref, xb_init_ref, xb_ref, sem):
    del xb_init_ref

    def issue(t, c):
        for k in range(TOP_K):
            _row_copy(x_ref, t, xb_ref, dest_ref[0, 0, t * TOP_K + k], sem).start(priority=k % 2)
        return c

    lax.fori_loop(0, TOK_TILE, issue, 0, unroll=DMA_UNROLL)

    def drain(t, c):
        for k in range(TOP_K):
            _row_copy(x_ref, 0, xb_ref, 0, sem).wait()
        return c

    lax.fori_loop(0, TOK_TILE, drain, 0, unroll=DMA_UNROLL)


def _moe_dispatch(x1, dest, xb):
    n, d = x1.shape
    nt = n // TOK_TILE
    return pl.pallas_call(
        _moe_dispatch_kernel,
        grid=(nt,),
        in_specs=[pl.BlockSpec((1, 1, TOK_TILE * TOP_K), lambda i: (i, 0, 0), memory_space=pltpu.SMEM),
                  pl.BlockSpec((TOK_TILE, d), lambda i: (i, 0)),
                  pl.BlockSpec(memory_space=pl.ANY)],
        out_specs=pl.BlockSpec(memory_space=pl.ANY),
        out_shape=jax.ShapeDtypeStruct(xb.shape, f32),
        scratch_shapes=[pltpu.SemaphoreType.DMA(())],
        input_output_aliases={2: 0},
        compiler_params=_cparams(("arbitrary",)),
        name="moe_dispatch",
    )(dest.reshape(nt, 1, TOK_TILE * TOP_K), x1, xb)


def _moe_combine_kernel(dest_ref, yb_ref, gate_ref, x1_ref, g_ref, b_ref, o_ref, ybuf, sem):
    def issue(t, c):
        for k in range(TOP_K):
            _row_copy(yb_ref, dest_ref[0, 0, t * TOP_K + k], ybuf.at[k], t, sem).start(priority=k % 2)
        return c

    lax.fori_loop(0, TOK_TILE, issue, 0, unroll=DMA_UNROLL)

    def drain(t, c):
        for k in range(TOP_K):
            _row_copy(yb_ref, 0, ybuf.at[k], 0, sem).wait()
        return c

    lax.fori_loop(0, TOK_TILE, drain, 0, unroll=DMA_UNROLL)
    gate = gate_ref[...]
    y = gate[:, 0:1] * ybuf[0]
    for k in range(1, TOP_K):
        y = y + gate[:, k:k + 1] * ybuf[k]
    o_ref[...] = _layer_norm_rows(DN_ALPHA * x1_ref[...] + y, g_ref[...], b_ref[...])


def _moe_combine(yb, dest, gate, x1, g, b):
    n, d = x1.shape
    nt = n // TOK_TILE
    row = lambda i: (i, 0)
    fix = lambda i: (0, 0)
    return pl.pallas_call(
        _moe_combine_kernel,
        grid=(nt,),
        in_specs=[pl.BlockSpec((1, 1, TOK_TILE * TOP_K), lambda i: (i, 0, 0), memory_space=pltpu.SMEM),
                  pl.BlockSpec(memory_space=pl.ANY),
                  pl.BlockSpec((TOK_TILE, TOP_K), row), pl.BlockSpec((TOK_TILE, d), row),
                  pl.BlockSpec((1, d), fix), pl.BlockSpec((1, d), fix)],
        out_specs=pl.BlockSpec((TOK_TILE, d), row),
        out_shape=jax.ShapeDtypeStruct((n, d), f32),
        scratch_shapes=[pltpu.VMEM((TOP_K, TOK_TILE, d), f32), pltpu.SemaphoreType.DMA(())],
        compiler_params=_cparams(("arbitrary",)),
        name="moe_combine",
    )(dest.reshape(nt, 1, TOK_TILE * TOP_K), yb, gate, x1, g.reshape(1, d), b.reshape(1, d))


def _rank_kernel(e_ref, rank_ref, cnt_ref, carry, upper):
    @pl.when(pl.program_id(0) == 0)
    def _():
        carry[...] = jnp.zeros(carry.shape, f32)
        upper[...] = (lax.broadcasted_iota(i32, (RANK_TILE, RANK_TILE), 0)
                      < lax.broadcasted_iota(i32, (RANK_TILE, RANK_TILE), 1)).astype(bf16)

    e = e_ref[0]
    onehot = (lax.broadcasted_iota(i32, (N_EXP, RANK_TILE), 0) == e).astype(bf16)
    before = jnp.dot(onehot, upper[...], preferred_element_type=f32) + carry[:, 0:1]
    hot = onehot.astype(f32)
    rank_ref[0] = jnp.sum(hot * before, axis=0, keepdims=True).astype(i32)
    carry[...] = carry[...] + jnp.sum(hot, axis=1, keepdims=True)
    cnt_ref[...] = carry[...].astype(i32)


def _expert_ranks(flat_e):
    a = flat_e.shape[0]
    nt = a // RANK_TILE
    rank, cnt = pl.pallas_call(
        _rank_kernel,
        grid=(nt,),
        in_specs=[pl.BlockSpec((1, 1, RANK_TILE), lambda i: (i, 0, 0))],
        out_specs=[pl.BlockSpec((1, 1, RANK_TILE), lambda i: (i, 0, 0)),
                   pl.BlockSpec((N_EXP, LANES), lambda i: (0, 0))],
        out_shape=[jax.ShapeDtypeStruct((nt, 1, RANK_TILE), i32), jax.ShapeDtypeStruct((N_EXP, LANES), i32)],
        scratch_shapes=[pltpu.VMEM((N_EXP, LANES), f32), pltpu.VMEM((RANK_TILE, RANK_TILE), bf16)],
        compiler_params=_cparams(("arbitrary",)),
        name="expert_ranks",
    )(flat_e.reshape(nt, 1, RANK_TILE))
    return rank.reshape(a), cnt[:, 0]


def _moe_blocks(a):
    return -(-a // MOE_ROWS) + N_EXP


def _moe_slots(flat_e):
    a = flat_e.shape[0]
    rank, counts = _expert_ranks(flat_e)
    padded = (counts + MOE_ROWS - 1) // MOE_ROWS * MOE_ROWS
    eidx = jnp.arange(N_EXP, dtype=i32)
    pend = jnp.sum(jnp.where(eidx[None, :] <= eidx[:, None], padded[None, :], 0), axis=1)
    dest = (pend - padded)[flat_e] + rank
    blk_start = jnp.arange(_moe_blocks(a), dtype=i32) * MOE_ROWS
    nreal = (pend[-1] // MOE_ROWS).astype(i32)
    blk_exp = jnp.sum((pend[None, :] <= blk_start[:, None]).astype(i32), axis=1)
    last_exp = blk_exp[jnp.maximum(nreal - 1, 0)]
    blk_exp = jnp.where(blk_start < pend[-1], jnp.minimum(blk_exp, N_EXP - 1), last_exp)
    return dest, blk_exp, nreal.reshape(1)


def _mem_attn_kernel(q_ref, kv_ref, o_ref, *, feat_major):
    bt, tq, _ = q_ref.shape
    lane = lax.broadcasted_iota(i32, (tq, LANES), 1)
    zero = jnp.zeros((tq, LANES), bf16)

    def kv(b, lo):
        blk = kv_ref[0, b, lo:lo + LANES, :] if feat_major else kv_ref[b, :, lo:lo + LANES]
        return blk.astype(bf16)

    for pair in range(MEM_HEADS // 2):
        cols = slice(LANES * pair, LANES * (pair + 1))
        qts = [q_ref[b, :, cols] for b in range(bt)]
        kts = [kv(b, LANES * pair) for b in range(bt)]
        vts = [kv(b, MEM_W + LANES * pair) for b in range(bt)]
        halves = []
        for h in range(2):
            keep = (lane < MEM_HD) if h == 0 else (lane >= MEM_HD)
            qms = [jnp.where(keep, qt, zero) for qt in qts]
            if feat_major:
                ss = [jnp.dot(qm, kt, preferred_element_type=f32) for qm, kt in zip(qms, kts)]
            else:
                ss = [lax.dot_general(qm, kt, _NT, preferred_element_type=f32) for qm, kt in zip(qms, kts)]
            es = [jnp.exp(s - jnp.max(s, axis=-1, keepdims=True)) for s in ss]
            ps = [(e * (1.0 / jnp.sum(e, axis=-1, keepdims=True))).astype(bf16) for e in es]
            if feat_major:
                halves.append([lax.dot_general(p, vt, _NT, preferred_element_type=f32) for p, vt in zip(ps, vts)])
            else:
                halves.append([jnp.dot(p, vt, preferred_element_type=f32) for p, vt in zip(ps, vts)])
        for b in range(bt):
            o_ref[b, :, cols] = jnp.where(lane < MEM_HD, halves[0][b], halves[1][b])


def _mem_attn(mq, mem_kv, layer=None):
    b, t, w = mq.shape
    tq = min(t, ROW_TILE)
    bt = max(1, min(b, MEM_ROWS // tq))
    if layer is None:
        kv_spec = pl.BlockSpec((bt, MEM_TOK, 2 * w), lambda bb, i: (bb, 0, 0))
    else:
        kv_spec = pl.BlockSpec((1, bt, 2 * w, MEM_TOK), lambda bb, i: (layer, bb, 0, 0))
    return pl.pallas_call(
        functools.partial(_mem_attn_kernel, feat_major=layer is not None),
        grid=(b // bt, t // tq),
        in_specs=[pl.BlockSpec((bt, tq, w), lambda bb, i: (bb, i, 0)), kv_spec],
        out_specs=pl.BlockSpec((bt, tq, w), lambda bb, i: (bb, i, 0)),
        out_shape=jax.ShapeDtypeStruct((b, t, w), f32),
        compiler_params=_cparams(("parallel", "parallel")),
        name="mem_attn",
    )(mq, mem_kv)


def _pad_state(s):
    s = jnp.concatenate([s, jnp.zeros((GLA_KP - GLA_DK, GLA_DV), f32)], axis=0)
    return jnp.concatenate([s, jnp.zeros((GLA_KP, GLA_VP - GLA_DV), f32)], axis=1)


def _gla_kernel(q_ref, k_ref, v_ref, r_ref, glr_ref, wg_ref, bg_ref, gn_ref, s0_ref, o_ref, s_out_ref, st_sc,
                *, bt, chunk, zero_init):
    c = pl.program_id(1)

    @pl.when(c == 0)
    def _():
        if zero_init:
            st_sc[...] = jnp.zeros(st_sc.shape, f32)
        else:
            for b in range(bt):
                for h in range(GLA_HEADS):
                    st_sc[b, h] = jnp.transpose(_pad_state(s0_ref[b, h]))

    tri = lax.broadcasted_iota(i32, (chunk, chunk), 0) >= lax.broadcasted_iota(i32, (chunk, chunk), 1)
    trif = tri.astype(f32)
    problems = [(b, h) for b in range(bt) for h in range(GLA_HEADS)]
    ksl = lambda h: slice(GLA_KP * h, GLA_KP * (h + 1))
    vsl = lambda h: slice(GLA_VP * h, GLA_VP * (h + 1))
    qts, kts, kds, ebl = [], [], [], []
    for b in range(bt):
        z = jnp.dot(glr_ref[b].astype(bf16), wg_ref[...], preferred_element_type=f32) + bg_ref[...]
        log_a = jax.nn.log_sigmoid(z) / GLA_TAU
        bc = jnp.dot(trif, log_a, preferred_element_type=f32, precision=lax.Precision.HIGHEST)
        bl = bc[chunk - 1:chunk, :]
        k = k_ref[b]
        qts.append((q_ref[b] * (GLA_DK ** -0.5) * jnp.exp(bc)).astype(bf16))
        kts.append((k * jnp.exp(-bc)).astype(bf16))
        kds.append((k * jnp.exp(bl - bc)).astype(bf16))
        ebl.append(jnp.exp(bl))
    vbs = [v_ref[b].astype(bf16) for b in range(bt)]
    atts = [jnp.where(tri, lax.dot_general(qts[b][:, ksl(h)], kts[b][:, ksl(h)], _NT, preferred_element_type=f32),
                      0.0).astype(bf16) for b, h in problems]
    sts = [st_sc[b, h] for b, h in problems]
    outs = [lax.dot_general(qts[b][:, ksl(h)], st.astype(bf16), _NT, preferred_element_type=f32)
            + jnp.dot(att, vbs[b][:, vsl(h)], preferred_element_type=f32)
            for (b, h), st, att in zip(problems, sts, atts)]
    for (b, h), st in zip(problems, sts):
        st_sc[b, h] = ebl[b][:, ksl(h)] * st + lax.dot_general(vbs[b][:, vsl(h)], kds[b][:, ksl(h)], _TN,
                                                              preferred_element_type=f32)
    for (b, h), o in zip(problems, outs):
        ms = jnp.sum(o * o, axis=-1, keepdims=True) * (1.0 / GLA_DV)
        o = o * lax.rsqrt(ms + LN_EPS) * gn_ref[:, vsl(h)]
        o_ref[b, :, vsl(h)] = o * jax.nn.silu(r_ref[b, :, vsl(h)])

    @pl.when(c == pl.num_programs(1) - 1)
    def _():
        for b in range(bt):
            for h in range(GLA_HEADS):
                s_out_ref[b, h] = jnp.transpose(st_sc[b, h])[:GLA_DK, :GLA_DV]


def _gla(q, k, v, r, glr, wg, bg, gn, s0, *, bt):
    nb, t, _ = q.shape
    chunk = GLA_CHUNK if t % GLA_CHUNK == 0 else t
    zero_init = s0 is None
    if zero_init:
        s0 = jnp.zeros((1, 1, 8, LANES), f32)
    tok = lambda w: pl.BlockSpec((bt, chunk, w), lambda i, c: (i, c, 0))
    fix = lambda a: pl.BlockSpec(a.shape, lambda i, c: (0,) * a.ndim)
    state = pl.BlockSpec((bt, GLA_HEADS, GLA_DK, GLA_DV), lambda i, c: (i, 0, 0, 0))
    return pl.pallas_call(
        functools.partial(_gla_kernel, bt=bt, chunk=chunk, zero_init=zero_init),
        grid=(nb // bt, t // chunk),
        in_specs=[tok(GLA_HEADS * GLA_KP), tok(GLA_HEADS * GLA_KP), tok(GLA_HEADS * GLA_VP), tok(GLA_HEADS * GLA_VP),
                  tok(LANES), fix(wg), fix(bg), fix(gn), fix(s0) if zero_init else state],
        out_specs=[tok(GLA_HEADS * GLA_VP), state],
        out_shape=[jax.ShapeDtypeStruct((nb, t, GLA_HEADS * GLA_VP), f32),
                   jax.ShapeDtypeStruct((nb, GLA_HEADS, GLA_DK, GLA_DV), f32)],
        scratch_shapes=[pltpu.VMEM((bt, GLA_HEADS, GLA_VP, GLA_KP), f32)],
        compiler_params=_cparams(("parallel", "arbitrary")),
        name="gla",
    )(q, k, v, r, glr, wg, bg, gn, s0)


def _pad_heads(a, w, wp):
    lead = a.shape[:-1]
    a = jnp.pad(a.reshape(*lead, GLA_HEADS, w), [(0, 0)] * (len(lead) + 1) + [(0, wp - w)])
    return a.reshape(*lead, GLA_HEADS * wp)


def _softmax_rows(sm, maskf):
    m = jnp.max(sm, axis=-1, keepdims=True)
    e = jnp.exp(sm - m)
    return e * (1.0 / jnp.sum(e, axis=-1, keepdims=True)) * maskf


def _block_scores_t(imp):
    nc = imp.shape[1]
    jj = lax.broadcasted_iota(i32, (NSB_PAD, nc), 0)
    nn = lax.broadcasted_iota(i32, (NSB_PAD, nc), 1)
    mt = ((nn >> RATIO_SHIFT) == jj).astype(f32) + (((nn + 1) >> RATIO_SHIFT) == jj).astype(f32)
    return lax.dot_general(mt, imp, _NT, preferred_element_type=f32, precision=lax.Precision.HIGHEST)


def _select_t(blk_t, qpos_t):
    nq = blk_t.shape[1]
    j_t = lax.broadcasted_iota(i32, (NSB_PAD, nq), 0)
    valid = j_t * SLC_L <= qpos_t
    cur = qpos_t >> SLC_SHIFT
    forced = (j_t == 0) | (j_t == cur) | (j_t == cur - 1)
    score = jnp.where(valid, jnp.where(forced, 1e9, blk_t), -1e9)
    cnt = jnp.zeros((NSB_PAD, nq), i32)
    for k in range(NSB_PAD):
        row = score[k:k + 1, :]
        beats = (row > score) | ((row == score) & (j_t > k))
        cnt = cnt + beats.astype(i32)
    return ((cnt < SLC_K) & valid).astype(f32)


def _untranspose(sel_t):
    nq = sel_t.shape[1]
    eye = lax.broadcasted_iota(i32, (nq, nq), 0) == lax.broadcasted_iota(i32, (nq, nq), 1)
    return lax.dot_general(eye.astype(bf16), sel_t.astype(bf16), _NT, preferred_element_type=f32).astype(bf16)


def _nsa_prompt_kernel(qp_ref, gl_ref, bg_ref, kc_ref, vc_ref, ks_ref, vs_ref, kw_ref, vw_ref, o_ref,
                       p_sc, m_sc, l_sc, acc_sc, og_sc, used_sm):
    nc = kc_ref.shape[1]
    q0 = pl.program_id(1) * Q_BLK
    qpos_i = q0 + lax.broadcasted_iota(i32, (Q_BLK, 1), 0)
    lane = lax.broadcasted_iota(i32, (Q_BLK, LANES), 1)
    gates = jax.nn.sigmoid(gl_ref[0] + bg_ref[...])
    qp = qp_ref[0]
    zero_b = jnp.zeros((Q_BLK, LANES), bf16)

    for g in range(NSA_KV):
        half = (lane < HD) if g == 0 else (lane >= HD)
        qg = jnp.concatenate([jnp.where(half, qp[:, LANES * hh:LANES * (hh + 1)], zero_b)
                              for hh in range(NSA_HPG)], axis=0)
        slopes = [SLOPES[NSA_HPG * g + hh] for hh in range(NSA_HPG)]

        n_idx = lax.broadcasted_iota(i32, (Q_BLK, nc), 1)
        cd = (qpos_i - (n_idx * CMP_S + (CMP_L - 1))).astype(f32)
        cmask = cd >= 0
        cmaskf = cmask.astype(f32)
        s = lax.dot_general(qg, kc_ref[0], _NT, preferred_element_type=f32)
        imp = jnp.zeros((Q_BLK, nc), f32)
        for hh in range(NSA_HPG):
            rows = slice(hh * Q_BLK, (hh + 1) * Q_BLK)
            p = _softmax_rows(jnp.where(cmask, s[rows] - slopes[hh] * cd, NEG), cmaskf)
            imp = imp + p
            p_sc[rows, :nc] = p.astype(bf16)
        o_cmp = jnp.dot(p_sc[:, :nc], vc_ref[0], preferred_element_type=f32)

        qpos_t = q0 + lax.broadcasted_iota(i32, (NSB_PAD, Q_BLK), 1)
        sel_t = _select_t(_block_scores_t(imp), qpos_t)
        sel_b = _untranspose(sel_t)
        blocks_per_tile = SLC_TILE // SLC_L
        for t in range(NSB_PAD // blocks_per_tile):
            used_sm[t] = (jnp.max(sel_t[t * blocks_per_tile:(t + 1) * blocks_per_tile, :]) > 0.5).astype(i32)

        m_sc[...] = jnp.full(m_sc.shape, NEG, f32)
        l_sc[...] = jnp.zeros(l_sc.shape, f32)
        acc_sc[...] = jnp.zeros(acc_sc.shape, f32)

        def tile(t, carry):
            @pl.when(used_sm[t] > 0)
            def _():
                k0 = pl.multiple_of(t * SLC_TILE, SLC_TILE)
                kt = ks_ref[0, pl.ds(k0, SLC_TILE), :]
                vt = vs_ref[0, pl.ds(k0, SLC_TILE), :]
                st = lax.dot_general(qg, kt, _NT, preferred_element_type=f32)
                kpos = k0 + lax.broadcasted_iota(i32, (Q_BLK, SLC_TILE), 1)
                blk_of_key = (k0 + lax.broadcasted_iota(i32, (NSB_PAD, SLC_TILE), 1)) >> SLC_SHIFT
                expand = (lax.broadcasted_iota(i32, (NSB_PAD, SLC_TILE), 0) == blk_of_key).astype(bf16)
                selexp = jnp.dot(sel_b, expand, preferred_element_type=f32)
                addmask = jnp.where((qpos_i >= kpos) & (selexp > 0.5), 0.0, NEG)
                krel = (k0 - q0 + lax.broadcasted_iota(i32, (1, SLC_TILE), 1)).astype(f32)
                for hh in range(NSA_HPG):
                    rows = slice(hh * Q_BLK, (hh + 1) * Q_BLK)
                    sm = st[rows] + slopes[hh] * krel + addmask
                    m_old = m_sc[rows]
                    m_new = jnp.maximum(m_old, jnp.max(sm, axis=-1, keepdims=True))
                    a = jnp.exp(m_old - m_new)
                    e = jnp.exp(sm - m_new)
                    l_sc[rows] = a * l_sc[rows] + jnp.sum(e, axis=-1, keepdims=True)
                    m_sc[rows] = m_new
                    acc_sc[rows] = a * acc_sc[rows]
                    p_sc[rows, :SLC_TILE] = e.astype(bf16)
                acc_sc[...] += jnp.dot(p_sc[:, :SLC_TILE], vt, preferred_element_type=f32)
            return carry

        lax.fori_loop(0, (q0 + Q_BLK + SLC_TILE - 1) // SLC_TILE, tile, 0)
        o_slc = acc_sc[...] / l_sc[...]

        start = pl.multiple_of(jnp.maximum(q0 - WINDOW, 0), Q_BLK)
        kt = kw_ref[0, pl.ds(start, WIN_KEYS), :]
        vt = vw_ref[0, pl.ds(start, WIN_KEYS), :]
        st = lax.dot_general(qg, kt, _NT, preferred_element_type=f32)
        wd = (qpos_i - (start + lax.broadcasted_iota(i32, (Q_BLK, WIN_KEYS), 1))).astype(f32)
        wmask = (wd >= 0) & (wd <= WINDOW)
        wmaskf = wmask.astype(f32)
        for hh in range(NSA_HPG):
            rows = slice(hh * Q_BLK, (hh + 1) * Q_BLK)
            p = _softmax_rows(jnp.where(wmask, st[rows] - slopes[hh] * wd, NEG), wmaskf)
            p_sc[rows, :WIN_KEYS] = p.astype(bf16)
        o_win = jnp.dot(p_sc[:, :WIN_KEYS], vt, preferred_element_type=f32)

        for hh in range(NSA_HPG):
            rows = slice(hh * Q_BLK, (hh + 1) * Q_BLK)
            c = 3 * (NSA_HPG * g + hh)
            og_sc[g, rows] = (gates[:, c:c + 1] * o_cmp[rows] + gates[:, c + 1:c + 2] * o_slc[rows]
                              + gates[:, c + 2:c + 3] * o_win[rows])

    for hh in range(NSA_HPG):
        rows = slice(hh * Q_BLK, (hh + 1) * Q_BLK)
        o_ref[0, :, LANES * hh:LANES * (hh + 1)] = jnp.where(lane < HD, og_sc[0, rows], og_sc[1, rows])


def _nsa_prompt_call(qp, gl, bg, kc, vc, ks, vs, kw, vw):
    b, t, w = qp.shape
    nc = kc.shape[1]
    assert t % SLC_TILE == 0 and t >= WIN_KEYS and nc % LANES == 0 and t // SLC_L <= NSB_PAD
    rows = NSA_HPG * Q_BLK
    qblk = lambda bb, i: (bb, i, 0)
    whole = lambda bb, i: (bb, 0, 0)
    return pl.pallas_call(
        _nsa_prompt_kernel,
        grid=(b, t // Q_BLK),
        in_specs=[pl.BlockSpec((1, Q_BLK, w), qblk), pl.BlockSpec((1, Q_BLK, LANES), qblk),
                  pl.BlockSpec((1, LANES), lambda bb, i: (0, 0)),
                  pl.BlockSpec((1, nc, LANES), whole), pl.BlockSpec((1, nc, LANES), whole),
                  pl.BlockSpec((1, t, LANES), whole), pl.BlockSpec((1, t, LANES), whole),
                  pl.BlockSpec((1, t, LANES), whole), pl.BlockSpec((1, t, LANES), whole)],
        out_specs=pl.BlockSpec((1, Q_BLK, w), qblk),
        out_shape=jax.ShapeDtypeStruct((b, t, w), f32),
        scratch_shapes=[pltpu.VMEM((rows, WIN_KEYS), bf16), pltpu.VMEM((rows, 1), f32), pltpu.VMEM((rows, 1), f32),
                        pltpu.VMEM((rows, LANES), f32), pltpu.VMEM((NSA_KV, rows, LANES), f32),
                        pltpu.SMEM((NSB_PAD * SLC_L // SLC_TILE,), i32)],
        compiler_params=_cparams(("parallel", "parallel")),
        name="nsa_prompt",
    )(qp, gl, bg, kc, vc, ks, vs, kw, vw)


def _nsa_sample_kernel(pt_ref, qp_ref, gl_ref, bg_ref, kc_ref, vc_ref, kn_ref, vn_ref, wst_ref, wkn_ref, wvn_ref,
                       *rest, n_pages, past_len, bt):
    page_refs = rest[:n_pages * bt]
    o_ref, k_sc, v_sc = rest[n_pages * bt:]
    nq = qp_ref.shape[1]
    nc = kc_ref.shape[1]
    wb = wst_ref.shape[1]
    nkeys = (n_pages + 1) * PAGE_SIZE
    rows_g = NSA_HPG * nq
    qpos_i = past_len + lax.broadcasted_iota(i32, (nq, 1), 0)
    lane = lax.broadcasted_iota(i32, (nq, LANES), 1)
    zero_b = jnp.zeros((nq, LANES), bf16)
    tail = jnp.zeros((PAGE_SIZE - nq, LANES), bf16)

    def rows_of(g, hh):
        r0 = g * rows_g + hh * nq
        return slice(r0, r0 + nq)

    n_idx = lax.broadcasted_iota(i32, (nq, nc), 1)
    cd = (qpos_i - (n_idx * CMP_S + (CMP_L - 1))).astype(f32)
    cmask = cd >= 0
    cmaskf = cmask.astype(f32)
    qpos_t = past_len + lax.broadcasted_iota(i32, (NSB_PAD, nq), 1)
    blk_of_key = lax.broadcasted_iota(i32, (NSB_PAD, nkeys), 1) >> SLC_SHIFT
    expand = (lax.broadcasted_iota(i32, (NSB_PAD, nkeys), 0) == blk_of_key).astype(bf16)
    sd = (qpos_i - lax.broadcasted_iota(i32, (nq, nkeys), 1)).astype(f32)
    wpos = (past_len - wb) + lax.broadcasted_iota(i32, (nq, wb + PAGE_SIZE), 1)
    wd = (qpos_i - wpos).astype(f32)
    wmask = (wd >= 0) & (wd <= WINDOW) & (wpos >= 0)
    wmaskf = wmask.astype(f32)

    def sequence(j):
        gates = jax.nn.sigmoid(gl_ref[j] + bg_ref[...])
        qp = qp_ref[j]
        qs = jnp.concatenate([jnp.where((lane < HD) if g == 0 else (lane >= HD),
                                        qp[:, LANES * hh:LANES * (hh + 1)], zero_b)
                              for g in range(NSA_KV) for hh in range(NSA_HPG)], axis=0)

        s = lax.dot_general(qs, kc_ref[j], _NT, preferred_element_type=f32)
        yield
        imps, ps = [], []
        for g in range(NSA_KV):
            imp = jnp.zeros((nq, nc), f32)
            for hh in range(NSA_HPG):
                p = _softmax_rows(jnp.where(cmask, s[rows_of(g, hh)] - SLOPES[NSA_HPG * g + hh] * cd, NEG), cmaskf)
                imp = imp + p
                ps.append(p)
            imps.append(imp)
        o_cmp = jnp.dot(jnp.concatenate(ps, axis=0).astype(bf16), vc_ref[j], preferred_element_type=f32)
        yield

        masks = []
        for g in range(NSA_KV):
            sel_b = _untranspose(_select_t(_block_scores_t(imps[g]), qpos_t))
            selexp = jnp.dot(sel_b, expand, preferred_element_type=f32)
            masks.append((sd >= 0) & (selexp > 0.5))
        yield

        for pg in range(n_pages):
            blk = page_refs[j * n_pages + pg][0]
            k_sc[j, :, pg * PAGE_SIZE:(pg + 1) * PAGE_SIZE] = blk[:LANES].astype(bf16)
            v_sc[j, :, pg * PAGE_SIZE:(pg + 1) * PAGE_SIZE] = blk[LANES:].astype(bf16)
        new_t = lambda ref: jnp.transpose(jnp.concatenate([ref[j], tail], axis=0).astype(f32)).astype(bf16)
        k_sc[j, :, n_pages * PAGE_SIZE:] = new_t(kn_ref)
        v_sc[j, :, n_pages * PAGE_SIZE:] = new_t(vn_ref)
        s = jnp.dot(qs, k_sc[j], preferred_element_type=f32)
        yield
        ps = []
        for g in range(NSA_KV):
            mf = masks[g].astype(f32)
            for hh in range(NSA_HPG):
                ps.append(_softmax_rows(jnp.where(masks[g], s[rows_of(g, hh)] - SLOPES[NSA_HPG * g + hh] * sd, NEG),
                                        mf))
        o_slc = lax.dot_general(jnp.concatenate(ps, axis=0).astype(bf16), v_sc[j], _NT,
                                preferred_element_type=f32)
        yield

        wk = jnp.concatenate([wst_ref[j, :, :LANES].astype(bf16), wkn_ref[j], tail], axis=0)
        wv = jnp.concatenate([wst_ref[j, :, LANES:].astype(bf16), wvn_ref[j], tail], axis=0)
        s = lax.dot_general(qs, wk, _NT, preferred_element_type=f32)
        yield
        ps = []
        for g in range(NSA_KV):
            for hh in range(NSA_HPG):
                ps.append(_softmax_rows(jnp.where(wmask, s[rows_of(g, hh)] - SLOPES[NSA_HPG * g + hh] * wd, NEG),
                                        wmaskf))
        o_win = jnp.dot(jnp.concatenate(ps, axis=0).astype(bf16), wv, preferred_element_type=f32)
        yield

        for hh in range(NSA_HPG):
            halves = []
            for g in range(NSA_KV):
                r = rows_of(g, hh)
                c = 3 * (NSA_HPG * g + hh)
                halves.append(gates[:, c:c + 1] * o_cmp[r] + gates[:, c + 1:c + 2] * o_slc[r]
                              + gates[:, c + 2:c + 3] * o_win[r])
            o_ref[j, :, LANES * hh:LANES * (hh + 1)] = jnp.where(lane < HD, halves[0], halves[1])

    running = [sequence(j) for j in range(bt)]
    done = object()
    while running:
        running = [seq for seq in running if next(seq, done) is not done]


def _nsa_sample_call(page_table, qp, gl, bg, kc, vc, kn, vn, wst, wkn, wvn, cache):
    b, nq, w = qp.shape
    n_pages = page_table.shape[1]
    nc = kc.shape[1]
    wb = wst.shape[1]
    nkeys = (n_pages + 1) * PAGE_SIZE
    assert nkeys // SLC_L <= NSB_PAD and nq <= PAGE_SIZE
    bt = NSA_SAMPLE_BT if b % NSA_SAMPLE_BT == 0 else 1
    per_b = lambda bb, pt: (bb, 0, 0)
    in_specs = [pl.BlockSpec((bt, nq, w), per_b), pl.BlockSpec((bt, nq, LANES), per_b),
                pl.BlockSpec((1, LANES), lambda bb, pt: (0, 0)),
                pl.BlockSpec((bt, nc, LANES), per_b), pl.BlockSpec((bt, nc, LANES), per_b),
                pl.BlockSpec((bt, nq, LANES), per_b), pl.BlockSpec((bt, nq, LANES), per_b),
                pl.BlockSpec((bt, wb, 2 * LANES), per_b),
                pl.BlockSpec((bt, nq, LANES), per_b), pl.BlockSpec((bt, nq, LANES), per_b)]
    for j in range(bt):
        for pg in range(n_pages):
            in_specs.append(pl.BlockSpec(
                (1, 2 * LANES, PAGE_SIZE),
                functools.partial(lambda bb, pt, j, pg: (pt[bb * bt + j, pg], 1, 0), j=j, pg=pg)))
    grid_spec = pltpu.PrefetchScalarGridSpec(
        num_scalar_prefetch=1, grid=(b // bt,), in_specs=in_specs,
        out_specs=pl.BlockSpec((bt, nq, w), per_b),
        scratch_shapes=[pltpu.VMEM((bt, LANES, nkeys), bf16), pltpu.VMEM((bt, LANES, nkeys), bf16)])
    return pl.pallas_call(
        functools.partial(_nsa_sample_kernel, n_pages=n_pages, past_len=n_pages * PAGE_SIZE, bt=bt),
        grid_spec=grid_spec,
        out_shape=jax.ShapeDtypeStruct((b, nq, w), f32),
        compiler_params=_cparams(("parallel",)),
        name="nsa_sample",
    )(page_table, qp, gl, bg, kc, vc, kn, vn, wst, wkn, wvn, *([cache] * (n_pages * bt)))


def _compress_kernel(x_ref, pos_ref, w1_ref, b1_ref, w2_ref, o_ref):
    x = x_ref[0, 0, 0]
    half = CMP_S * HD
    xa = (x + pos_ref[0, :, :half]).astype(bf16)
    xb = (x + pos_ref[0, :, half:]).astype(bf16)
    a = jnp.dot(xa, w1_ref[0, :half, :], preferred_element_type=f32)
    bsec = jnp.dot(xb, w1_ref[0, half:, :], preferred_element_type=f32)
    nch = x.shape[0]
    hid = jax.nn.gelu(a + pltpu.roll(bsec, nch - 1, 0) + b1_ref[0])
    o_ref[0, 0, 0] = jnp.dot(hid.astype(bf16), w2_ref[0], preferred_element_type=f32).astype(o_ref.dtype)


def _compress_call(x2, cmp_pos, w_cmp1, b_cmp1, w_cmp2):
    b, _, g, nch, cw = x2.shape
    hid = w_cmp1.shape[-1]
    pos = cmp_pos.reshape(2, 1, CMP_L * HD)
    return pl.pallas_call(
        _compress_kernel,
        grid=(b, 2, g),
        in_specs=[pl.BlockSpec((1, 1, 1, nch, cw), lambda bb, s, gg: (bb, s, gg, 0, 0)),
                  pl.BlockSpec((1, 1, CMP_L * HD), lambda bb, s, gg: (s, 0, 0)),
                  pl.BlockSpec((1, CMP_L * HD, hid), lambda bb, s, gg: (s, 0, 0)),
                  pl.BlockSpec((1, 1, hid), lambda bb, s, gg: (s, 0, 0)),
                  pl.BlockSpec((1, hid, HD), lambda bb, s, gg: (s, 0, 0))],
        out_specs=pl.BlockSpec((1, 1, 1, nch, HD), lambda bb, s, gg: (bb, s, gg, 0, 0)),
        out_shape=jax.ShapeDtypeStruct((b, 2, g, nch, HD), bf16),
        compiler_params=_cparams(("parallel", "parallel", "parallel")),
        name="compress",
    )(x2, pos, w_cmp1.astype(bf16), b_cmp1.reshape(2, 1, hid), w_cmp2.astype(bf16))


def _compressed_kv(tok, cmp_pos, w_cmp1, b_cmp1, w_cmp2):
    b, t = tok.shape[:2]
    nch = t // CMP_S
    x2 = jnp.transpose(tok[:, :nch * CMP_S], (0, 2, 3, 1, 4)).reshape(b, 2, NSA_KV, nch, CMP_S * HD)
    ck = _compress_call(x2, cmp_pos, w_cmp1, b_cmp1, w_cmp2)
    ck = jnp.transpose(ck, (0, 1, 3, 2, 4)).reshape(b, 2, nch, NSA_KV * HD)
    return ck[:, 0], ck[:, 1]


def _compress_paged_kernel(pt_ref, pos_ref, w1a_ref, w1b_ref, b1_ref, w2_ref, *rest, n_pages):
    page_refs = rest[:n_pages]
    o_ref, x_sc, xcat_sc = rest[n_pages:]
    nch = n_pages * (PAGE_SIZE // CMP_S)
    role = NSA_KV * HD
    for pg in range(n_pages):
        xt = jnp.transpose(page_refs[pg][0])
        for s in range(2):
            x_sc[s, pg * PAGE_SIZE:(pg + 1) * PAGE_SIZE, :] = xt[:, s * role:(s + 1) * role]
    for s in range(2):
        for t in range(CMP_S):
            xcat_sc[:, t * role:(t + 1) * role] = x_sc[s, pl.ds(t, nch, stride=CMP_S), :]
        xc = xcat_sc[...]
        xa = (xc + pos_ref[s, 0:1, :]).astype(bf16)
        xb = (xc + pos_ref[s, 1:2, :]).astype(bf16)
        a = jnp.dot(xa, w1a_ref[s], preferred_element_type=f32)
        bsec = jnp.dot(xb, w1b_ref[s], preferred_element_type=f32)
        hid = jax.nn.gelu(a + pltpu.roll(bsec, nch - 1, 0) + b1_ref[s])
        o_ref[0, s] = jnp.dot(hid.astype(bf16), w2_ref[s], preferred_element_type=f32).astype(o_ref.dtype)


def _compress_paged(page_table, cache_t, cmp_pos, w_cmp1, b_cmp1, w_cmp2):
    b, n_pages = page_table.shape
    nch = n_pages * (PAGE_SIZE // CMP_S)
    hid = w_cmp1.shape[-1]
    half = CMP_S * HD
    eye = jnp.eye(NSA_KV, dtype=f32)

    def widen_rows(w):
        w = w.reshape(2, CMP_S, HD, hid)
        return jnp.einsum('stdh,ag->stadgh', w, eye).reshape(2, CMP_S * NSA_KV * HD, NSA_KV * hid)

    w1a = widen_rows(w_cmp1[:, :half]).astype(bf16)
    w1b = widen_rows(w_cmp1[:, half:]).astype(bf16)
    w2 = jnp.einsum('shd,ag->sahgd', w_cmp2, eye).reshape(2, NSA_KV * hid, NSA_KV * HD).astype(bf16)
    b1 = jnp.tile(b_cmp1, (1, NSA_KV)).reshape(2, 1, NSA_KV * hid)
    pos = jnp.broadcast_to(cmp_pos.reshape(2, 2, CMP_S, 1, HD), (2, 2, CMP_S, NSA_KV, HD))
    pos = pos.reshape(2, 2, CMP_S * NSA_KV * HD)
    fix = lambda a: pl.BlockSpec(a.shape, lambda bb, pt: (0,) * a.ndim)
    in_specs = [fix(pos), fix(w1a), fix(w1b), fix(b1), fix(w2)]
    for pg in range(n_pages):
        in_specs.append(pl.BlockSpec((1, 2 * NSA_KV * HD, PAGE_SIZE),
                                     functools.partial(lambda bb, pt, pg: (pt[bb, pg], 0, 0), pg=pg)))
    grid_spec = pltpu.PrefetchScalarGridSpec(
        num_scalar_prefetch=1, grid=(b,), in_specs=in_specs,
        out_specs=pl.BlockSpec((1, 2, nch, NSA_KV * HD), lambda bb, pt: (bb, 0, 0, 0)),
        scratch_shapes=[pltpu.VMEM((2, n_pages * PAGE_SIZE, NSA_KV * HD), f32),
                        pltpu.VMEM((nch, CMP_S * NSA_KV * HD), f32)])
    return pl.pallas_call(
        functools.partial(_compress_paged_kernel, n_pages=n_pages),
        grid_spec=grid_spec,
        out_shape=jax.ShapeDtypeStruct((b, 2, nch, NSA_KV * HD), bf16),
        compiler_params=_cparams(("parallel",)),
        name="compress_paged",
    )(page_table, pos, w1a, w1b, b1, w2, *([cache_t] * n_pages))


def _pair_pack_cols(w):
    k = w.shape[0]
    return jnp.transpose(w.reshape(k, NSA_KV, NSA_HPG, HD), (0, 2, 1, 3)).reshape(k, NSA_W)


def _last_rows(a, n):
    t = a.shape[1]
    if t >= n:
        return a[:, t - n:]
    return jnp.pad(a, ((0, 0), (n - t, 0)) + ((0, 0),) * (a.ndim - 2))


def kernel(x_prompt, x_sample, mem_prompt, state_gla, cache_nsa_kv, state_win_kv, cache_mem_kv, page_table,
           w_in_a, w_gate_a, b_gate_a, gla_norm, w_in_b, b_gate_b, w_kv_b, cmp_pos, w_cmp1, b_cmp1, w_cmp2,
           w_mem_kv, w_out, ln1_g, ln1_b, ln2_g, ln2_b, w_router, b_router, w_e1, b_e1, w_e2, b_e2):
    bp, tp, d = x_prompt.shape
    bs, ts, _ = x_sample.shape
    n_p, n_s = bp * tp, bs * ts
    wb = min(WINDOW, state_win_kv.shape[1])
    n_pool = cache_nsa_kv.shape[0]
    past_len = page_table.shape[1] * PAGE_SIZE

    mem_rows = mem_prompt.reshape(bp * MEM_TOK, d)
    w_mem = [w_mem_kv[l].astype(bf16) for l in range(DEPTH)]
    mem_kv_l = _proj(mem_rows, w_mem, _whole(w_mem))
    mem_kv_prompt = jnp.stack(mem_kv_l).reshape(DEPTH, bp, MEM_TOK, 2, MEM_HEADS, MEM_HD)
    mem_p = [m.reshape(bp, MEM_TOK, 2 * MEM_W) for m in mem_kv_l]
    mem_s = jnp.transpose(cache_mem_kv, (0, 1, 3, 4, 5, 2)).reshape(DEPTH, bs, 2 * MEM_W, MEM_TOK)

    x_p, x_s = x_prompt.reshape(n_p, d), x_sample.reshape(n_s, d)
    as_p = lambda a: a.reshape(bp, tp, a.shape[-1])
    as_s = lambda a: a.reshape(bs, ts, a.shape[-1])

    hk = GLA_HEADS * GLA_DK
    cuts_a = [0, hk, 2 * hk, 2 * hk + GLA_W, 2 * hk + 2 * GLA_W, 2 * hk + 2 * GLA_W + GLA_RANK,
              2 * hk + 2 * GLA_W + GLA_RANK + MEM_W]
    mem_scale = MEM_HD ** -0.5
    q_scale = HD ** -0.5
    xb = jnp.zeros((_moe_blocks((n_p + n_s) * TOP_K) * MOE_ROWS, d), f32)

    gla_p, gla_s = [], []
    for l in range(DEPTH):
        if l < N_A:
            wq, wk, wv, wr, wgl, wm = [w_in_a[l][:, cuts_a[i]:cuts_a[i + 1]] for i in range(6)]
            ws = [_pad_heads(wq, GLA_DK, GLA_KP), _pad_heads(wk, GLA_DK, GLA_KP), _pad_heads(wv, GLA_DV, GLA_VP),
                  _pad_heads(wr, GLA_DV, GLA_VP), jnp.pad(wgl, ((0, 0), (0, LANES - GLA_RANK))), wm * mem_scale]
            ws = [w.astype(bf16) for w in ws]
            outs = _whole(ws)
            outs[5] = outs[5][:3] + (bf16,)
            wg = _pad_heads(jnp.pad(w_gate_a[l], ((0, LANES - GLA_RANK), (0, 0))), GLA_DK, GLA_KP).astype(bf16)
            bg = _pad_heads(b_gate_a[l], GLA_DK, GLA_KP).reshape(1, GLA_HEADS * GLA_KP)
            gn = _pad_heads(gla_norm[l], GLA_DV, GLA_VP).reshape(1, GLA_HEADS * GLA_VP)
            q, k, v, r, glr, mq_p = _proj(x_p, ws, outs)
            o_p, s_new = _gla(as_p(q), as_p(k), as_p(v), as_p(r), as_p(glr), wg, bg, gn, None, bt=min(bp, GLA_BT))
            gla_p.append(s_new)
            q, k, v, r, glr, mq_s = _proj(x_s, ws, outs)
            o_s, s_new = _gla(as_s(q), as_s(k), as_s(v), as_s(r), as_s(glr), wg, bg, gn, state_gla[l],
                              bt=min(bs, GLA_BT))
            gla_s.append(s_new)
            w_o = _pad_heads(w_out[l][:GLA_W].T, GLA_DV, GLA_VP).T
        else:
            j = l - N_A
            if l == N_A:
                role = NSA_KV * HD
                kv_outs = [(0, 0, 4 * role, f32), (0, 4 * role, 6 * role, f32)]
                kv_outs += [(0, r * role, (r + 1) * role, bf16) for r in range(2, 6)]
                wkv = [w_kv_b.astype(bf16)]
                rows_p, win_p, ks_p, vs_p, kw_p, vw_p = [as_p(a) for a in _proj(x_p, wkv, kv_outs)]
                rows_s, win_s, ks_s, vs_s, kw_s, vw_s = [as_s(a) for a in _proj(x_s, wkv, kv_outs)]
                kc_p, vc_p = _compressed_kv(rows_p.reshape(bp, tp, 4, NSA_KV, HD)[:, :, :2],
                                            cmp_pos, w_cmp1, b_cmp1, w_cmp2)
                assert (past_len + ts) // CMP_S == past_len // CMP_S
                cache_t = jnp.transpose(cache_nsa_kv, (0, 2, 3, 4, 1)).reshape(n_pool, 4 * role, PAGE_SIZE)
                ck_s = _compress_paged(page_table, cache_t, cmp_pos, w_cmp1, b_cmp1, w_cmp2)
                kc_s, vc_s = ck_s[:, 0], ck_s[:, 1]
                wst = state_win_kv.reshape(bs, state_win_kv.shape[1], 2 * role)
                wkv_s = jnp.concatenate([state_win_kv, win_s.reshape(bs, ts, 2, NSA_KV, HD)], axis=1)
                nsa_out = (rows_p.reshape(bp, tp, 4, NSA_KV, HD), _last_rows(win_p.reshape(bp, tp, 2, NSA_KV, HD), wb),
                           rows_s.reshape(bs, ts, 4, NSA_KV, HD), wkv_s[:, -wb:])
            wq = _pair_pack_cols(w_in_b[j][:, :NSA_W]) * q_scale
            wg = jnp.pad(w_in_b[j][:, NSA_W:NSA_W + 3 * NSA_HEADS], ((0, 0), (0, LANES - 3 * NSA_HEADS)))
            wm = w_in_b[j][:, NSA_W + 3 * NSA_HEADS:] * mem_scale
            ws = [wq.astype(bf16), wg.astype(bf16), wm.astype(bf16)]
            outs = [(0, 0, NSA_W, bf16), (1, 0, LANES, f32), (2, 0, MEM_W, bf16)]
            bg = jnp.pad(b_gate_b[j], (0, LANES - 3 * NSA_HEADS)).reshape(1, LANES)
            qp, gl, mq_p = _proj(x_p, ws, outs)
            o_p = _nsa_prompt_call(as_p(qp), as_p(gl), bg, kc_p, vc_p, ks_p, vs_p, kw_p, vw_p)
            qp, gl, mq_s = _proj(x_s, ws, outs)
            o_s = _nsa_sample_call(page_table, as_s(qp), as_s(gl), bg, kc_s, vc_s, ks_s, vs_s, wst, kw_s, vw_s,
                                   cache_t)
            w_o = _pair_pack_cols(w_out[l][:NSA_W].T).T
        om_p = _mem_attn(as_p(mq_p), mem_p[l])
        om_s = _mem_attn(as_s(mq_s), mem_s, layer=l)
        w_o, w_om = w_o.astype(bf16), w_out[l][-MEM_W:].astype(bf16)
        flat = lambda a: a.reshape(-1, a.shape[-1])
        x1_p, gate_p, idx_p = _mix_ln([(flat(o_p), w_o), (flat(om_p), w_om)], x_p, ln1_g[l], ln1_b[l],
                                      w_router[l], b_router[l])
        x1_s, gate_s, idx_s = _mix_ln([(flat(o_s), w_o), (flat(om_s), w_om)], x_s, ln1_g[l], ln1_b[l],
                                      w_router[l], b_router[l])
        dest, blk_exp, nreal = _moe_slots(jnp.concatenate([idx_p.reshape(-1), idx_s.reshape(-1)]))
        dest_p, dest_s = dest[:n_p * TOP_K], dest[n_p * TOP_K:]
        xb = _moe_dispatch(x1_s, dest_s, _moe_dispatch(x1_p, dest_p, xb))
        yb = _moe_ffn_blocks(xb, blk_exp, nreal, w_e1, b_e1, w_e2, b_e2, l)
        x_p = _moe_combine(yb, dest_p, gate_p, x1_p, ln2_g[l], ln2_b[l])
        x_s = _moe_combine(yb, dest_s, gate_s, x1_s, ln2_g[l], ln2_b[l])

    rows_p, win_p_out, rows_s, win_s_out = nsa_out
    return (x_p.reshape(bp, tp, d), x_s.reshape(bs, ts, d),
            jnp.stack(gla_p), jnp.stack(gla_s), rows_p, rows_s, win_p_out, win_s_out, mem_kv_prompt)
```

```python
import functools
import math

import jax
import jax.numpy as jnp
from jax import lax
from jax.experimental import pallas as pl
from jax.experimental.pallas import tpu as pltpu

D_MODEL = 1024
DEPTH = 4
PAGE_SIZE = 128
N_A = DEPTH // 2
GLA_HEADS = 4
GLA_DV = (3 * D_MODEL) // (4 * GLA_HEADS)
GLA_DK = GLA_DV // 2
GLA_RANK = 16
GLA_TAU = 16.0
GLA_CHUNK = 64
HD = 64
NSA_HEADS = (3 * D_MODEL) // (4 * HD)
NSA_KV = 2
NSA_HPG = NSA_HEADS // NSA_KV
CMP_S = 16
CMP_L = 2 * CMP_S
SLC_L = 64
SLC_K = 16
WINDOW = 512
Q_BLK = 128
MEM_TOK = 256
MEM_HEADS = 4
MEM_HD = D_MODEL // (4 * MEM_HEADS)
N_EXP = 32
TOP_K = 4
D_FF = D_MODEL
SWIGLU_LIMIT = 7.0
SWIGLU_ALPHA = 1.702
GLA_W = GLA_HEADS * GLA_DV
NSA_W = NSA_HEADS * HD
MEM_W = MEM_HEADS * MEM_HD
DN_ALPHA = (2 * DEPTH) ** 0.25
LN_EPS = 1e-5
NEG = -1e30

f32, bf16, i32 = jnp.float32, jnp.bfloat16, jnp.int32

VMEM_LIMIT_BYTES = 56 * 1024 * 1024
LANES = 128
ROW_TILE = 512
MOE_ROWS = 512
FF_CHUNK = 512
SLC_TILE = 512
WIN_KEYS = WINDOW + Q_BLK
NSB_PAD = 64
TOK_TILE = 512
DMA_UNROLL = 4
RANK_TILE = 1024
MEM_ROWS = 64
NSA_SAMPLE_BT = 4
GLA_KP = 128
GLA_VP = 256
GLA_BT = 8
_NT = (((1,), (1,)), ((), ()))
_TN = (((0,), (0,)), ((), ()))


def _cparams(sem):
    return pltpu.CompilerParams(dimension_semantics=sem, vmem_limit_bytes=VMEM_LIMIT_BYTES)


def _alibi_slopes(n):
    def pow2(m):
        start = 2.0 ** (-8.0 / m)
        return [start ** (i + 1) for i in range(m)]
    if math.log2(n).is_integer():
        return pow2(n)
    c = 2 ** math.floor(math.log2(n))
    return pow2(c) + pow2(2 * c)[0::2][: n - c]


SLOPES = _alibi_slopes(NSA_HEADS)
SLC_SHIFT = int(math.log2(SLC_L))
RATIO_SHIFT = int(math.log2(SLC_L // CMP_S))


def _proj_kernel(x_ref, *refs, n_w, outs):
    xb = x_ref[...].astype(bf16)
    res = {}
    for o_ref, (wi, lo, hi) in zip(refs[n_w:], outs):
        if wi not in res:
            res[wi] = jnp.dot(xb, refs[wi][...], preferred_element_type=f32)
        o_ref[...] = res[wi][:, lo:hi].astype(o_ref.dtype)


def _proj(x, ws, outs, tm=ROW_TILE):
    m, k = x.shape
    tm = min(tm, m)
    in_specs = [pl.BlockSpec((tm, k), lambda i: (i, 0))]
    in_specs += [pl.BlockSpec(w.shape, lambda i: (0, 0)) for w in ws]
    out_specs = [pl.BlockSpec((tm, hi - lo), lambda i: (i, 0)) for _, lo, hi, _ in outs]
    out_shape = [jax.ShapeDtypeStruct((m, hi - lo), dt) for _, lo, hi, dt in outs]
    return pl.pallas_call(
        functools.partial(_proj_kernel, n_w=len(ws), outs=tuple(o[:3] for o in outs)),
        grid=(m // tm,),
        in_specs=in_specs,
        out_specs=out_specs,
        out_shape=out_shape,
        compiler_params=_cparams(("parallel",)),
        name="proj",
    )(x, *ws)


def _whole(ws, dtype=f32):
    return [(i, 0, w.shape[1], dtype) for i, w in enumerate(ws)]


def _layer_norm_rows(y, g, b):
    mu = jnp.mean(y, axis=-1, keepdims=True)
    d = y - mu
    var = jnp.mean(d * d, axis=-1, keepdims=True)
    return d * lax.rsqrt(var + LN_EPS) * g + b


def _mix_ln_kernel(*refs, n_parts):
    a_refs, w_refs = refs[:n_parts], refs[n_parts:2 * n_parts]
    x_ref, g_ref, b_ref, wr_ref, br_ref, x1_ref, gate_ref, idx_ref = refs[2 * n_parts:]
    tm = x_ref.shape[0]
    sub = tm // 2 if tm % 16 == 0 else tm
    for r0 in range(0, tm, sub):
        rows = slice(r0, r0 + sub)
        mix = jnp.dot(a_refs[0][rows, :].astype(bf16), w_refs[0][...], preferred_element_type=f32)
        for a_ref, w_ref in zip(a_refs[1:], w_refs[1:]):
            mix = mix + jnp.dot(a_ref[rows, :].astype(bf16), w_ref[...], preferred_element_type=f32)
        x1 = _layer_norm_rows(DN_ALPHA * x_ref[rows, :] + mix, g_ref[...], b_ref[...])
        x1_ref[rows, :] = x1
        lg = jnp.dot(x1, wr_ref[...], preferred_element_type=f32, precision=lax.Precision.HIGHEST) + br_ref[...]
        lane = lax.broadcasted_iota(i32, lg.shape, 1)
        vals, idxs = [], []
        for _ in range(TOP_K):
            m = jnp.max(lg, axis=-1, keepdims=True)
            i = jnp.min(jnp.where(lg == m, lane, N_EXP), axis=-1, keepdims=True)
            vals.append(m)
            idxs.append(i)
            lg = jnp.where(lane == i, -jnp.inf, lg)
        es = [jnp.exp(v - vals[0]) for v in vals]
        tot = es[0] + es[1] + es[2] + es[3]
        col = lax.broadcasted_iota(i32, (sub, TOP_K), 1)
        gate = jnp.zeros((sub, TOP_K), f32)
        idx = jnp.zeros((sub, TOP_K), i32)
        for k in range(TOP_K):
            gate = jnp.where(col == k, es[k] / tot, gate)
            idx = jnp.where(col == k, idxs[k], idx)
        gate_ref[rows, :] = gate
        idx_ref[rows, :] = idx


def _mix_ln(parts, x, g, b, w_r, b_r, tm=ROW_TILE):
    m, d = x.shape
    row = lambda i: (i, 0)
    fix = lambda i: (0, 0)
    in_specs = [pl.BlockSpec((tm, a.shape[1]), row) for a, _ in parts]
    in_specs += [pl.BlockSpec(w.shape, fix) for _, w in parts]
    in_specs += [pl.BlockSpec((tm, d), row), pl.BlockSpec((1, d), fix), pl.BlockSpec((1, d), fix),
                 pl.BlockSpec((d, N_EXP), fix), pl.BlockSpec((1, N_EXP), fix)]
    return pl.pallas_call(
        functools.partial(_mix_ln_kernel, n_parts=len(parts)),
        grid=(m // tm,),
        in_specs=in_specs,
        out_specs=[pl.BlockSpec((tm, d), row), pl.BlockSpec((tm, TOP_K), row), pl.BlockSpec((tm, TOP_K), row)],
        out_shape=[jax.ShapeDtypeStruct((m, d), f32), jax.ShapeDtypeStruct((m, TOP_K), f32),
                   jax.ShapeDtypeStruct((m, TOP_K), i32)],
        compiler_params=_cparams(("parallel",)),
        name="mix_ln",
    )(*[a for a, _ in parts], *[w for _, w in parts], x, g.reshape(1, d), b.reshape(1, d), w_r,
      b_r.reshape(1, N_EXP))


def _moe_kernel(be_ref, nreal_ref, x_ref, w1_ref, b1_ref, w2_ref, b2_ref, o_ref, w1s, w2s):
    i = pl.program_id(0)
    real = i < nreal_ref[0]
    prev = be_ref[jnp.maximum(i - 1, 0)]
    fresh = jnp.logical_or(i == 0, be_ref[i] != prev)

    @pl.when(jnp.logical_and(real, fresh))
    def _():
        w1s[...] = w1_ref[0, 0].astype(bf16)
        w2s[...] = w2_ref[0, 0].astype(bf16)

    @pl.when(real)
    def _():
        x = x_ref[...].astype(bf16)
        acc = jnp.zeros((MOE_ROWS, D_MODEL), f32)
        for c in range(D_FF // FF_CHUNK):
            lo = c * FF_CHUNK
            hg = jnp.dot(x, w1s[:, lo:lo + FF_CHUNK], preferred_element_type=f32)
            hg = hg + b1_ref[0, 0, :, lo:lo + FF_CHUNK]
            hu = jnp.dot(x, w1s[:, D_FF + lo:D_FF + lo + FF_CHUNK], preferred_element_type=f32)
            hu = hu + b1_ref[0, 0, :, D_FF + lo:D_FF + lo + FF_CHUNK]
            g = jnp.minimum(hg, SWIGLU_LIMIT)
            u = jnp.clip(hu, -SWIGLU_LIMIT, SWIGLU_LIMIT)
            a = (u + 1.0) * g * jax.nn.sigmoid(SWIGLU_ALPHA * g)
            acc = acc + jnp.dot(a.astype(bf16), w2s[lo:lo + FF_CHUNK, :], preferred_element_type=f32)
        o_ref[...] = acc + b2_ref[0, 0]

    @pl.when(jnp.logical_not(real))
    def _():
        o_ref[...] = jnp.zeros_like(o_ref)


def _moe_ffn_blocks(xb, blk_exp, nreal, w1, b1, w2, b2, layer):
    p, d = xb.shape
    nblk = p // MOE_ROWS
    n_layers = w1.shape[0]
    per_expert = lambda i, be, nr: (layer, be[i], 0, 0)
    grid_spec = pltpu.PrefetchScalarGridSpec(
        num_scalar_prefetch=2,
        grid=(nblk,),
        in_specs=[
            pl.BlockSpec((MOE_ROWS, d), lambda i, be, nr: (i, 0)),
            pl.BlockSpec((1, 1, d, 2 * D_FF), per_expert),
            pl.BlockSpec((1, 1, 1, 2 * D_FF), per_expert),
            pl.BlockSpec((1, 1, D_FF, d), per_expert),
            pl.BlockSpec((1, 1, 1, d), per_expert),
        ],
        out_specs=pl.BlockSpec((MOE_ROWS, d), lambda i, be, nr: (i, 0)),
        scratch_shapes=[pltpu.VMEM((d, 2 * D_FF), bf16), pltpu.VMEM((D_FF, d), bf16)],
    )
    return pl.pallas_call(
        _moe_kernel,
        grid_spec=grid_spec,
        out_shape=jax.ShapeDtypeStruct((p, d), f32),
        compiler_params=_cparams(("arbitrary",)),
        name="moe_ffn",
    )(blk_exp, nreal, xb, w1, b1.reshape(n_layers, N_EXP, 1, 2 * D_FF), w2, b2.reshape(n_layers, N_EXP, 1, d))


def _row_copy(src, src_row, dst, dst_row, sem):
    return pltpu.make_async_copy(src.at[pl.ds(src_row, 1), :], dst.at[pl.ds(dst_row, 1), :], sem)


def _moe_dispatch_kernel(dest_ref, x_ref, xb_init_ref, xb_ref, sem):
    del xb_init_ref

    def issue(t, c):
        for k in range(TOP_K):
            _row_copy(x_ref, t, xb_ref, dest_ref[0, 0, t * TOP_K + k], sem).start(priority=k % 2)
        return c

    lax.fori_loop(0, TOK_TILE, issue, 0, unroll=DMA_UNROLL)

    def drain(t, c):
        for k in range(TOP_K):
            _row_copy(x_ref, 0, xb_ref, 0, sem).wait()
        return c

    lax.fori_loop(0, TOK_TILE, drain, 0, unroll=DMA_UNROLL)


def _moe_dispatch(x1, dest, xb):
    n, d = x1.shape
    nt = n // TOK_TILE
    return pl.pallas_call(
        _moe_dispatch_kernel,
        grid=(nt,),
        in_specs=[pl.BlockSpec((1, 1, TOK_TILE * TOP_K), lambda i: (i, 0, 0), memory_space=pltpu.SMEM),
                  pl.BlockSpec((TOK_TILE, d), lambda i: (i, 0)),
                  pl.BlockSpec(memory_space=pl.ANY)],
        out_specs=pl.BlockSpec(memory_space=pl.ANY),
        out_shape=jax.ShapeDtypeStruct(xb.shape, f32),
        scratch_shapes=[pltpu.SemaphoreType.DMA(())],
        input_output_aliases={2: 0},
        compiler_params=_cparams(("arbitrary",)),
        name="moe_dispatch",
    )(dest.reshape(nt, 1, TOK_TILE * TOP_K), x1, xb)


def _moe_combine_kernel(dest_ref, yb_ref, gate_ref, x1_ref, g_ref, b_ref, o_ref, ybuf, sem):
    def issue(t, c):
        for k in range(TOP_K):
            _row_copy(yb_ref, dest_ref[0, 0, t * TOP_K + k], ybuf.at[k], t, sem).start(priority=k % 2)
        return c

    lax.fori_loop(0, TOK_TILE, issue, 0, unroll=DMA_UNROLL)

    def drain(t, c):
        for k in range(TOP_K):
            _row_copy(yb_ref, 0, ybuf.at[k], 0, sem).wait()
        return c

    lax.fori_loop(0, TOK_TILE, drain, 0, unroll=DMA_UNROLL)
    gate = gate_ref[...]
    y = gate[:, 0:1] * ybuf[0]
    for k in range(1, TOP_K):
        y = y + gate[:, k:k + 1] * ybuf[k]
    o_ref[...] = _layer_norm_rows(DN_ALPHA * x1_ref[...] + y, g_ref[...], b_ref[...])


def _moe_combine(yb, dest, gate, x1, g, b):
    n, d = x1.shape
    nt = n // TOK_TILE
    row = lambda i: (i, 0)
    fix = lambda i: (0, 0)
    return pl.pallas_call(
        _moe_combine_kernel,
        grid=(nt,),
        in_specs=[pl.BlockSpec((1, 1, TOK_TILE * TOP_K), lambda i: (i, 0, 0), memory_space=pltpu.SMEM),
                  pl.BlockSpec(memory_space=pl.ANY),
                  pl.BlockSpec((TOK_TILE, TOP_K), row), pl.BlockSpec((TOK_TILE, d), row),
                  pl.BlockSpec((1, d), fix), pl.BlockSpec((1, d), fix)],
        out_specs=pl.BlockSpec((TOK_TILE, d), row),
        out_shape=jax.ShapeDtypeStruct((n, d), f32),
        scratch_shapes=[pltpu.VMEM((TOP_K, TOK_TILE, d), f32), pltpu.SemaphoreType.DMA(())],
        compiler_params=_cparams(("arbitrary",)),
        name="moe_combine",
    )(dest.reshape(nt, 1, TOK_TILE * TOP_K), yb, gate, x1, g.reshape(1, d), b.reshape(1, d))


def _rank_kernel(e_ref, rank_ref, cnt_ref, carry, upper):
    @pl.when(pl.program_id(0) == 0)
    def _():
        carry[...] = jnp.zeros(carry.shape, f32)
        upper[...] = (lax.broadcasted_iota(i32, (RANK_TILE, RANK_TILE), 0)
                      < lax.broadcasted_iota(i32, (RANK_TILE, RANK_TILE), 1)).astype(bf16)

    e = e_ref[0]
    onehot = (lax.broadcasted_iota(i32, (N_EXP, RANK_TILE), 0) == e).astype(bf16)
    before = jnp.dot(onehot, upper[...], preferred_element_type=f32) + carry[:, 0:1]
    hot = onehot.astype(f32)
    rank_ref[0] = jnp.sum(hot * before, axis=0, keepdims=True).astype(i32)
    carry[...] = carry[...] + jnp.sum(hot, axis=1, keepdims=True)
    cnt_ref[...] = carry[...].astype(i32)


def _expert_ranks(flat_e):
    a = flat_e.shape[0]
    nt = a // RANK_TILE
    rank, cnt = pl.pallas_call(
        _rank_kernel,
        grid=(nt,),
        in_specs=[pl.BlockSpec((1, 1, RANK_TILE), lambda i: (i, 0, 0))],
        out_specs=[pl.BlockSpec((1, 1, RANK_TILE), lambda i: (i, 0, 0)),
                   pl.BlockSpec((N_EXP, LANES), lambda i: (0, 0))],
        out_shape=[jax.ShapeDtypeStruct((nt, 1, RANK_TILE), i32), jax.ShapeDtypeStruct((N_EXP, LANES), i32)],
        scratch_shapes=[pltpu.VMEM((N_EXP, LANES), f32), pltpu.VMEM((RANK_TILE, RANK_TILE), bf16)],
        compiler_params=_cparams(("arbitrary",)),
        name="expert_ranks",
    )(flat_e.reshape(nt, 1, RANK_TILE))
    return rank.reshape(a), cnt[:, 0]


def _moe_blocks(a):
    return -(-a // MOE_ROWS) + N_EXP


def _moe_slots(flat_e):
    a = flat_e.shape[0]
    rank, counts = _expert_ranks(flat_e)
    padded = (counts + MOE_ROWS - 1) // MOE_ROWS * MOE_ROWS
    eidx = jnp.arange(N_EXP, dtype=i32)
    pend = jnp.sum(jnp.where(eidx[None, :] <= eidx[:, None], padded[None, :], 0), axis=1)
    dest = (pend - padded)[flat_e] + rank
    blk_start = jnp.arange(_moe_blocks(a), dtype=i32) * MOE_ROWS
    nreal = (pend[-1] // MOE_ROWS).astype(i32)
    blk_exp = jnp.sum((pend[None, :] <= blk_start[:, None]).astype(i32), axis=1)
    last_exp = blk_exp[jnp.maximum(nreal - 1, 0)]
    blk_exp = jnp.where(blk_start < pend[-1], jnp.minimum(blk_exp, N_EXP - 1), last_exp)
    return dest, blk_exp, nreal.reshape(1)


def _mem_attn_kernel(q_ref, kv_ref, o_ref, *, feat_major):
    bt, tq, _ = q_ref.shape
    lane = lax.broadcasted_iota(i32, (tq, LANES), 1)
    zero = jnp.zeros((tq, LANES), bf16)

    def kv(b, lo):
        blk = kv_ref[0, b, lo:lo + LANES, :] if feat_major else kv_ref[b, :, lo:lo + LANES]
        return blk.astype(bf16)

    for pair in range(MEM_HEADS // 2):
        cols = slice(LANES * pair, LANES * (pair + 1))
        qts = [q_ref[b, :, cols] for b in range(bt)]
        kts = [kv(b, LANES * pair) for b in range(bt)]
        vts = [kv(b, MEM_W + LANES * pair) for b in range(bt)]
        halves = []
        for h in range(2):
            keep = (lane < MEM_HD) if h == 0 else (lane >= MEM_HD)
            qms = [jnp.where(keep, qt, zero) for qt in qts]
            if feat_major:
                ss = [jnp.dot(qm, kt, preferred_element_type=f32) for qm, kt in zip(qms, kts)]
            else:
                ss = [lax.dot_general(qm, kt, _NT, preferred_element_type=f32) for qm, kt in zip(qms, kts)]
            es = [jnp.exp(s - jnp.max(s, axis=-1, keepdims=True)) for s in ss]
            ps = [(e * (1.0 / jnp.sum(e, axis=-1, keepdims=True))).astype(bf16) for e in es]
            if feat_major:
                halves.append([lax.dot_general(p, vt, _NT, preferred_element_type=f32) for p, vt in zip(ps, vts)])
            else:
                halves.append([jnp.dot(p, vt, preferred_element_type=f32) for p, vt in zip(ps, vts)])
        for b in range(bt):
            o_ref[b, :, cols] = jnp.where(lane < MEM_HD, halves[0][b], halves[1][b])


def _mem_attn(mq, mem_kv, layer=None):
    b, t, w = mq.shape
    tq = min(t, ROW_TILE)
    bt = max(1, min(b, MEM_ROWS // tq))
    if layer is None:
        kv_spec = pl.BlockSpec((bt, MEM_TOK, 2 * w), lambda bb, i: (bb, 0, 0))
    else:
        kv_spec = pl.BlockSpec((1, bt, 2 * w, MEM_TOK), lambda bb, i: (layer, bb, 0, 0))
    return pl.pallas_call(
        functools.partial(_mem_attn_kernel, feat_major=layer is not None),
        grid=(b // bt, t // tq),
        in_specs=[pl.BlockSpec((bt, tq, w), lambda bb, i: (bb, i, 0)), kv_spec],
        out_specs=pl.BlockSpec((bt, tq, w), lambda bb, i: (bb, i, 0)),
        out_shape=jax.ShapeDtypeStruct((b, t, w), f32),
        compiler_params=_cparams(("parallel", "parallel")),
        name="mem_attn",
    )(mq, mem_kv)


def _pad_state(s):
    s = jnp.concatenate([s, jnp.zeros((GLA_KP - GLA_DK, GLA_DV), f32)], axis=0)
    return jnp.concatenate([s, jnp.zeros((GLA_KP, GLA_VP - GLA_DV), f32)], axis=1)


def _gla_kernel(q_ref, k_ref, v_ref, r_ref, glr_ref, wg_ref, bg_ref, gn_ref, s0_ref, o_ref, s_out_ref, st_sc,
                *, bt, chunk, zero_init):
    c = pl.program_id(1)

    @pl.when(c == 0)
    def _():
        if zero_init:
            st_sc[...] = jnp.zeros(st_sc.shape, f32)
        else:
            for b in range(bt):
                for h in range(GLA_HEADS):
                    st_sc[b, h] = jnp.transpose(_pad_state(s0_ref[b, h]))

    tri = lax.broadcasted_iota(i32, (chunk, chunk), 0) >= lax.broadcasted_iota(i32, (chunk, chunk), 1)
    trif = tri.astype(f32)
    problems = [(b, h) for b in range(bt) for h in range(GLA_HEADS)]
    ksl = lambda h: slice(GLA_KP * h, GLA_KP * (h + 1))
    vsl = lambda h: slice(GLA_VP * h, GLA_VP * (h + 1))
    qts, kts, kds, ebl = [], [], [], []
    for b in range(bt):
        z = jnp.dot(glr_ref[b].astype(bf16), wg_ref[...], preferred_element_type=f32) + bg_ref[...]
        log_a = jax.nn.log_sigmoid(z) / GLA_TAU
        bc = jnp.dot(trif, log_a, preferred_element_type=f32, precision=lax.Precision.HIGHEST)
        bl = bc[chunk - 1:chunk, :]
        k = k_ref[b]
        qts.append((q_ref[b] * (GLA_DK ** -0.5) * jnp.exp(bc)).astype(bf16))
        kts.append((k * jnp.exp(-bc)).astype(bf16))
        kds.append((k * jnp.exp(bl - bc)).astype(bf16))
        ebl.append(jnp.exp(bl))
    vbs = [v_ref[b].astype(bf16) for b in range(bt)]
    atts = [jnp.where(tri, lax.dot_general(qts[b][:, ksl(h)], kts[b][:, ksl(h)], _NT, preferred_element_type=f32),
                      0.0).astype(bf16) for b, h in problems]
    sts = [st_sc[b, h] for b, h in problems]
    outs = [lax.dot_general(qts[b][:, ksl(h)], st.astype(bf16), _NT, preferred_element_type=f32)
            + jnp.dot(att, vbs[b][:, vsl(h)], preferred_element_type=f32)
            for (b, h), st, att in zip(problems, sts, atts)]
    for (b, h), st in zip(problems, sts):
        st_sc[b, h] = ebl[b][:, ksl(h)] * st + lax.dot_general(vbs[b][:, vsl(h)], kds[b][:, ksl(h)], _TN,
                                                              preferred_element_type=f32)
    for (b, h), o in zip(problems, outs):
        ms = jnp.sum(o * o, axis=-1, keepdims=True) * (1.0 / GLA_DV)
        o = o * lax.rsqrt(ms + LN_EPS) * gn_ref[:, vsl(h)]
        o_ref[b, :, vsl(h)] = o * jax.nn.silu(r_ref[b, :, vsl(h)])

    @pl.when(c == pl.num_programs(1) - 1)
    def _():
        for b in range(bt):
            for h in range(GLA_HEADS):
                s_out_ref[b, h] = jnp.transpose(st_sc[b, h])[:GLA_DK, :GLA_DV]


def _gla(q, k, v, r, glr, wg, bg, gn, s0, *, bt):
    nb, t, _ = q.shape
    chunk = GLA_CHUNK if t % GLA_CHUNK == 0 else t
    zero_init = s0 is None
    if zero_init:
        s0 = jnp.zeros((1, 1, 8, LANES), f32)
    tok = lambda w: pl.BlockSpec((bt, chunk, w), lambda i, c: (i, c, 0))
    fix = lambda a: pl.BlockSpec(a.shape, lambda i, c: (0,) * a.ndim)
    state = pl.BlockSpec((bt, GLA_HEADS, GLA_DK, GLA_DV), lambda i, c: (i, 0, 0, 0))
    return pl.pallas_call(
        functools.partial(_gla_kernel, bt=bt, chunk=chunk, zero_init=zero_init),
        grid=(nb // bt, t // chunk),
        in_specs=[tok(GLA_HEADS * GLA_KP), tok(GLA_HEADS * GLA_KP), tok(GLA_HEADS * GLA_VP), tok(GLA_HEADS * GLA_VP),
                  tok(LANES), fix(wg), fix(bg), fix(gn), fix(s0) if zero_init else state],
        out_specs=[tok(GLA_HEADS * GLA_VP), state],
        out_shape=[jax.ShapeDtypeStruct((nb, t, GLA_HEADS * GLA_VP), f32),
                   jax.ShapeDtypeStruct((nb, GLA_HEADS, GLA_DK, GLA_DV), f32)],
        scratch_shapes=[pltpu.VMEM((bt, GLA_HEADS, GLA_VP, GLA_KP), f32)],
        compiler_params=_cparams(("parallel", "arbitrary")),
        name="gla",
    )(q, k, v, r, glr, wg, bg, gn, s0)


def _pad_heads(a, w, wp):
    lead = a.shape[:-1]
    a = jnp.pad(a.reshape(*lead, GLA_HEADS, w), [(0, 0)] * (len(lead) + 1) + [(0, wp - w)])
    return a.reshape(*lead, GLA_HEADS * wp)


def _softmax_rows(sm, maskf):
    m = jnp.max(sm, axis=-1, keepdims=True)
    e = jnp.exp(sm - m)
    return e * (1.0 / jnp.sum(e, axis=-1, keepdims=True)) * maskf


def _block_scores_t(imp):
    nc = imp.shape[1]
    jj = lax.broadcasted_iota(i32, (NSB_PAD, nc), 0)
    nn = lax.broadcasted_iota(i32, (NSB_PAD, nc), 1)
    mt = ((nn >> RATIO_SHIFT) == jj).astype(f32) + (((nn + 1) >> RATIO_SHIFT) == jj).astype(f32)
    return lax.dot_general(mt, imp, _NT, preferred_element_type=f32, precision=lax.Precision.HIGHEST)


def _select_t(blk_t, qpos_t):
    nq = blk_t.shape[1]
    j_t = lax.broadcasted_iota(i32, (NSB_PAD, nq), 0)
    valid = j_t * SLC_L <= qpos_t
    cur = qpos_t >> SLC_SHIFT
    forced = (j_t == 0) | (j_t == cur) | (j_t == cur - 1)
    score = jnp.where(valid, jnp.where(forced, 1e9, blk_t), -1e9)
    cnt = jnp.zeros((NSB_PAD, nq), i32)
    for k in range(NSB_PAD):
        row = score[k:k + 1, :]
        beats = (row > score) | ((row == score) & (j_t > k))
        cnt = cnt + beats.astype(i32)
    return ((cnt < SLC_K) & valid).astype(f32)


def _untranspose(sel_t):
    nq = sel_t.shape[1]
    eye = lax.broadcasted_iota(i32, (nq, nq), 0) == lax.broadcasted_iota(i32, (nq, nq), 1)
    return lax.dot_general(eye.astype(bf16), sel_t.astype(bf16), _NT, preferred_element_type=f32).astype(bf16)


def _nsa_prompt_kernel(qp_ref, gl_ref, bg_ref, kc_ref, vc_ref, ks_ref, vs_ref, kw_ref, vw_ref, o_ref,
                       p_sc, m_sc, l_sc, acc_sc, og_sc, used_sm):
    nc = kc_ref.shape[1]
    q0 = pl.program_id(1) * Q_BLK
    qpos_i = q0 + lax.broadcasted_iota(i32, (Q_BLK, 1), 0)
    lane = lax.broadcasted_iota(i32, (Q_BLK, LANES), 1)
    gates = jax.nn.sigmoid(gl_ref[0] + bg_ref[...])
    qp = qp_ref[0]
    zero_b = jnp.zeros((Q_BLK, LANES), bf16)

    for g in range(NSA_KV):
        half = (lane < HD) if g == 0 else (lane >= HD)
        qg = jnp.concatenate([jnp.where(half, qp[:, LANES * hh:LANES * (hh + 1)], zero_b)
                              for hh in range(NSA_HPG)], axis=0)
        slopes = [SLOPES[NSA_HPG * g + hh] for hh in range(NSA_HPG)]

        n_idx = lax.broadcasted_iota(i32, (Q_BLK, nc), 1)
        cd = (qpos_i - (n_idx * CMP_S + (CMP_L - 1))).astype(f32)
        cmask = cd >= 0
        cmaskf = cmask.astype(f32)
        s = lax.dot_general(qg, kc_ref[0], _NT, preferred_element_type=f32)
        imp = jnp.zeros((Q_BLK, nc), f32)
        for hh in range(NSA_HPG):
            rows = slice(hh * Q_BLK, (hh + 1) * Q_BLK)
            p = _softmax_rows(jnp.where(cmask, s[rows] - slopes[hh] * cd, NEG), cmaskf)
            imp = imp + p
            p_sc[rows, :nc] = p.astype(bf16)
        o_cmp = jnp.dot(p_sc[:, :nc], vc_ref[0], preferred_element_type=f32)

        qpos_t = q0 + lax.broadcasted_iota(i32, (NSB_PAD, Q_BLK), 1)
        sel_t = _select_t(_block_scores_t(imp), qpos_t)
        sel_b = _untranspose(sel_t)
        blocks_per_tile = SLC_TILE // SLC_L
        for t in range(NSB_PAD // blocks_per_tile):
            used_sm[t] = (jnp.max(sel_t[t * blocks_per_tile:(t + 1) * blocks_per_tile, :]) > 0.5).astype(i32)

        m_sc[...] = jnp.full(m_sc.shape, NEG, f32)
        l_sc[...] = jnp.zeros(l_sc.shape, f32)
        acc_sc[...] = jnp.zeros(acc_sc.shape, f32)

        def tile(t, carry):
            @pl.when(used_sm[t] > 0)
            def _():
                k0 = pl.multiple_of(t * SLC_TILE, SLC_TILE)
                kt = ks_ref[0, pl.ds(k0, SLC_TILE), :]
                vt = vs_ref[0, pl.ds(k0, SLC_TILE), :]
                st = lax.dot_general(qg, kt, _NT, preferred_element_type=f32)
                kpos = k0 + lax.broadcasted_iota(i32, (Q_BLK, SLC_TILE), 1)
                blk_of_key = (k0 + lax.broadcasted_iota(i32, (NSB_PAD, SLC_TILE), 1)) >> SLC_SHIFT
                expand = (lax.broadcasted_iota(i32, (NSB_PAD, SLC_TILE), 0) == blk_of_key).astype(bf16)
                selexp = jnp.dot(sel_b, expand, preferred_element_type=f32)
                addmask = jnp.where((qpos_i >= kpos) & (selexp > 0.5), 0.0, NEG)
                krel = (k0 - q0 + lax.broadcasted_iota(i32, (1, SLC_TILE), 1)).astype(f32)
                for hh in range(NSA_HPG):
                    rows = slice(hh * Q_BLK, (hh + 1) * Q_BLK)
                    sm = st[rows] + slopes[hh] * krel + addmask
                    m_old = m_sc[rows]
                    m_new = jnp.maximum(m_old, jnp.max(sm, axis=-1, keepdims=True))
                    a = jnp.exp(m_old - m_new)
                    e = jnp.exp(sm - m_new)
                    l_sc[rows] = a * l_sc[rows] + jnp.sum(e, axis=-1, keepdims=True)
                    m_sc[rows] = m_new
                    acc_sc[rows] = a * acc_sc[rows]
                    p_sc[rows, :SLC_TILE] = e.astype(bf16)
                acc_sc[...] += jnp.dot(p_sc[:, :SLC_TILE], vt, preferred_element_type=f32)
            return carry

        lax.fori_loop(0, (q0 + Q_BLK + SLC_TILE - 1) // SLC_TILE, tile, 0)
        o_slc = acc_sc[...] / l_sc[...]

        start = pl.multiple_of(jnp.maximum(q0 - WINDOW, 0), Q_BLK)
        kt = kw_ref[0, pl.ds(start, WIN_KEYS), :]
        vt = vw_ref[0, pl.ds(start, WIN_KEYS), :]
        st = lax.dot_general(qg, kt, _NT, preferred_element_type=f32)
        wd = (qpos_i - (start + lax.broadcasted_iota(i32, (Q_BLK, WIN_KEYS), 1))).astype(f32)
        wmask = (wd >= 0) & (wd <= WINDOW)
        wmaskf = wmask.astype(f32)
        for hh in range(NSA_HPG):
            rows = slice(hh * Q_BLK, (hh + 1) * Q_BLK)
            p = _softmax_rows(jnp.where(wmask, st[rows] - slopes[hh] * wd, NEG), wmaskf)
            p_sc[rows, :WIN_KEYS] = p.astype(bf16)
        o_win = jnp.dot(p_sc[:, :WIN_KEYS], vt, preferred_element_type=f32)

        for hh in range(NSA_HPG):
            rows = slice(hh * Q_BLK, (hh + 1) * Q_BLK)
            c = 3 * (NSA_HPG * g + hh)
            og_sc[g, rows] = (gates[:, c:c + 1] * o_cmp[rows] + gates[:, c + 1:c + 2] * o_slc[rows]
                              + gates[:, c + 2:c + 3] * o_win[rows])

    for hh in range(NSA_HPG):
        rows = slice(hh * Q_BLK, (hh + 1) * Q_BLK)
        o_ref[0, :, LANES * hh:LANES * (hh + 1)] = jnp.where(lane < HD, og_sc[0, rows], og_sc[1, rows])


def _nsa_prompt_call(qp, gl, bg, kc, vc, ks, vs, kw, vw):
    b, t, w = qp.shape
    nc = kc.shape[1]
    assert t % SLC_TILE == 0 and t >= WIN_KEYS and nc % LANES == 0 and t // SLC_L <= NSB_PAD
    rows = NSA_HPG * Q_BLK
    qblk = lambda bb, i: (bb, i, 0)
    whole = lambda bb, i: (bb, 0, 0)
    return pl.pallas_call(
        _nsa_prompt_kernel,
        grid=(b, t // Q_BLK),
        in_specs=[pl.BlockSpec((1, Q_BLK, w), qblk), pl.BlockSpec((1, Q_BLK, LANES), qblk),
                  pl.BlockSpec((1, LANES), lambda bb, i: (0, 0)),
                  pl.BlockSpec((1, nc, LANES), whole), pl.BlockSpec((1, nc, LANES), whole),
                  pl.BlockSpec((1, t, LANES), whole), pl.BlockSpec((1, t, LANES), whole),
                  pl.BlockSpec((1, t, LANES), whole), pl.BlockSpec((1, t, LANES), whole)],
        out_specs=pl.BlockSpec((1, Q_BLK, w), qblk),
        out_shape=jax.ShapeDtypeStruct((b, t, w), f32),
        scratch_shapes=[pltpu.VMEM((rows, WIN_KEYS), bf16), pltpu.VMEM((rows, 1), f32), pltpu.VMEM((rows, 1), f32),
                        pltpu.VMEM((rows, LANES), f32), pltpu.VMEM((NSA_KV, rows, LANES), f32),
                        pltpu.SMEM((NSB_PAD * SLC_L // SLC_TILE,), i32)],
        compiler_params=_cparams(("parallel", "parallel")),
        name="nsa_prompt",
    )(qp, gl, bg, kc, vc, ks, vs, kw, vw)


def _nsa_sample_kernel(pt_ref, qp_ref, gl_ref, bg_ref, kc_ref, vc_ref, kn_ref, vn_ref, wst_ref, wkn_ref, wvn_ref,
                       *rest, n_pages, past_len, bt):
    page_refs = rest[:n_pages * bt]
    o_ref, k_sc, v_sc = rest[n_pages * bt:]
    nq = qp_ref.shape[1]
    nc = kc_ref.shape[1]
    wb = wst_ref.shape[1]
    nkeys = (n_pages + 1) * PAGE_SIZE
    rows_g = NSA_HPG * nq
    qpos_i = past_len + lax.broadcasted_iota(i32, (nq, 1), 0)
    lane = lax.broadcasted_iota(i32, (nq, LANES), 1)
    zero_b = jnp.zeros((nq, LANES), bf16)
    tail = jnp.zeros((PAGE_SIZE - nq, LANES), bf16)

    def rows_of(g, hh):
        r0 = g * rows_g + hh * nq
        return slice(r0, r0 + nq)

    n_idx = lax.broadcasted_iota(i32, (nq, nc), 1)
    cd = (qpos_i - (n_idx * CMP_S + (CMP_L - 1))).astype(f32)
    cmask = cd >= 0
    cmaskf = cmask.astype(f32)
    qpos_t = past_len + lax.broadcasted_iota(i32, (NSB_PAD, nq), 1)
    blk_of_key = lax.broadcasted_iota(i32, (NSB_PAD, nkeys), 1) >> SLC_SHIFT
    expand = (lax.broadcasted_iota(i32, (NSB_PAD, nkeys), 0) == blk_of_key).astype(bf16)
    sd = (qpos_i - lax.broadcasted_iota(i32, (nq, nkeys), 1)).astype(f32)
    wpos = (past_len - wb) + lax.broadcasted_iota(i32, (nq, wb + PAGE_SIZE), 1)
    wd = (qpos_i - wpos).astype(f32)
    wmask = (wd >= 0) & (wd <= WINDOW) & (wpos >= 0)
    wmaskf = wmask.astype(f32)

    def sequence(j):
        gates = jax.nn.sigmoid(gl_ref[j] + bg_ref[...])
        qp = qp_ref[j]
        qs = jnp.concatenate([jnp.where((lane < HD) if g == 0 else (lane >= HD),
                                        qp[:, LANES * hh:LANES * (hh + 1)], zero_b)
                              for g in range(NSA_KV) for hh in range(NSA_HPG)], axis=0)

        s = lax.dot_general(qs, kc_ref[j], _NT, preferred_element_type=f32)
        yield
        imps, ps = [], []
        for g in range(NSA_KV):
            imp = jnp.zeros((nq, nc), f32)
            for hh in range(NSA_HPG):
                p = _softmax_rows(jnp.where(cmask, s[rows_of(g, hh)] - SLOPES[NSA_HPG * g + hh] * cd, NEG), cmaskf)
                imp = imp + p
                ps.append(p)
            imps.append(imp)
        o_cmp = jnp.dot(jnp.concatenate(ps, axis=0).astype(bf16), vc_ref[j], preferred_element_type=f32)
        yield

        masks = []
        for g in range(NSA_KV):
            sel_b = _untranspose(_select_t(_block_scores_t(imps[g]), qpos_t))
            selexp = jnp.dot(sel_b, expand, preferred_element_type=f32)
            masks.append((sd >= 0) & (selexp > 0.5))
        yield

        for pg in range(n_pages):
            blk = page_refs[j * n_pages + pg][0]
            k_sc[j, :, pg * PAGE_SIZE:(pg + 1) * PAGE_SIZE] = blk[:LANES].astype(bf16)
            v_sc[j, :, pg * PAGE_SIZE:(pg + 1) * PAGE_SIZE] = blk[LANES:].astype(bf16)
        new_t = lambda ref: jnp.transpose(jnp.concatenate([ref[j], tail], axis=0).astype(f32)).astype(bf16)
        k_sc[j, :, n_pages * PAGE_SIZE:] = new_t(kn_ref)
        v_sc[j, :, n_pages * PAGE_SIZE:] = new_t(vn_ref)
        s = jnp.dot(qs, k_sc[j], preferred_element_type=f32)
        yield
        ps = []
        for g in range(NSA_KV):
            mf = masks[g].astype(f32)
            for hh in range(NSA_HPG):
                ps.append(_softmax_rows(jnp.where(masks[g], s[rows_of(g, hh)] - SLOPES[NSA_HPG * g + hh] * sd, NEG),
                                        mf))
        o_slc = lax.dot_general(jnp.concatenate(ps, axis=0).astype(bf16), v_sc[j], _NT,
                                preferred_element_type=f32)
        yield

        wk = jnp.concatenate([wst_ref[j, :, :LANES].astype(bf16), wkn_ref[j], tail], axis=0)
        wv = jnp.concatenate([wst_ref[j, :, LANES:].astype(bf16), wvn_ref[j], tail], axis=0)
        s = lax.dot_general(qs, wk, _NT, preferred_element_type=f32)
        yield
        ps = []
        for g in range(NSA_KV):
            for hh in range(NSA_HPG):
                ps.append(_softmax_rows(jnp.where(wmask, s[rows_of(g, hh)] - SLOPES[NSA_HPG * g + hh] * wd, NEG),
                                        wmaskf))
        o_win = jnp.dot(jnp.concatenate(ps, axis=0).astype(bf16), wv, preferred_element_type=f32)
        yield

        for hh in range(NSA_HPG):
            halves = []
            for g in range(NSA_KV):
                r = rows_of(g, hh)
                c = 3 * (NSA_HPG * g + hh)
                halves.append(gates[:, c:c + 1] * o_cmp[r] + gates[:, c + 1:c + 2] * o_slc[r]
                              + gates[:, c + 2:c + 3] * o_win[r])
            o_ref[j, :, LANES * hh:LANES * (hh + 1)] = jnp.where(lane < HD, halves[0], halves[1])

    running = [sequence(j) for j in range(bt)]
    done = object()
    while running:
        running = [seq for seq in running if next(seq, done) is not done]


def _nsa_sample_call(page_table, qp, gl, bg, kc, vc, kn, vn, wst, wkn, wvn, cache):
    b, nq, w = qp.shape
    n_pages = page_table.shape[1]
    nc = kc.shape[1]
    wb = wst.shape[1]
    nkeys = (n_pages + 1) * PAGE_SIZE
    assert nkeys // SLC_L <= NSB_PAD and nq <= PAGE_SIZE
    bt = NSA_SAMPLE_BT if b % NSA_SAMPLE_BT == 0 else 1
    per_b = lambda bb, pt: (bb, 0, 0)
    in_specs = [pl.BlockSpec((bt, nq, w), per_b), pl.BlockSpec((bt, nq, LANES), per_b),
                pl.BlockSpec((1, LANES), lambda bb, pt: (0, 0)),
                pl.BlockSpec((bt, nc, LANES), per_b), pl.BlockSpec((bt, nc, LANES), per_b),
                pl.BlockSpec((bt, nq, LANES), per_b), pl.BlockSpec((bt, nq, LANES), per_b),
                pl.BlockSpec((bt, wb, 2 * LANES), per_b),
                pl.BlockSpec((bt, nq, LANES), per_b), pl.BlockSpec((bt, nq, LANES), per_b)]
    for j in range(bt):
        for pg in range(n_pages):
            in_specs.append(pl.BlockSpec(
                (1, 2 * LANES, PAGE_SIZE),
                functools.partial(lambda bb, pt, j, pg: (pt[bb * bt + j, pg], 1, 0), j=j, pg=pg)))
    grid_spec = pltpu.PrefetchScalarGridSpec(
        num_scalar_prefetch=1, grid=(b // bt,), in_specs=in_specs,
        out_specs=pl.BlockSpec((bt, nq, w), per_b),
        scratch_shapes=[pltpu.VMEM((bt, LANES, nkeys), bf16), pltpu.VMEM((bt, LANES, nkeys), bf16)])
    return pl.pallas_call(
        functools.partial(_nsa_sample_kernel, n_pages=n_pages, past_len=n_pages * PAGE_SIZE, bt=bt),
        grid_spec=grid_spec,
        out_shape=jax.ShapeDtypeStruct((b, nq, w), f32),
        compiler_params=_cparams(("parallel",)),
        name="nsa_sample",
    )(page_table, qp, gl, bg, kc, vc, kn, vn, wst, wkn, wvn, *([cache] * (n_pages * bt)))


def _compress_kernel(x_ref, pos_ref, w1_ref, b1_ref, w2_ref, o_ref):
    x = x_ref[0, 0, 0]
    half = CMP_S * HD
    xa = (x + pos_ref[0, :, :half]).astype(bf16)
    xb = (x + pos_ref[0, :, half:]).astype(bf16)
    a = jnp.dot(xa, w1_ref[0, :half, :], preferred_element_type=f32)
    bsec = jnp.dot(xb, w1_ref[0, half:, :], preferred_element_type=f32)
    nch = x.shape[0]
    hid = jax.nn.gelu(a + pltpu.roll(bsec, nch - 1, 0) + b1_ref[0])
    o_ref[0, 0, 0] = jnp.dot(hid.astype(bf16), w2_ref[0], preferred_element_type=f32).astype(o_ref.dtype)


def _compress_call(x2, cmp_pos, w_cmp1, b_cmp1, w_cmp2):
    b, _, g, nch, cw = x2.shape
    hid = w_cmp1.shape[-1]
    pos = cmp_pos.reshape(2, 1, CMP_L * HD)
    return pl.pallas_call(
        _compress_kernel,
        grid=(b, 2, g),
        in_specs=[pl.BlockSpec((1, 1, 1, nch, cw), lambda bb, s, gg: (bb, s, gg, 0, 0)),
                  pl.BlockSpec((1, 1, CMP_L * HD), lambda bb, s, gg: (s, 0, 0)),
                  pl.BlockSpec((1, CMP_L * HD, hid), lambda bb, s, gg: (s, 0, 0)),
                  pl.BlockSpec((1, 1, hid), lambda bb, s, gg: (s, 0, 0)),
                  pl.BlockSpec((1, hid, HD), lambda bb, s, gg: (s, 0, 0))],
        out_specs=pl.BlockSpec((1, 1, 1, nch, HD), lambda bb, s, gg: (bb, s, gg, 0, 0)),
        out_shape=jax.ShapeDtypeStruct((b, 2, g, nch, HD), bf16),
        compiler_params=_cparams(("parallel", "parallel", "parallel")),
        name="compress",
    )(x2, pos, w_cmp1.astype(bf16), b_cmp1.reshape(2, 1, hid), w_cmp2.astype(bf16))


def _compressed_kv(tok, cmp_pos, w_cmp1, b_cmp1, w_cmp2):
    b, t = tok.shape[:2]
    nch = t // CMP_S
    x2 = jnp.transpose(tok[:, :nch * CMP_S], (0, 2, 3, 1, 4)).reshape(b, 2, NSA_KV, nch, CMP_S * HD)
    ck = _compress_call(x2, cmp_pos, w_cmp1, b_cmp1, w_cmp2)
    ck = jnp.transpose(ck, (0, 1, 3, 2, 4)).reshape(b, 2, nch, NSA_KV * HD)
    return ck[:, 0], ck[:, 1]


def _compress_paged_kernel(pt_ref, pos_ref, w1a_ref, w1b_ref, b1_ref, w2_ref, *rest, n_pages):
    page_refs = rest[:n_pages]
    o_ref, x_sc, xcat_sc = rest[n_pages:]
    nch = n_pages * (PAGE_SIZE // CMP_S)
    role = NSA_KV * HD
    for pg in range(n_pages):
        xt = jnp.transpose(page_refs[pg][0])
        for s in range(2):
            x_sc[s, pg * PAGE_SIZE:(pg + 1) * PAGE_SIZE, :] = xt[:, s * role:(s + 1) * role]
    for s in range(2):
        for t in range(CMP_S):
            xcat_sc[:, t * role:(t + 1) * role] = x_sc[s, pl.ds(t, nch, stride=CMP_S), :]
        xc = xcat_sc[...]
        xa = (xc + pos_ref[s, 0:1, :]).astype(bf16)
        xb = (xc + pos_ref[s, 1:2, :]).astype(bf16)
        a = jnp.dot(xa, w1a_ref[s], preferred_element_type=f32)
        bsec = jnp.dot(xb, w1b_ref[s], preferred_element_type=f32)
        hid = jax.nn.gelu(a + pltpu.roll(bsec, nch - 1, 0) + b1_ref[s])
        o_ref[0, s] = jnp.dot(hid.astype(bf16), w2_ref[s], preferred_element_type=f32).astype(o_ref.dtype)


def _compress_paged(page_table, cache_t, cmp_pos, w_cmp1, b_cmp1, w_cmp2):
    b, n_pages = page_table.shape
    nch = n_pages * (PAGE_SIZE // CMP_S)
    hid = w_cmp1.shape[-1]
    half = CMP_S * HD
    eye = jnp.eye(NSA_KV, dtype=f32)

    def widen_rows(w):
        w = w.reshape(2, CMP_S, HD, hid)
        return jnp.einsum('stdh,ag->stadgh', w, eye).reshape(2, CMP_S * NSA_KV * HD, NSA_KV * hid)

    w1a = widen_rows(w_cmp1[:, :half]).astype(bf16)
    w1b = widen_rows(w_cmp1[:, half:]).astype(bf16)
    w2 = jnp.einsum('shd,ag->sahgd', w_cmp2, eye).reshape(2, NSA_KV * hid, NSA_KV * HD).astype(bf16)
    b1 = jnp.tile(b_cmp1, (1, NSA_KV)).reshape(2, 1, NSA_KV * hid)
    pos = jnp.broadcast_to(cmp_pos.reshape(2, 2, CMP_S, 1, HD), (2, 2, CMP_S, NSA_KV, HD))
    pos = pos.reshape(2, 2, CMP_S * NSA_KV * HD)
    fix = lambda a: pl.BlockSpec(a.shape, lambda bb, pt: (0,) * a.ndim)
    in_specs = [fix(pos), fix(w1a), fix(w1b), fix(b1), fix(w2)]
    for pg in range(n_pages):
        in_specs.append(pl.BlockSpec((1, 2 * NSA_KV * HD, PAGE_SIZE),
                                     functools.partial(lambda bb, pt, pg: (pt[bb, pg], 0, 0), pg=pg)))
    grid_spec = pltpu.PrefetchScalarGridSpec(
        num_scalar_prefetch=1, grid=(b,), in_specs=in_specs,
        out_specs=pl.BlockSpec((1, 2, nch, NSA_KV * HD), lambda bb, pt: (bb, 0, 0, 0)),
        scratch_shapes=[pltpu.VMEM((2, n_pages * PAGE_SIZE, NSA_KV * HD), f32),
                        pltpu.VMEM((nch, CMP_S * NSA_KV * HD), f32)])
    return pl.pallas_call(
        functools.partial(_compress_paged_kernel, n_pages=n_pages),
        grid_spec=grid_spec,
        out_shape=jax.ShapeDtypeStruct((b, 2, nch, NSA_KV * HD), bf16),
        compiler_params=_cparams(("parallel",)),
        name="compress_paged",
    )(page_table, pos, w1a, w1b, b1, w2, *([cache_t] * n_pages))


def _pair_pack_cols(w):
    k = w.shape[0]
    return jnp.transpose(w.reshape(k, NSA_KV, NSA_HPG, HD), (0, 2, 1, 3)).reshape(k, NSA_W)


def _last_rows(a, n):
    t = a.shape[1]
    if t >= n:
        return a[:, t - n:]
    return jnp.pad(a, ((0, 0), (n - t, 0)) + ((0, 0),) * (a.ndim - 2))


def kernel(x_prompt, x_sample, mem_prompt, state_gla, cache_nsa_kv, state_win_kv, cache_mem_kv, page_table,
           w_in_a, w_gate_a, b_gate_a, gla_norm, w_in_b, b_gate_b, w_kv_b, cmp_pos, w_cmp1, b_cmp1, w_cmp2,
           w_mem_kv, w_out, ln1_g, ln1_b, ln2_g, ln2_b, w_router, b_router, w_e1, b_e1, w_e2, b_e2):
    bp, tp, d = x_prompt.shape
    bs, ts, _ = x_sample.shape
    n_p, n_s = bp * tp, bs * ts
    wb = min(WINDOW, state_win_kv.shape[1])
    n_pool = cache_nsa_kv.shape[0]
    past_len = page_table.shape[1] * PAGE_SIZE

    mem_rows = mem_prompt.reshape(bp * MEM_TOK, d)
    w_mem = [w_mem_kv[l].astype(bf16) for l in range(DEPTH)]
    mem_kv_l = _proj(mem_rows, w_mem, _whole(w_mem))
    mem_kv_prompt = jnp.stack(mem_kv_l).reshape(DEPTH, bp, MEM_TOK, 2, MEM_HEADS, MEM_HD)
    mem_p = [m.reshape(bp, MEM_TOK, 2 * MEM_W) for m in mem_kv_l]
    mem_s = jnp.transpose(cache_mem_kv, (0, 1, 3, 4, 5, 2)).reshape(DEPTH, bs, 2 * MEM_W, MEM_TOK)

    x_p, x_s = x_prompt.reshape(n_p, d), x_sample.reshape(n_s, d)
    as_p = lambda a: a.reshape(bp, tp, a.shape[-1])
    as_s = lambda a: a.reshape(bs, ts, a.shape[-1])

    hk = GLA_HEADS * GLA_DK
    cuts_a = [0, hk, 2 * hk, 2 * hk + GLA_W, 2 * hk + 2 * GLA_W, 2 * hk + 2 * GLA_W + GLA_RANK,
              2 * hk + 2 * GLA_W + GLA_RANK + MEM_W]
    mem_scale = MEM_HD ** -0.5
    q_scale = HD ** -0.5
    xb = jnp.zeros((_moe_blocks((n_p + n_s) * TOP_K) * MOE_ROWS, d), f32)

    gla_p, gla_s = [], []
    for l in range(DEPTH):
        if l < N_A:
            wq, wk, wv, wr, wgl, wm = [w_in_a[l][:, cuts_a[i]:cuts_a[i + 1]] for i in range(6)]
            ws = [_pad_heads(wq, GLA_DK, GLA_KP), _pad_heads(wk, GLA_DK, GLA_KP), _pad_heads(wv, GLA_DV, GLA_VP),
                  _pad_heads(wr, GLA_DV, GLA_VP), jnp.pad(wgl, ((0, 0), (0, LANES - GLA_RANK))), wm * mem_scale]
            ws = [w.astype(bf16) for w in ws]
            outs = _whole(ws)
            outs[5] = outs[5][:3] + (bf16,)
            wg = _pad_heads(jnp.pad(w_gate_a[l], ((0, LANES - GLA_RANK), (0, 0))), GLA_DK, GLA_KP).astype(bf16)
            bg = _pad_heads(b_gate_a[l], GLA_DK, GLA_KP).reshape(1, GLA_HEADS * GLA_KP)
            gn = _pad_heads(gla_norm[l], GLA_DV, GLA_VP).reshape(1, GLA_HEADS * GLA_VP)
            q, k, v, r, glr, mq_p = _proj(x_p, ws, outs)
            o_p, s_new = _gla(as_p(q), as_p(k), as_p(v), as_p(r), as_p(glr), wg, bg, gn, None, bt=min(bp, GLA_BT))
            gla_p.append(s_new)
            q, k, v, r, glr, mq_s = _proj(x_s, ws, outs)
            o_s, s_new = _gla(as_s(q), as_s(k), as_s(v), as_s(r), as_s(glr), wg, bg, gn, state_gla[l],
                              bt=min(bs, GLA_BT))
            gla_s.append(s_new)
            w_o = _pad_heads(w_out[l][:GLA_W].T, GLA_DV, GLA_VP).T
        else:
            j = l - N_A
            if l == N_A:
                role = NSA_KV * HD
                kv_outs = [(0, 0, 4 * role, f32), (0, 4 * role, 6 * role, f32)]
                kv_outs += [(0, r * role, (r + 1) * role, bf16) for r in range(2, 6)]
                wkv = [w_kv_b.astype(bf16)]
                rows_p, win_p, ks_p, vs_p, kw_p, vw_p = [as_p(a) for a in _proj(x_p, wkv, kv_outs)]
                rows_s, win_s, ks_s, vs_s, kw_s, vw_s = [as_s(a) for a in _proj(x_s, wkv, kv_outs)]
                kc_p, vc_p = _compressed_kv(rows_p.reshape(bp, tp, 4, NSA_KV, HD)[:, :, :2],
                                            cmp_pos, w_cmp1, b_cmp1, w_cmp2)
                assert (past_len + ts) // CMP_S == past_len // CMP_S
                cache_t = jnp.transpose(cache_nsa_kv, (0, 2, 3, 4, 1)).reshape(n_pool, 4 * role, PAGE_SIZE)
                ck_s = _compress_paged(page_table, cache_t, cmp_pos, w_cmp1, b_cmp1, w_cmp2)
                kc_s, vc_s = ck_s[:, 0], ck_s[:, 1]
                wst = state_win_kv.reshape(bs, state_win_kv.shape[1], 2 * role)
                wkv_s = jnp.concatenate([state_win_kv, win_s.reshape(bs, ts, 2, NSA_KV, HD)], axis=1)
                nsa_out = (rows_p.reshape(bp, tp, 4, NSA_KV, HD), _last_rows(win_p.reshape(bp, tp, 2, NSA_KV, HD), wb),
                           rows_s.reshape(bs, ts, 4, NSA_KV, HD), wkv_s[:, -wb:])
            wq = _pair_pack_cols(w_in_b[j][:, :NSA_W]) * q_scale
            wg = jnp.pad(w_in_b[j][:, NSA_W:NSA_W + 3 * NSA_HEADS], ((0, 0), (0, LANES - 3 * NSA_HEADS)))
            wm = w_in_b[j][:, NSA_W + 3 * NSA_HEADS:] * mem_scale
            ws = [wq.astype(bf16), wg.astype(bf16), wm.astype(bf16)]
            outs = [(0, 0, NSA_W, bf16), (1, 0, LANES, f32), (2, 0, MEM_W, bf16)]
            bg = jnp.pad(b_gate_b[j], (0, LANES - 3 * NSA_HEADS)).reshape(1, LANES)
            qp, gl, mq_p = _proj(x_p, ws, outs)
            o_p = _nsa_prompt_call(as_p(qp), as_p(gl), bg, kc_p, vc_p, ks_p, vs_p, kw_p, vw_p)
            qp, gl, mq_s = _proj(x_s, ws, outs)
            o_s = _nsa_sample_call(page_table, as_s(qp), as_s(gl), bg, kc_s, vc_s, ks_s, vs_s, wst, kw_s, vw_s,
                                   cache_t)
            w_o = _pair_pack_cols(w_out[l][:NSA_W].T).T
        om_p = _mem_attn(as_p(mq_p), mem_p[l])
        om_s = _mem_attn(as_s(mq_s), mem_s, layer=l)
        w_o, w_om = w_o.astype(bf16), w_out[l][-MEM_W:].astype(bf16)
        flat = lambda a: a.reshape(-1, a.shape[-1])
        x1_p, gate_p, idx_p = _mix_ln([(flat(o_p), w_o), (flat(om_p), w_om)], x_p, ln1_g[l], ln1_b[l],
                                      w_router[l], b_router[l])
        x1_s, gate_s, idx_s = _mix_ln([(flat(o_s), w_o), (flat(om_s), w_om)], x_s, ln1_g[l], ln1_b[l],
                                      w_router[l], b_router[l])
        dest, blk_exp, nreal = _moe_slots(jnp.concatenate([idx_p.reshape(-1), idx_s.reshape(-1)]))
        dest_p, dest_s = dest[:n_p * TOP_K], dest[n_p * TOP_K:]
        xb = _moe_dispatch(x1_s, dest_s, _moe_dispatch(x1_p, dest_p, xb))
        yb = _moe_ffn_blocks(xb, blk_exp, nreal, w_e1, b_e1, w_e2, b_e2, l)
        x_p = _moe_combine(yb, dest_p, gate_p, x1_p, ln2_g[l], ln2_b[l])
        x_s = _moe_combine(yb, dest_s, gate_s, x1_s, ln2_g[l], ln2_b[l])

    rows_p, win_p_out, rows_s, win_s_out = nsa_out
    return (x_p.reshape(bp, tp, d), x_s.reshape(bs, ts, d),
            jnp.stack(gla_p), jnp.stack(gla_s), rows_p, rows_s, win_p_out, win_s_out, mem_kv_prompt)
```

```python
import functools
import math

import jax
import jax.numpy as jnp
from jax import lax
from jax.experimental import pallas as pl
from jax.experimental.pallas import tpu as pltpu

D_MODEL = 1024
DEPTH = 4
PAGE_SIZE = 128
N_A = DEPTH // 2
GLA_HEADS = 4
GLA_DV = (3 * D_MODEL) // (4 * GLA_HEADS)
GLA_DK = GLA_DV // 2
GLA_RANK = 16
GLA_TAU = 16.0
GLA_CHUNK = 64
HD = 64
NSA_HEADS = (3 * D_MODEL) // (4 * HD)
NSA_KV = 2
NSA_HPG = NSA_HEADS // NSA_KV
CMP_S = 16
CMP_L = 2 * CMP_S
SLC_L = 64
SLC_K = 16
WINDOW = 512
Q_BLK = 128
MEM_TOK = 256
MEM_HEADS = 4
MEM_HD = D_MODEL // (4 * MEM_HEADS)
N_EXP = 32
TOP_K = 4
D_FF = D_MODEL
SWIGLU_LIMIT = 7.0
SWIGLU_ALPHA = 1.702
GLA_W = GLA_HEADS * GLA_DV
NSA_W = NSA_HEADS * HD
MEM_W = MEM_HEADS * MEM_HD
DN_ALPHA = (2 * DEPTH) ** 0.25
LN_EPS = 1e-5
NEG = -1e30

f32, bf16, i32 = jnp.float32, jnp.bfloat16, jnp.int32

VMEM_LIMIT_BYTES = 56 * 1024 * 1024
LANES = 128
ROW_TILE = 512
MOE_ROWS = 512
FF_CHUNK = 512
SLC_TILE = 1024
WIN_KEYS = WINDOW + Q_BLK
NSB_PAD = 64
TOK_TILE = 512
DMA_UNROLL = 4
RANK_TILE = 1024
MEM_ROWS = 64
NSA_SAMPLE_BT = 4
GLA_KP = 128
GLA_VP = 256
GLA_BT = 8
_NT = (((1,), (1,)), ((), ()))
_TN = (((0,), (0,)), ((), ()))


def _cparams(sem):
    return pltpu.CompilerParams(dimension_semantics=sem, vmem_limit_bytes=VMEM_LIMIT_BYTES)


def _alibi_slopes(n):
    def pow2(m):
        start = 2.0 ** (-8.0 / m)
        return [start ** (i + 1) for i in range(m)]
    if math.log2(n).is_integer():
        return pow2(n)
    c = 2 ** math.floor(math.log2(n))
    return pow2(c) + pow2(2 * c)[0::2][: n - c]


SLOPES = _alibi_slopes(NSA_HEADS)
SLC_SHIFT = int(math.log2(SLC_L))
RATIO_SHIFT = int(math.log2(SLC_L // CMP_S))


def _proj_kernel(x_ref, *refs, n_w, outs):
    xb = x_ref[...].astype(bf16)
    res = {}
    for o_ref, (wi, lo, hi) in zip(refs[n_w:], outs):
        if wi not in res:
            res[wi] = jnp.dot(xb, refs[wi][...], preferred_element_type=f32)
        o_ref[...] = res[wi][:, lo:hi].astype(o_ref.dtype)


def _proj(x, ws, outs, tm=ROW_TILE):
    m, k = x.shape
    tm = min(tm, m)
    in_specs = [pl.BlockSpec((tm, k), lambda i: (i, 0))]
    in_specs += [pl.BlockSpec(w.shape, lambda i: (0, 0)) for w in ws]
    out_specs = [pl.BlockSpec((tm, hi - lo), lambda i: (i, 0)) for _, lo, hi, _ in outs]
    out_shape = [jax.ShapeDtypeStruct((m, hi - lo), dt) for _, lo, hi, dt in outs]
    return pl.pallas_call(
        functools.partial(_proj_kernel, n_w=len(ws), outs=tuple(o[:3] for o in outs)),
        grid=(m // tm,),
        in_specs=in_specs,
        out_specs=out_specs,
        out_shape=out_shape,
        compiler_params=_cparams(("parallel",)),
        name="proj",
    )(x, *ws)


def _whole(ws, dtype=f32):
    return [(i, 0, w.shape[1], dtype) for i, w in enumerate(ws)]


def _layer_norm_rows(y, g, b):
    mu = jnp.mean(y, axis=-1, keepdims=True)
    d = y - mu
    var = jnp.mean(d * d, axis=-1, keepdims=True)
    return d * lax.rsqrt(var + LN_EPS) * g + b


def _mix_ln_kernel(*refs, n_parts):
    a_refs, w_refs = refs[:n_parts], refs[n_parts:2 * n_parts]
    x_ref, g_ref, b_ref, wr_ref, br_ref, x1_ref, gate_ref, idx_ref = refs[2 * n_parts:]
    tm = x_ref.shape[0]
    sub = tm // 2 if tm % 16 == 0 else tm
    for r0 in range(0, tm, sub):
        rows = slice(r0, r0 + sub)
        mix = jnp.dot(a_refs[0][rows, :].astype(bf16), w_refs[0][...], preferred_element_type=f32)
        for a_ref, w_ref in zip(a_refs[1:], w_refs[1:]):
            mix = mix + jnp.dot(a_ref[rows, :].astype(bf16), w_ref[...], preferred_element_type=f32)
        x1 = _layer_norm_rows(DN_ALPHA * x_ref[rows, :] + mix, g_ref[...], b_ref[...])
        x1_ref[rows, :] = x1
        lg = jnp.dot(x1, wr_ref[...], preferred_element_type=f32, precision=lax.Precision.HIGHEST) + br_ref[...]
        lane = lax.broadcasted_iota(i32, lg.shape, 1)
        vals, idxs = [], []
        for _ in range(TOP_K):
            m = jnp.max(lg, axis=-1, keepdims=True)
            i = jnp.min(jnp.where(lg == m, lane, N_EXP), axis=-1, keepdims=True)
            vals.append(m)
            idxs.append(i)
            lg = jnp.where(lane == i, -jnp.inf, lg)
        es = [jnp.exp(v - vals[0]) for v in vals]
        tot = es[0] + es[1] + es[2] + es[3]
        col = lax.broadcasted_iota(i32, (sub, TOP_K), 1)
        gate = jnp.zeros((sub, TOP_K), f32)
        idx = jnp.zeros((sub, TOP_K), i32)
        for k in range(TOP_K):
            gate = jnp.where(col == k, es[k] / tot, gate)
            idx = jnp.where(col == k, idxs[k], idx)
        gate_ref[rows, :] = gate
        idx_ref[rows, :] = idx


def _mix_ln(parts, x, g, b, w_r, b_r, tm=ROW_TILE):
    m, d = x.shape
    row = lambda i: (i, 0)
    fix = lambda i: (0, 0)
    in_specs = [pl.BlockSpec((tm, a.shape[1]), row) for a, _ in parts]
    in_specs += [pl.BlockSpec(w.shape, fix) for _, w in parts]
    in_specs += [pl.BlockSpec((tm, d), row), pl.BlockSpec((1, d), fix), pl.BlockSpec((1, d), fix),
                 pl.BlockSpec((d, N_EXP), fix), pl.BlockSpec((1, N_EXP), fix)]
    return pl.pallas_call(
        functools.partial(_mix_ln_kernel, n_parts=len(parts)),
        grid=(m // tm,),
        in_specs=in_specs,
        out_specs=[pl.BlockSpec((tm, d), row), pl.BlockSpec((tm, TOP_K), row), pl.BlockSpec((tm, TOP_K), row)],
        out_shape=[jax.ShapeDtypeStruct((m, d), f32), jax.ShapeDtypeStruct((m, TOP_K), f32),
                   jax.ShapeDtypeStruct((m, TOP_K), i32)],
        compiler_params=_cparams(("parallel",)),
        name="mix_ln",
    )(*[a for a, _ in parts], *[w for _, w in parts], x, g.reshape(1, d), b.reshape(1, d), w_r,
      b_r.reshape(1, N_EXP))


def _moe_kernel(be_ref, nreal_ref, x_ref, w1_ref, b1_ref, w2_ref, b2_ref, o_ref, w1s, w2s):
    i = pl.program_id(0)
    real = i < nreal_ref[0]
    prev = be_ref[jnp.maximum(i - 1, 0)]
    fresh = jnp.logical_or(i == 0, be_ref[i] != prev)

    @pl.when(jnp.logical_and(real, fresh))
    def _():
        w1s[...] = w1_ref[0, 0].astype(bf16)
        w2s[...] = w2_ref[0, 0].astype(bf16)

    @pl.when(real)
    def _():
        x = x_ref[...].astype(bf16)
        acc = jnp.zeros((MOE_ROWS, D_MODEL), f32)
        for c in range(D_FF // FF_CHUNK):
            lo = c * FF_CHUNK
            hg = jnp.dot(x, w1s[:, lo:lo + FF_CHUNK], preferred_element_type=f32)
            hg = hg + b1_ref[0, 0, :, lo:lo + FF_CHUNK]
            hu = jnp.dot(x, w1s[:, D_FF + lo:D_FF + lo + FF_CHUNK], preferred_element_type=f32)
            hu = hu + b1_ref[0, 0, :, D_FF + lo:D_FF + lo + FF_CHUNK]
            g = jnp.minimum(hg, SWIGLU_LIMIT)
            u = jnp.clip(hu, -SWIGLU_LIMIT, SWIGLU_LIMIT)
            a = (u + 1.0) * g * jax.nn.sigmoid(SWIGLU_ALPHA * g)
            acc = acc + jnp.dot(a.astype(bf16), w2s[lo:lo + FF_CHUNK, :], preferred_element_type=f32)
        o_ref[...] = acc + b2_ref[0, 0]

    @pl.when(jnp.logical_not(real))
    def _():
        o_ref[...] = jnp.zeros_like(o_ref)


def _moe_ffn_blocks(xb, blk_exp, nreal, w1, b1, w2, b2, layer):
    p, d = xb.shape
    nblk = p // MOE_ROWS
    n_layers = w1.shape[0]
    per_expert = lambda i, be, nr: (layer, be[i], 0, 0)
    grid_spec = pltpu.PrefetchScalarGridSpec(
        num_scalar_prefetch=2,
        grid=(nblk,),
        in_specs=[
            pl.BlockSpec((MOE_ROWS, d), lambda i, be, nr: (i, 0)),
            pl.BlockSpec((1, 1, d, 2 * D_FF), per_expert),
            pl.BlockSpec((1, 1, 1, 2 * D_FF), per_expert),
            pl.BlockSpec((1, 1, D_FF, d), per_expert),
            pl.BlockSpec((1, 1, 1, d), per_expert),
        ],
        out_specs=pl.BlockSpec((MOE_ROWS, d), lambda i, be, nr: (i, 0)),
        scratch_shapes=[pltpu.VMEM((d, 2 * D_FF), bf16), pltpu.VMEM((D_FF, d), bf16)],
    )
    return pl.pallas_call(
        _moe_kernel,
        grid_spec=grid_spec,
        out_shape=jax.ShapeDtypeStruct((p, d), f32),
        compiler_params=_cparams(("arbitrary",)),
        name="moe_ffn",
    )(blk_exp, nreal, xb, w1, b1.reshape(n_layers, N_EXP, 1, 2 * D_FF), w2, b2.reshape(n_layers, N_EXP, 1, d))


def _row_copy(src, src_row, dst, dst_row, sem):
    return pltpu.make_async_copy(src.at[pl.ds(src_row, 1), :], dst.at[pl.ds(dst_row, 1), :], sem)


def _moe_dispatch_kernel(dest_ref, x_ref, xb_init_ref, xb_ref, sem):
    del xb_init_ref

    def issue(t, c):
        for k in range(TOP_K):
            _row_copy(x_ref, t, xb_ref, dest_ref[0, 0, t * TOP_K + k], sem).start(priority=k % 2)
        return c

    lax.fori_loop(0, TOK_TILE, issue, 0, unroll=DMA_UNROLL)

    def drain(t, c):
        for k in range(TOP_K):
            _row_copy(x_ref, 0, xb_ref, 0, sem).wait()
        return c

    lax.fori_loop(0, TOK_TILE, drain, 0, unroll=DMA_UNROLL)


def _moe_dispatch(x1, dest, xb):
    n, d = x1.shape
    nt = n // TOK_TILE
    return pl.pallas_call(
        _moe_dispatch_kernel,
        grid=(nt,),
        in_specs=[pl.BlockSpec((1, 1, TOK_TILE * TOP_K), lambda i: (i, 0, 0), memory_space=pltpu.SMEM),
                  pl.BlockSpec((TOK_TILE, d), lambda i: (i, 0)),
                  pl.BlockSpec(memory_space=pl.ANY)],
        out_specs=pl.BlockSpec(memory_space=pl.ANY),
        out_shape=jax.ShapeDtypeStruct(xb.shape, f32),
        scratch_shapes=[pltpu.SemaphoreType.DMA(())],
        input_output_aliases={2: 0},
        compiler_params=_cparams(("arbitrary",)),
        name="moe_dispatch",
    )(dest.reshape(nt, 1, TOK_TILE * TOP_K), x1, xb)


def _moe_combine_kernel(dest_ref, yb_ref, gate_ref, x1_ref, g_ref, b_ref, o_ref, ybuf, sem):
    def issue(t, c):
        for k in range(TOP_K):
            _row_copy(yb_ref, dest_ref[0, 0, t * TOP_K + k], ybuf.at[k], t, sem).start(priority=k % 2)
        return c

    lax.fori_loop(0, TOK_TILE, issue, 0, unroll=DMA_UNROLL)

    def drain(t, c):
        for k in range(TOP_K):
            _row_copy(yb_ref, 0, ybuf.at[k], 0, sem).wait()
        return c

    lax.fori_loop(0, TOK_TILE, drain, 0, unroll=DMA_UNROLL)
    gate = gate_ref[...]
    y = gate[:, 0:1] * ybuf[0]
    for k in range(1, TOP_K):
        y = y + gate[:, k:k + 1] * ybuf[k]
    o_ref[...] = _layer_norm_rows(DN_ALPHA * x1_ref[...] + y, g_ref[...], b_ref[...])


def _moe_combine(yb, dest, gate, x1, g, b):
    n, d = x1.shape
    nt = n // TOK_TILE
    row = lambda i: (i, 0)
    fix = lambda i: (0, 0)
    return pl.pallas_call(
        _moe_combine_kernel,
        grid=(nt,),
        in_specs=[pl.BlockSpec((1, 1, TOK_TILE * TOP_K), lambda i: (i, 0, 0), memory_space=pltpu.SMEM),
                  pl.BlockSpec(memory_space=pl.ANY),
                  pl.BlockSpec((TOK_TILE, TOP_K), row), pl.BlockSpec((TOK_TILE, d), row),
                  pl.BlockSpec((1, d), fix), pl.BlockSpec((1, d), fix)],
        out_specs=pl.BlockSpec((TOK_TILE, d), row),
        out_shape=jax.ShapeDtypeStruct((n, d), f32),
        scratch_shapes=[pltpu.VMEM((TOP_K, TOK_TILE, d), f32), pltpu.SemaphoreType.DMA(())],
        compiler_params=_cparams(("arbitrary",)),
        name="moe_combine",
    )(dest.reshape(nt, 1, TOK_TILE * TOP_K), yb, gate, x1, g.reshape(1, d), b.reshape(1, d))


def _rank_kernel(e_ref, rank_ref, cnt_ref, carry, upper):
    @pl.when(pl.program_id(0) == 0)
    def _():
        carry[...] = jnp.zeros(carry.shape, f32)
        upper[...] = (lax.broadcasted_iota(i32, (RANK_TILE, RANK_TILE), 0)
                      < lax.broadcasted_iota(i32, (RANK_TILE, RANK_TILE), 1)).astype(bf16)

    e = e_ref[0]
    onehot = (lax.broadcasted_iota(i32, (N_EXP, RANK_TILE), 0) == e).astype(bf16)
    before = jnp.dot(onehot, upper[...], preferred_element_type=f32) + carry[:, 0:1]
    hot = onehot.astype(f32)
    rank_ref[0] = jnp.sum(hot * before, axis=0, keepdims=True).astype(i32)
    carry[...] = carry[...] + jnp.sum(hot, axis=1, keepdims=True)
    cnt_ref[...] = carry[...].astype(i32)


def _expert_ranks(flat_e):
    a = flat_e.shape[0]
    nt = a // RANK_TILE
    rank, cnt = pl.pallas_call(
        _rank_kernel,
        grid=(nt,),
        in_specs=[pl.BlockSpec((1, 1, RANK_TILE), lambda i: (i, 0, 0))],
        out_specs=[pl.BlockSpec((1, 1, RANK_TILE), lambda i: (i, 0, 0)),
                   pl.BlockSpec((N_EXP, LANES), lambda i: (0, 0))],
        out_shape=[jax.ShapeDtypeStruct((nt, 1, RANK_TILE), i32), jax.ShapeDtypeStruct((N_EXP, LANES), i32)],
        scratch_shapes=[pltpu.VMEM((N_EXP, LANES), f32), pltpu.VMEM((RANK_TILE, RANK_TILE), bf16)],
        compiler_params=_cparams(("arbitrary",)),
        name="expert_ranks",
    )(flat_e.reshape(nt, 1, RANK_TILE))
    return rank.reshape(a), cnt[:, 0]


def _moe_blocks(a):
    return -(-a // MOE_ROWS) + N_EXP


def _moe_slots(flat_e):
    a = flat_e.shape[0]
    rank, counts = _expert_ranks(flat_e)
    padded = (counts + MOE_ROWS - 1) // MOE_ROWS * MOE_ROWS
    eidx = jnp.arange(N_EXP, dtype=i32)
    pend = jnp.sum(jnp.where(eidx[None, :] <= eidx[:, None], padded[None, :], 0), axis=1)
    dest = (pend - padded)[flat_e] + rank
    blk_start = jnp.arange(_moe_blocks(a), dtype=i32) * MOE_ROWS
    nreal = (pend[-1] // MOE_ROWS).astype(i32)
    blk_exp = jnp.sum((pend[None, :] <= blk_start[:, None]).astype(i32), axis=1)
    last_exp = blk_exp[jnp.maximum(nreal - 1, 0)]
    blk_exp = jnp.where(blk_start < pend[-1], jnp.minimum(blk_exp, N_EXP - 1), last_exp)
    return dest, blk_exp, nreal.reshape(1)


def _mem_attn_kernel(q_ref, kv_ref, o_ref, *, feat_major):
    bt, tq, _ = q_ref.shape
    lane = lax.broadcasted_iota(i32, (tq, LANES), 1)
    zero = jnp.zeros((tq, LANES), bf16)

    def kv(b, lo):
        blk = kv_ref[0, b, lo:lo + LANES, :] if feat_major else kv_ref[b, :, lo:lo + LANES]
        return blk.astype(bf16)

    for pair in range(MEM_HEADS // 2):
        cols = slice(LANES * pair, LANES * (pair + 1))
        qts = [q_ref[b, :, cols] for b in range(bt)]
        kts = [kv(b, LANES * pair) for b in range(bt)]
        vts = [kv(b, MEM_W + LANES * pair) for b in range(bt)]
        halves = []
        for h in range(2):
            keep = (lane < MEM_HD) if h == 0 else (lane >= MEM_HD)
            qms = [jnp.where(keep, qt, zero) for qt in qts]
            if feat_major:
                ss = [jnp.dot(qm, kt, preferred_element_type=f32) for qm, kt in zip(qms, kts)]
            else:
                ss = [lax.dot_general(qm, kt, _NT, preferred_element_type=f32) for qm, kt in zip(qms, kts)]
            es = [jnp.exp(s - jnp.max(s, axis=-1, keepdims=True)) for s in ss]
            ps = [(e * (1.0 / jnp.sum(e, axis=-1, keepdims=True))).astype(bf16) for e in es]
            if feat_major:
                halves.append([lax.dot_general(p, vt, _NT, preferred_element_type=f32) for p, vt in zip(ps, vts)])
            else:
                halves.append([jnp.dot(p, vt, preferred_element_type=f32) for p, vt in zip(ps, vts)])
        for b in range(bt):
            o_ref[b, :, cols] = jnp.where(lane < MEM_HD, halves[0][b], halves[1][b])


def _mem_attn(mq, mem_kv, layer=None):
    b, t, w = mq.shape
    tq = min(t, ROW_TILE)
    bt = max(1, min(b, MEM_ROWS // tq))
    if layer is None:
        kv_spec = pl.BlockSpec((bt, MEM_TOK, 2 * w), lambda bb, i: (bb, 0, 0))
    else:
        kv_spec = pl.BlockSpec((1, bt, 2 * w, MEM_TOK), lambda bb, i: (layer, bb, 0, 0))
    return pl.pallas_call(
        functools.partial(_mem_attn_kernel, feat_major=layer is not None),
        grid=(b // bt, t // tq),
        in_specs=[pl.BlockSpec((bt, tq, w), lambda bb, i: (bb, i, 0)), kv_spec],
        out_specs=pl.BlockSpec((bt, tq, w), lambda bb, i: (bb, i, 0)),
        out_shape=jax.ShapeDtypeStruct((b, t, w), f32),
        compiler_params=_cparams(("parallel", "parallel")),
        name="mem_attn",
    )(mq, mem_kv)


def _pad_state(s):
    s = jnp.concatenate([s, jnp.zeros((GLA_KP - GLA_DK, GLA_DV), f32)], axis=0)
    return jnp.concatenate([s, jnp.zeros((GLA_KP, GLA_VP - GLA_DV), f32)], axis=1)


def _gla_kernel(q_ref, k_ref, v_ref, r_ref, glr_ref, wg_ref, bg_ref, gn_ref, s0_ref, o_ref, s_out_ref, st_sc,
                *, bt, chunk, zero_init):
    c = pl.program_id(1)

    @pl.when(c == 0)
    def _():
        if zero_init:
            st_sc[...] = jnp.zeros(st_sc.shape, f32)
        else:
            for b in range(bt):
                for h in range(GLA_HEADS):
                    st_sc[b, h] = jnp.transpose(_pad_state(s0_ref[b, h]))

    tri = lax.broadcasted_iota(i32, (chunk, chunk), 0) >= lax.broadcasted_iota(i32, (chunk, chunk), 1)
    trif = tri.astype(f32)
    problems = [(b, h) for b in range(bt) for h in range(GLA_HEADS)]
    ksl = lambda h: slice(GLA_KP * h, GLA_KP * (h + 1))
    vsl = lambda h: slice(GLA_VP * h, GLA_VP * (h + 1))
    qts, kts, kds, ebl = [], [], [], []
    for b in range(bt):
        z = jnp.dot(glr_ref[b].astype(bf16), wg_ref[...], preferred_element_type=f32) + bg_ref[...]
        log_a = jax.nn.log_sigmoid(z) / GLA_TAU
        bc = jnp.dot(trif, log_a, preferred_element_type=f32, precision=lax.Precision.HIGHEST)
        bl = bc[chunk - 1:chunk, :]
        k = k_ref[b]
        qts.append((q_ref[b] * (GLA_DK ** -0.5) * jnp.exp(bc)).astype(bf16))
        kts.append((k * jnp.exp(-bc)).astype(bf16))
        kds.append((k * jnp.exp(bl - bc)).astype(bf16))
        ebl.append(jnp.exp(bl))
    vbs = [v_ref[b].astype(bf16) for b in range(bt)]
    atts = [jnp.where(tri, lax.dot_general(qts[b][:, ksl(h)], kts[b][:, ksl(h)], _NT, preferred_element_type=f32),
                      0.0).astype(bf16) for b, h in problems]
    sts = [st_sc[b, h] for b, h in problems]
    outs = [lax.dot_general(qts[b][:, ksl(h)], st.astype(bf16), _NT, preferred_element_type=f32)
            + jnp.dot(att, vbs[b][:, vsl(h)], preferred_element_type=f32)
            for (b, h), st, att in zip(problems, sts, atts)]
    for (b, h), st in zip(problems, sts):
        st_sc[b, h] = ebl[b][:, ksl(h)] * st + lax.dot_general(vbs[b][:, vsl(h)], kds[b][:, ksl(h)], _TN,
                                                              preferred_element_type=f32)
    for (b, h), o in zip(problems, outs):
        ms = jnp.sum(o * o, axis=-1, keepdims=True) * (1.0 / GLA_DV)
        o = o * lax.rsqrt(ms + LN_EPS) * gn_ref[:, vsl(h)]
        o_ref[b, :, vsl(h)] = o * jax.nn.silu(r_ref[b, :, vsl(h)])

    @pl.when(c == pl.num_programs(1) - 1)
    def _():
        for b in range(bt):
            for h in range(GLA_HEADS):
                s_out_ref[b, h] = jnp.transpose(st_sc[b, h])[:GLA_DK, :GLA_DV]


def _gla(q, k, v, r, glr, wg, bg, gn, s0, *, bt):
    nb, t, _ = q.shape
    chunk = GLA_CHUNK if t % GLA_CHUNK == 0 else t
    zero_init = s0 is None
    if zero_init:
        s0 = jnp.zeros((1, 1, 8, LANES), f32)
    tok = lambda w: pl.BlockSpec((bt, chunk, w), lambda i, c: (i, c, 0))
    fix = lambda a: pl.BlockSpec(a.shape, lambda i, c: (0,) * a.ndim)
    state = pl.BlockSpec((bt, GLA_HEADS, GLA_DK, GLA_DV), lambda i, c: (i, 0, 0, 0))
    return pl.pallas_call(
        functools.partial(_gla_kernel, bt=bt, chunk=chunk, zero_init=zero_init),
        grid=(nb // bt, t // chunk),
        in_specs=[tok(GLA_HEADS * GLA_KP), tok(GLA_HEADS * GLA_KP), tok(GLA_HEADS * GLA_VP), tok(GLA_HEADS * GLA_VP),
                  tok(LANES), fix(wg), fix(bg), fix(gn), fix(s0) if zero_init else state],
        out_specs=[tok(GLA_HEADS * GLA_VP), state],
        out_shape=[jax.ShapeDtypeStruct((nb, t, GLA_HEADS * GLA_VP), f32),
                   jax.ShapeDtypeStruct((nb, GLA_HEADS, GLA_DK, GLA_DV), f32)],
        scratch_shapes=[pltpu.VMEM((bt, GLA_HEADS, GLA_VP, GLA_KP), f32)],
        compiler_params=_cparams(("parallel", "arbitrary")),
        name="gla",
    )(q, k, v, r, glr, wg, bg, gn, s0)


def _pad_heads(a, w, wp):
    lead = a.shape[:-1]
    a = jnp.pad(a.reshape(*lead, GLA_HEADS, w), [(0, 0)] * (len(lead) + 1) + [(0, wp - w)])
    return a.reshape(*lead, GLA_HEADS * wp)


def _softmax_rows(sm, maskf):
    m = jnp.max(sm, axis=-1, keepdims=True)
    e = jnp.exp(sm - m)
    return e * (1.0 / jnp.sum(e, axis=-1, keepdims=True)) * maskf


def _block_scores_t(imp):
    nc = imp.shape[1]
    jj = lax.broadcasted_iota(i32, (NSB_PAD, nc), 0)
    nn = lax.broadcasted_iota(i32, (NSB_PAD, nc), 1)
    mt = ((nn >> RATIO_SHIFT) == jj).astype(f32) + (((nn + 1) >> RATIO_SHIFT) == jj).astype(f32)
    return lax.dot_general(mt, imp, _NT, preferred_element_type=f32, precision=lax.Precision.HIGHEST)


def _select_t(blk_t, qpos_t):
    nq = blk_t.shape[1]
    j_t = lax.broadcasted_iota(i32, (NSB_PAD, nq), 0)
    valid = j_t * SLC_L <= qpos_t
    cur = qpos_t >> SLC_SHIFT
    forced = (j_t == 0) | (j_t == cur) | (j_t == cur - 1)
    score = jnp.where(valid, jnp.where(forced, 1e9, blk_t), -1e9)
    cnt = jnp.zeros((NSB_PAD, nq), i32)
    for k in range(NSB_PAD):
        row = score[k:k + 1, :]
        beats = (row > score) | ((row == score) & (j_t > k))
        cnt = cnt + beats.astype(i32)
    return ((cnt < SLC_K) & valid).astype(f32)


def _untranspose(sel_t):
    nq = sel_t.shape[1]
    eye = lax.broadcasted_iota(i32, (nq, nq), 0) == lax.broadcasted_iota(i32, (nq, nq), 1)
    return lax.dot_general(eye.astype(bf16), sel_t.astype(bf16), _NT, preferred_element_type=f32).astype(bf16)


def _nsa_prompt_kernel(qp_ref, gl_ref, bg_ref, kc_ref, vc_ref, ks_ref, vs_ref, kw_ref, vw_ref, o_ref,
                       p_sc, m_sc, l_sc, acc_sc, og_sc, used_sm):
    nc = kc_ref.shape[1]
    q0 = pl.program_id(1) * Q_BLK
    qpos_i = q0 + lax.broadcasted_iota(i32, (Q_BLK, 1), 0)
    lane = lax.broadcasted_iota(i32, (Q_BLK, LANES), 1)
    gates = jax.nn.sigmoid(gl_ref[0] + bg_ref[...])
    qp = qp_ref[0]
    zero_b = jnp.zeros((Q_BLK, LANES), bf16)

    for g in range(NSA_KV):
        half = (lane < HD) if g == 0 else (lane >= HD)
        qg = jnp.concatenate([jnp.where(half, qp[:, LANES * hh:LANES * (hh + 1)], zero_b)
                              for hh in range(NSA_HPG)], axis=0)
        slopes = [SLOPES[NSA_HPG * g + hh] for hh in range(NSA_HPG)]

        n_idx = lax.broadcasted_iota(i32, (Q_BLK, nc), 1)
        cd = (qpos_i - (n_idx * CMP_S + (CMP_L - 1))).astype(f32)
        cmask = cd >= 0
        cmaskf = cmask.astype(f32)
        s = lax.dot_general(qg, kc_ref[0], _NT, preferred_element_type=f32)
        imp = jnp.zeros((Q_BLK, nc), f32)
        for hh in range(NSA_HPG):
            rows = slice(hh * Q_BLK, (hh + 1) * Q_BLK)
            p = _softmax_rows(jnp.where(cmask, s[rows] - slopes[hh] * cd, NEG), cmaskf)
            imp = imp + p
            p_sc[rows, :nc] = p.astype(bf16)
        o_cmp = jnp.dot(p_sc[:, :nc], vc_ref[0], preferred_element_type=f32)

        qpos_t = q0 + lax.broadcasted_iota(i32, (NSB_PAD, Q_BLK), 1)
        sel_t = _select_t(_block_scores_t(imp), qpos_t)
        sel_b = _untranspose(sel_t)
        blocks_per_tile = SLC_TILE // SLC_L
        for t in range(NSB_PAD // blocks_per_tile):
            used_sm[t] = (jnp.max(sel_t[t * blocks_per_tile:(t + 1) * blocks_per_tile, :]) > 0.5).astype(i32)

        m_sc[...] = jnp.full(m_sc.shape, NEG, f32)
        l_sc[...] = jnp.zeros(l_sc.shape, f32)
        acc_sc[...] = jnp.zeros(acc_sc.shape, f32)

        def tile(t, carry):
            @pl.when(used_sm[t] > 0)
            def _():
                k0 = pl.multiple_of(t * SLC_TILE, SLC_TILE)
                kt = ks_ref[0, pl.ds(k0, SLC_TILE), :]
                vt = vs_ref[0, pl.ds(k0, SLC_TILE), :]
                st = lax.dot_general(qg, kt, _NT, preferred_element_type=f32)
                kpos = k0 + lax.broadcasted_iota(i32, (Q_BLK, SLC_TILE), 1)
                blk_of_key = (k0 + lax.broadcasted_iota(i32, (NSB_PAD, SLC_TILE), 1)) >> SLC_SHIFT
                expand = (lax.broadcasted_iota(i32, (NSB_PAD, SLC_TILE), 0) == blk_of_key).astype(bf16)
                selexp = jnp.dot(sel_b, expand, preferred_element_type=f32)
                addmask = jnp.where((qpos_i >= kpos) & (selexp > 0.5), 0.0, NEG)
                krel = (k0 - q0 + lax.broadcasted_iota(i32, (1, SLC_TILE), 1)).astype(f32)
                for hh in range(NSA_HPG):
                    rows = slice(hh * Q_BLK, (hh + 1) * Q_BLK)
                    sm = st[rows] + slopes[hh] * krel + addmask
                    m_old = m_sc[rows]
                    m_new = jnp.maximum(m_old, jnp.max(sm, axis=-1, keepdims=True))
                    a = jnp.exp(m_old - m_new)
                    e = jnp.exp(sm - m_new)
                    l_sc[rows] = a * l_sc[rows] + jnp.sum(e, axis=-1, keepdims=True)
                    m_sc[rows] = m_new
                    acc_sc[rows] = a * acc_sc[rows]
                    p_sc[rows, :SLC_TILE] = e.astype(bf16)
                acc_sc[...] += jnp.dot(p_sc[:, :SLC_TILE], vt, preferred_element_type=f32)
            return carry

        lax.fori_loop(0, (q0 + Q_BLK + SLC_TILE - 1) // SLC_TILE, tile, 0)
        o_slc = acc_sc[...] / l_sc[...]

        start = pl.multiple_of(jnp.maximum(q0 - WINDOW, 0), Q_BLK)
        kt = kw_ref[0, pl.ds(start, WIN_KEYS), :]
        vt = vw_ref[0, pl.ds(start, WIN_KEYS), :]
        st = lax.dot_general(qg, kt, _NT, preferred_element_type=f32)
        wd = (qpos_i - (start + lax.broadcasted_iota(i32, (Q_BLK, WIN_KEYS), 1))).astype(f32)
        wmask = (wd >= 0) & (wd <= WINDOW)
        wmaskf = wmask.astype(f32)
        for hh in range(NSA_HPG):
            rows = slice(hh * Q_BLK, (hh + 1) * Q_BLK)
            p = _softmax_rows(jnp.where(wmask, st[rows] - slopes[hh] * wd, NEG), wmaskf)
            p_sc[rows, :WIN_KEYS] = p.astype(bf16)
        o_win = jnp.dot(p_sc[:, :WIN_KEYS], vt, preferred_element_type=f32)

        for hh in range(NSA_HPG):
            rows = slice(hh * Q_BLK, (hh + 1) * Q_BLK)
            c = 3 * (NSA_HPG * g + hh)
            og_sc[g, rows] = (gates[:, c:c + 1] * o_cmp[rows] + gates[:, c + 1:c + 2] * o_slc[rows]
                              + gates[:, c + 2:c + 3] * o_win[rows])

    for hh in range(NSA_HPG):
        rows = slice(hh * Q_BLK, (hh + 1) * Q_BLK)
        o_ref[0, :, LANES * hh:LANES * (hh + 1)] = jnp.where(lane < HD, og_sc[0, rows], og_sc[1, rows])


def _nsa_prompt_call(qp, gl, bg, kc, vc, ks, vs, kw, vw):
    b, t, w = qp.shape
    nc = kc.shape[1]
    assert t % SLC_TILE == 0 and t >= WIN_KEYS and nc % LANES == 0 and t // SLC_L <= NSB_PAD
    rows = NSA_HPG * Q_BLK
    qblk = lambda bb, i: (bb, i, 0)
    whole = lambda bb, i: (bb, 0, 0)
    return pl.pallas_call(
        _nsa_prompt_kernel,
        grid=(b, t // Q_BLK),
        in_specs=[pl.BlockSpec((1, Q_BLK, w), qblk), pl.BlockSpec((1, Q_BLK, LANES), qblk),
                  pl.BlockSpec((1, LANES), lambda bb, i: (0, 0)),
                  pl.BlockSpec((1, nc, LANES), whole), pl.BlockSpec((1, nc, LANES), whole),
                  pl.BlockSpec((1, t, LANES), whole), pl.BlockSpec((1, t, LANES), whole),
                  pl.BlockSpec((1, t, LANES), whole), pl.BlockSpec((1, t, LANES), whole)],
        out_specs=pl.BlockSpec((1, Q_BLK, w), qblk),
        out_shape=jax.ShapeDtypeStruct((b, t, w), f32),
        scratch_shapes=[pltpu.VMEM((rows, max(WIN_KEYS, SLC_TILE)), bf16), pltpu.VMEM((rows, 1), f32),
                        pltpu.VMEM((rows, 1), f32),
                        pltpu.VMEM((rows, LANES), f32), pltpu.VMEM((NSA_KV, rows, LANES), f32),
                        pltpu.SMEM((NSB_PAD * SLC_L // SLC_TILE,), i32)],
        compiler_params=_cparams(("parallel", "parallel")),
        name="nsa_prompt",
    )(qp, gl, bg, kc, vc, ks, vs, kw, vw)


def _nsa_sample_kernel(pt_ref, qp_ref, gl_ref, bg_ref, kc_ref, vc_ref, kn_ref, vn_ref, wst_ref, wkn_ref, wvn_ref,
                       *rest, n_pages, past_len, bt):
    page_refs = rest[:n_pages * bt]
    o_ref, k_sc, v_sc = rest[n_pages * bt:]
    nq = qp_ref.shape[1]
    nc = kc_ref.shape[1]
    wb = wst_ref.shape[1]
    nkeys = (n_pages + 1) * PAGE_SIZE
    rows_g = NSA_HPG * nq
    qpos_i = past_len + lax.broadcasted_iota(i32, (nq, 1), 0)
    lane = lax.broadcasted_iota(i32, (nq, LANES), 1)
    zero_b = jnp.zeros((nq, LANES), bf16)
    tail = jnp.zeros((PAGE_SIZE - nq, LANES), bf16)

    def rows_of(g, hh):
        r0 = g * rows_g + hh * nq
        return slice(r0, r0 + nq)

    n_idx = lax.broadcasted_iota(i32, (nq, nc), 1)
    cd = (qpos_i - (n_idx * CMP_S + (CMP_L - 1))).astype(f32)
    cmask = cd >= 0
    cmaskf = cmask.astype(f32)
    qpos_t = past_len + lax.broadcasted_iota(i32, (NSB_PAD, nq), 1)
    blk_of_key = lax.broadcasted_iota(i32, (NSB_PAD, nkeys), 1) >> SLC_SHIFT
    expand = (lax.broadcasted_iota(i32, (NSB_PAD, nkeys), 0) == blk_of_key).astype(bf16)
    sd = (qpos_i - lax.broadcasted_iota(i32, (nq, nkeys), 1)).astype(f32)
    wpos = (past_len - wb) + lax.broadcasted_iota(i32, (nq, wb + PAGE_SIZE), 1)
    wd = (qpos_i - wpos).astype(f32)
    wmask = (wd >= 0) & (wd <= WINDOW) & (wpos >= 0)
    wmaskf = wmask.astype(f32)

    def sequence(j):
        gates = jax.nn.sigmoid(gl_ref[j] + bg_ref[...])
        qp = qp_ref[j]
        qs = jnp.concatenate([jnp.where((lane < HD) if g == 0 else (lane >= HD),
                                        qp[:, LANES * hh:LANES * (hh + 1)], zero_b)
                              for g in range(NSA_KV) for hh in range(NSA_HPG)], axis=0)

        s = lax.dot_general(qs, kc_ref[j], _NT, preferred_element_type=f32)
        yield
        imps, ps = [], []
        for g in range(NSA_KV):
            imp = jnp.zeros((nq, nc), f32)
            for hh in range(NSA_HPG):
                p = _softmax_rows(jnp.where(cmask, s[rows_of(g, hh)] - SLOPES[NSA_HPG * g + hh] * cd, NEG), cmaskf)
                imp = imp + p
                ps.append(p)
            imps.append(imp)
        o_cmp = jnp.dot(jnp.concatenate(ps, axis=0).astype(bf16), vc_ref[j], preferred_element_type=f32)
        yield

        masks = []
        for g in range(NSA_KV):
            sel_b = _untranspose(_select_t(_block_scores_t(imps[g]), qpos_t))
            selexp = jnp.dot(sel_b, expand, preferred_element_type=f32)
            masks.append((sd >= 0) & (selexp > 0.5))
        yield

        for pg in range(n_pages):
            blk = page_refs[j * n_pages + pg][0]
            k_sc[j, :, pg * PAGE_SIZE:(pg + 1) * PAGE_SIZE] = blk[:LANES].astype(bf16)
            v_sc[j, :, pg * PAGE_SIZE:(pg + 1) * PAGE_SIZE] = blk[LANES:].astype(bf16)
        new_t = lambda ref: jnp.transpose(jnp.concatenate([ref[j], tail], axis=0).astype(f32)).astype(bf16)
        k_sc[j, :, n_pages * PAGE_SIZE:] = new_t(kn_ref)
        v_sc[j, :, n_pages * PAGE_SIZE:] = new_t(vn_ref)
        s = jnp.dot(qs, k_sc[j], preferred_element_type=f32)
        yield
        ps = []
        for g in range(NSA_KV):
            mf = masks[g].astype(f32)
            for hh in range(NSA_HPG):
                ps.append(_softmax_rows(jnp.where(masks[g], s[rows_of(g, hh)] - SLOPES[NSA_HPG * g + hh] * sd, NEG),
                                        mf))
        o_slc = lax.dot_general(jnp.concatenate(ps, axis=0).astype(bf16), v_sc[j], _NT,
                                preferred_element_type=f32)
        yield

        wk = jnp.concatenate([wst_ref[j, :, :LANES].astype(bf16), wkn_ref[j], tail], axis=0)
        wv = jnp.concatenate([wst_ref[j, :, LANES:].astype(bf16), wvn_ref[j], tail], axis=0)
        s = lax.dot_general(qs, wk, _NT, preferred_element_type=f32)
        yield
        ps = []
        for g in range(NSA_KV):
            for hh in range(NSA_HPG):
                ps.append(_softmax_rows(jnp.where(wmask, s[rows_of(g, hh)] - SLOPES[NSA_HPG * g + hh] * wd, NEG),
                                        wmaskf))
        o_win = jnp.dot(jnp.concatenate(ps, axis=0).astype(bf16), wv, preferred_element_type=f32)
        yield

        for hh in range(NSA_HPG):
            halves = []
            for g in range(NSA_KV):
                r = rows_of(g, hh)
                c = 3 * (NSA_HPG * g + hh)
                halves.append(gates[:, c:c + 1] * o_cmp[r] + gates[:, c + 1:c + 2] * o_slc[r]
                              + gates[:, c + 2:c + 3] * o_win[r])
            o_ref[j, :, LANES * hh:LANES * (hh + 1)] = jnp.where(lane < HD, halves[0], halves[1])

    running = [sequence(j) for j in range(bt)]
    done = object()
    while running:
        running = [seq for seq in running if next(seq, done) is not done]


def _nsa_sample_call(page_table, qp, gl, bg, kc, vc, kn, vn, wst, wkn, wvn, cache):
    b, nq, w = qp.shape
    n_pages = page_table.shape[1]
    nc = kc.shape[1]
    wb = wst.shape[1]
    nkeys = (n_pages + 1) * PAGE_SIZE
    assert nkeys // SLC_L <= NSB_PAD and nq <= PAGE_SIZE
    bt = NSA_SAMPLE_BT if b % NSA_SAMPLE_BT == 0 else 1
    per_b = lambda bb, pt: (bb, 0, 0)
    in_specs = [pl.BlockSpec((bt, nq, w), per_b), pl.BlockSpec((bt, nq, LANES), per_b),
                pl.BlockSpec((1, LANES), lambda bb, pt: (0, 0)),
                pl.BlockSpec((bt, nc, LANES), per_b), pl.BlockSpec((bt, nc, LANES), per_b),
                pl.BlockSpec((bt, nq, LANES), per_b), pl.BlockSpec((bt, nq, LANES), per_b),
                pl.BlockSpec((bt, wb, 2 * LANES), per_b),
                pl.BlockSpec((bt, nq, LANES), per_b), pl.BlockSpec((bt, nq, LANES), per_b)]
    for j in range(bt):
        for pg in range(n_pages):
            in_specs.append(pl.BlockSpec(
                (1, 2 * LANES, PAGE_SIZE),
                functools.partial(lambda bb, pt, j, pg: (pt[bb * bt + j, pg], 1, 0), j=j, pg=pg)))
    grid_spec = pltpu.PrefetchScalarGridSpec(
        num_scalar_prefetch=1, grid=(b // bt,), in_specs=in_specs,
        out_specs=pl.BlockSpec((bt, nq, w), per_b),
        scratch_shapes=[pltpu.VMEM((bt, LANES, nkeys), bf16), pltpu.VMEM((bt, LANES, nkeys), bf16)])
    return pl.pallas_call(
        functools.partial(_nsa_sample_kernel, n_pages=n_pages, past_len=n_pages * PAGE_SIZE, bt=bt),
        grid_spec=grid_spec,
        out_shape=jax.ShapeDtypeStruct((b, nq, w), f32),
        compiler_params=_cparams(("parallel",)),
        name="nsa_sample",
    )(page_table, qp, gl, bg, kc, vc, kn, vn, wst, wkn, wvn, *([cache] * (n_pages * bt)))


def _compress_kernel(x_ref, pos_ref, w1_ref, b1_ref, w2_ref, o_ref):
    x = x_ref[0, 0, 0]
    half = CMP_S * HD
    xa = (x + pos_ref[0, :, :half]).astype(bf16)
    xb = (x + pos_ref[0, :, half:]).astype(bf16)
    a = jnp.dot(xa, w1_ref[0, :half, :], preferred_element_type=f32)
    bsec = jnp.dot(xb, w1_ref[0, half:, :], preferred_element_type=f32)
    nch = x.shape[0]
    hid = jax.nn.gelu(a + pltpu.roll(bsec, nch - 1, 0) + b1_ref[0])
    o_ref[0, 0, 0] = jnp.dot(hid.astype(bf16), w2_ref[0], preferred_element_type=f32).astype(o_ref.dtype)


def _compress_call(x2, cmp_pos, w_cmp1, b_cmp1, w_cmp2):
    b, _, g, nch, cw = x2.shape
    hid = w_cmp1.shape[-1]
    pos = cmp_pos.reshape(2, 1, CMP_L * HD)
    return pl.pallas_call(
        _compress_kernel,
        grid=(b, 2, g),
        in_specs=[pl.BlockSpec((1, 1, 1, nch, cw), lambda bb, s, gg: (bb, s, gg, 0, 0)),
                  pl.BlockSpec((1, 1, CMP_L * HD), lambda bb, s, gg: (s, 0, 0)),
                  pl.BlockSpec((1, CMP_L * HD, hid), lambda bb, s, gg: (s, 0, 0)),
                  pl.BlockSpec((1, 1, hid), lambda bb, s, gg: (s, 0, 0)),
                  pl.BlockSpec((1, hid, HD), lambda bb, s, gg: (s, 0, 0))],
        out_specs=pl.BlockSpec((1, 1, 1, nch, HD), lambda bb, s, gg: (bb, s, gg, 0, 0)),
        out_shape=jax.ShapeDtypeStruct((b, 2, g, nch, HD), bf16),
        compiler_params=_cparams(("parallel", "parallel", "parallel")),
        name="compress",
    )(x2, pos, w_cmp1.astype(bf16), b_cmp1.reshape(2, 1, hid), w_cmp2.astype(bf16))


def _compressed_kv(tok, cmp_pos, w_cmp1, b_cmp1, w_cmp2):
    b, t = tok.shape[:2]
    nch = t // CMP_S
    x2 = jnp.transpose(tok[:, :nch * CMP_S], (0, 2, 3, 1, 4)).reshape(b, 2, NSA_KV, nch, CMP_S * HD)
    ck = _compress_call(x2, cmp_pos, w_cmp1, b_cmp1, w_cmp2)
    ck = jnp.transpose(ck, (0, 1, 3, 2, 4)).reshape(b, 2, nch, NSA_KV * HD)
    return ck[:, 0], ck[:, 1]


def _compress_paged_kernel(pt_ref, pos_ref, w1a_ref, w1b_ref, b1_ref, w2_ref, *rest, n_pages):
    page_refs = rest[:n_pages]
    o_ref, x_sc, xcat_sc = rest[n_pages:]
    nch = n_pages * (PAGE_SIZE // CMP_S)
    role = NSA_KV * HD
    for pg in range(n_pages):
        xt = jnp.transpose(page_refs[pg][0])
        for s in range(2):
            x_sc[s, pg * PAGE_SIZE:(pg + 1) * PAGE_SIZE, :] = xt[:, s * role:(s + 1) * role]
    for s in range(2):
        for t in range(CMP_S):
            xcat_sc[:, t * role:(t + 1) * role] = x_sc[s, pl.ds(t, nch, stride=CMP_S), :]
        xc = xcat_sc[...]
        xa = (xc + pos_ref[s, 0:1, :]).astype(bf16)
        xb = (xc + pos_ref[s, 1:2, :]).astype(bf16)
        a = jnp.dot(xa, w1a_ref[s], preferred_element_type=f32)
        bsec = jnp.dot(xb, w1b_ref[s], preferred_element_type=f32)
        hid = jax.nn.gelu(a + pltpu.roll(bsec, nch - 1, 0) + b1_ref[s])
        o_ref[0, s] = jnp.dot(hid.astype(bf16), w2_ref[s], preferred_element_type=f32).astype(o_ref.dtype)


def _compress_paged(page_table, cache_t, cmp_pos, w_cmp1, b_cmp1, w_cmp2):
    b, n_pages = page_table.shape
    nch = n_pages * (PAGE_SIZE // CMP_S)
    hid = w_cmp1.shape[-1]
    half = CMP_S * HD
    eye = jnp.eye(NSA_KV, dtype=f32)

    def widen_rows(w):
        w = w.reshape(2, CMP_S, HD, hid)
        return jnp.einsum('stdh,ag->stadgh', w, eye).reshape(2, CMP_S * NSA_KV * HD, NSA_KV * hid)

    w1a = widen_rows(w_cmp1[:, :half]).astype(bf16)
    w1b = widen_rows(w_cmp1[:, half:]).astype(bf16)
    w2 = jnp.einsum('shd,ag->sahgd', w_cmp2, eye).reshape(2, NSA_KV * hid, NSA_KV * HD).astype(bf16)
    b1 = jnp.tile(b_cmp1, (1, NSA_KV)).reshape(2, 1, NSA_KV * hid)
    pos = jnp.broadcast_to(cmp_pos.reshape(2, 2, CMP_S, 1, HD), (2, 2, CMP_S, NSA_KV, HD))
    pos = pos.reshape(2, 2, CMP_S * NSA_KV * HD)
    fix = lambda a: pl.BlockSpec(a.shape, lambda bb, pt: (0,) * a.ndim)
    in_specs = [fix(pos), fix(w1a), fix(w1b), fix(b1), fix(w2)]
    for pg in range(n_pages):
        in_specs.append(pl.BlockSpec((1, 2 * NSA_KV * HD, PAGE_SIZE),
                                     functools.partial(lambda bb, pt, pg: (pt[bb, pg], 0, 0), pg=pg)))
    grid_spec = pltpu.PrefetchScalarGridSpec(
        num_scalar_prefetch=1, grid=(b,), in_specs=in_specs,
        out_specs=pl.BlockSpec((1, 2, nch, NSA_KV * HD), lambda bb, pt: (bb, 0, 0, 0)),
        scratch_shapes=[pltpu.VMEM((2, n_pages * PAGE_SIZE, NSA_KV * HD), f32),
                        pltpu.VMEM((nch, CMP_S * NSA_KV * HD), f32)])
    return pl.pallas_call(
        functools.partial(_compress_paged_kernel, n_pages=n_pages),
        grid_spec=grid_spec,
        out_shape=jax.ShapeDtypeStruct((b, 2, nch, NSA_KV * HD), bf16),
        compiler_params=_cparams(("parallel",)),
        name="compress_paged",
    )(page_table, pos, w1a, w1b, b1, w2, *([cache_t] * n_pages))


def _pair_pack_cols(w):
    k = w.shape[0]
    return jnp.transpose(w.reshape(k, NSA_KV, NSA_HPG, HD), (0, 2, 1, 3)).reshape(k, NSA_W)


def _last_rows(a, n):
    t = a.shape[1]
    if t >= n:
        return a[:, t - n:]
    return jnp.pad(a, ((0, 0), (n - t, 0)) + ((0, 0),) * (a.ndim - 2))


def kernel(x_prompt, x_sample, mem_prompt, state_gla, cache_nsa_kv, state_win_kv, cache_mem_kv, page_table,
           w_in_a, w_gate_a, b_gate_a, gla_norm, w_in_b, b_gate_b, w_kv_b, cmp_pos, w_cmp1, b_cmp1, w_cmp2,
           w_mem_kv, w_out, ln1_g, ln1_b, ln2_g, ln2_b, w_router, b_router, w_e1, b_e1, w_e2, b_e2):
    bp, tp, d = x_prompt.shape
    bs, ts, _ = x_sample.shape
    n_p, n_s = bp * tp, bs * ts
    wb = min(WINDOW, state_win_kv.shape[1])
    n_pool = cache_nsa_kv.shape[0]
    past_len = page_table.shape[1] * PAGE_SIZE

    mem_rows = mem_prompt.reshape(bp * MEM_TOK, d)
    w_mem = [w_mem_kv[l].astype(bf16) for l in range(DEPTH)]
    mem_kv_l = _proj(mem_rows, w_mem, _whole(w_mem))
    mem_kv_prompt = jnp.stack(mem_kv_l).reshape(DEPTH, bp, MEM_TOK, 2, MEM_HEADS, MEM_HD)
    mem_p = [m.reshape(bp, MEM_TOK, 2 * MEM_W) for m in mem_kv_l]
    mem_s = jnp.transpose(cache_mem_kv, (0, 1, 3, 4, 5, 2)).reshape(DEPTH, bs, 2 * MEM_W, MEM_TOK)

    x_p, x_s = x_prompt.reshape(n_p, d), x_sample.reshape(n_s, d)
    as_p = lambda a: a.reshape(bp, tp, a.shape[-1])
    as_s = lambda a: a.reshape(bs, ts, a.shape[-1])

    hk = GLA_HEADS * GLA_DK
    cuts_a = [0, hk, 2 * hk, 2 * hk + GLA_W, 2 * hk + 2 * GLA_W, 2 * hk + 2 * GLA_W + GLA_RANK,
              2 * hk + 2 * GLA_W + GLA_RANK + MEM_W]
    mem_scale = MEM_HD ** -0.5
    q_scale = HD ** -0.5
    xb = jnp.zeros((_moe_blocks((n_p + n_s) * TOP_K) * MOE_ROWS, d), f32)

    gla_p, gla_s = [], []
    for l in range(DEPTH):
        if l < N_A:
            wq, wk, wv, wr, wgl, wm = [w_in_a[l][:, cuts_a[i]:cuts_a[i + 1]] for i in range(6)]
            ws = [_pad_heads(wq, GLA_DK, GLA_KP), _pad_heads(wk, GLA_DK, GLA_KP), _pad_heads(wv, GLA_DV, GLA_VP),
                  _pad_heads(wr, GLA_DV, GLA_VP), jnp.pad(wgl, ((0, 0), (0, LANES - GLA_RANK))), wm * mem_scale]
            ws = [w.astype(bf16) for w in ws]
            outs = _whole(ws)
            outs[5] = outs[5][:3] + (bf16,)
            wg = _pad_heads(jnp.pad(w_gate_a[l], ((0, LANES - GLA_RANK), (0, 0))), GLA_DK, GLA_KP).astype(bf16)
            bg = _pad_heads(b_gate_a[l], GLA_DK, GLA_KP).reshape(1, GLA_HEADS * GLA_KP)
            gn = _pad_heads(gla_norm[l], GLA_DV, GLA_VP).reshape(1, GLA_HEADS * GLA_VP)
            q, k, v, r, glr, mq_p = _proj(x_p, ws, outs)
            o_p, s_new = _gla(as_p(q), as_p(k), as_p(v), as_p(r), as_p(glr), wg, bg, gn, None, bt=min(bp, GLA_BT))
            gla_p.append(s_new)
            q, k, v, r, glr, mq_s = _proj(x_s, ws, outs)
            o_s, s_new = _gla(as_s(q), as_s(k), as_s(v), as_s(r), as_s(glr), wg, bg, gn, state_gla[l],
                              bt=min(bs, GLA_BT))
            gla_s.append(s_new)
            w_o = _pad_heads(w_out[l][:GLA_W].T, GLA_DV, GLA_VP).T
        else:
            j = l - N_A
            if l == N_A:
                role = NSA_KV * HD
                kv_outs = [(0, 0, 4 * role, f32), (0, 4 * role, 6 * role, f32)]
                kv_outs += [(0, r * role, (r + 1) * role, bf16) for r in range(2, 6)]
                wkv = [w_kv_b.astype(bf16)]
                rows_p, win_p, ks_p, vs_p, kw_p, vw_p = [as_p(a) for a in _proj(x_p, wkv, kv_outs)]
                rows_s, win_s, ks_s, vs_s, kw_s, vw_s = [as_s(a) for a in _proj(x_s, wkv, kv_outs)]
                kc_p, vc_p = _compressed_kv(rows_p.reshape(bp, tp, 4, NSA_KV, HD)[:, :, :2],
                                            cmp_pos, w_cmp1, b_cmp1, w_cmp2)
                assert (past_len + ts) // CMP_S == past_len // CMP_S
                cache_t = jnp.transpose(cache_nsa_kv, (0, 2, 3, 4, 1)).reshape(n_pool, 4 * role, PAGE_SIZE)
                ck_s = _compress_paged(page_table, cache_t, cmp_pos, w_cmp1, b_cmp1, w_cmp2)
                kc_s, vc_s = ck_s[:, 0], ck_s[:, 1]
                wst = state_win_kv.reshape(bs, state_win_kv.shape[1], 2 * role)
                wkv_s = jnp.concatenate([state_win_kv, win_s.reshape(bs, ts, 2, NSA_KV, HD)], axis=1)
                nsa_out = (rows_p.reshape(bp, tp, 4, NSA_KV, HD), _last_rows(win_p.reshape(bp, tp, 2, NSA_KV, HD), wb),
                           rows_s.reshape(bs, ts, 4, NSA_KV, HD), wkv_s[:, -wb:])
            wq = _pair_pack_cols(w_in_b[j][:, :NSA_W]) * q_scale
            wg = jnp.pad(w_in_b[j][:, NSA_W:NSA_W + 3 * NSA_HEADS], ((0, 0), (0, LANES - 3 * NSA_HEADS)))
            wm = w_in_b[j][:, NSA_W + 3 * NSA_HEADS:] * mem_scale
            ws = [wq.astype(bf16), wg.astype(bf16), wm.astype(bf16)]
            outs = [(0, 0, NSA_W, bf16), (1, 0, LANES, f32), (2, 0, MEM_W, bf16)]
            bg = jnp.pad(b_gate_b[j], (0, LANES - 3 * NSA_HEADS)).reshape(1, LANES)
            qp, gl, mq_p = _proj(x_p, ws, outs)
            o_p = _nsa_prompt_call(as_p(qp), as_p(gl), bg, kc_p, vc_p, ks_p, vs_p, kw_p, vw_p)
            qp, gl, mq_s = _proj(x_s, ws, outs)
            o_s = _nsa_sample_call(page_table, as_s(qp), as_s(gl), bg, kc_s, vc_s, ks_s, vs_s, wst, kw_s, vw_s,
                                   cache_t)
            w_o = _pair_pack_cols(w_out[l][:NSA_W].T).T
        om_p = _mem_attn(as_p(mq_p), mem_p[l])
        om_s = _mem_attn(as_s(mq_s), mem_s, layer=l)
        w_o, w_om = w_o.astype(bf16), w_out[l][-MEM_W:].astype(bf16)
        flat = lambda a: a.reshape(-1, a.shape[-1])
        x1_p, gate_p, idx_p = _mix_ln([(flat(o_p), w_o), (flat(om_p), w_om)], x_p, ln1_g[l], ln1_b[l],
                                      w_router[l], b_router[l])
        x1_s, gate_s, idx_s = _mix_ln([(flat(o_s), w_o), (flat(om_s), w_om)], x_s, ln1_g[l], ln1_b[l],
                                      w_router[l], b_router[l])
        dest, blk_exp, nreal = _moe_slots(jnp.concatenate([idx_p.reshape(-1), idx_s.reshape(-1)]))
        dest_p, dest_s = dest[:n_p * TOP_K], dest[n_p * TOP_K:]
        xb = _moe_dispatch(x1_s, dest_s, _moe_dispatch(x1_p, dest_p, xb))
        yb = _moe_ffn_blocks(xb, blk_exp, nreal, w_e1, b_e1, w_e2, b_e2, l)
        x_p = _moe_combine(yb, dest_p, gate_p, x1_p, ln2_g[l], ln2_b[l])
        x_s = _moe_combine(yb, dest_s, gate_s, x1_s, ln2_g[l], ln2_b[l])

    rows_p, win_p_out, rows_s, win_s_out = nsa_out
    return (x_p.reshape(bp, tp, d), x_s.reshape(bs, ts, d),
            jnp.stack(gla_p), jnp.stack(gla_s), rows_p, rows_s, win_p_out, win_s_out, mem_kv_prompt)
```
